```python
import jax
import jax.numpy as jnp
from jax import lax
import numpy as np


D_MODEL = 1024
BATCH = 8
SEQ = 2048
DEPTH = 2

HEAD_DIM = 64
N_MIXERS = 4
GROUP_WIDTH = D_MODEL // N_MIXERS
RMS_EPS = 1e-6
ROPE_THETA = 500000.0
ROPE_DIM = HEAD_DIM // 4

NSA_HEADS = GROUP_WIDTH // HEAD_DIM
NSA_KV_DIM = HEAD_DIM
CMP_BLOCK = 32
CMP_STRIDE = 16
SEL_BLOCK = 64
SEL_TOP_N = 16
WINDOW = 512
SEL_Q_BLOCK = 64
WIN_Q_BLOCK = 128
FORCE_SCORE = 1e4
NSA_PROJ = GROUP_WIDTH + 6 * NSA_KV_DIM + 3 * NSA_HEADS

POOL_WINDOWS = (2, 4, 8, 16)
POOL_GROUP = GROUP_WIDTH // len(POOL_WINDOWS)

RWKV_HEADS = GROUP_WIDTH // HEAD_DIM
DECAY_LORA = 64
AAA_LORA = 64
MV_LORA = 32
GATE_LORA = 160
RWKV_GN_EPS = 64e-5
RWKV_PROJ = 3 * GROUP_WIDTH + DECAY_LORA + AAA_LORA + GATE_LORA

DIL_HEADS = GROUP_WIDTH // HEAD_DIM
DIL_PATTERNS = ((128, 1), (512, 4), (2048, 16))
DIL_Q_BLOCK = 128
DIL_PROJ = 3 * GROUP_WIDTH

IN_WIDTH = NSA_PROJ + GROUP_WIDTH + RWKV_PROJ + DIL_PROJ

N_EXPERTS = 16
N_EXPERT_GROUPS = 4
EXPERTS_PER_GROUP = N_EXPERTS // N_EXPERT_GROUPS
TOP_K = 2
D_EXPERT = 256

MAX_POS_OFFSET = 4096
NEG_INF = -1e30
TINY = 1e-30
F32 = jnp.float32

kernel_name = 'hybrid_nsa_pool_rwkv7_dilated_moe'


def _split(z, sizes):
    out, off = [], 0
    for n in sizes:
        out.append(z[..., off:off + n])
        off += n
    return out


def _rms_norm(x, g):
    xf = x.astype(F32)
    y = xf * lax.rsqrt(jnp.mean(xf * xf, axis=-1, keepdims=True) + RMS_EPS)
    return (y * g.astype(F32)).astype(x.dtype)


def _rope_partial(x, pos):
    half = ROPE_DIM // 2
    inv_freq = ROPE_THETA ** (-2.0 * jnp.arange(half, dtype=F32) / ROPE_DIM)
    ang = pos.astype(F32)[..., None] * inv_freq
    cos, sin = jnp.cos(ang), jnp.sin(ang)
    xf = x.astype(F32)
    x1, x2, xp = xf[..., :half], xf[..., half:ROPE_DIM], xf[..., ROPE_DIM:]
    out = jnp.concatenate([x1 * cos - x2 * sin, x2 * cos + x1 * sin, xp], axis=-1)
    return out.astype(x.dtype)


def _masked_softmax(s, mask):
    s = jnp.where(mask, s, NEG_INF)
    m = jnp.max(s, axis=-1, keepdims=True)
    e = jnp.where(mask, jnp.exp(s - m), 0.0)
    den = jnp.maximum(jnp.sum(e, axis=-1, keepdims=True), TINY)
    return e / den, m + jnp.log(den)


def _nsa(q, kc, vc, ks, vs, kw, vw, gate_logit, pos, q_norm, k_norm, cmp_pe, cmp_w1, cmp_w2):
    B, S, H, hd = q.shape
    scale = hd ** -0.5
    t_idx = jnp.arange(S)
    q = _rope_partial(_rms_norm(q, q_norm), pos[:, :, None])

    ratio = CMP_BLOCK // CMP_STRIDE
    n_chunk = S // CMP_STRIDE
    n_cmp = n_chunk - ratio + 1
    c_end = jnp.arange(n_cmp) * CMP_STRIDE + CMP_BLOCK - 1

    def compress(z, j):
        zc = z.reshape(B, n_chunk, CMP_STRIDE, hd)
        blk = jnp.concatenate([zc[:, i:i + n_cmp] for i in range(ratio)], axis=2)
        blk = (blk + cmp_pe[j]).reshape(B, n_cmp, CMP_BLOCK * hd)
        return jax.nn.gelu(blk @ cmp_w1[j]) @ cmp_w2[j]

    k_cmp = _rope_partial(_rms_norm(compress(kc, 0), k_norm[0]), pos[:, c_end])
    v_cmp = compress(vc, 1).astype(F32)
    s = jnp.einsum('bshd,bcd->bhsc', q, k_cmp).astype(F32) * scale
    p_cmp, _ = _masked_softmax(s, c_end[None, :] <= t_idx[:, None])
    o_cmp = jnp.einsum('bhsc,bcd->bshd', p_cmp, v_cmp)

    n_blk = S // SEL_BLOCK
    n_sel = min(SEL_TOP_N, n_blk)
    b_start = jnp.arange(n_blk) * SEL_BLOCK
    cover = jnp.clip(jnp.minimum(c_end[:, None] + 1, b_start[None, :] + SEL_BLOCK)
                     - jnp.maximum(c_end[:, None] + 1 - CMP_BLOCK, b_start[None, :]), 0)
    cover = cover.astype(F32) / CMP_BLOCK
    imp = jnp.einsum('bhsc,cn->bsn', p_cmp, cover)
    cur = (t_idx // SEL_BLOCK)[:, None]
    blk = jnp.arange(n_blk)[None, :]
    forced = (blk == 0) | (blk == cur) | (blk == cur - 1)
    imp = jnp.where(blk > cur, -1.0, jnp.where(forced, FORCE_SCORE, imp))
    _, sel_idx = lax.top_k(imp, n_sel)

    ks = _rope_partial(_rms_norm(ks, k_norm[1]), pos)
    ks_blk = ks.reshape(B, n_blk, SEL_BLOCK, hd)
    vs_blk = vs.reshape(B, n_blk, SEL_BLOCK, hd)
    nq = S // SEL_Q_BLOCK
    q_b = q.reshape(B, nq, SEL_Q_BLOCK, H, hd).swapaxes(0, 1)
    idx_b = sel_idx.reshape(B, nq, SEL_Q_BLOCK, n_sel).swapaxes(0, 1)
    t_b = t_idx.reshape(nq, SEL_Q_BLOCK)
    b_ar = jnp.arange(B)[:, None, None]
    n_key = n_sel * SEL_BLOCK

    def sel_block(args):
        qq, ii, tt = args
        kg = ks_blk[b_ar, ii]
        vg = vs_blk[b_ar, ii].reshape(B, SEL_Q_BLOCK, n_key, hd).astype(F32)
        kpos = ii[..., None] * SEL_BLOCK + jnp.arange(SEL_BLOCK)
        mask = (kpos <= tt[None, :, None, None]).reshape(B, 1, SEL_Q_BLOCK, n_key)
        sc = jnp.einsum('bqhd,bqnkd->bhqnk', qq, kg).astype(F32).reshape(B, H, SEL_Q_BLOCK, n_key) * scale
        p, _ = _masked_softmax(sc, mask)
        return jnp.einsum('bhqk,bqkd->bqhd', p, vg)

    o_sel = lax.map(sel_block, (q_b, idx_b, t_b)).swapaxes(0, 1).reshape(B, S, H, hd)

    kw = _rope_partial(_rms_norm(kw, k_norm[2]), pos)
    nb = S // WIN_Q_BLOCK
    n_prev = WINDOW // WIN_Q_BLOCK

    def band(z):
        zp = jnp.pad(z, ((0, 0), (n_prev * WIN_Q_BLOCK, 0), (0, 0))).reshape(B, nb + n_prev, WIN_Q_BLOCK, hd)
        return jnp.concatenate([zp[:, j:j + nb] for j in range(n_prev + 1)], axis=2)

    kpos = (jnp.arange(nb)[:, None] - n_prev) * WIN_Q_BLOCK + jnp.arange((n_prev + 1) * WIN_Q_BLOCK)[None, :]
    dist = t_idx.reshape(nb, WIN_Q_BLOCK)[:, :, None] - kpos[:, None, :]
    mask = (kpos[:, None, :] >= 0) & (dist >= 0) & (dist < WINDOW)
    sc = jnp.einsum('bnqhd,bnkd->bhnqk', q.reshape(B, nb, WIN_Q_BLOCK, H, hd), band(kw)).astype(F32) * scale
    p, _ = _masked_softmax(sc, mask)
    o_win = jnp.einsum('bhnqk,bnkd->bnqhd', p, band(vw).astype(F32)).reshape(B, S, H, hd)

    g = jax.nn.sigmoid(gate_logit.astype(F32))
    o = g[..., 0:1] * o_cmp + g[..., 1:2] * o_sel + g[..., 2:3] * o_win
    return o.reshape(B, S, H * hd)


def _pool_mixer(u, pool_w, pool_scale):
    B, S, _ = u.shape
    count = jnp.arange(1, S + 1, dtype=F32)[None, :, None]
    outs = []
    for gi, w in enumerate(POOL_WINDOWS):
        ug = u[..., gi * POOL_GROUP:(gi + 1) * POOL_GROUP].astype(F32)
        cs = jnp.cumsum(ug, axis=1)
        cs_lag = jnp.pad(cs, ((0, 0), (w, 0), (0, 0)))[:, :S]
        mean = (cs - cs_lag) / jnp.minimum(count, float(w))
        outs.append((mean - ug) @ pool_w[gi])
    return jnp.concatenate(outs, axis=-1) * pool_scale


def _rwkv7(z, v_first, v_res, mu, w0, w2, a0, a2, g2, k_k, k_a, r_k, ln_w, ln_b):
    B, S, _ = z.shape
    H, N = RWKV_HEADS, HEAD_DIM
    zf = z.astype(F32)
    z_prev = jnp.pad(zf, ((0, 0), (1, 0), (0, 0)))[:, :-1]
    zs = zf + (z_prev - zf) * mu
    r, k, v, xw, xa, xg = _split(zs, (GROUP_WIDTH,) * 3 + (DECAY_LORA, AAA_LORA, GATE_LORA))
    w_log = -jax.nn.softplus(-(w0 + jnp.tanh(xw) @ w2)) - 0.5
    decay = jnp.exp(-jnp.exp(w_log))
    a = jax.nn.sigmoid(a0 + xa @ a2)
    g = jax.nn.sigmoid(xg) @ g2
    if v_res is None:
        v_first = v
    else:
        v0, v1, v2 = v_res
        v = v + (v_first - v) * jax.nn.sigmoid(v0 + (v @ v1) @ v2)

    def heads(t):
        return t.reshape(B, S, H, N)

    kk = heads(k * k_k)
    kk = kk / jnp.maximum(jnp.sqrt(jnp.sum(kk * kk, axis=-1, keepdims=True)), 1e-12)
    k = k * (1.0 + (a - 1.0) * k_a)
    rh, kh, vh, wh, ah = heads(r), heads(k), heads(v), heads(decay), heads(a)

    def step(state, inp):
        r_t, w_t, k_t, v_t, kk_t, a_t = inp
        sa = jnp.einsum('bhij,bhj->bhi', state, -kk_t)
        state = (state * w_t[:, :, None, :] + sa[..., None] * (kk_t * a_t)[:, :, None, :]
                 + v_t[..., None] * k_t[:, :, None, :])
        return state, jnp.einsum('bhij,bhj->bhi', state, r_t)

    def tmaj(t):
        return jnp.moveaxis(t, 1, 0)

    state0 = jnp.zeros((B, H, N, N), F32)
    _, o = lax.scan(step, state0, (tmaj(rh), tmaj(wh), tmaj(kh), tmaj(vh), tmaj(kk), tmaj(ah)))
    o = jnp.moveaxis(o, 0, 1)
    mean = jnp.mean(o, axis=-1, keepdims=True)
    var = jnp.mean(jnp.square(o - mean), axis=-1, keepdims=True)
    o = ((o - mean) * lax.rsqrt(var + RWKV_GN_EPS)).reshape(B, S, H * N) * ln_w + ln_b
    bonus = jnp.sum(rh * kh * r_k, axis=-1, keepdims=True) * vh
    o = (o + bonus.reshape(B, S, H * N)) * g
    return o, v_first


def _dilated_branch(q, k, v, window, dil):
    B, H, S, hd = q.shape
    L = S // dil
    n_back = window // dil
    nb = -(-L // DIL_Q_BLOCK)
    Lp = nb * DIL_Q_BLOCK
    n_prev = -(-n_back // DIL_Q_BLOCK)

    def strided(z):
        z = z.reshape(B, H, L, dil, hd).swapaxes(2, 3)
        return jnp.pad(z, ((0, 0), (0, 0), (0, 0), (0, Lp - L), (0, 0)))

    def band(z):
        zp = jnp.pad(z, ((0, 0), (0, 0), (0, 0), (n_prev * DIL_Q_BLOCK, 0), (0, 0)))
        zp = zp.reshape(B, H, dil, nb + n_prev, DIL_Q_BLOCK, hd)
        return jnp.concatenate([zp[:, :, :, j:j + nb] for j in range(n_prev + 1)], axis=4)

    qs = strided(q).reshape(B, H, dil, nb, DIL_Q_BLOCK, hd)
    kpos = (jnp.arange(nb)[:, None] - n_prev) * DIL_Q_BLOCK + jnp.arange((n_prev + 1) * DIL_Q_BLOCK)[None, :]
    dist = jnp.arange(Lp).reshape(nb, DIL_Q_BLOCK)[:, :, None] - kpos[:, None, :]
    mask = (kpos[:, None, :] >= 0) & (kpos[:, None, :] < L) & (dist >= 0) & (dist <= n_back)
    s = jnp.einsum('bhrnqd,bhrnkd->bhrnqk', qs, band(strided(k))).astype(F32) * (hd ** -0.5)
    p, lse = _masked_softmax(s, mask)
    o = jnp.einsum('bhrnqk,bhrnkd->bhrnqd', p, band(strided(v)).astype(F32))
    o = o.reshape(B, H, dil, Lp, hd)[:, :, :, :L].swapaxes(2, 3).reshape(B, H, S, hd)
    lse = lse[..., 0].reshape(B, H, dil, Lp)[..., :L].swapaxes(2, 3).reshape(B, H, S)
    return o, lse


def _dilated(q, k, v, pos, q_norm, k_norm):
    B, S, H, hd = q.shape
    q = _rope_partial(_rms_norm(q, q_norm), pos[:, :, None])
    k = _rope_partial(_rms_norm(k, k_norm), pos[:, :, None])
    qh, kh, vh = q.transpose(0, 2, 1, 3), k.transpose(0, 2, 1, 3), v.transpose(0, 2, 1, 3)
    outs, lses = [], []
    for window, dil in DIL_PATTERNS:
        o, lse = _dilated_branch(qh, kh, vh, window, dil)
        outs.append(o)
        lses.append(lse)
    wts = jax.nn.softmax(jnp.stack(lses), axis=0)
    o = jnp.sum(wts[..., None] * jnp.stack(outs), axis=0)
    return o.transpose(0, 2, 1, 3).reshape(B, S, H * hd)


def _moe(h, router_w, router_b, w_gate, w_up, w_down):
    B, S, _ = h.shape
    aff = jax.nn.sigmoid(jnp.einsum('bsd,de->bse', h, router_w).astype(F32))
    sel = (aff + router_b).reshape(B, S, N_EXPERT_GROUPS, EXPERTS_PER_GROUP)
    grp_score = jnp.sum(lax.top_k(sel, TOP_K)[0], axis=-1)
    grp = jnp.argmax(grp_score, axis=-1)
    sel_in = jnp.sum(sel * jax.nn.one_hot(grp, N_EXPERT_GROUPS, dtype=F32)[..., None], axis=2)
    _, loc = lax.top_k(sel_in, TOP_K)
    eidx = grp[..., None] * EXPERTS_PER_GROUP + loc
    a_sel = jnp.take_along_axis(aff, eidx, axis=-1)
    wts = a_sel / jnp.sum(a_sel, axis=-1, keepdims=True)
    gates = jnp.sum(jax.nn.one_hot(eidx, N_EXPERTS, dtype=F32) * wts[..., None], axis=-2)
    hg = jnp.einsum('bsd,edf->bsef', h, w_gate)
    hu = jnp.einsum('bsd,edf->bsef', h, w_up)
    act = jax.nn.silu(hg) * hu * gates[..., None].astype(h.dtype)
    return jnp.einsum('bsef,efd->bsd', act, w_down)


def setup_inputs(seed: int = 0) -> dict:
    key = jax.random.key(seed)
    keys = iter(jax.random.split(key, 64))
    L, D, hd, GW = DEPTH, D_MODEL, HEAD_DIM, GROUP_WIDTH

    def nrm(shape, scale):
        return jax.random.normal(next(keys), shape, F32) * scale

    def gain(shape):
        return 1.0 + nrm(shape, 0.05)

    x = nrm((BATCH, SEQ, D), 1.0)
    c = nrm((BATCH, D), 1.0)
    positions = (jnp.arange(SEQ, dtype=jnp.int32)[None, :]
                 + jax.random.randint(next(keys), (BATCH, 1), 0, MAX_POS_OFFSET, dtype=jnp.int32))
    return {
        'x': x,
        'c': c,
        'positions': positions,
        'ada_w': nrm((L, D, 6 * D), 0.5 * D ** -0.5),
        'ada_b': nrm((L, 6 * D), 0.02),
        'norm_mix_g': gain((L, D)),
        'norm_ffn_g': gain((L, D)),
        'w_in': nrm((L, D, IN_WIDTH), D ** -0.5),
        'w_out': nrm((L, D, D), D ** -0.5),
        'nsa_q_norm': gain((L, hd)),
        'nsa_k_norm': gain((L, 3, hd)),
        'nsa_cmp_pe': nrm((L, 2, CMP_BLOCK, hd), 0.1),
        'nsa_cmp_w1': nrm((L, 2, CMP_BLOCK * hd, hd), (CMP_BLOCK * hd) ** -0.5),
        'nsa_cmp_w2': nrm((L, 2, hd, hd), hd ** -0.5),
        'pool_w': nrm((L, len(POOL_WINDOWS), POOL_GROUP, POOL_GROUP), POOL_GROUP ** -0.5),
        'pool_scale': gain((L, GW)),
        'rwkv_mu': jax.random.uniform(next(keys), (L, RWKV_PROJ), F32),
        'rwkv_w0': nrm((L, GW), 1.0),
        'rwkv_w2': nrm((L, DECAY_LORA, GW), 0.5 * DECAY_LORA ** -0.5),
        'rwkv_a0': nrm((L, GW), 0.5),
        'rwkv_a2': nrm((L, AAA_LORA, GW), 0.5 * AAA_LORA ** -0.5),
        'rwkv_g2': nrm((L, GATE_LORA, GW), GATE_LORA ** -0.5),
        'rwkv_k_k': 0.85 + nrm((L, GW), 0.05),
        'rwkv_k_a': gain((L, GW)),
        'rwkv_r_k': nrm((L, RWKV_HEADS, hd), 0.1),
        'rwkv_ln_w': gain((L, GW)),
        'rwkv_ln_b': nrm((L, GW), 0.02),
        'rwkv_v0': nrm((L - 1, GW), 0.5),
        'rwkv_v1': nrm((L - 1, GW, MV_LORA), GW ** -0.5),
        'rwkv_v2': nrm((L - 1, MV_LORA, GW), 0.5 * MV_LORA ** -0.5),
        'dil_q_norm': gain((L, hd)),
        'dil_k_norm': gain((L, hd)),
        'router_w': nrm((D, N_EXPERTS), D ** -0.5),
        'router_b': nrm((N_EXPERTS,), 0.01),
        'moe_w_gate': nrm((L, N_EXPERTS, D, D_EXPERT), D ** -0.5),
        'moe_w_up': nrm((L, N_EXPERTS, D, D_EXPERT), D ** -0.5),
        'moe_w_down': nrm((L, N_EXPERTS, D_EXPERT, D), D_EXPERT ** -0.5),
    }


def reference(x, c, positions, ada_w, ada_b, norm_mix_g, norm_ffn_g, w_in, w_out,
              nsa_q_norm, nsa_k_norm, nsa_cmp_pe, nsa_cmp_w1, nsa_cmp_w2,
              pool_w, pool_scale,
              rwkv_mu, rwkv_w0, rwkv_w2, rwkv_a0, rwkv_a2, rwkv_g2, rwkv_k_k, rwkv_k_a,
              rwkv_r_k, rwkv_ln_w, rwkv_ln_b, rwkv_v0, rwkv_v1, rwkv_v2,
              dil_q_norm, dil_k_norm,
              router_w, router_b, moe_w_gate, moe_w_up, moe_w_down):
    B, S, D = x.shape
    mod_all = jnp.einsum('bd,lde->lbe', jax.nn.silu(c), ada_w) + ada_b[:, None, :]
    v_first = None
    for l in range(DEPTH):
        sh1, sc1, g1, sh2, sc2, g2 = jnp.split(mod_all[l][:, None, :], 6, axis=-1)

        h = _rms_norm(x, norm_mix_g[l]) * (1.0 + sc1) + sh1
        proj = h @ w_in[l]
        (q_a, kc, vc, ks, vs, kw, vw, gl, u_pool, z_rwkv, q_d, k_d, v_d) = _split(
            proj, (GROUP_WIDTH,) + (NSA_KV_DIM,) * 6 + (3 * NSA_HEADS, GROUP_WIDTH, RWKV_PROJ) + (GROUP_WIDTH,) * 3)

        y_nsa = _nsa(q_a.reshape(B, S, NSA_HEADS, HEAD_DIM), kc, vc, ks, vs, kw, vw,
                     gl.reshape(B, S, NSA_HEADS, 3), positions,
                     nsa_q_norm[l], nsa_k_norm[l], nsa_cmp_pe[l], nsa_cmp_w1[l], nsa_cmp_w2[l])
        y_pool = _pool_mixer(u_pool, pool_w[l], pool_scale[l])
        v_res = None if l == 0 else (rwkv_v0[l - 1], rwkv_v1[l - 1], rwkv_v2[l - 1])
        y_rwkv, v_first = _rwkv7(z_rwkv, v_first, v_res, rwkv_mu[l], rwkv_w0[l], rwkv_w2[l],
                                 rwkv_a0[l], rwkv_a2[l], rwkv_g2[l], rwkv_k_k[l], rwkv_k_a[l],
                                 rwkv_r_k[l], rwkv_ln_w[l], rwkv_ln_b[l])
        y_dil = _dilated(q_d.reshape(B, S, DIL_HEADS, HEAD_DIM), k_d.reshape(B, S, DIL_HEADS, HEAD_DIM),
                         v_d.reshape(B, S, DIL_HEADS, HEAD_DIM), positions, dil_q_norm[l], dil_k_norm[l])
        mix = jnp.concatenate([y_nsa, y_pool, y_rwkv, y_dil], axis=-1).astype(x.dtype)
        x = x + g1 * (mix @ w_out[l])

        h2 = _rms_norm(x, norm_ffn_g[l]) * (1.0 + sc2) + sh2
        x = x + g2 * _moe(h2, router_w, router_b, moe_w_gate[l], moe_w_up[l], moe_w_down[l])
    return x
```

```python
import functools

import numpy as np
import jax
import jax.numpy as jnp
from jax import lax
from jax.experimental import pallas as pl
from jax.experimental.pallas import tpu as pltpu

F32 = jnp.float32
BF16 = jnp.bfloat16
HI = lax.Precision.HIGHEST

D_MODEL = 1024
HEAD_DIM = 64
GROUP_WIDTH = 256
N_HEADS = 4
RMS_EPS = 1e-6
ROPE_THETA = 500000.0
ROPE_DIM = 16
CMP_BLOCK = 32
CMP_STRIDE = 16
SEL_BLOCK = 64
SEL_TOP_N = 16
WINDOW = 512
FORCE_SCORE = 1e4
POOL_WINDOWS = (2, 4, 8, 16)
RWKV_GN_EPS = 64e-5
DIL_PATTERNS = ((128, 1), (512, 4), (2048, 16))
N_EXPERTS = 16
D_EXPERT = 256
NEG_INF = -1e30
TINY = 1e-30

LANES = 128
TQ = 128
TK = 512
RWKV_CHUNK = 64
VMEM_LIMIT = 56 * 1024 * 1024

_IN_COLS = dict(q=(0, 256), kv=(256, 640), gl=(640, 768), pool=(768, 1024),
                rwkv=(1024, 2304), dq=(2304, 2560), dkv=(2560, 3072))
IN_PACKED = 3072
RWKV_PACKED = 1280


def _dot(a, b):
    return jnp.dot(a.astype(BF16), b.astype(BF16), preferred_element_type=F32)


def _dot_hi(a, b):
    return jnp.dot(a, b, precision=HI, preferred_element_type=F32)


def _dot_nt(a, b):
    return lax.dot_general(a, b, (((1,), (1,)), ((), ())), preferred_element_type=F32)


def _dot_nt_hi(a, b):
    return lax.dot_general(a, b, (((1,), (1,)), ((), ())), precision=HI, preferred_element_type=F32)


def _sigmoid(x):
    return 1.0 / (1.0 + jnp.exp(-x))


def _iota(shape, dim):
    return lax.broadcasted_iota(jnp.int32, shape, dim)


def _rope_lanes(y, cos, sin):
    n = y.shape[-1]
    lane = _iota(y.shape, y.ndim - 1)
    partner = jnp.where((lane & 15) < 8, pltpu.roll(y, n - 8, y.ndim - 1), pltpu.roll(y, 8, y.ndim - 1))
    return y * cos + partner * sin


def _params(sem, vmem=VMEM_LIMIT):
    return pltpu.CompilerParams(dimension_semantics=sem, vmem_limit_bytes=vmem)


def _mod_kernel(c_ref, w_ref, b_ref, o_ref):
    c = c_ref[...]
    o_ref[...] = _dot_hi(c * _sigmoid(c), w_ref[...]) + b_ref[...]


def _modulation(c, ada_w, ada_b):
    L, D, E = ada_w.shape
    B = c.shape[0]
    tn = 1024
    out = pl.pallas_call(
        _mod_kernel,
        grid=(L, E // tn),
        in_specs=[pl.BlockSpec((B, D), lambda l, j: (0, 0)),
                  pl.BlockSpec((None, D, tn), lambda l, j: (l, 0, j)),
                  pl.BlockSpec((None, 1, tn), lambda l, j: (l, 0, j))],
        out_specs=pl.BlockSpec((None, B, tn), lambda l, j: (l, 0, j)),
        out_shape=jax.ShapeDtypeStruct((L, B, E), F32),
        compiler_params=_params(("parallel", "parallel")),
        name="modulation",
    )(c, ada_w, ada_b.reshape(L, 1, E))
    return out.reshape(L, B, 6, D)


def _rope_kernel(pos_ref, cpos_ref, freq_ref, sgn_ref, cos_ref, sin_ref, ccos_ref, csin_ref):
    freq = freq_ref[...]
    sgn = sgn_ref[...]
    ang = pos_ref[...].astype(F32) * freq
    cos_ref[...] = jnp.cos(ang)
    sin_ref[...] = jnp.sin(ang) * sgn
    cang = cpos_ref[...].astype(F32) * freq
    ccos_ref[...] = jnp.cos(cang)
    csin_ref[...] = jnp.sin(cang) * sgn


def _rope_tables(positions):
    B, S = positions.shape
    half = ROPE_DIM // 2
    inv_freq = ROPE_THETA ** (-2.0 * jnp.arange(half, dtype=F32) / ROPE_DIM)
    lane = np.arange(LANES)
    rot = (lane % HEAD_DIM) < ROPE_DIM
    freq = jnp.where(rot, inv_freq[lane % half], 0.0).reshape(1, LANES)
    sgn = jnp.asarray(np.where(rot, np.where(lane % HEAD_DIM < half, -1.0, 1.0), 0.0), F32).reshape(1, LANES)
    n_cmp = S // CMP_STRIDE
    cpos = positions[:, CMP_BLOCK - 1::CMP_STRIDE]
    cpos = jnp.pad(cpos, ((0, 0), (0, n_cmp - cpos.shape[1])))
    tab = jax.ShapeDtypeStruct((B, S, LANES), F32)
    ctab = jax.ShapeDtypeStruct((B, n_cmp, LANES), F32)
    return pl.pallas_call(
        _rope_kernel,
        grid=(B,),
        in_specs=[pl.BlockSpec((None, S, 1), lambda b: (b, 0, 0)),
                  pl.BlockSpec((None, n_cmp, 1), lambda b: (b, 0, 0)),
                  pl.BlockSpec((1, LANES), lambda b: (0, 0)),
                  pl.BlockSpec((1, LANES), lambda b: (0, 0))],
        out_specs=[pl.BlockSpec((None, S, LANES), lambda b: (b, 0, 0)),
                   pl.BlockSpec((None, S, LANES), lambda b: (b, 0, 0)),
                   pl.BlockSpec((None, n_cmp, LANES), lambda b: (b, 0, 0)),
                   pl.BlockSpec((None, n_cmp, LANES), lambda b: (b, 0, 0))],
        out_shape=[tab, tab, ctab, ctab],
        compiler_params=_params(("parallel",)),
        name="rope_tables",
    )(positions.reshape(B, S, 1), cpos.reshape(B, n_cmp, 1), freq, sgn)


def _in_kernel(x_ref, mod_ref, g_ref, w_ref, *out_refs):
    x = x_ref[...]
    ms = jnp.mean(x * x, axis=-1, keepdims=True)
    y = x * lax.rsqrt(ms + RMS_EPS) * g_ref[...]
    h = y * (1.0 + mod_ref[1:2, :]) + mod_ref[0:1, :]
    p = _dot(h, w_ref[...])
    for ref, (lo, hi) in zip(out_refs, _IN_COLS.values()):
        ref[...] = p[:, lo:hi]


def _in_proj(x, mod_l, g, w_packed):
    B, S, D = x.shape
    tm = 256
    widths = [hi - lo for lo, hi in _IN_COLS.values()]
    return pl.pallas_call(
        _in_kernel,
        grid=(B, S // tm),
        in_specs=[pl.BlockSpec((None, tm, D), lambda b, i: (b, i, 0)),
                  pl.BlockSpec((None, 6, D), lambda b, i: (b, 0, 0)),
                  pl.BlockSpec((1, D), lambda b, i: (0, 0)),
                  pl.BlockSpec((D, IN_PACKED), lambda b, i: (0, 0))],
        out_specs=[pl.BlockSpec((None, tm, w), lambda b, i: (b, i, 0)) for w in widths],
        out_shape=[jax.ShapeDtypeStruct((B, S, w), F32) for w in widths],
        compiler_params=_params(("parallel", "parallel")),
        name="in_proj",
    )(x, mod_l, g.reshape(1, D), w_packed)


def _pack_w_in(w):
    D = w.shape[0]
    z = lambda n: jnp.zeros((D, n), w.dtype)
    parts = [w[:, 0:640], w[:, 640:652], z(116), w[:, 652:908],
             w[:, 908:1676], w[:, 1676:1740], z(64), w[:, 1740:1804], z(64), w[:, 1804:1964], z(96),
             w[:, 1964:2732]]
    return jnp.concatenate(parts, axis=1).astype(BF16)


def _pack_rwkv_row(v):
    z = lambda n: jnp.zeros((n,), v.dtype)
    return jnp.concatenate([v[0:768], v[768:832], z(64), v[832:896], z(64), v[896:1056], z(96)]).reshape(1, RWKV_PACKED)


def _pool_kernel(u_ref, w_ref, scale_ref, o_ref):
    u = u_ref[...]
    row = _iota(u.shape, 0)
    lane = _iota(u.shape, 1)

    def lag(x, s):
        return jnp.where(row >= s, pltpu.roll(x, s, 0), 0.0)

    s2 = u + lag(u, 1)
    s4 = s2 + lag(s2, 2)
    s8 = s4 + lag(s4, 4)
    s16 = s8 + lag(s8, 8)
    g = lane // (GROUP_WIDTH // len(POOL_WINDOWS))
    tot = jnp.where(g == 0, s2, jnp.where(g == 1, s4, jnp.where(g == 2, s8, s16)))
    win = jnp.where(g == 0, 2.0, jnp.where(g == 1, 4.0, jnp.where(g == 2, 8.0, 16.0)))
    cnt = jnp.minimum((row + 1).astype(F32), win)
    o_ref[...] = _dot(tot / cnt - u, w_ref[...]) * scale_ref[...]


def _pool(u, pool_w, pool_scale):
    B, S, W = u.shape
    wbd = jax.scipy.linalg.block_diag(*[pool_w[i] for i in range(pool_w.shape[0])]).astype(BF16)
    return pl.pallas_call(
        _pool_kernel,
        grid=(B,),
        in_specs=[pl.BlockSpec((None, S, W), lambda b: (b, 0, 0)),
                  pl.BlockSpec((W, W), lambda b: (0, 0)),
                  pl.BlockSpec((1, W), lambda b: (0, 0))],
        out_specs=pl.BlockSpec((None, S, W), lambda b: (b, 0, 0)),
        out_shape=jax.ShapeDtypeStruct((B, S, W), F32),
        compiler_params=_params(("parallel",)),
        name="pool_mixer",
    )(u, wbd, pool_scale.reshape(1, W))


def _head_mean_matrix():
    blk = np.kron(np.eye(N_HEADS), np.ones((HEAD_DIM, HEAD_DIM))) / HEAD_DIM
    return jnp.asarray(blk, F32)


def _pad_heads(x256, as_bf16=True):
    n = x256.shape[0]
    lane = _iota((n, LANES), 1)
    out = []
    for h in range(N_HEADS):
        blk = x256[:, (h // 2) * LANES:(h // 2 + 1) * LANES]
        if h % 2:
            blk = pltpu.roll(blk, HEAD_DIM, 1)
        blk = jnp.where(lane < HEAD_DIM, blk, 0.0)
        out.append(blk.astype(BF16) if as_bf16 else blk)
    return out


def _merge_heads(slabs):
    n = slabs[0].shape[0]
    lane = _iota((n, LANES), 1)
    cols = []
    for p in range(N_HEADS // 2):
        even = pltpu.roll(slabs[2 * p], HEAD_DIM, 1)
        cols.append(jnp.where(lane < HEAD_DIM, even, slabs[2 * p + 1]))
    return jnp.concatenate(cols, axis=1)


def _flash_update(carry, q, kt, bias):
    m, l, acc = carry
    s = _dot_nt(q, kt) + bias
    m_new = jnp.maximum(m, jnp.max(s, axis=-1, keepdims=True))
    alpha = jnp.exp(m - m_new)
    p = jnp.exp(s - m_new)
    l = alpha * l + jnp.sum(p, axis=-1, keepdims=True)
    acc = alpha * acc + jnp.dot(p.astype(BF16), kt, preferred_element_type=F32)
    return m_new, l, acc


def _flash_init(n):
    return (jnp.full((n, 1), NEG_INF, F32), jnp.zeros((n, 1), F32), jnp.zeros((n, LANES), F32))


def _nsa_kernel(q_ref, gl_ref, kv_ref, zc_ref, cos_ref, sin_ref, ccos_ref, csin_ref,
                qn_ref, kn_ref, pet_ref, peb_ref, wt_ref, wb_ref, w2_ref,
                hb_ref, cover_ref, ebig_ref, eg_ref,
                o_ref, ksvs_ref, kwvw_ref, kvc_ref, bias_ref):
    S = kv_ref.shape[0]
    qi = pl.program_id(1)
    n_cmp = zc_ref.shape[0]

    @pl.when(qi == 0)
    def _prep_keys():
        rows = 256

        def body(i, _):
            r0 = pl.multiple_of(i * rows, rows)
            cs = cos_ref[pl.ds(r0, rows), :]
            sn = sin_ref[pl.ds(r0, rows), :]
            first = _iota((rows, LANES), 1) < HEAD_DIM
            for col, j, dst in ((128, 1, ksvs_ref), (256, 2, kwvw_ref)):
                slab = kv_ref[pl.ds(r0, rows), col:col + LANES]
                ms = jnp.sum(jnp.where(first, slab * slab, 0.0), axis=-1, keepdims=True) * (1.0 / HEAD_DIM)
                y = slab * lax.rsqrt(ms + RMS_EPS) * kn_ref[j:j + 1, :]
                dst[pl.ds(r0, rows), :] = jnp.where(first, _rope_lanes(y, cs, sn), slab).astype(BF16)
            return 0

        lax.fori_loop(0, S // rows, body, 0)

        zc = zc_ref[...]
        top = _dot_hi(zc + pet_ref[...], wt_ref[...])
        bot = _dot_hi(zc + peb_ref[...], wb_ref[...])
        pre = top + pltpu.roll(bot, n_cmp - 1, 0)
        act = 0.5 * pre * (1.0 + jnp.tanh(float(np.sqrt(2.0 / np.pi)) * (pre + 0.044715 * (pre * pre * pre))))
        out = _dot_hi(act, w2_ref[...])
        first = _iota(out.shape, 1) < HEAD_DIM
        ms = jnp.sum(jnp.where(first, out * out, 0.0), axis=-1, keepdims=True) * (1.0 / HEAD_DIM)
        y = out * lax.rsqrt(ms + RMS_EPS) * kn_ref[0:1, :]
        kvc_ref[...] = jnp.where(first, _rope_lanes(y, ccos_ref[...], csin_ref[...]), out)

    r0 = pl.multiple_of(qi * TQ, TQ)
    q = q_ref[...]
    ms = _dot_hi(q * q, hb_ref[...])
    qn = q * lax.rsqrt(ms + RMS_EPS) * qn_ref[...]
    cs = cos_ref[pl.ds(r0, TQ), :]
    sn = sin_ref[pl.ds(r0, TQ), :]
    qr = _rope_lanes(qn, jnp.concatenate([cs, cs], axis=1), jnp.concatenate([sn, sn], axis=1)) * (HEAD_DIM ** -0.5)
    qst = jnp.concatenate(_pad_heads(qr, as_bf16=False), axis=0)
    n_rows = N_HEADS * TQ

    kvc = kvc_ref[...]
    s = _dot_nt_hi(qst, kvc)
    t_st = r0 + (_iota((n_rows, n_cmp), 0) & (TQ - 1))
    c_end = _iota((n_rows, n_cmp), 1) * CMP_STRIDE + (CMP_BLOCK - 1)
    mask = c_end <= t_st
    sm = jnp.where(mask, s, NEG_INF)
    m = jnp.max(sm, axis=-1, keepdims=True)
    e = jnp.where(mask, jnp.exp(sm - m), 0.0)
    p = e / jnp.maximum(jnp.sum(e, axis=-1, keepdims=True), TINY)
    o_cmp = _dot_hi(p, kvc)

    psum = p[0:TQ] + p[TQ:2 * TQ] + p[2 * TQ:3 * TQ] + p[3 * TQ:4 * TQ]
    imp = _dot_hi(psum, cover_ref[...])
    n_blk = S // SEL_BLOCK
    lane = _iota((TQ, LANES), 1)
    blk = lane & (n_blk - 1)
    tq = r0 + _iota((TQ, LANES), 0)
    cur = tq // SEL_BLOCK
    forced = (blk == 0) | (blk == cur) | (blk == cur - 1)
    imp = jnp.where(blk > cur, -1.0, jnp.where(forced, FORCE_SCORE, imp))
    rank = jnp.zeros((TQ, LANES), F32)
    for d in range(1, n_blk):
        other = pltpu.roll(imp, LANES - d, 1)
        rank = rank + jnp.where(blk + d >= n_blk,
                                jnp.where(other >= imp, 1.0, 0.0),
                                jnp.where(other > imp, 1.0, 0.0))
    sel = jnp.where(rank < SEL_TOP_N, 1.0, 0.0).astype(BF16)
    selk = jnp.dot(sel, ebig_ref[...], preferred_element_type=F32)
    kpos = _iota((TQ, S), 1)
    tq_s = r0 + _iota((TQ, S), 0)
    bias_ref[...] = jnp.where((selk > 0.5) & (kpos <= tq_s), 0.0, NEG_INF)

    qb = qst.astype(BF16)

    def sel_body(j, carry):
        k0 = pl.multiple_of(j * TK, TK)
        b = bias_ref[:, pl.ds(k0, TK)]
        return _flash_update(carry, qb, ksvs_ref[pl.ds(k0, TK), :], jnp.concatenate([b, b, b, b], axis=0))

    m, l, acc = lax.fori_loop(0, qi // (TK // TQ) + 1, sel_body, _flash_init(n_rows))
    o_sel = acc / l

    n_prev = WINDOW // TQ
    span = (n_prev + 1) * TQ
    k0 = pl.multiple_of(jnp.maximum(qi - n_prev, 0) * TQ, TQ)
    dist = (r0 + _iota((TQ, span), 0)) - (k0 + _iota((TQ, span), 1))
    wb = jnp.where((dist >= 0) & (dist < WINDOW), 0.0, NEG_INF)
    m, l, acc = _flash_update(_flash_init(n_rows), qb, kwvw_ref[pl.ds(k0, span), :],
                              jnp.concatenate([wb, wb, wb, wb], axis=0))
    o_win = acc / l

    gate = _dot_hi(_sigmoid(gl_ref[...]), eg_ref[...])
    split = lambda o: _merge_heads([o[h * TQ:(h + 1) * TQ] for h in range(N_HEADS)])
    W = GROUP_WIDTH
    o_ref[...] = (gate[:, 0:W] * split(o_cmp) + gate[:, W:2 * W] * split(o_sel) + gate[:, 2 * W:3 * W] * split(o_win))


def _nsa(q, kv, gl, cos, sin, ccos, csin, q_norm, k_norm, cmp_pe, cmp_w1, cmp_w2):
    B, S, _ = q.shape
    n_cmp = S // CMP_STRIDE
    n_blk = S // SEL_BLOCK
    hd = HEAD_DIM
    zc = kv[:, :, 0:LANES].reshape(B, n_cmp, CMP_STRIDE * LANES)

    qn = jnp.tile(q_norm, N_HEADS).reshape(1, GROUP_WIDTH)
    kn = jnp.concatenate([k_norm, jnp.ones_like(k_norm)], axis=1)

    def interleave(a, b):
        return jnp.concatenate([a, b], axis=1).reshape(1, -1)

    half = CMP_BLOCK // 2
    pet = interleave(cmp_pe[0, :half], cmp_pe[1, :half])
    peb = interleave(cmp_pe[0, half:], cmp_pe[1, half:])

    def w1_half(lo):
        wk = cmp_w1[0].reshape(CMP_BLOCK, hd, hd)[lo:lo + half]
        wv = cmp_w1[1].reshape(CMP_BLOCK, hd, hd)[lo:lo + half]
        z = jnp.zeros_like(wk)
        top = jnp.concatenate([wk, z], axis=2)
        bot = jnp.concatenate([z, wv], axis=2)
        return jnp.concatenate([top, bot], axis=1).reshape(half * 2 * hd, 2 * hd)

    wt, wb = w1_half(0), w1_half(half)
    w2 = jax.scipy.linalg.block_diag(cmp_w2[0], cmp_w2[1])
    hb = _head_mean_matrix()

    c_end = np.arange(n_cmp) * CMP_STRIDE + CMP_BLOCK - 1
    b_start = np.arange(n_blk) * SEL_BLOCK
    cover = np.maximum(np.minimum(c_end[:, None] + 1, b_start[None, :] + SEL_BLOCK)
                       - np.maximum(c_end[:, None] + 1 - CMP_BLOCK, b_start[None, :]), 0).astype(np.float32) / CMP_BLOCK
    cover[n_cmp - 1] = 0.0
    cover = jnp.asarray(np.tile(cover, (1, LANES // n_blk)), F32)
    ebig = np.zeros((LANES, S), np.float32)
    ebig[np.arange(S) // SEL_BLOCK, np.arange(S)] = 1.0
    ebig = jnp.asarray(ebig, BF16)
    eg = np.zeros((LANES, 3 * GROUP_WIDTH), np.float32)
    for h in range(N_HEADS):
        for br in range(3):
            eg[h * 3 + br, br * GROUP_WIDTH + h * hd: br * GROUP_WIDTH + (h + 1) * hd] = 1.0
    eg = jnp.asarray(eg)

    full = lambda shape: pl.BlockSpec(shape, lambda b, i: (0,) * len(shape))
    return pl.pallas_call(
        _nsa_kernel,
        grid=(B, S // TQ),
        in_specs=[pl.BlockSpec((None, TQ, GROUP_WIDTH), lambda b, i: (b, i, 0)),
                  pl.BlockSpec((None, TQ, LANES), lambda b, i: (b, i, 0)),
                  pl.BlockSpec((None, S, 3 * LANES), lambda b, i: (b, 0, 0)),
                  pl.BlockSpec((None, n_cmp, CMP_STRIDE * LANES), lambda b, i: (b, 0, 0)),
                  pl.BlockSpec((None, S, LANES), lambda b, i: (b, 0, 0)),
                  pl.BlockSpec((None, S, LANES), lambda b, i: (b, 0, 0)),
                  pl.BlockSpec((None, n_cmp, LANES), lambda b, i: (b, 0, 0)),
                  pl.BlockSpec((None, n_cmp, LANES), lambda b, i: (b, 0, 0)),
                  full((1, GROUP_WIDTH)), full((3, LANES)),
                  full((1, CMP_STRIDE * LANES)), full((1, CMP_STRIDE * LANES)),
                  full((CMP_STRIDE * LANES, LANES)), full((CMP_STRIDE * LANES, LANES)), full((LANES, LANES)),
                  full((GROUP_WIDTH, GROUP_WIDTH)), full((n_cmp, LANES)), full((LANES, S)),
                  full((LANES, 3 * GROUP_WIDTH))],
        out_specs=pl.BlockSpec((None, TQ, GROUP_WIDTH), lambda b, i: (b, i, 0)),
        out_shape=jax.ShapeDtypeStruct((B, S, GROUP_WIDTH), F32),
        scratch_shapes=[pltpu.VMEM((S, LANES), BF16), pltpu.VMEM((S, LANES), BF16),
                        pltpu.VMEM((n_cmp, LANES), F32), pltpu.VMEM((TQ, S), F32)],
        compiler_params=_params(("parallel", "arbitrary")),
        name="nsa",
    )(q, gl, kv, zc, cos, sin, ccos, csin, qn, kn, pet, peb, wt, wb, w2, hb, cover, ebig, eg)


def _dil_kernel(q_ref, kv_ref, cos_ref, sin_ref, qn_ref, kn_ref, hb_ref, tb_ref, o_ref, kvs_ref):
    S = kv_ref.shape[0]
    qi = pl.program_id(1)
    W = GROUP_WIDTH

    @pl.when(qi == 0)
    def _prep_keys():
        rows = 256

        def body(i, _):
            r0 = pl.multiple_of(i * rows, rows)
            cs = cos_ref[pl.ds(r0, rows), :]
            sn = sin_ref[pl.ds(r0, rows), :]
            k = kv_ref[pl.ds(r0, rows), 0:W]
            v = kv_ref[pl.ds(r0, rows), W:2 * W]
            ms = _dot_hi(k * k, hb_ref[...])
            kn = k * lax.rsqrt(ms + RMS_EPS) * kn_ref[...]
            kr = _rope_lanes(kn, jnp.concatenate([cs, cs], axis=1), jnp.concatenate([sn, sn], axis=1))
            lane = _iota((rows, LANES), 1)
            for h in range(N_HEADS):
                kb = kr[:, (h // 2) * LANES:(h // 2 + 1) * LANES]
                vb = v[:, (h // 2) * LANES:(h // 2 + 1) * LANES]
                if h % 2:
                    kb = pltpu.roll(kb, HEAD_DIM, 1)
                else:
                    vb = pltpu.roll(vb, HEAD_DIM, 1)
                kvs_ref[h, pl.ds(r0, rows), :] = jnp.where(lane < HEAD_DIM, kb, vb).astype(BF16)
            return 0

        lax.fori_loop(0, S // rows, body, 0)

    r0 = pl.multiple_of(qi * TQ, TQ)
    q = q_ref[...]
    ms = _dot_hi(q * q, hb_ref[...])
    qn = q * lax.rsqrt(ms + RMS_EPS) * qn_ref[...]
    cs = cos_ref[pl.ds(r0, TQ), :]
    sn = sin_ref[pl.ds(r0, TQ), :]
    qr = _rope_lanes(qn, jnp.concatenate([cs, cs], axis=1), jnp.concatenate([sn, sn], axis=1)) * (HEAD_DIM ** -0.5)
    qh = _pad_heads(qr)
    ratio = TK // TQ

    def body(j, carry):
        k0 = pl.multiple_of(j * TK, TK)
        bias = tb_ref[qi - ratio * j]
        return tuple(_flash_update(carry[h], qh[h], kvs_ref[h, pl.ds(k0, TK), :], bias) for h in range(N_HEADS))

    res = lax.fori_loop(0, qi // ratio + 1, body, tuple(_flash_init(TQ) for _ in range(N_HEADS)))
    o_ref[...] = _merge_heads([acc / l for (_, l, acc) in res])


def _dil_bias_table(S):
    n = S // TQ
    d = (np.arange(n)[:, None, None] * TQ + np.arange(TQ)[None, :, None] - np.arange(TK)[None, None, :])
    cnt = np.zeros(d.shape, np.float32)
    for window, dil in DIL_PATTERNS:
        cnt += ((d >= 0) & (d <= window) & (d % dil == 0)).astype(np.float32)
    cnt = jnp.asarray(cnt)
    return jnp.where(cnt > 0, jnp.log(jnp.maximum(cnt, 1.0)), NEG_INF)


def _dilated(q, kv, cos, sin, q_norm, k_norm):
    B, S, W = q.shape
    tb = _dil_bias_table(S)
    qn = jnp.tile(q_norm, N_HEADS).reshape(1, W)
    kn = jnp.tile(k_norm, N_HEADS).reshape(1, W)
    hb = _head_mean_matrix()
    full = lambda shape: pl.BlockSpec(shape, lambda b, i: (0,) * len(shape))
    return pl.pallas_call(
        _dil_kernel,
        grid=(B, S // TQ),
        in_specs=[pl.BlockSpec((None, TQ, W), lambda b, i: (b, i, 0)),
                  pl.BlockSpec((None, S, 2 * W), lambda b, i: (b, 0, 0)),
                  pl.BlockSpec((None, S, LANES), lambda b, i: (b, 0, 0)),
                  pl.BlockSpec((None, S, LANES), lambda b, i: (b, 0, 0)),
                  full((1, W)), full((1, W)), full((W, W)), full(tb.shape)],
        out_specs=pl.BlockSpec((None, TQ, W), lambda b, i: (b, i, 0)),
        out_shape=jax.ShapeDtypeStruct((B, S, W), F32),
        scratch_shapes=[pltpu.VMEM((N_HEADS, S, LANES), BF16)],
        compiler_params=_params(("parallel", "arbitrary")),
        name="dilated",
    )(q, kv, cos, sin, qn, kn, hb, tb)


def _softplus(x):
    return jnp.maximum(x, 0.0) + jnp.log(1.0 + jnp.exp(-jnp.abs(x)))


def _rwkv_kernel(first, *refs):
    if first:
        (z_ref, mu_ref, w0_ref, w2_ref, a0_ref, a2_ref, g2_ref, kk_ref, ka_ref, rk_ref, lnw_ref, lnb_ref,
         hb_ref, y_ref, vf_out_ref, st_ref) = refs
    else:
        (z_ref, vf_ref, mu_ref, w0_ref, w2_ref, a0_ref, a2_ref, g2_ref, kk_ref, ka_ref, rk_ref, lnw_ref, lnb_ref,
         v0_ref, v1_ref, v2_ref, hb_ref, y_ref, st_ref) = refs
    S = z_ref.shape[0]
    C = RWKV_CHUNK
    W = GROUP_WIDTH
    st_ref[...] = jnp.zeros_like(st_ref)

    ri = _iota((C, C), 0)
    ci = _iota((C, C), 1)
    tri_incl = jnp.where(ci <= ri, 1.0, 0.0)
    eye_c = jnp.where(ci == ri, 1.0, 0.0)
    r2 = _iota((2 * C, LANES), 0)
    c2 = _iota((2 * C, LANES), 1)
    pair_mask = (c2 & (C - 1)) <= jnp.where(r2 < C, r2 - 1, r2 - C)
    eye_l = jnp.where(r2 == c2, 1.0, 0.0)
    block_diag = (r2 // HEAD_DIM) == (c2 // HEAD_DIM)
    lane_lo = _iota((C, LANES), 1) < HEAD_DIM
    head_lanes = (c2 < HEAD_DIM, c2 >= HEAD_DIM)
    hb = hb_ref[...]

    def chunk(n, _):
        t0 = pl.multiple_of(n * C, C)
        zc = z_ref[pl.ds(t0, C), :]
        prev = z_ref[pl.ds(jnp.maximum(t0 - 1, 0), 1), :] * jnp.where(n > 0, 1.0, 0.0)
        zp = jnp.where(_iota(zc.shape, 0) == 0, prev, pltpu.roll(zc, 1, 0))
        zs = zc + (zp - zc) * mu_ref[...]
        r, k, v = zs[:, 0:W], zs[:, W:2 * W], zs[:, 2 * W:3 * W]
        xw, xa, xg = zs[:, 768:896], zs[:, 896:1024], zs[:, 1024:1280]
        w_log = -_softplus(-(w0_ref[...] + _dot_hi(jnp.tanh(xw), w2_ref[...]))) - 0.5
        lw = -jnp.exp(w_log)
        a = _sigmoid(a0_ref[...] + _dot_hi(xa, a2_ref[...]))
        g = _dot_hi(_sigmoid(xg), g2_ref[...])
        if first:
            vf_out_ref[pl.ds(t0, C), :] = v
        else:
            vf = vf_ref[pl.ds(t0, C), :]
            v = v + (vf - v) * _sigmoid(v0_ref[...] + _dot_hi(_dot_hi(v, v1_ref[...]), v2_ref[...]))
        kkr = k * kk_ref[...]
        nrm = jnp.sqrt(_dot_hi(kkr * kkr, hb) * HEAD_DIM)
        kk = kkr / jnp.maximum(nrm, 1e-12)
        k2 = k * (1.0 + (a - 1.0) * ka_ref[...])

        outs = []
        for p in range(N_HEADS // 2):
            sl = slice(p * LANES, (p + 1) * LANES)
            lwp = lw[:, sl]
            cum = _dot_hi(tri_incl, lwp)
            cum_end = cum[C - 1:C, :]
            e_pos, e_neg = jnp.exp(cum), jnp.exp(-cum)
            e_exc, e_tail = jnp.exp(cum - lwp), jnp.exp(cum_end - cum)
            kkp, vp = kk[:, sl], v[:, sl]
            ka_p = kkp * a[:, sl]
            ar = jnp.concatenate([-kkp * e_exc, r[:, sl] * e_pos], axis=0)
            bk = jnp.concatenate([ka_p * e_neg, k2[:, sl] * e_neg], axis=0)
            st = st_ref[p]
            xx = _dot_hi(ar, st)
            x, xr = xx[0:C], xx[C:2 * C]
            zero_v = jnp.concatenate([jnp.zeros_like(vp), vp], axis=0)
            gms, zs_h = [], []
            for hh in range(2):
                arm = jnp.where(head_lanes[hh], ar, 0.0)
                gm = jnp.where(pair_mask, _dot_nt_hi(arm, bk), 0.0)
                gms.append(gm)
                lab = gm[0:C, 0:C]
                tinv = eye_c + lab
                lp = lab
                for _ in range(5):
                    lp = _dot_hi(lp, lp)
                    tinv = tinv + _dot_hi(lp, tinv)
                zs_h.append(_dot_hi(tinv, x + _dot_hi(gm[0:C], zero_v)))
            z = jnp.where(lane_lo, zs_h[0], zs_h[1])
            zv = jnp.concatenate([z, vp], axis=0)
            o_h = [xr + _dot_hi(gms[hh][C:2 * C], zv) for hh in range(2)]
            outs.append(jnp.where(lane_lo, o_h[0], o_h[1]))
            tails = jnp.concatenate([ka_p * e_tail, k2[:, sl] * e_tail], axis=0)
            lhs = jnp.concatenate([eye_l * jnp.exp(cum_end), _dot_nt_hi(eye_l, tails)], axis=1)
            st_new = _dot_hi(lhs, jnp.concatenate([st, zv], axis=0))
            st_ref[p] = jnp.where(block_diag, st_new, 0.0)

        o = jnp.concatenate(outs, axis=1)
        mean = _dot_hi(o, hb)
        dev = o - mean
        var = _dot_hi(dev * dev, hb)
        on = dev * lax.rsqrt(var + RWKV_GN_EPS) * lnw_ref[...] + lnb_ref[...]
        bonus = _dot_hi(r * k2 * rk_ref[...], hb) * HEAD_DIM * v
        y_ref[pl.ds(t0, C), :] = (on + bonus) * g
        return 0

    lax.fori_loop(0, S // C, chunk, 0)


def _rwkv(z, v_first, v_res, mu, w0, w2, a0, a2, g2, k_k, k_a, r_k, ln_w, ln_b):
    B, S, P = z.shape
    W = GROUP_WIDTH
    first = v_res is None
    row = lambda v: v.reshape(1, W)
    padr = lambda m, n: jnp.pad(m, ((0, n - m.shape[0]), (0, 0)))
    seq = lambda w: pl.BlockSpec((None, S, w), lambda b: (b, 0, 0))
    full = lambda shape: pl.BlockSpec(shape, lambda b: (0,) * len(shape))
    args = [z]
    specs = [seq(P)]
    if not first:
        args.append(v_first)
        specs.append(seq(W))
    args += [_pack_rwkv_row(mu), row(w0), padr(w2, LANES), row(a0), padr(a2, LANES), padr(g2, 2 * LANES),
             row(k_k), row(k_a), r_k.reshape(1, W), row(ln_w), row(ln_b)]
    specs += [full((1, P)), full((1, W)), full((LANES, W)), full((1, W)), full((LANES, W)), full((2 * LANES, W)),
              full((1, W)), full((1, W)), full((1, W)), full((1, W)), full((1, W))]
    if not first:
        v0, v1, v2 = v_res
        args += [row(v0), jnp.pad(v1, ((0, 0), (0, LANES - v1.shape[1]))), padr(v2, LANES)]
        specs += [full((1, W)), full((W, LANES)), full((LANES, W))]
    args.append(_head_mean_matrix())
    specs.append(full((W, W)))
    out_shape = [jax.ShapeDtypeStruct((B, S, W), F32)]
    out_specs = [seq(W)]
    if first:
        out_shape.append(jax.ShapeDtypeStruct((B, S, W), F32))
        out_specs.append(seq(W))
    res = pl.pallas_call(
        functools.partial(_rwkv_kernel, first),
        grid=(B,),
        in_specs=specs,
        out_specs=out_specs,
        out_shape=out_shape,
        scratch_shapes=[pltpu.VMEM((N_HEADS // 2, LANES, LANES), F32)],
        compiler_params=_params(("parallel",)),
        name="rwkv7_first" if first else "rwkv7",
    )(*args)
    return (res[0], res[1]) if first else (res[0], v_first)


def _out_kernel(x_ref, ya_ref, yb_ref, yc_ref, yd_ref, wo_ref, mod_ref, g_ref, rw_ref, rb_ref,
                xo_ref, h_ref, gates_ref):
    W = GROUP_WIDTH
    mix = (_dot(ya_ref[...], wo_ref[0:W, :]) + _dot(yb_ref[...], wo_ref[W:2 * W, :])
           + _dot(yc_ref[...], wo_ref[2 * W:3 * W, :]) + _dot(yd_ref[...], wo_ref[3 * W:4 * W, :]))
    x = x_ref[...] + mod_ref[2:3, :] * mix
    xo_ref[...] = x
    ms = jnp.mean(x * x, axis=-1, keepdims=True)
    h = x * lax.rsqrt(ms + RMS_EPS) * g_ref[...] * (1.0 + mod_ref[4:5, :]) + mod_ref[3:4, :]
    h_ref[...] = h.astype(BF16)

    aff = _sigmoid(_dot_hi(h, rw_ref[...]))
    s = aff + rb_ref[...]
    lane = _iota(s.shape, 1)
    l4 = lane & 3
    grp = (lane & 15) >> 2

    def in_group(x, d):
        return jnp.where(l4 + d < 4, pltpu.roll(x, LANES - d, 1), pltpu.roll(x, 4 - d, 1))

    def across(x, d):
        return jnp.where(grp + d < 4, pltpu.roll(x, LANES - 4 * d, 1), pltpu.roll(x, 16 - 4 * d, 1))

    rank = jnp.zeros(s.shape, F32)
    for d in range(1, 4):
        o = in_group(s, d)
        rank = rank + jnp.where(l4 + d < 4, jnp.where(o > s, 1.0, 0.0), jnp.where(o >= s, 1.0, 0.0))
    top2 = rank < 2.0
    ts = jnp.where(top2, s, 0.0)
    gs = ts + in_group(ts, 1) + in_group(ts, 2) + in_group(ts, 3)
    lost = jnp.zeros(s.shape, F32)
    for d in range(1, 4):
        o = across(gs, d)
        lost = lost + jnp.where(grp + d < 4, jnp.where(o > gs, 1.0, 0.0), jnp.where(o >= gs, 1.0, 0.0))
    chosen = top2 & (lost < 0.5) & (lane < N_EXPERTS)
    a_sel = jnp.where(chosen, aff, 0.0)
    den = a_sel + in_group(a_sel, 1) + in_group(a_sel, 2) + in_group(a_sel, 3)
    gates_ref[...] = jnp.where(chosen, aff / jnp.where(chosen, den, 1.0), 0.0)


def _out_proj(x, ys, w_out, mod_l, g, router_w, router_b):
    B, S, D = x.shape
    tm = 256
    W = GROUP_WIDTH
    rw = jnp.pad(router_w, ((0, 0), (0, LANES - N_EXPERTS)))
    rb = jnp.pad(router_b, (0, LANES - N_EXPERTS)).reshape(1, LANES)
    tile = lambda w: pl.BlockSpec((None, tm, w), lambda b, i: (b, i, 0))
    full = lambda shape: pl.BlockSpec(shape, lambda b, i: (0,) * len(shape))
    return pl.pallas_call(
        _out_kernel,
        grid=(B, S // tm),
        in_specs=[tile(D), tile(W), tile(W), tile(W), tile(W), full((D, D)),
                  pl.BlockSpec((None, 6, D), lambda b, i: (b, 0, 0)), full((1, D)), full((D, LANES)), full((1, LANES))],
        out_specs=[tile(D), tile(D), tile(LANES)],
        out_shape=[jax.ShapeDtypeStruct((B, S, D), F32), jax.ShapeDtypeStruct((B, S, D), BF16),
                   jax.ShapeDtypeStruct((B, S, LANES), F32)],
        compiler_params=_params(("parallel", "parallel")),
        name="out_proj_router",
    )(x, *ys, w_out.astype(BF16), mod_l, g.reshape(1, D), rw, rb)


def _moe_kernel(x_ref, h_ref, gates_ref, mod_ref, wgu_ref, wd_ref, o_ref):
    e = pl.program_id(2)

    @pl.when(e == 0)
    def _init():
        o_ref[...] = x_ref[...]

    gu = jnp.dot(h_ref[...], wgu_ref[...], preferred_element_type=F32)
    hg, hu = gu[:, 0:D_EXPERT], gu[:, D_EXPERT:2 * D_EXPERT]
    gates = gates_ref[...]
    ge = jnp.sum(jnp.where(_iota(gates.shape, 1) == e, gates, 0.0), axis=-1, keepdims=True)
    act = hg * _sigmoid(hg) * hu * ge
    o_ref[...] += mod_ref[5:6, :] * jnp.dot(act.astype(BF16), wd_ref[...], preferred_element_type=F32)


def _moe(x, h, gates, mod_l, w_gate, w_up, w_down):
    B, S, D = x.shape
    E = w_gate.shape[0]
    tm = 1024
    wgu = jnp.concatenate([w_gate, w_up], axis=2).astype(BF16)
    wd = w_down.astype(BF16)
    tile = lambda w: pl.BlockSpec((None, tm, w), lambda b, i, e: (b, i, 0))
    return pl.pallas_call(
        _moe_kernel,
        grid=(B, S // tm, E),
        in_specs=[tile(D), tile(D), tile(LANES),
                  pl.BlockSpec((None, 6, D), lambda b, i, e: (b, 0, 0)),
                  pl.BlockSpec((None, D, 2 * D_EXPERT), lambda b, i, e: (e, 0, 0)),
                  pl.BlockSpec((None, D_EXPERT, D), lambda b, i, e: (e, 0, 0))],
        out_specs=tile(D),
        out_shape=jax.ShapeDtypeStruct((B, S, D), F32),
        compiler_params=_params(("parallel", "parallel", "arbitrary")),
        name="moe",
    )(x, h, gates, mod_l, wgu, wd)


def kernel(x, c, positions, ada_w, ada_b, norm_mix_g, norm_ffn_g, w_in, w_out, nsa_q_norm, nsa_k_norm, nsa_cmp_pe, nsa_cmp_w1, nsa_cmp_w2, pool_w, pool_scale, rwkv_mu, rwkv_w0, rwkv_w2, rwkv_a0, rwkv_a2, rwkv_g2, rwkv_k_k, rwkv_k_a, rwkv_r_k, rwkv_ln_w, rwkv_ln_b, rwkv_v0, rwkv_v1, rwkv_v2, dil_q_norm, dil_k_norm, router_w, router_b, moe_w_gate, moe_w_up, moe_w_down):
    depth = ada_w.shape[0]
    mod = _modulation(c, ada_w, ada_b)
    cos, sin, ccos, csin = _rope_tables(positions)
    v_first = None
    for l in range(depth):
        q_a, kv_a, gl, u_pool, z_rwkv, q_d, kv_d = _in_proj(x, mod[l], norm_mix_g[l], _pack_w_in(w_in[l]))
        y_nsa = _nsa(q_a, kv_a, gl, cos, sin, ccos, csin, nsa_q_norm[l], nsa_k_norm[l],
                     nsa_cmp_pe[l], nsa_cmp_w1[l], nsa_cmp_w2[l])
        y_pool = _pool(u_pool, pool_w[l], pool_scale[l])
        v_res = None if l == 0 else (rwkv_v0[l - 1], rwkv_v1[l - 1], rwkv_v2[l - 1])
        y_rwkv, v_first = _rwkv(z_rwkv, v_first, v_res, rwkv_mu[l], rwkv_w0[l], rwkv_w2[l], rwkv_a0[l], rwkv_a2[l],
                                rwkv_g2[l], rwkv_k_k[l], rwkv_k_a[l], rwkv_r_k[l], rwkv_ln_w[l], rwkv_ln_b[l])
        y_dil = _dilated(q_d, kv_d, cos, sin, dil_q_norm[l], dil_k_norm[l])
        x_mid, h2, gates = _out_proj(x, (y_nsa, y_pool, y_rwkv, y_dil), w_out[l], mod[l], norm_ffn_g[l],
                                     router_w, router_b)
        x = _moe(x_mid, h2, gates, mod[l], moe_w_gate[l], moe_w_up[l], moe_w_down[l])
    return x
```

```python
import functools

import numpy as np
import jax
import jax.numpy as jnp
from jax import lax
from jax.experimental import pallas as pl
from jax.experimental.pallas import tpu as pltpu

F32 = jnp.float32
BF16 = jnp.bfloat16
HI = lax.Precision.HIGHEST

D_MODEL = 1024
HEAD_DIM = 64
GROUP_WIDTH = 256
N_HEADS = 4
RMS_EPS = 1e-6
ROPE_THETA = 500000.0
ROPE_DIM = 16
CMP_BLOCK = 32
CMP_STRIDE = 16
SEL_BLOCK = 64
SEL_TOP_N = 16
WINDOW = 512
FORCE_SCORE = 1e4
POOL_WINDOWS = (2, 4, 8, 16)
RWKV_GN_EPS = 64e-5
DIL_PATTERNS = ((128, 1), (512, 4), (2048, 16))
N_EXPERTS = 16
D_EXPERT = 256
NEG_INF = -1e30
TINY = 1e-30

LANES = 128
TQ = 128
TK = 512
RWKV_CHUNK = 64
RWKV_GROUP = 4
VMEM_LIMIT = 56 * 1024 * 1024

_IN_COLS = dict(q=(0, 256), kv=(256, 640), gl=(640, 768), pool=(768, 1024),
                rwkv=(1024, 2304), dq=(2304, 2560), dkv=(2560, 3072))
IN_PACKED = 3072
RWKV_PACKED = 1280


def _dot(a, b):
    return jnp.dot(a.astype(BF16), b.astype(BF16), preferred_element_type=F32)


def _dot_hi(a, b):
    return jnp.dot(a, b, precision=HI, preferred_element_type=F32)


def _dot_nt(a, b):
    return lax.dot_general(a, b, (((1,), (1,)), ((), ())), preferred_element_type=F32)


def _split(x):
    hi = x.astype(BF16)
    return hi, (x - hi.astype(F32)).astype(BF16)


def _dot_split(a, b):
    hi, lo = _split(a)
    b = b.astype(BF16)
    return jnp.dot(hi, b, preferred_element_type=F32) + jnp.dot(lo, b, preferred_element_type=F32)


def _dot3(a, b, nt=False):
    f = _dot_nt if nt else functools.partial(jnp.dot, preferred_element_type=F32)
    a_hi, a_lo = _split(a)
    b_hi, b_lo = _split(b)
    return f(a_hi, b_hi) + f(a_lo, b_hi) + f(a_hi, b_lo)


def _sigmoid(x):
    return 1.0 / (1.0 + jnp.exp(-x))


def _iota(shape, dim):
    return lax.broadcasted_iota(jnp.int32, shape, dim)


def _rope_lanes(y, cos, sin):
    n = y.shape[-1]
    lane = _iota(y.shape, y.ndim - 1)
    partner = jnp.where((lane & 15) < 8, pltpu.roll(y, n - 8, y.ndim - 1), pltpu.roll(y, 8, y.ndim - 1))
    return y * cos + partner * sin


def _params(sem, vmem=VMEM_LIMIT):
    return pltpu.CompilerParams(dimension_semantics=sem, vmem_limit_bytes=vmem)


def _mod_kernel(c_ref, w_ref, b_ref, o_ref):
    c = c_ref[...]
    o_ref[...] = _dot_hi(c * _sigmoid(c), w_ref[...]) + b_ref[...]


def _modulation(c, ada_w, ada_b):
    L, D, E = ada_w.shape
    B = c.shape[0]
    tn = 1024
    out = pl.pallas_call(
        _mod_kernel,
        grid=(L, E // tn),
        in_specs=[pl.BlockSpec((B, D), lambda l, j: (0, 0)),
                  pl.BlockSpec((None, D, tn), lambda l, j: (l, 0, j)),
                  pl.BlockSpec((None, 1, tn), lambda l, j: (l, 0, j))],
        out_specs=pl.BlockSpec((None, B, tn), lambda l, j: (l, 0, j)),
        out_shape=jax.ShapeDtypeStruct((L, B, E), F32),
        compiler_params=_params(("parallel", "parallel")),
        name="modulation",
    )(c, ada_w, ada_b.reshape(L, 1, E))
    return out.reshape(L, B, 6, D)


def _rope_kernel(pos_ref, cpos_ref, freq_ref, sgn_ref, cos_ref, sin_ref, ccos_ref, csin_ref):
    freq = freq_ref[...]
    sgn = sgn_ref[...]
    ang = pos_ref[...].astype(F32) * freq
    cos_ref[...] = jnp.cos(ang)
    sin_ref[...] = jnp.sin(ang) * sgn
    cang = cpos_ref[...].astype(F32) * freq
    ccos_ref[...] = jnp.cos(cang)
    csin_ref[...] = jnp.sin(cang) * sgn


def _rope_tables(positions):
    B, S = positions.shape
    half = ROPE_DIM // 2
    inv_freq = ROPE_THETA ** (-2.0 * jnp.arange(half, dtype=F32) / ROPE_DIM)
    lane = np.arange(LANES)
    rot = (lane % HEAD_DIM) < ROPE_DIM
    freq = jnp.where(rot, inv_freq[lane % half], 0.0).reshape(1, LANES)
    sgn = jnp.asarray(np.where(rot, np.where(lane % HEAD_DIM < half, -1.0, 1.0), 0.0), F32).reshape(1, LANES)
    n_cmp = S // CMP_STRIDE
    cpos = positions[:, CMP_BLOCK - 1::CMP_STRIDE]
    cpos = jnp.pad(cpos, ((0, 0), (0, n_cmp - cpos.shape[1])))
    tab = jax.ShapeDtypeStruct((B, S, LANES), F32)
    ctab = jax.ShapeDtypeStruct((B, n_cmp, LANES), F32)
    return pl.pallas_call(
        _rope_kernel,
        grid=(B,),
        in_specs=[pl.BlockSpec((None, S, 1), lambda b: (b, 0, 0)),
                  pl.BlockSpec((None, n_cmp, 1), lambda b: (b, 0, 0)),
                  pl.BlockSpec((1, LANES), lambda b: (0, 0)),
                  pl.BlockSpec((1, LANES), lambda b: (0, 0))],
        out_specs=[pl.BlockSpec((None, S, LANES), lambda b: (b, 0, 0)),
                   pl.BlockSpec((None, S, LANES), lambda b: (b, 0, 0)),
                   pl.BlockSpec((None, n_cmp, LANES), lambda b: (b, 0, 0)),
                   pl.BlockSpec((None, n_cmp, LANES), lambda b: (b, 0, 0))],
        out_shape=[tab, tab, ctab, ctab],
        compiler_params=_params(("parallel",)),
        name="rope_tables",
    )(positions.reshape(B, S, 1), cpos.reshape(B, n_cmp, 1), freq, sgn)


def _in_kernel(x_ref, mod_ref, g_ref, w_ref, *out_refs):
    x = x_ref[...]
    ms = jnp.mean(x * x, axis=-1, keepdims=True)
    y = x * lax.rsqrt(ms + RMS_EPS) * g_ref[...]
    h = y * (1.0 + mod_ref[1:2, :]) + mod_ref[0:1, :]
    p = _dot(h, w_ref[...])
    for ref, (lo, hi) in zip(out_refs, _IN_COLS.values()):
        ref[...] = p[:, lo:hi]


def _in_proj(x, mod_l, g, w_packed):
    B, S, D = x.shape
    tm = 256
    widths = [hi - lo for lo, hi in _IN_COLS.values()]
    return pl.pallas_call(
        _in_kernel,
        grid=(B, S // tm),
        in_specs=[pl.BlockSpec((None, tm, D), lambda b, i: (b, i, 0)),
                  pl.BlockSpec((None, 6, D), lambda b, i: (b, 0, 0)),
                  pl.BlockSpec((1, D), lambda b, i: (0, 0)),
                  pl.BlockSpec((D, IN_PACKED), lambda b, i: (0, 0))],
        out_specs=[pl.BlockSpec((None, tm, w), lambda b, i: (b, i, 0)) for w in widths],
        out_shape=[jax.ShapeDtypeStruct((B, S, w), F32) for w in widths],
        compiler_params=_params(("parallel", "parallel")),
        name="in_proj",
    )(x, mod_l, g.reshape(1, D), w_packed)


def _pack_w_in(w):
    D = w.shape[0]
    z = lambda n: jnp.zeros((D, n), w.dtype)
    parts = [w[:, 0:640], w[:, 640:652], z(116), w[:, 652:908],
             w[:, 908:1676], w[:, 1676:1740], z(64), w[:, 1740:1804], z(64), w[:, 1804:1964], z(96),
             w[:, 1964:2732]]
    return jnp.concatenate(parts, axis=1).astype(BF16)


def _pack_rwkv_row(v):
    z = lambda n: jnp.zeros((n,), v.dtype)
    return jnp.concatenate([v[0:768], v[768:832], z(64), v[832:896], z(64), v[896:1056], z(96)]).reshape(1, RWKV_PACKED)


def _pool_kernel(u_ref, w_ref, scale_ref, o_ref):
    u = u_ref[...]
    row = _iota(u.shape, 0)
    lane = _iota(u.shape, 1)

    def lag(x, s):
        return jnp.where(row >= s, pltpu.roll(x, s, 0), 0.0)

    s2 = u + lag(u, 1)
    s4 = s2 + lag(s2, 2)
    s8 = s4 + lag(s4, 4)
    s16 = s8 + lag(s8, 8)
    g = lane // (GROUP_WIDTH // len(POOL_WINDOWS))
    tot = jnp.where(g == 0, s2, jnp.where(g == 1, s4, jnp.where(g == 2, s8, s16)))
    win = jnp.where(g == 0, 2.0, jnp.where(g == 1, 4.0, jnp.where(g == 2, 8.0, 16.0)))
    cnt = jnp.minimum((row + 1).astype(F32), win)
    o_ref[...] = _dot(tot / cnt - u, w_ref[...]) * scale_ref[...]


def _pool(u, pool_w, pool_scale):
    B, S, W = u.shape
    wbd = jax.scipy.linalg.block_diag(*[pool_w[i] for i in range(pool_w.shape[0])]).astype(BF16)
    return pl.pallas_call(
        _pool_kernel,
        grid=(B,),
        in_specs=[pl.BlockSpec((None, S, W), lambda b: (b, 0, 0)),
                  pl.BlockSpec((W, W), lambda b: (0, 0)),
                  pl.BlockSpec((1, W), lambda b: (0, 0))],
        out_specs=pl.BlockSpec((None, S, W), lambda b: (b, 0, 0)),
        out_shape=jax.ShapeDtypeStruct((B, S, W), F32),
        compiler_params=_params(("parallel",)),
        name="pool_mixer",
    )(u, wbd, pool_scale.reshape(1, W))


def _head_mean_matrix():
    blk = np.kron(np.eye(N_HEADS), np.ones((HEAD_DIM, HEAD_DIM))) / HEAD_DIM
    return jnp.asarray(blk, F32)


def _pad_heads(x256, as_bf16=True):
    n = x256.shape[0]
    lane = _iota((n, LANES), 1)
    out = []
    for h in range(N_HEADS):
        blk = x256[:, (h // 2) * LANES:(h // 2 + 1) * LANES]
        if h % 2:
            blk = pltpu.roll(blk, HEAD_DIM, 1)
        blk = jnp.where(lane < HEAD_DIM, blk, 0.0)
        out.append(blk.astype(BF16) if as_bf16 else blk)
    return out


def _merge_heads(slabs):
    n = slabs[0].shape[0]
    lane = _iota((n, LANES), 1)
    cols = []
    for p in range(N_HEADS // 2):
        even = pltpu.roll(slabs[2 * p], HEAD_DIM, 1)
        cols.append(jnp.where(lane < HEAD_DIM, even, slabs[2 * p + 1]))
    return jnp.concatenate(cols, axis=1)


def _flash_update(carry, q, kt, bias):
    m, l, acc = carry
    s = _dot_nt(q, kt) + bias
    m_new = jnp.maximum(m, jnp.max(s, axis=-1, keepdims=True))
    alpha = jnp.exp(m - m_new)
    p = jnp.exp(s - m_new)
    l = alpha * l + jnp.sum(p, axis=-1, keepdims=True)
    acc = alpha * acc + jnp.dot(p.astype(BF16), kt, preferred_element_type=F32)
    return m_new, l, acc


def _flash_update_heads(carries, qs, kts, bias):
    hs = range(len(qs))
    s = [_dot_nt(qs[h], kts[h]) + bias for h in hs]
    m_new = [jnp.maximum(carries[h][0], jnp.max(s[h], axis=-1, keepdims=True)) for h in hs]
    alpha = [jnp.exp(carries[h][0] - m_new[h]) for h in hs]
    p = [jnp.exp(s[h] - m_new[h]) for h in hs]
    l = [alpha[h] * carries[h][1] + jnp.sum(p[h], axis=-1, keepdims=True) for h in hs]
    pv = [jnp.dot(p[h].astype(BF16), kts[h], preferred_element_type=F32) for h in hs]
    return tuple((m_new[h], l[h], alpha[h] * carries[h][2] + pv[h]) for h in hs)


def _flash_init(n):
    return (jnp.full((n, 1), NEG_INF, F32), jnp.zeros((n, 1), F32), jnp.zeros((n, LANES), F32))


def _nsa_kernel(q_ref, gl_ref, kv_ref, zc_ref, cos_ref, sin_ref, ccos_ref, csin_ref,
                qn_ref, kn_ref, pet_ref, peb_ref, wt_ref, wb_ref, w2_ref,
                hb_ref, cover_ref, ebig_ref, eg_ref,
                o_ref, ksvs_ref, kwvw_ref, kvc_ref, bias_ref):
    S = kv_ref.shape[0]
    qi = pl.program_id(1)
    n_cmp = zc_ref.shape[0]

    @pl.when(qi == 0)
    def _prep_keys():
        rows = 256

        def body(i, _):
            r0 = pl.multiple_of(i * rows, rows)
            cs = cos_ref[pl.ds(r0, rows), :]
            sn = sin_ref[pl.ds(r0, rows), :]
            first = _iota((rows, LANES), 1) < HEAD_DIM
            for col, j, dst in ((128, 1, ksvs_ref), (256, 2, kwvw_ref)):
                slab = kv_ref[pl.ds(r0, rows), col:col + LANES]
                ms = jnp.sum(jnp.where(first, slab * slab, 0.0), axis=-1, keepdims=True) * (1.0 / HEAD_DIM)
                y = slab * lax.rsqrt(ms + RMS_EPS) * kn_ref[j:j + 1, :]
                dst[pl.ds(r0, rows), :] = jnp.where(first, _rope_lanes(y, cs, sn), slab).astype(BF16)
            return 0

        lax.fori_loop(0, S // rows, body, 0)

        zc = zc_ref[...]
        top = _dot3(zc + pet_ref[...], wt_ref[...])
        bot = _dot3(zc + peb_ref[...], wb_ref[...])
        pre = top + pltpu.roll(bot, n_cmp - 1, 0)
        act = 0.5 * pre * (1.0 + jnp.tanh(float(np.sqrt(2.0 / np.pi)) * (pre + 0.044715 * (pre * pre * pre))))
        out = _dot3(act, w2_ref[...])
        first = _iota(out.shape, 1) < HEAD_DIM
        ms = jnp.sum(jnp.where(first, out * out, 0.0), axis=-1, keepdims=True) * (1.0 / HEAD_DIM)
        y = out * lax.rsqrt(ms + RMS_EPS) * kn_ref[0:1, :]
        kvc_ref[...] = jnp.where(first, _rope_lanes(y, ccos_ref[...], csin_ref[...]), out)

    r0 = pl.multiple_of(qi * TQ, TQ)
    q = q_ref[...]
    ms = _dot_split(q * q, hb_ref[...])
    qn = q * lax.rsqrt(ms + RMS_EPS) * qn_ref[...]
    cs = cos_ref[pl.ds(r0, TQ), :]
    sn = sin_ref[pl.ds(r0, TQ), :]
    qr = _rope_lanes(qn, jnp.concatenate([cs, cs], axis=1), jnp.concatenate([sn, sn], axis=1)) * (HEAD_DIM ** -0.5)
    qst = jnp.concatenate(_pad_heads(qr, as_bf16=False), axis=0)
    n_rows = N_HEADS * TQ

    kvc = kvc_ref[...]
    s = _dot3(qst, kvc, nt=True)
    t_st = r0 + (_iota((n_rows, n_cmp), 0) & (TQ - 1))
    c_end = _iota((n_rows, n_cmp), 1) * CMP_STRIDE + (CMP_BLOCK - 1)
    mask = c_end <= t_st
    sm = jnp.where(mask, s, NEG_INF)
    m = jnp.max(sm, axis=-1, keepdims=True)
    e = jnp.where(mask, jnp.exp(sm - m), 0.0)
    p = e / jnp.maximum(jnp.sum(e, axis=-1, keepdims=True), TINY)
    o_cmp = _dot(p, kvc)

    psum = p[0:TQ] + p[TQ:2 * TQ] + p[2 * TQ:3 * TQ] + p[3 * TQ:4 * TQ]
    imp = _dot_split(psum, cover_ref[...])
    n_blk = S // SEL_BLOCK
    lane = _iota((TQ, LANES), 1)
    blk = lane & (n_blk - 1)
    tq = r0 + _iota((TQ, LANES), 0)
    cur = tq // SEL_BLOCK
    forced = (blk == 0) | (blk == cur) | (blk == cur - 1)
    imp = jnp.where(blk > cur, -1.0, jnp.where(forced, FORCE_SCORE, imp))
    rank = jnp.zeros((TQ, LANES), F32)
    for d in range(1, n_blk):
        other = pltpu.roll(imp, LANES - d, 1)
        rank = rank + jnp.where(blk + d >= n_blk,
                                jnp.where(other >= imp, 1.0, 0.0),
                                jnp.where(other > imp, 1.0, 0.0))
    sel = jnp.where(rank < SEL_TOP_N, 1.0, 0.0).astype(BF16)
    selk = jnp.dot(sel, ebig_ref[...], preferred_element_type=F32)
    kpos = _iota((TQ, S), 1)
    tq_s = r0 + _iota((TQ, S), 0)
    bias_ref[...] = jnp.where((selk > 0.5) & (kpos <= tq_s), 0.0, NEG_INF)

    qb = qst.astype(BF16)

    def sel_body(j, carry):
        k0 = pl.multiple_of(j * TK, TK)
        b = bias_ref[:, pl.ds(k0, TK)]
        return _flash_update(carry, qb, ksvs_ref[pl.ds(k0, TK), :], jnp.concatenate([b, b, b, b], axis=0))

    m, l, acc = lax.fori_loop(0, qi // (TK // TQ) + 1, sel_body, _flash_init(n_rows))
    o_sel = acc / l

    n_prev = WINDOW // TQ
    span = (n_prev + 1) * TQ
    k0 = pl.multiple_of(jnp.maximum(qi - n_prev, 0) * TQ, TQ)
    dist = (r0 + _iota((TQ, span), 0)) - (k0 + _iota((TQ, span), 1))
    wb = jnp.where((dist >= 0) & (dist < WINDOW), 0.0, NEG_INF)
    m, l, acc = _flash_update(_flash_init(n_rows), qb, kwvw_ref[pl.ds(k0, span), :],
                              jnp.concatenate([wb, wb, wb, wb], axis=0))
    o_win = acc / l

    gate = _dot_split(_sigmoid(gl_ref[...]), eg_ref[...])
    split = lambda o: _merge_heads([o[h * TQ:(h + 1) * TQ] for h in range(N_HEADS)])
    W = GROUP_WIDTH
    o_ref[...] = (gate[:, 0:W] * split(o_cmp) + gate[:, W:2 * W] * split(o_sel) + gate[:, 2 * W:3 * W] * split(o_win))


def _nsa(q, kv, gl, cos, sin, ccos, csin, q_norm, k_norm, cmp_pe, cmp_w1, cmp_w2):
    B, S, _ = q.shape
    n_cmp = S // CMP_STRIDE
    n_blk = S // SEL_BLOCK
    hd = HEAD_DIM
    zc = kv[:, :, 0:LANES].reshape(B, n_cmp, CMP_STRIDE * LANES)

    qn = jnp.tile(q_norm, N_HEADS).reshape(1, GROUP_WIDTH)
    kn = jnp.concatenate([k_norm, jnp.ones_like(k_norm)], axis=1)

    def interleave(a, b):
        return jnp.concatenate([a, b], axis=1).reshape(1, -1)

    half = CMP_BLOCK // 2
    pet = interleave(cmp_pe[0, :half], cmp_pe[1, :half])
    peb = interleave(cmp_pe[0, half:], cmp_pe[1, half:])

    def w1_half(lo):
        wk = cmp_w1[0].reshape(CMP_BLOCK, hd, hd)[lo:lo + half]
        wv = cmp_w1[1].reshape(CMP_BLOCK, hd, hd)[lo:lo + half]
        z = jnp.zeros_like(wk)
        top = jnp.concatenate([wk, z], axis=2)
        bot = jnp.concatenate([z, wv], axis=2)
        return jnp.concatenate([top, bot], axis=1).reshape(half * 2 * hd, 2 * hd)

    wt, wb = w1_half(0), w1_half(half)
    w2 = jax.scipy.linalg.block_diag(cmp_w2[0], cmp_w2[1])
    hb = _head_mean_matrix()

    c_end = np.arange(n_cmp) * CMP_STRIDE + CMP_BLOCK - 1
    b_start = np.arange(n_blk) * SEL_BLOCK
    cover = np.maximum(np.minimum(c_end[:, None] + 1, b_start[None, :] + SEL_BLOCK)
                       - np.maximum(c_end[:, None] + 1 - CMP_BLOCK, b_start[None, :]), 0).astype(np.float32) / CMP_BLOCK
    cover[n_cmp - 1] = 0.0
    cover = jnp.asarray(np.tile(cover, (1, LANES // n_blk)), F32)
    ebig = np.zeros((LANES, S), np.float32)
    ebig[np.arange(S) // SEL_BLOCK, np.arange(S)] = 1.0
    ebig = jnp.asarray(ebig, BF16)
    eg = np.zeros((LANES, 3 * GROUP_WIDTH), np.float32)
    for h in range(N_HEADS):
        for br in range(3):
            eg[h * 3 + br, br * GROUP_WIDTH + h * hd: br * GROUP_WIDTH + (h + 1) * hd] = 1.0
    eg = jnp.asarray(eg)

    full = lambda shape: pl.BlockSpec(shape, lambda b, i: (0,) * len(shape))
    return pl.pallas_call(
        _nsa_kernel,
        grid=(B, S // TQ),
        in_specs=[pl.BlockSpec((None, TQ, GROUP_WIDTH), lambda b, i: (b, i, 0)),
                  pl.BlockSpec((None, TQ, LANES), lambda b, i: (b, i, 0)),
                  pl.BlockSpec((None, S, 3 * LANES), lambda b, i: (b, 0, 0)),
                  pl.BlockSpec((None, n_cmp, CMP_STRIDE * LANES), lambda b, i: (b, 0, 0)),
                  pl.BlockSpec((None, S, LANES), lambda b, i: (b, 0, 0)),
                  pl.BlockSpec((None, S, LANES), lambda b, i: (b, 0, 0)),
                  pl.BlockSpec((None, n_cmp, LANES), lambda b, i: (b, 0, 0)),
                  pl.BlockSpec((None, n_cmp, LANES), lambda b, i: (b, 0, 0)),
                  full((1, GROUP_WIDTH)), full((3, LANES)),
                  full((1, CMP_STRIDE * LANES)), full((1, CMP_STRIDE * LANES)),
                  full((CMP_STRIDE * LANES, LANES)), full((CMP_STRIDE * LANES, LANES)), full((LANES, LANES)),
                  full((GROUP_WIDTH, GROUP_WIDTH)), full((n_cmp, LANES)), full((LANES, S)),
                  full((LANES, 3 * GROUP_WIDTH))],
        out_specs=pl.BlockSpec((None, TQ, GROUP_WIDTH), lambda b, i: (b, i, 0)),
        out_shape=jax.ShapeDtypeStruct((B, S, GROUP_WIDTH), F32),
        scratch_shapes=[pltpu.VMEM((S, LANES), BF16), pltpu.VMEM((S, LANES), BF16),
                        pltpu.VMEM((n_cmp, LANES), F32), pltpu.VMEM((TQ, S), F32)],
        compiler_params=_params(("parallel", "arbitrary")),
        name="nsa",
    )(q, gl, kv, zc, cos, sin, ccos, csin, qn, kn, pet, peb, wt, wb, w2, hb, cover, ebig, eg)


def _dil_kernel(q_ref, kv_ref, cos_ref, sin_ref, qn_ref, kn_ref, hb_ref, tb_ref, o_ref, kvs_ref):
    S = kv_ref.shape[0]
    qi = pl.program_id(1)
    W = GROUP_WIDTH

    @pl.when(qi == 0)
    def _prep_keys():
        rows = 256

        def body(i, _):
            r0 = pl.multiple_of(i * rows, rows)
            cs = cos_ref[pl.ds(r0, rows), :]
            sn = sin_ref[pl.ds(r0, rows), :]
            k = kv_ref[pl.ds(r0, rows), 0:W]
            v = kv_ref[pl.ds(r0, rows), W:2 * W]
            ms = _dot_split(k * k, hb_ref[...])
            kn = k * lax.rsqrt(ms + RMS_EPS) * kn_ref[...]
            kr = _rope_lanes(kn, jnp.concatenate([cs, cs], axis=1), jnp.concatenate([sn, sn], axis=1))
            lane = _iota((rows, LANES), 1)
            for h in range(N_HEADS):
                kb = kr[:, (h // 2) * LANES:(h // 2 + 1) * LANES]
                vb = v[:, (h // 2) * LANES:(h // 2 + 1) * LANES]
                if h % 2:
                    kb = pltpu.roll(kb, HEAD_DIM, 1)
                else:
                    vb = pltpu.roll(vb, HEAD_DIM, 1)
                kvs_ref[h, pl.ds(r0, rows), :] = jnp.where(lane < HEAD_DIM, kb, vb).astype(BF16)
            return 0

        lax.fori_loop(0, S // rows, body, 0)

    r0 = pl.multiple_of(qi * TQ, TQ)
    q = q_ref[...]
    ms = _dot_split(q * q, hb_ref[...])
    qn = q * lax.rsqrt(ms + RMS_EPS) * qn_ref[...]
    cs = cos_ref[pl.ds(r0, TQ), :]
    sn = sin_ref[pl.ds(r0, TQ), :]
    qr = _rope_lanes(qn, jnp.concatenate([cs, cs], axis=1), jnp.concatenate([sn, sn], axis=1)) * (HEAD_DIM ** -0.5)
    qh = _pad_heads(qr)
    ratio = TK // TQ

    def body(j, carry):
        k0 = pl.multiple_of(j * TK, TK)
        bias = tb_ref[qi - ratio * j]
        return _flash_update_heads(carry, qh, [kvs_ref[h, pl.ds(k0, TK), :] for h in range(N_HEADS)], bias)

    res = lax.fori_loop(0, qi // ratio + 1, body, tuple(_flash_init(TQ) for _ in range(N_HEADS)))
    o_ref[...] = _merge_heads([acc / l for (_, l, acc) in res])


def _dil_bias_table(S):
    n = S // TQ
    d = (np.arange(n)[:, None, None] * TQ + np.arange(TQ)[None, :, None] - np.arange(TK)[None, None, :])
    cnt = np.zeros(d.shape, np.float32)
    for window, dil in DIL_PATTERNS:
        cnt += ((d >= 0) & (d <= window) & (d % dil == 0)).astype(np.float32)
    cnt = jnp.asarray(cnt)
    return jnp.where(cnt > 0, jnp.log(jnp.maximum(cnt, 1.0)), NEG_INF)


def _dilated(q, kv, cos, sin, q_norm, k_norm):
    B, S, W = q.shape
    tb = _dil_bias_table(S)
    qn = jnp.tile(q_norm, N_HEADS).reshape(1, W)
    kn = jnp.tile(k_norm, N_HEADS).reshape(1, W)
    hb = _head_mean_matrix()
    full = lambda shape: pl.BlockSpec(shape, lambda b, i: (0,) * len(shape))
    return pl.pallas_call(
        _dil_kernel,
        grid=(B, S // TQ),
        in_specs=[pl.BlockSpec((None, TQ, W), lambda b, i: (b, i, 0)),
                  pl.BlockSpec((None, S, 2 * W), lambda b, i: (b, 0, 0)),
                  pl.BlockSpec((None, S, LANES), lambda b, i: (b, 0, 0)),
                  pl.BlockSpec((None, S, LANES), lambda b, i: (b, 0, 0)),
                  full((1, W)), full((1, W)), full((W, W)), full(tb.shape)],
        out_specs=pl.BlockSpec((None, TQ, W), lambda b, i: (b, i, 0)),
        out_shape=jax.ShapeDtypeStruct((B, S, W), F32),
        scratch_shapes=[pltpu.VMEM((N_HEADS, S, LANES), BF16)],
        compiler_params=_params(("parallel", "arbitrary")),
        name="dilated",
    )(q, kv, cos, sin, qn, kn, hb, tb)


def _softplus(x):
    return jnp.maximum(x, 0.0) + jnp.log(1.0 + jnp.exp(-jnp.abs(x)))


def _rwkv_kernel(first, *refs):
    if first:
        (z_ref, mu_ref, w0_ref, w2_ref, a0_ref, a2_ref, g2_ref, kk_ref, ka_ref, rk_ref, lnw_ref, lnb_ref,
         hb_ref, y_ref, vf_out_ref, st_ref, prev_ref, ro_ref, ub_ref, colb_ref, g_ref, t2_ref) = refs
    else:
        (z_ref, vf_ref, mu_ref, w0_ref, w2_ref, a0_ref, a2_ref, g2_ref, kk_ref, ka_ref, rk_ref, lnw_ref, lnb_ref,
         v0_ref, v1_ref, v2_ref, hb_ref, y_ref, st_ref, prev_ref, ro_ref, ub_ref, colb_ref, g_ref, t2_ref) = refs
    SB = z_ref.shape[0]
    C = RWKV_CHUNK
    W = GROUP_WIDTH
    n_pairs = N_HEADS // 2
    R = RWKV_GROUP * C

    @pl.when(pl.program_id(1) == 0)
    def _reset():
        st_ref[...] = jnp.zeros_like(st_ref)
        prev_ref[...] = jnp.zeros_like(prev_ref)

    ri = _iota((C, C), 0)
    ci = _iota((C, C), 1)
    tri_incl = jnp.where(ci <= ri, 1.0, 0.0).astype(BF16)
    rr = _iota((2 * LANES, 2 * LANES), 0)
    cc = _iota((2 * LANES, 2 * LANES), 1)
    g_mask = ((((rr >> 6) & 1) == ((cc >> 6) & 1))
              & ((cc & (C - 1)) <= jnp.where(rr < LANES, (rr & (C - 1)) - 1, rr & (C - 1))))
    r2 = _iota((LANES, LANES), 0)
    c2 = _iota((LANES, LANES), 1)
    eye_f = jnp.where(r2 == c2, 1.0, 0.0)
    eye_b = eye_f.astype(BF16)
    lane_lo = _iota((C, LANES), 1) < HEAD_DIM
    hb = hb_ref[...].astype(BF16)

    def stack(x):
        return jnp.concatenate([jnp.where(lane_lo, x, 0.0), jnp.where(lane_lo, 0.0, x)], axis=0)

    def phase1(i, _):
        t0 = pl.multiple_of(i * R, R)
        zc = z_ref[pl.ds(t0, R), :]
        inside = jnp.where(i > 0, 1.0, 0.0)
        prev = inside * z_ref[pl.ds(jnp.maximum(t0 - 1, 0), 1), :] + (1.0 - inside) * prev_ref[...]
        zp = jnp.where(_iota(zc.shape, 0) == 0, prev, pltpu.roll(zc, 1, 0))
        zs = zc + (zp - zc) * mu_ref[...]
        r, k, v = zs[:, 0:W], zs[:, W:2 * W], zs[:, 2 * W:3 * W]
        xw, xa, xg = zs[:, 768:896], zs[:, 896:1024], zs[:, 1024:1280]
        w_log = -_softplus(-(w0_ref[...] + _dot(jnp.tanh(xw), w2_ref[...]))) - 0.5
        lw = -jnp.exp(w_log)
        a = _sigmoid(a0_ref[...] + _dot(xa, a2_ref[...]))
        g = _dot(_sigmoid(xg), g2_ref[...])
        if first:
            vf_out_ref[pl.ds(t0, R), :] = v
        else:
            vf = vf_ref[pl.ds(t0, R), :]
            v = v + (vf - v) * _sigmoid(v0_ref[...] + _dot(_dot(v, v1_ref[...]), v2_ref[...]))
        kkr = k * kk_ref[...]
        nrm = jnp.sqrt(_dot_split(kkr * kkr, hb) * HEAD_DIM)
        kk = kkr / jnp.maximum(nrm, 1e-12)
        k2 = k * (1.0 + (a - 1.0) * ka_ref[...])
        bonus = _dot_split(r * k2 * rk_ref[...], hb) * HEAD_DIM * v
        g_ref[pl.ds(t0, R), :] = g
        t2_ref[pl.ds(t0, R), :] = (lnb_ref[...] + bonus) * g

        items = [(c, p) for c in range(RWKV_GROUP) for p in range(n_pairs)]
        idx = range(len(items))
        sub = lambda x, c, p: x[c * C:(c + 1) * C, p * LANES:(p + 1) * LANES]
        cum = [_cumsum_rows(tri_incl, sub(lw, c, p)) for c, p in items]
        a_st, r_st, v_st, gm, tails = [], [], [], [], []
        for j, (c, p) in enumerate(items):
            lwp = sub(lw, c, p)
            cum_end = cum[j][C - 1:C, :]
            e_pos, e_neg = jnp.exp(cum[j]), jnp.exp(-cum[j])
            e_exc, e_tail = jnp.exp(cum[j] - lwp), jnp.exp(cum_end - cum[j])
            kkp, k2p = sub(kk, c, p), sub(k2, c, p)
            ka_p = kkp * sub(a, c, p)
            a_st.append(stack(-kkp * e_exc).astype(BF16))
            r_st.append(stack(sub(r, c, p) * e_pos))
            v_st.append(stack(sub(v, c, p)).astype(BF16))
            bhat = (ka_p * e_neg).astype(BF16)
            khat = (k2p * e_neg).astype(BF16)
            tails.append(jnp.concatenate([stack(ka_p * e_tail), stack(k2p * e_tail)], axis=0).astype(BF16))
            gm.append(jnp.where(g_mask,
                                _dot_nt(jnp.concatenate([a_st[j], r_st[j].astype(BF16)], axis=0),
                                        jnp.concatenate([bhat, bhat, khat, khat], axis=0)), 0.0))
            decay_end = jnp.broadcast_to(jnp.exp(cum_end), (LANES, LANES))
            hi = decay_end.astype(BF16)
            lo = (decay_end - hi.astype(F32)).astype(BF16)
            colb_ref[i * RWKV_GROUP + c, p] = _dot_nt(eye_b, hi) + _dot_nt(eye_b, lo)
        tt = [_dot_nt(eye_b, tails[j]).astype(BF16) for j in idx]
        lakv = [_dot(gm[j][0:LANES, LANES:2 * LANES], v_st[j]) for j in idx]
        mr = [gm[j][LANES:2 * LANES, :].astype(BF16) for j in idx]
        lp = [gm[j][0:LANES, 0:LANES] for j in idx]
        tinv = [eye_f + lp[j] for j in idx]
        for _ in range(5):
            lpb = [lp[j].astype(BF16) for j in idx]
            lp = [jnp.dot(lpb[j], lpb[j], preferred_element_type=F32) for j in idx]
            tinv = [tinv[j] + _dot(lp[j], tinv[j]) for j in idx]
        wu = [_dot(tinv[j], jnp.concatenate([a_st[j].astype(F32), lakv[j]], axis=1)) for j in idx]
        wm = [wu[j][:, 0:LANES].astype(BF16) for j in idx]
        u = [wu[j][:, LANES:2 * LANES].astype(BF16) for j in idx]
        mwbw = [jnp.dot(jnp.concatenate([mr[j][:, 0:LANES], tt[j][:, 0:LANES]], axis=0), wm[j],
                        preferred_element_type=F32) for j in idx]
        ub = [jnp.dot(jnp.concatenate([mr[j], tt[j]], axis=0), jnp.concatenate([u[j], v_st[j]], axis=0),
                      preferred_element_type=F32) for j in idx]
        for j, (c, p) in enumerate(items):
            n = i * RWKV_GROUP + c
            ro_ref[n, p] = jnp.concatenate([r_st[j] + mwbw[j][0:LANES], mwbw[j][LANES:2 * LANES]], axis=0).astype(BF16)
            ub_ref[n, p] = ub[j]
        return 0

    def phase2(n, _):
        t0 = pl.multiple_of(n * C, C)
        pairs = range(n_pairs)
        st = [st_ref[p] for p in pairs]
        res = [jnp.dot(ro_ref[n, p], st[p].astype(BF16), preferred_element_type=F32) for p in pairs]
        ub = [ub_ref[n, p] for p in pairs]
        for p in pairs:
            st_ref[p] = colb_ref[n, p] * st[p] + res[p][LANES:2 * LANES] + ub[p][LANES:2 * LANES]
        o_st = [res[p][0:LANES] + ub[p][0:LANES] for p in pairs]
        y_ref[pl.ds(t0, C), :] = jnp.concatenate([o[0:C] + o[C:2 * C] for o in o_st], axis=1)
        return 0

    lax.fori_loop(0, SB // R, phase1, 0)
    lax.fori_loop(0, SB // C, phase2, 0)
    prev_ref[...] = z_ref[SB - 1:SB, :]

    for t in range(SB // LANES):
        rows = slice(t * LANES, (t + 1) * LANES)
        o = y_ref[rows, :]
        dev = o - _dot_split(o, hb)
        var = _dot_split(dev * dev, hb)
        y_ref[rows, :] = dev * lax.rsqrt(var + RWKV_GN_EPS) * (lnw_ref[...] * g_ref[rows, :]) + t2_ref[rows, :]


def _cumsum_rows(tri_incl, x):
    hi = x.astype(BF16)
    lo = (x - hi.astype(F32)).astype(BF16)
    return jnp.dot(tri_incl, hi, preferred_element_type=F32) + jnp.dot(tri_incl, lo, preferred_element_type=F32)


RWKV_BLOCK = 512


def _rwkv(z, v_first, v_res, mu, w0, w2, a0, a2, g2, k_k, k_a, r_k, ln_w, ln_b):
    B, S, P = z.shape
    W = GROUP_WIDTH
    SB = RWKV_BLOCK
    ncb = SB // RWKV_CHUNK
    n_pairs = N_HEADS // 2
    first = v_res is None
    row = lambda v: v.reshape(1, W)
    padr = lambda m, n: jnp.pad(m, ((0, n - m.shape[0]), (0, 0)))
    seq = lambda w: pl.BlockSpec((None, SB, w), lambda b, j: (b, j, 0))
    full = lambda shape: pl.BlockSpec(shape, lambda b, j: (0,) * len(shape))
    args = [z]
    specs = [seq(P)]
    if not first:
        args.append(v_first)
        specs.append(seq(W))
    args += [_pack_rwkv_row(mu), row(w0), padr(w2, LANES), row(a0), padr(a2, LANES), padr(g2, 2 * LANES),
             row(k_k), row(k_a), r_k.reshape(1, W), row(ln_w), row(ln_b)]
    specs += [full((1, P)), full((1, W)), full((LANES, W)), full((1, W)), full((LANES, W)), full((2 * LANES, W)),
              full((1, W)), full((1, W)), full((1, W)), full((1, W)), full((1, W))]
    if not first:
        v0, v1, v2 = v_res
        args += [row(v0), jnp.pad(v1, ((0, 0), (0, LANES - v1.shape[1]))), padr(v2, LANES)]
        specs += [full((1, W)), full((W, LANES)), full((LANES, W))]
    args.append(_head_mean_matrix())
    specs.append(full((W, W)))
    out_shape = [jax.ShapeDtypeStruct((B, S, W), F32)]
    out_specs = [seq(W)]
    if first:
        out_shape.append(jax.ShapeDtypeStruct((B, S, W), F32))
        out_specs.append(seq(W))
    res = pl.pallas_call(
        functools.partial(_rwkv_kernel, first),
        grid=(B, S // SB),
        in_specs=specs,
        out_specs=out_specs,
        out_shape=out_shape,
        scratch_shapes=[pltpu.VMEM((n_pairs, LANES, LANES), F32), pltpu.VMEM((1, P), F32),
                        pltpu.VMEM((ncb, n_pairs, 2 * LANES, LANES), BF16),
                        pltpu.VMEM((ncb, n_pairs, 2 * LANES, LANES), F32),
                        pltpu.VMEM((ncb, n_pairs, LANES, LANES), F32),
                        pltpu.VMEM((SB, W), F32), pltpu.VMEM((SB, W), F32)],
        compiler_params=_params(("parallel", "arbitrary")),
        name="rwkv7_first" if first else "rwkv7",
    )(*args)
    return (res[0], res[1]) if first else (res[0], v_first)


def _out_kernel(x_ref, ya_ref, yb_ref, yc_ref, yd_ref, wo_ref, mod_ref, g_ref, rwh_ref, rwl_ref, rb_ref,
                xo_ref, h_ref, gates_ref):
    W = GROUP_WIDTH
    mix = (_dot(ya_ref[...], wo_ref[0:W, :]) + _dot(yb_ref[...], wo_ref[W:2 * W, :])
           + _dot(yc_ref[...], wo_ref[2 * W:3 * W, :]) + _dot(yd_ref[...], wo_ref[3 * W:4 * W, :]))
    x = x_ref[...] + mod_ref[2:3, :] * mix
    xo_ref[...] = x
    ms = jnp.mean(x * x, axis=-1, keepdims=True)
    h = x * lax.rsqrt(ms + RMS_EPS) * g_ref[...] * (1.0 + mod_ref[4:5, :]) + mod_ref[3:4, :]
    h_hi = h.astype(BF16)
    h_ref[...] = h_hi

    h_lo = (h - h_hi.astype(F32)).astype(BF16)
    logits = (jnp.dot(h_hi, rwh_ref[...], preferred_element_type=F32)
              + jnp.dot(h_lo, rwh_ref[...], preferred_element_type=F32)
              + jnp.dot(h_hi, rwl_ref[...], preferred_element_type=F32))
    aff = _sigmoid(logits)
    s = aff + rb_ref[...]
    lane = _iota(s.shape, 1)
    l4 = lane & 3
    grp = (lane & 15) >> 2

    def in_group(x, d):
        return jnp.where(l4 + d < 4, pltpu.roll(x, LANES - d, 1), pltpu.roll(x, 4 - d, 1))

    def across(x, d):
        return jnp.where(grp + d < 4, pltpu.roll(x, LANES - 4 * d, 1), pltpu.roll(x, 16 - 4 * d, 1))

    rank = jnp.zeros(s.shape, F32)
    for d in range(1, 4):
        o = in_group(s, d)
        rank = rank + jnp.where(l4 + d < 4, jnp.where(o > s, 1.0, 0.0), jnp.where(o >= s, 1.0, 0.0))
    top2 = rank < 2.0
    ts = jnp.where(top2, s, 0.0)
    gs = ts + in_group(ts, 1) + in_group(ts, 2) + in_group(ts, 3)
    lost = jnp.zeros(s.shape, F32)
    for d in range(1, 4):
        o = across(gs, d)
        lost = lost + jnp.where(grp + d < 4, jnp.where(o > gs, 1.0, 0.0), jnp.where(o >= gs, 1.0, 0.0))
    chosen = top2 & (lost < 0.5) & (lane < N_EXPERTS)
    a_sel = jnp.where(chosen, aff, 0.0)
    den = a_sel + in_group(a_sel, 1) + in_group(a_sel, 2) + in_group(a_sel, 3)
    gates_ref[...] = jnp.where(chosen, aff / jnp.where(chosen, den, 1.0), 0.0)


def _out_proj(x, ys, w_out, mod_l, g, router_w, router_b):
    B, S, D = x.shape
    tm = 512
    W = GROUP_WIDTH
    rw = jnp.pad(router_w, ((0, 0), (0, LANES - N_EXPERTS)))
    rw_hi = rw.astype(BF16)
    rw_lo = (rw - rw_hi.astype(F32)).astype(BF16)
    rb = jnp.pad(router_b, (0, LANES - N_EXPERTS)).reshape(1, LANES)
    tile = lambda w: pl.BlockSpec((None, tm, w), lambda b, i: (b, i, 0))
    full = lambda shape: pl.BlockSpec(shape, lambda b, i: (0,) * len(shape))
    return pl.pallas_call(
        _out_kernel,
        grid=(B, S // tm),
        in_specs=[tile(D), tile(W), tile(W), tile(W), tile(W), full((D, D)),
                  pl.BlockSpec((None, 6, D), lambda b, i: (b, 0, 0)), full((1, D)), full((D, LANES)), full((D, LANES)),
                  full((1, LANES))],
        out_specs=[tile(D), tile(D), tile(LANES)],
        out_shape=[jax.ShapeDtypeStruct((B, S, D), F32), jax.ShapeDtypeStruct((B, S, D), BF16),
                   jax.ShapeDtypeStruct((B, S, LANES), F32)],
        compiler_params=_params(("parallel", "parallel")),
        name="out_proj_router",
    )(x, *ys, w_out.astype(BF16), mod_l, g.reshape(1, D), rw_hi, rw_lo, rb)


def _moe_kernel(x_ref, h_ref, gates_ref, mod_ref, wgu_ref, wd_ref, o_ref):
    e = pl.program_id(2)

    @pl.when(e == 0)
    def _init():
        o_ref[...] = x_ref[...]

    gu = jnp.dot(h_ref[...], wgu_ref[...], preferred_element_type=F32)
    hg, hu = gu[:, 0:D_EXPERT], gu[:, D_EXPERT:2 * D_EXPERT]
    gates = gates_ref[...]
    ge = jnp.sum(jnp.where(_iota(gates.shape, 1) == e, gates, 0.0), axis=-1, keepdims=True)
    act = hg * _sigmoid(hg) * hu * ge
    o_ref[...] += mod_ref[5:6, :] * jnp.dot(act.astype(BF16), wd_ref[...], preferred_element_type=F32)


def _moe(x, h, gates, mod_l, w_gate, w_up, w_down):
    B, S, D = x.shape
    E = w_gate.shape[0]
    tm = 1024
    wgu = jnp.concatenate([w_gate, w_up], axis=2).astype(BF16)
    wd = w_down.astype(BF16)
    tile = lambda w: pl.BlockSpec((None, tm, w), lambda b, i, e: (b, i, 0))
    return pl.pallas_call(
        _moe_kernel,
        grid=(B, S // tm, E),
        in_specs=[tile(D), tile(D), tile(LANES),
                  pl.BlockSpec((None, 6, D), lambda b, i, e: (b, 0, 0)),
                  pl.BlockSpec((None, D, 2 * D_EXPERT), lambda b, i, e: (e, 0, 0)),
                  pl.BlockSpec((None, D_EXPERT, D), lambda b, i, e: (e, 0, 0))],
        out_specs=tile(D),
        out_shape=jax.ShapeDtypeStruct((B, S, D), F32),
        compiler_params=_params(("parallel", "parallel", "arbitrary")),
        name="moe",
    )(x, h, gates, mod_l, wgu, wd)


def kernel(x, c, positions, ada_w, ada_b, norm_mix_g, norm_ffn_g, w_in, w_out, nsa_q_norm, nsa_k_norm, nsa_cmp_pe, nsa_cmp_w1, nsa_cmp_w2, pool_w, pool_scale, rwkv_mu, rwkv_w0, rwkv_w2, rwkv_a0, rwkv_a2, rwkv_g2, rwkv_k_k, rwkv_k_a, rwkv_r_k, rwkv_ln_w, rwkv_ln_b, rwkv_v0, rwkv_v1, rwkv_v2, dil_q_norm, dil_k_norm, router_w, router_b, moe_w_gate, moe_w_up, moe_w_down):
    depth = ada_w.shape[0]
    mod = _modulation(c, ada_w, ada_b)
    cos, sin, ccos, csin = _rope_tables(positions)
    v_first = None
    for l in range(depth):
        q_a, kv_a, gl, u_pool, z_rwkv, q_d, kv_d = _in_proj(x, mod[l], norm_mix_g[l], _pack_w_in(w_in[l]))
        y_nsa = _nsa(q_a, kv_a, gl, cos, sin, ccos, csin, nsa_q_norm[l], nsa_k_norm[l],
                     nsa_cmp_pe[l], nsa_cmp_w1[l], nsa_cmp_w2[l])
        y_pool = _pool(u_pool, pool_w[l], pool_scale[l])
        v_res = None if l == 0 else (rwkv_v0[l - 1], rwkv_v1[l - 1], rwkv_v2[l - 1])
        y_rwkv, v_first = _rwkv(z_rwkv, v_first, v_res, rwkv_mu[l], rwkv_w0[l], rwkv_w2[l], rwkv_a0[l], rwkv_a2[l],
                                rwkv_g2[l], rwkv_k_k[l], rwkv_k_a[l], rwkv_r_k[l], rwkv_ln_w[l], rwkv_ln_b[l])
        y_dil = _dilated(q_d, kv_d, cos, sin, dil_q_norm[l], dil_k_norm[l])
        x_mid, h2, gates = _out_proj(x, (y_nsa, y_pool, y_rwkv, y_dil), w_out[l], mod[l], norm_ffn_g[l],
                                     router_w, router_b)
        x = _moe(x_mid, h2, gates, mod[l], moe_w_gate[l], moe_w_up[l], moe_w_down[l])
    return x
```

```python
import functools

import numpy as np
import jax
import jax.numpy as jnp
from jax import lax
from jax.experimental import pallas as pl
from jax.experimental.pallas import tpu as pltpu

F32 = jnp.float32
BF16 = jnp.bfloat16
HI = lax.Precision.HIGHEST

D_MODEL = 1024
HEAD_DIM = 64
GROUP_WIDTH = 256
N_HEADS = 4
RMS_EPS = 1e-6
ROPE_THETA = 500000.0
ROPE_DIM = 16
CMP_BLOCK = 32
CMP_STRIDE = 16
SEL_BLOCK = 64
SEL_TOP_N = 16
WINDOW = 512
FORCE_SCORE = 1e4
POOL_WINDOWS = (2, 4, 8, 16)
RWKV_GN_EPS = 64e-5
DIL_PATTERNS = ((128, 1), (512, 4), (2048, 16))
N_EXPERTS = 16
D_EXPERT = 256
NEG_INF = -1e30
LOG2E = 1.4426950408889634
TINY = 1e-30

LANES = 128
TQ = 128
TK = 512
RWKV_CHUNK = 64
RWKV_GROUP = 4
VMEM_LIMIT = 56 * 1024 * 1024

_IN_COLS = dict(q=(0, 256), kv=(256, 640), gl=(640, 768), pool=(768, 1024),
                rwkv=(1024, 2304), dq=(2304, 2560), dkv=(2560, 3072))
IN_PACKED = 3072
RWKV_PACKED = 1280


def _dot(a, b):
    return jnp.dot(a.astype(BF16), b.astype(BF16), preferred_element_type=F32)


def _dot_hi(a, b):
    return jnp.dot(a, b, precision=HI, preferred_element_type=F32)


def _dot_nt(a, b):
    return lax.dot_general(a, b, (((1,), (1,)), ((), ())), preferred_element_type=F32)


def _split(x):
    hi = x.astype(BF16)
    return hi, (x - hi.astype(F32)).astype(BF16)


def _dot_split(a, b):
    hi, lo = _split(a)
    b = b.astype(BF16)
    return jnp.dot(hi, b, preferred_element_type=F32) + jnp.dot(lo, b, preferred_element_type=F32)


def _dot3(a, b, nt=False):
    f = _dot_nt if nt else functools.partial(jnp.dot, preferred_element_type=F32)
    a_hi, a_lo = _split(a)
    b_hi, b_lo = _split(b)
    return f(a_hi, b_hi) + f(a_lo, b_hi) + f(a_hi, b_lo)


def _sigmoid(x):
    return 1.0 / (1.0 + jnp.exp(-x))


def _iota(shape, dim):
    return lax.broadcasted_iota(jnp.int32, shape, dim)


def _rope_lanes(y, cos, sin):
    n = y.shape[-1]
    lane = _iota(y.shape, y.ndim - 1)
    partner = jnp.where((lane & 15) < 8, pltpu.roll(y, n - 8, y.ndim - 1), pltpu.roll(y, 8, y.ndim - 1))
    return y * cos + partner * sin


def _params(sem, vmem=VMEM_LIMIT):
    return pltpu.CompilerParams(dimension_semantics=sem, vmem_limit_bytes=vmem)


def _mod_kernel(c_ref, w_ref, b_ref, o_ref):
    c = c_ref[...]
    o_ref[...] = _dot_hi(c * _sigmoid(c), w_ref[...]) + b_ref[...]


def _modulation(c, ada_w, ada_b):
    L, D, E = ada_w.shape
    B = c.shape[0]
    tn = 1024
    out = pl.pallas_call(
        _mod_kernel,
        grid=(L, E // tn),
        in_specs=[pl.BlockSpec((B, D), lambda l, j: (0, 0)),
                  pl.BlockSpec((None, D, tn), lambda l, j: (l, 0, j)),
                  pl.BlockSpec((None, 1, tn), lambda l, j: (l, 0, j))],
        out_specs=pl.BlockSpec((None, B, tn), lambda l, j: (l, 0, j)),
        out_shape=jax.ShapeDtypeStruct((L, B, E), F32),
        compiler_params=_params(("parallel", "parallel")),
        name="modulation",
    )(c, ada_w, ada_b.reshape(L, 1, E))
    return out.reshape(L, B, 6, D)


def _rope_kernel(pos_ref, cpos_ref, freq_ref, sgn_ref, cos_ref, sin_ref, ccos_ref, csin_ref):
    freq = freq_ref[...]
    sgn = sgn_ref[...]
    ang = pos_ref[...].astype(F32) * freq
    cos_ref[...] = jnp.cos(ang)
    sin_ref[...] = jnp.sin(ang) * sgn
    cang = cpos_ref[...].astype(F32) * freq
    ccos_ref[...] = jnp.cos(cang)
    csin_ref[...] = jnp.sin(cang) * sgn


def _rope_tables(positions):
    B, S = positions.shape
    half = ROPE_DIM // 2
    inv_freq = ROPE_THETA ** (-2.0 * jnp.arange(half, dtype=F32) / ROPE_DIM)
    lane = np.arange(LANES)
    rot = (lane % HEAD_DIM) < ROPE_DIM
    freq = jnp.where(rot, inv_freq[lane % half], 0.0).reshape(1, LANES)
    sgn = jnp.asarray(np.where(rot, np.where(lane % HEAD_DIM < half, -1.0, 1.0), 0.0), F32).reshape(1, LANES)
    n_cmp = S // CMP_STRIDE
    cpos = positions[:, CMP_BLOCK - 1::CMP_STRIDE]
    cpos = jnp.pad(cpos, ((0, 0), (0, n_cmp - cpos.shape[1])))
    tab = jax.ShapeDtypeStruct((B, S, LANES), F32)
    ctab = jax.ShapeDtypeStruct((B, n_cmp, LANES), F32)
    return pl.pallas_call(
        _rope_kernel,
        grid=(B,),
        in_specs=[pl.BlockSpec((None, S, 1), lambda b: (b, 0, 0)),
                  pl.BlockSpec((None, n_cmp, 1), lambda b: (b, 0, 0)),
                  pl.BlockSpec((1, LANES), lambda b: (0, 0)),
                  pl.BlockSpec((1, LANES), lambda b: (0, 0))],
        out_specs=[pl.BlockSpec((None, S, LANES), lambda b: (b, 0, 0)),
                   pl.BlockSpec((None, S, LANES), lambda b: (b, 0, 0)),
                   pl.BlockSpec((None, n_cmp, LANES), lambda b: (b, 0, 0)),
                   pl.BlockSpec((None, n_cmp, LANES), lambda b: (b, 0, 0))],
        out_shape=[tab, tab, ctab, ctab],
        compiler_params=_params(("parallel",)),
        name="rope_tables",
    )(positions.reshape(B, S, 1), cpos.reshape(B, n_cmp, 1), freq, sgn)


def _in_kernel(x_ref, mod_ref, g_ref, w_ref, *out_refs):
    x = x_ref[...]
    ms = jnp.mean(x * x, axis=-1, keepdims=True)
    y = x * lax.rsqrt(ms + RMS_EPS) * g_ref[...]
    h = y * (1.0 + mod_ref[1:2, :]) + mod_ref[0:1, :]
    p = _dot(h, w_ref[...])
    for ref, (lo, hi) in zip(out_refs, _IN_COLS.values()):
        ref[...] = p[:, lo:hi]


def _in_proj(x, mod_l, g, w_packed):
    B, S, D = x.shape
    tm = 256
    widths = [hi - lo for lo, hi in _IN_COLS.values()]
    return pl.pallas_call(
        _in_kernel,
        grid=(B, S // tm),
        in_specs=[pl.BlockSpec((None, tm, D), lambda b, i: (b, i, 0)),
                  pl.BlockSpec((None, 6, D), lambda b, i: (b, 0, 0)),
                  pl.BlockSpec((1, D), lambda b, i: (0, 0)),
                  pl.BlockSpec((D, IN_PACKED), lambda b, i: (0, 0))],
        out_specs=[pl.BlockSpec((None, tm, w), lambda b, i: (b, i, 0)) for w in widths],
        out_shape=[jax.ShapeDtypeStruct((B, S, w), F32) for w in widths],
        compiler_params=_params(("parallel", "parallel")),
        name="in_proj",
    )(x, mod_l, g.reshape(1, D), w_packed)


def _pack_w_in(w):
    D = w.shape[0]
    z = lambda n: jnp.zeros((D, n), w.dtype)
    parts = [w[:, 0:640], w[:, 640:652], z(116), w[:, 652:908],
             w[:, 908:1676], w[:, 1676:1740], z(64), w[:, 1740:1804], z(64), w[:, 1804:1964], z(96),
             w[:, 1964:2732]]
    return jnp.concatenate(parts, axis=1).astype(BF16)


def _pack_rwkv_row(v):
    z = lambda n: jnp.zeros((n,), v.dtype)
    return jnp.concatenate([v[0:768], v[768:832], z(64), v[832:896], z(64), v[896:1056], z(96)]).reshape(1, RWKV_PACKED)


def _pool_kernel(u_ref, w_ref, scale_ref, o_ref):
    u = u_ref[...]
    row = _iota(u.shape, 0)
    lane = _iota(u.shape, 1)

    def lag(x, s):
        return jnp.where(row >= s, pltpu.roll(x, s, 0), 0.0)

    s2 = u + lag(u, 1)
    s4 = s2 + lag(s2, 2)
    s8 = s4 + lag(s4, 4)
    s16 = s8 + lag(s8, 8)
    g = lane // (GROUP_WIDTH // len(POOL_WINDOWS))
    tot = jnp.where(g == 0, s2, jnp.where(g == 1, s4, jnp.where(g == 2, s8, s16)))
    win = jnp.where(g == 0, 2.0, jnp.where(g == 1, 4.0, jnp.where(g == 2, 8.0, 16.0)))
    cnt = jnp.minimum((row + 1).astype(F32), win)
    o_ref[...] = _dot(tot / cnt - u, w_ref[...]) * scale_ref[...]


def _pool(u, pool_w, pool_scale):
    B, S, W = u.shape
    wbd = jax.scipy.linalg.block_diag(*[pool_w[i] for i in range(pool_w.shape[0])]).astype(BF16)
    return pl.pallas_call(
        _pool_kernel,
        grid=(B,),
        in_specs=[pl.BlockSpec((None, S, W), lambda b: (b, 0, 0)),
                  pl.BlockSpec((W, W), lambda b: (0, 0)),
                  pl.BlockSpec((1, W), lambda b: (0, 0))],
        out_specs=pl.BlockSpec((None, S, W), lambda b: (b, 0, 0)),
        out_shape=jax.ShapeDtypeStruct((B, S, W), F32),
        compiler_params=_params(("parallel",)),
        name="pool_mixer",
    )(u, wbd, pool_scale.reshape(1, W))


def _head_mean_matrix():
    blk = np.kron(np.eye(N_HEADS), np.ones((HEAD_DIM, HEAD_DIM))) / HEAD_DIM
    return jnp.asarray(blk, F32)


def _pad_heads(x256, as_bf16=True):
    n = x256.shape[0]
    lane = _iota((n, LANES), 1)
    out = []
    for h in range(N_HEADS):
        blk = x256[:, (h // 2) * LANES:(h // 2 + 1) * LANES]
        if h % 2:
            blk = pltpu.roll(blk, HEAD_DIM, 1)
        blk = jnp.where(lane < HEAD_DIM, blk, 0.0)
        out.append(blk.astype(BF16) if as_bf16 else blk)
    return out


def _merge_heads(slabs):
    n = slabs[0].shape[0]
    lane = _iota((n, LANES), 1)
    cols = []
    for p in range(N_HEADS // 2):
        even = pltpu.roll(slabs[2 * p], HEAD_DIM, 1)
        cols.append(jnp.where(lane < HEAD_DIM, even, slabs[2 * p + 1]))
    return jnp.concatenate(cols, axis=1)


def _flash_step(carries, qs, kas, vbs, biases):
    idx = range(len(qs))
    s = [_dot_nt(qs[i], kas[i]) + biases[i] for i in idx]
    m_new = [jnp.maximum(carries[i][0], jnp.max(s[i], axis=-1, keepdims=True)) for i in idx]
    alpha = [jnp.exp2(carries[i][0] - m_new[i]) for i in idx]
    p = [jnp.exp2(s[i] - m_new[i]).astype(BF16) for i in idx]
    pv = [jnp.dot(p[i], vbs[i], preferred_element_type=F32) for i in idx]
    return tuple((m_new[i], alpha[i] * carries[i][1] + pv[i]) for i in idx)


def _flash_init(n):
    return (jnp.full((n, 1), NEG_INF, F32), jnp.zeros((n, LANES), F32))


def _flash_finish(carry):
    acc = carry[1]
    return acc / pltpu.roll(acc, HEAD_DIM, 1)


def _nsa_kernel(q_ref, gl_ref, kv_ref, zc_ref, cos_ref, sin_ref, ccos_ref, csin_ref,
                qn_ref, kn_ref, pet_ref, peb_ref, wt_ref, wb_ref, w2_ref,
                hb_ref, cover_ref, ebig_ref, eg_ref,
                o_ref, ksvs_ref, kwvw_ref, vs1_ref, vw1_ref, kvc_ref, bias_ref):
    S = kv_ref.shape[0]
    qi = pl.program_id(1)
    n_cmp = zc_ref.shape[0]

    @pl.when(qi == 0)
    def _prep_keys():
        rows = 256

        def body(i, _):
            r0 = pl.multiple_of(i * rows, rows)
            cs = cos_ref[pl.ds(r0, rows), :]
            sn = sin_ref[pl.ds(r0, rows), :]
            first = _iota((rows, LANES), 1) < HEAD_DIM
            for col, j, dst, dst1 in ((128, 1, ksvs_ref, vs1_ref), (256, 2, kwvw_ref, vw1_ref)):
                slab = kv_ref[pl.ds(r0, rows), col:col + LANES]
                ms = jnp.sum(jnp.where(first, slab * slab, 0.0), axis=-1, keepdims=True) * (1.0 / HEAD_DIM)
                y = slab * lax.rsqrt(ms + RMS_EPS) * kn_ref[j:j + 1, :]
                dst[pl.ds(r0, rows), :] = jnp.where(first, _rope_lanes(y, cs, sn), slab).astype(BF16)
                dst1[pl.ds(r0, rows), :] = jnp.where(first, 1.0, slab).astype(BF16)
            return 0

        lax.fori_loop(0, S // rows, body, 0)

        zc = zc_ref[...]
        top = _dot3(zc + pet_ref[...], wt_ref[...])
        bot = _dot3(zc + peb_ref[...], wb_ref[...])
        pre = top + pltpu.roll(bot, n_cmp - 1, 0)
        act = 0.5 * pre * (1.0 + jnp.tanh(float(np.sqrt(2.0 / np.pi)) * (pre + 0.044715 * (pre * pre * pre))))
        out = _dot3(act, w2_ref[...])
        first = _iota(out.shape, 1) < HEAD_DIM
        ms = jnp.sum(jnp.where(first, out * out, 0.0), axis=-1, keepdims=True) * (1.0 / HEAD_DIM)
        y = out * lax.rsqrt(ms + RMS_EPS) * kn_ref[0:1, :]
        kvc_ref[...] = jnp.where(first, _rope_lanes(y, ccos_ref[...], csin_ref[...]), out)

    r0 = pl.multiple_of(qi * TQ, TQ)
    q = q_ref[...]
    ms = _dot_split(q * q, hb_ref[...])
    qn = q * lax.rsqrt(ms + RMS_EPS) * qn_ref[...]
    cs = cos_ref[pl.ds(r0, TQ), :]
    sn = sin_ref[pl.ds(r0, TQ), :]
    qr = _rope_lanes(qn, jnp.concatenate([cs, cs], axis=1), jnp.concatenate([sn, sn], axis=1)) * (HEAD_DIM ** -0.5)
    qst = jnp.concatenate(_pad_heads(qr, as_bf16=False), axis=0)
    n_rows = N_HEADS * TQ

    kvc = kvc_ref[...]
    s = _dot3(qst, kvc, nt=True)
    t_st = r0 + (_iota((n_rows, n_cmp), 0) & (TQ - 1))
    c_end = _iota((n_rows, n_cmp), 1) * CMP_STRIDE + (CMP_BLOCK - 1)
    mask = c_end <= t_st
    sm = jnp.where(mask, s, NEG_INF)
    m = jnp.max(sm, axis=-1, keepdims=True)
    e = jnp.where(mask, jnp.exp(sm - m), 0.0)
    p = e / jnp.maximum(jnp.sum(e, axis=-1, keepdims=True), TINY)
    o_cmp = _dot(p, kvc)

    psum = p[0:TQ] + p[TQ:2 * TQ] + p[2 * TQ:3 * TQ] + p[3 * TQ:4 * TQ]
    imp = _dot_split(psum, cover_ref[...])
    n_blk = S // SEL_BLOCK
    lane = _iota((TQ, LANES), 1)
    blk = lane & (n_blk - 1)
    tq = r0 + _iota((TQ, LANES), 0)
    cur = tq // SEL_BLOCK
    forced = (blk == 0) | (blk == cur) | (blk == cur - 1)
    imp = jnp.where(blk > cur, -1.0, jnp.where(forced, FORCE_SCORE, imp))
    rank = jnp.zeros((TQ, LANES), F32)
    for d in range(1, n_blk):
        other = pltpu.roll(imp, LANES - d, 1)
        rank = rank + jnp.where(blk + d >= n_blk,
                                jnp.where(other >= imp, 1.0, 0.0),
                                jnp.where(other > imp, 1.0, 0.0))
    sel = jnp.where(rank < SEL_TOP_N, 1.0, 0.0).astype(BF16)
    selk = jnp.dot(sel, ebig_ref[...], preferred_element_type=F32)
    kpos = _iota((TQ, S), 1)
    tq_s = r0 + _iota((TQ, S), 0)
    bias_ref[...] = jnp.where((selk > 0.5) & (kpos <= tq_s), 0.0, NEG_INF)

    qb = (qst * LOG2E).astype(BF16)
    halves = [qb[0:n_rows // 2], qb[n_rows // 2:n_rows]]

    def sel_body(j, carry):
        k0 = pl.multiple_of(j * TK, TK)
        b = bias_ref[:, pl.ds(k0, TK)]
        b = jnp.concatenate([b, b], axis=0)
        ka, vb = ksvs_ref[pl.ds(k0, TK), :], vs1_ref[pl.ds(k0, TK), :]
        return _flash_step(carry, halves, [ka, ka], [vb, vb], [b, b])

    res = lax.fori_loop(0, qi // (TK // TQ) + 1, sel_body, (_flash_init(n_rows // 2), _flash_init(n_rows // 2)))
    o_sel = jnp.concatenate([_flash_finish(c) for c in res], axis=0)

    n_prev = WINDOW // TQ
    span = (n_prev + 1) * TQ
    k0 = pl.multiple_of(jnp.maximum(qi - n_prev, 0) * TQ, TQ)
    dist = (r0 + _iota((TQ, span), 0)) - (k0 + _iota((TQ, span), 1))
    wb = jnp.where((dist >= 0) & (dist < WINDOW), 0.0, NEG_INF)
    wb = jnp.concatenate([wb, wb], axis=0)
    ka, vb = kwvw_ref[pl.ds(k0, span), :], vw1_ref[pl.ds(k0, span), :]
    res = _flash_step((_flash_init(n_rows // 2), _flash_init(n_rows // 2)), halves, [ka, ka], [vb, vb], [wb, wb])
    o_win = jnp.concatenate([_flash_finish(c) for c in res], axis=0)

    gate = _dot_split(_sigmoid(gl_ref[...]), eg_ref[...])
    split = lambda o: _merge_heads([o[h * TQ:(h + 1) * TQ] for h in range(N_HEADS)])
    W = GROUP_WIDTH
    o_ref[...] = (gate[:, 0:W] * split(o_cmp) + gate[:, W:2 * W] * split(o_sel) + gate[:, 2 * W:3 * W] * split(o_win))


def _nsa(q, kv, gl, cos, sin, ccos, csin, q_norm, k_norm, cmp_pe, cmp_w1, cmp_w2):
    B, S, _ = q.shape
    n_cmp = S // CMP_STRIDE
    n_blk = S // SEL_BLOCK
    hd = HEAD_DIM
    zc = kv[:, :, 0:LANES].reshape(B, n_cmp, CMP_STRIDE * LANES)

    qn = jnp.tile(q_norm, N_HEADS).reshape(1, GROUP_WIDTH)
    kn = jnp.concatenate([k_norm, jnp.ones_like(k_norm)], axis=1)

    def interleave(a, b):
        return jnp.concatenate([a, b], axis=1).reshape(1, -1)

    half = CMP_BLOCK // 2
    pet = interleave(cmp_pe[0, :half], cmp_pe[1, :half])
    peb = interleave(cmp_pe[0, half:], cmp_pe[1, half:])

    def w1_half(lo):
        wk = cmp_w1[0].reshape(CMP_BLOCK, hd, hd)[lo:lo + half]
        wv = cmp_w1[1].reshape(CMP_BLOCK, hd, hd)[lo:lo + half]
        z = jnp.zeros_like(wk)
        top = jnp.concatenate([wk, z], axis=2)
        bot = jnp.concatenate([z, wv], axis=2)
        return jnp.concatenate([top, bot], axis=1).reshape(half * 2 * hd, 2 * hd)

    wt, wb = w1_half(0), w1_half(half)
    w2 = jax.scipy.linalg.block_diag(cmp_w2[0], cmp_w2[1])
    hb = _head_mean_matrix()

    c_end = np.arange(n_cmp) * CMP_STRIDE + CMP_BLOCK - 1
    b_start = np.arange(n_blk) * SEL_BLOCK
    cover = np.maximum(np.minimum(c_end[:, None] + 1, b_start[None, :] + SEL_BLOCK)
                       - np.maximum(c_end[:, None] + 1 - CMP_BLOCK, b_start[None, :]), 0).astype(np.float32) / CMP_BLOCK
    cover[n_cmp - 1] = 0.0
    cover = jnp.asarray(np.tile(cover, (1, LANES // n_blk)), F32)
    ebig = np.zeros((LANES, S), np.float32)
    ebig[np.arange(S) // SEL_BLOCK, np.arange(S)] = 1.0
    ebig = jnp.asarray(ebig, BF16)
    eg = np.zeros((LANES, 3 * GROUP_WIDTH), np.float32)
    for h in range(N_HEADS):
        for br in range(3):
            eg[h * 3 + br, br * GROUP_WIDTH + h * hd: br * GROUP_WIDTH + (h + 1) * hd] = 1.0
    eg = jnp.asarray(eg)

    full = lambda shape: pl.BlockSpec(shape, lambda b, i: (0,) * len(shape))
    return pl.pallas_call(
        _nsa_kernel,
        grid=(B, S // TQ),
        in_specs=[pl.BlockSpec((None, TQ, GROUP_WIDTH), lambda b, i: (b, i, 0)),
                  pl.BlockSpec((None, TQ, LANES), lambda b, i: (b, i, 0)),
                  pl.BlockSpec((None, S, 3 * LANES), lambda b, i: (b, 0, 0)),
                  pl.BlockSpec((None, n_cmp, CMP_STRIDE * LANES), lambda b, i: (b, 0, 0)),
                  pl.BlockSpec((None, S, LANES), lambda b, i: (b, 0, 0)),
                  pl.BlockSpec((None, S, LANES), lambda b, i: (b, 0, 0)),
                  pl.BlockSpec((None, n_cmp, LANES), lambda b, i: (b, 0, 0)),
                  pl.BlockSpec((None, n_cmp, LANES), lambda b, i: (b, 0, 0)),
                  full((1, GROUP_WIDTH)), full((3, LANES)),
                  full((1, CMP_STRIDE * LANES)), full((1, CMP_STRIDE * LANES)),
                  full((CMP_STRIDE * LANES, LANES)), full((CMP_STRIDE * LANES, LANES)), full((LANES, LANES)),
                  full((GROUP_WIDTH, GROUP_WIDTH)), full((n_cmp, LANES)), full((LANES, S)),
                  full((LANES, 3 * GROUP_WIDTH))],
        out_specs=pl.BlockSpec((None, TQ, GROUP_WIDTH), lambda b, i: (b, i, 0)),
        out_shape=jax.ShapeDtypeStruct((B, S, GROUP_WIDTH), F32),
        scratch_shapes=[pltpu.VMEM((S, LANES), BF16), pltpu.VMEM((S, LANES), BF16),
                        pltpu.VMEM((S, LANES), BF16), pltpu.VMEM((S, LANES), BF16),
                        pltpu.VMEM((n_cmp, LANES), F32), pltpu.VMEM((TQ, S), F32)],
        compiler_params=_params(("parallel", "arbitrary")),
        name="nsa",
    )(q, gl, kv, zc, cos, sin, ccos, csin, qn, kn, pet, peb, wt, wb, w2, hb, cover, ebig, eg)


def _dil_kernel(q_ref, kv_ref, cos_ref, sin_ref, qn_ref, kn_ref, hb_ref, tb_ref, o_ref, kvs_ref, v1s_ref):
    S = kv_ref.shape[0]
    qi = pl.program_id(1)
    W = GROUP_WIDTH

    @pl.when(qi == 0)
    def _prep_keys():
        rows = 256

        def body(i, _):
            r0 = pl.multiple_of(i * rows, rows)
            cs = cos_ref[pl.ds(r0, rows), :]
            sn = sin_ref[pl.ds(r0, rows), :]
            k = kv_ref[pl.ds(r0, rows), 0:W]
            v = kv_ref[pl.ds(r0, rows), W:2 * W]
            ms = _dot_split(k * k, hb_ref[...])
            kn = k * lax.rsqrt(ms + RMS_EPS) * kn_ref[...]
            kr = _rope_lanes(kn, jnp.concatenate([cs, cs], axis=1), jnp.concatenate([sn, sn], axis=1))
            lane = _iota((rows, LANES), 1)
            for h in range(N_HEADS):
                kb = kr[:, (h // 2) * LANES:(h // 2 + 1) * LANES]
                vb = v[:, (h // 2) * LANES:(h // 2 + 1) * LANES]
                if h % 2:
                    kb = pltpu.roll(kb, HEAD_DIM, 1)
                else:
                    vb = pltpu.roll(vb, HEAD_DIM, 1)
                kvs_ref[h, pl.ds(r0, rows), :] = jnp.where(lane < HEAD_DIM, kb, vb).astype(BF16)
                v1s_ref[h, pl.ds(r0, rows), :] = jnp.where(lane < HEAD_DIM, 1.0, vb).astype(BF16)
            return 0

        lax.fori_loop(0, S // rows, body, 0)

    r0 = pl.multiple_of(qi * TQ, TQ)
    q = q_ref[...]
    ms = _dot_split(q * q, hb_ref[...])
    qn = q * lax.rsqrt(ms + RMS_EPS) * qn_ref[...]
    cs = cos_ref[pl.ds(r0, TQ), :]
    sn = sin_ref[pl.ds(r0, TQ), :]
    qr = _rope_lanes(qn, jnp.concatenate([cs, cs], axis=1), jnp.concatenate([sn, sn], axis=1)) * (HEAD_DIM ** -0.5 * LOG2E)
    qh = _pad_heads(qr)
    ratio = TK // TQ

    def body(j, carry):
        k0 = pl.multiple_of(j * TK, TK)
        bias = tb_ref[qi - ratio * j]
        return _flash_step(carry, qh, [kvs_ref[h, pl.ds(k0, TK), :] for h in range(N_HEADS)],
                           [v1s_ref[h, pl.ds(k0, TK), :] for h in range(N_HEADS)], [bias] * N_HEADS)

    res = lax.fori_loop(0, qi // ratio + 1, body, tuple(_flash_init(TQ) for _ in range(N_HEADS)))
    o_ref[...] = _merge_heads([_flash_finish(c) for c in res])


def _dil_bias_table(S):
    n = S // TQ
    d = (np.arange(n)[:, None, None] * TQ + np.arange(TQ)[None, :, None] - np.arange(TK)[None, None, :])
    cnt = np.zeros(d.shape, np.float32)
    for window, dil in DIL_PATTERNS:
        cnt += ((d >= 0) & (d <= window) & (d % dil == 0)).astype(np.float32)
    cnt = jnp.asarray(cnt)
    return jnp.where(cnt > 0, jnp.log2(jnp.maximum(cnt, 1.0)), NEG_INF)


def _dilated(q, kv, cos, sin, q_norm, k_norm):
    B, S, W = q.shape
    tb = _dil_bias_table(S)
    qn = jnp.tile(q_norm, N_HEADS).reshape(1, W)
    kn = jnp.tile(k_norm, N_HEADS).reshape(1, W)
    hb = _head_mean_matrix()
    full = lambda shape: pl.BlockSpec(shape, lambda b, i: (0,) * len(shape))
    return pl.pallas_call(
        _dil_kernel,
        grid=(B, S // TQ),
        in_specs=[pl.BlockSpec((None, TQ, W), lambda b, i: (b, i, 0)),
                  pl.BlockSpec((None, S, 2 * W), lambda b, i: (b, 0, 0)),
                  pl.BlockSpec((None, S, LANES), lambda b, i: (b, 0, 0)),
                  pl.BlockSpec((None, S, LANES), lambda b, i: (b, 0, 0)),
                  full((1, W)), full((1, W)), full((W, W)), full(tb.shape)],
        out_specs=pl.BlockSpec((None, TQ, W), lambda b, i: (b, i, 0)),
        out_shape=jax.ShapeDtypeStruct((B, S, W), F32),
        scratch_shapes=[pltpu.VMEM((N_HEADS, S, LANES), BF16), pltpu.VMEM((N_HEADS, S, LANES), BF16)],
        compiler_params=_params(("parallel", "arbitrary")),
        name="dilated",
    )(q, kv, cos, sin, qn, kn, hb, tb)


def _softplus(x):
    return jnp.maximum(x, 0.0) + jnp.log(1.0 + jnp.exp(-jnp.abs(x)))


def _rwkv_kernel(first, *refs):
    if first:
        (z_ref, mu_ref, w0_ref, w2_ref, a0_ref, a2_ref, g2_ref, kk_ref, ka_ref, rk_ref, lnw_ref, lnb_ref,
         hb_ref, y_ref, vf_out_ref, st_ref, prev_ref, ro_ref, ub_ref, colb_ref, g_ref, t2_ref) = refs
    else:
        (z_ref, vf_ref, mu_ref, w0_ref, w2_ref, a0_ref, a2_ref, g2_ref, kk_ref, ka_ref, rk_ref, lnw_ref, lnb_ref,
         v0_ref, v1_ref, v2_ref, hb_ref, y_ref, st_ref, prev_ref, ro_ref, ub_ref, colb_ref, g_ref, t2_ref) = refs
    SB = z_ref.shape[0]
    C = RWKV_CHUNK
    W = GROUP_WIDTH
    n_pairs = N_HEADS // 2
    R = RWKV_GROUP * C

    @pl.when(pl.program_id(1) == 0)
    def _reset():
        st_ref[...] = jnp.zeros_like(st_ref)
        prev_ref[...] = jnp.zeros_like(prev_ref)

    ri = _iota((C, C), 0)
    ci = _iota((C, C), 1)
    tri_incl = jnp.where(ci <= ri, 1.0, 0.0).astype(BF16)
    rr = _iota((2 * LANES, 2 * LANES), 0)
    cc = _iota((2 * LANES, 2 * LANES), 1)
    g_mask = ((((rr >> 6) & 1) == ((cc >> 6) & 1))
              & ((cc & (C - 1)) <= jnp.where(rr < LANES, (rr & (C - 1)) - 1, rr & (C - 1))))
    r2 = _iota((LANES, LANES), 0)
    c2 = _iota((LANES, LANES), 1)
    eye_f = jnp.where(r2 == c2, 1.0, 0.0)
    eye_b = eye_f.astype(BF16)
    lane_lo = _iota((C, LANES), 1) < HEAD_DIM
    hb = hb_ref[...].astype(BF16)

    def stack(x):
        return jnp.concatenate([jnp.where(lane_lo, x, 0.0), jnp.where(lane_lo, 0.0, x)], axis=0)

    def phase1(i, _):
        t0 = pl.multiple_of(i * R, R)
        zc = z_ref[pl.ds(t0, R), :]
        inside = jnp.where(i > 0, 1.0, 0.0)
        prev = inside * z_ref[pl.ds(jnp.maximum(t0 - 1, 0), 1), :] + (1.0 - inside) * prev_ref[...]
        zp = jnp.where(_iota(zc.shape, 0) == 0, prev, pltpu.roll(zc, 1, 0))
        zs = zc + (zp - zc) * mu_ref[...]
        r, k, v = zs[:, 0:W], zs[:, W:2 * W], zs[:, 2 * W:3 * W]
        xw, xa, xg = zs[:, 768:896], zs[:, 896:1024], zs[:, 1024:1280]
        w_log = -_softplus(-(w0_ref[...] + _dot(jnp.tanh(xw), w2_ref[...]))) - 0.5
        lw = -jnp.exp(w_log)
        a = _sigmoid(a0_ref[...] + _dot(xa, a2_ref[...]))
        g = _dot(_sigmoid(xg), g2_ref[...])
        if first:
            vf_out_ref[pl.ds(t0, R), :] = v
        else:
            vf = vf_ref[pl.ds(t0, R), :]
            v = v + (vf - v) * _sigmoid(v0_ref[...] + _dot(_dot(v, v1_ref[...]), v2_ref[...]))
        kkr = k * kk_ref[...]
        nrm = jnp.sqrt(_dot_split(kkr * kkr, hb) * HEAD_DIM)
        kk = kkr / jnp.maximum(nrm, 1e-12)
        k2 = k * (1.0 + (a - 1.0) * ka_ref[...])
        bonus = _dot_split(r * k2 * rk_ref[...], hb) * HEAD_DIM * v
        g_ref[pl.ds(t0, R), :] = g
        t2_ref[pl.ds(t0, R), :] = (lnb_ref[...] + bonus) * g

        items = [(c, p) for c in range(RWKV_GROUP) for p in range(n_pairs)]
        idx = range(len(items))
        sub = lambda x, c, p: x[c * C:(c + 1) * C, p * LANES:(p + 1) * LANES]
        cum = [_cumsum_rows(tri_incl, sub(lw, c, p)) for c, p in items]
        a_st, r_st, v_st, gm, tails = [], [], [], [], []
        for j, (c, p) in enumerate(items):
            lwp = sub(lw, c, p)
            cum_end = cum[j][C - 1:C, :]
            e_pos, e_neg = jnp.exp(cum[j]), jnp.exp(-cum[j])
            e_exc, e_tail = jnp.exp(cum[j] - lwp), jnp.exp(cum_end - cum[j])
            kkp, k2p = sub(kk, c, p), sub(k2, c, p)
            ka_p = kkp * sub(a, c, p)
            a_st.append(stack(-kkp * e_exc).astype(BF16))
            r_st.append(stack(sub(r, c, p) * e_pos))
            v_st.append(stack(sub(v, c, p)).astype(BF16))
            bhat = (ka_p * e_neg).astype(BF16)
            khat = (k2p * e_neg).astype(BF16)
            tails.append(jnp.concatenate([stack(ka_p * e_tail), stack(k2p * e_tail)], axis=0).astype(BF16))
            gm.append(jnp.where(g_mask,
                                _dot_nt(jnp.concatenate([a_st[j], r_st[j].astype(BF16)], axis=0),
                                        jnp.concatenate([bhat, bhat, khat, khat], axis=0)), 0.0))
            decay_end = jnp.broadcast_to(jnp.exp(cum_end), (LANES, LANES))
            hi = decay_end.astype(BF16)
            lo = (decay_end - hi.astype(F32)).astype(BF16)
            colb_ref[i * RWKV_GROUP + c, p] = _dot_nt(eye_b, hi) + _dot_nt(eye_b, lo)
        tt = [_dot_nt(eye_b, tails[j]).astype(BF16) for j in idx]
        lakv = [_dot(gm[j][0:LANES, LANES:2 * LANES], v_st[j]) for j in idx]
        mr = [gm[j][LANES:2 * LANES, :].astype(BF16) for j in idx]
        lp = [gm[j][0:LANES, 0:LANES] for j in idx]
        tinv = [eye_f + lp[j] for j in idx]
        for _ in range(5):
            lpb = [lp[j].astype(BF16) for j in idx]
            lp = [jnp.dot(lpb[j], lpb[j], preferred_element_type=F32) for j in idx]
            tinv = [tinv[j] + _dot(lp[j], tinv[j]) for j in idx]
        wu = [_dot(tinv[j], jnp.concatenate([a_st[j].astype(F32), lakv[j]], axis=1)) for j in idx]
        wm = [wu[j][:, 0:LANES].astype(BF16) for j in idx]
        u = [wu[j][:, LANES:2 * LANES].astype(BF16) for j in idx]
        mwbw = [jnp.dot(jnp.concatenate([mr[j][:, 0:LANES], tt[j][:, 0:LANES]], axis=0), wm[j],
                        preferred_element_type=F32) for j in idx]
        ub = [jnp.dot(jnp.concatenate([mr[j], tt[j]], axis=0), jnp.concatenate([u[j], v_st[j]], axis=0),
                      preferred_element_type=F32) for j in idx]
        for j, (c, p) in enumerate(items):
            n = i * RWKV_GROUP + c
            ro_ref[n, p] = jnp.concatenate([r_st[j] + mwbw[j][0:LANES], mwbw[j][LANES:2 * LANES]], axis=0).astype(BF16)
            ub_ref[n, p] = ub[j]
        return 0

    def phase2(n, _):
        t0 = pl.multiple_of(n * C, C)
        pairs = range(n_pairs)
        st = [st_ref[p] for p in pairs]
        res = [jnp.dot(ro_ref[n, p], st[p].astype(BF16), preferred_element_type=F32) for p in pairs]
        ub = [ub_ref[n, p] for p in pairs]
        for p in pairs:
            st_ref[p] = colb_ref[n, p] * st[p] + res[p][LANES:2 * LANES] + ub[p][LANES:2 * LANES]
        o_st = [res[p][0:LANES] + ub[p][0:LANES] for p in pairs]
        y_ref[pl.ds(t0, C), :] = jnp.concatenate([o[0:C] + o[C:2 * C] for o in o_st], axis=1)
        return 0

    lax.fori_loop(0, SB // R, phase1, 0)
    lax.fori_loop(0, SB // C, phase2, 0)
    prev_ref[...] = z_ref[SB - 1:SB, :]

    for t in range(SB // LANES):
        rows = slice(t * LANES, (t + 1) * LANES)
        o = y_ref[rows, :]
        dev = o - _dot_split(o, hb)
        var = _dot_split(dev * dev, hb)
        y_ref[rows, :] = dev * lax.rsqrt(var + RWKV_GN_EPS) * (lnw_ref[...] * g_ref[rows, :]) + t2_ref[rows, :]


def _cumsum_rows(tri_incl, x):
    hi = x.astype(BF16)
    lo = (x - hi.astype(F32)).astype(BF16)
    return jnp.dot(tri_incl, hi, preferred_element_type=F32) + jnp.dot(tri_incl, lo, preferred_element_type=F32)


RWKV_BLOCK = 512


def _rwkv(z, v_first, v_res, mu, w0, w2, a0, a2, g2, k_k, k_a, r_k, ln_w, ln_b):
    B, S, P = z.shape
    W = GROUP_WIDTH
    SB = RWKV_BLOCK
    ncb = SB // RWKV_CHUNK
    n_pairs = N_HEADS // 2
    first = v_res is None
    row = lambda v: v.reshape(1, W)
    padr = lambda m, n: jnp.pad(m, ((0, n - m.shape[0]), (0, 0)))
    seq = lambda w: pl.BlockSpec((None, SB, w), lambda b, j: (b, j, 0))
    full = lambda shape: pl.BlockSpec(shape, lambda b, j: (0,) * len(shape))
    args = [z]
    specs = [seq(P)]
    if not first:
        args.append(v_first)
        specs.append(seq(W))
    args += [_pack_rwkv_row(mu), row(w0), padr(w2, LANES), row(a0), padr(a2, LANES), padr(g2, 2 * LANES),
             row(k_k), row(k_a), r_k.reshape(1, W), row(ln_w), row(ln_b)]
    specs += [full((1, P)), full((1, W)), full((LANES, W)), full((1, W)), full((LANES, W)), full((2 * LANES, W)),
              full((1, W)), full((1, W)), full((1, W)), full((1, W)), full((1, W))]
    if not first:
        v0, v1, v2 = v_res
        args += [row(v0), jnp.pad(v1, ((0, 0), (0, LANES - v1.shape[1]))), padr(v2, LANES)]
        specs += [full((1, W)), full((W, LANES)), full((LANES, W))]
    args.append(_head_mean_matrix())
    specs.append(full((W, W)))
    out_shape = [jax.ShapeDtypeStruct((B, S, W), F32)]
    out_specs = [seq(W)]
    if first:
        out_shape.append(jax.ShapeDtypeStruct((B, S, W), F32))
        out_specs.append(seq(W))
    res = pl.pallas_call(
        functools.partial(_rwkv_kernel, first),
        grid=(B, S // SB),
        in_specs=specs,
        out_specs=out_specs,
        out_shape=out_shape,
        scratch_shapes=[pltpu.VMEM((n_pairs, LANES, LANES), F32), pltpu.VMEM((1, P), F32),
                        pltpu.VMEM((ncb, n_pairs, 2 * LANES, LANES), BF16),
                        pltpu.VMEM((ncb, n_pairs, 2 * LANES, LANES), F32),
                        pltpu.VMEM((ncb, n_pairs, LANES, LANES), F32),
                        pltpu.VMEM((SB, W), F32), pltpu.VMEM((SB, W), F32)],
        compiler_params=_params(("parallel", "arbitrary")),
        name="rwkv7_first" if first else "rwkv7",
    )(*args)
    return (res[0], res[1]) if first else (res[0], v_first)


def _out_kernel(x_ref, ya_ref, yb_ref, yc_ref, yd_ref, wo_ref, mod_ref, g_ref, rwh_ref, rwl_ref, rb_ref,
                xo_ref, h_ref, gates_ref):
    W = GROUP_WIDTH
    mix = (_dot(ya_ref[...], wo_ref[0:W, :]) + _dot(yb_ref[...], wo_ref[W:2 * W, :])
           + _dot(yc_ref[...], wo_ref[2 * W:3 * W, :]) + _dot(yd_ref[...], wo_ref[3 * W:4 * W, :]))
    x = x_ref[...] + mod_ref[2:3, :] * mix
    xo_ref[...] = x
    ms = jnp.mean(x * x, axis=-1, keepdims=True)
    h = x * lax.rsqrt(ms + RMS_EPS) * g_ref[...] * (1.0 + mod_ref[4:5, :]) + mod_ref[3:4, :]
    h_hi = h.astype(BF16)
    h_ref[...] = h_hi

    h_lo = (h - h_hi.astype(F32)).astype(BF16)
    logits = (jnp.dot(h_hi, rwh_ref[...], preferred_element_type=F32)
              + jnp.dot(h_lo, rwh_ref[...], preferred_element_type=F32)
              + jnp.dot(h_hi, rwl_ref[...], preferred_element_type=F32))
    aff = _sigmoid(logits)
    s = aff + rb_ref[...]
    lane = _iota(s.shape, 1)
    l4 = lane & 3
    grp = (lane & 15) >> 2

    def in_group(x, d):
        return jnp.where(l4 + d < 4, pltpu.roll(x, LANES - d, 1), pltpu.roll(x, 4 - d, 1))

    def across(x, d):
        return jnp.where(grp + d < 4, pltpu.roll(x, LANES - 4 * d, 1), pltpu.roll(x, 16 - 4 * d, 1))

    rank = jnp.zeros(s.shape, F32)
    for d in range(1, 4):
        o = in_group(s, d)
        rank = rank + jnp.where(l4 + d < 4, jnp.where(o > s, 1.0, 0.0), jnp.where(o >= s, 1.0, 0.0))
    top2 = rank < 2.0
    ts = jnp.where(top2, s, 0.0)
    gs = ts + in_group(ts, 1) + in_group(ts, 2) + in_group(ts, 3)
    lost = jnp.zeros(s.shape, F32)
    for d in range(1, 4):
        o = across(gs, d)
        lost = lost + jnp.where(grp + d < 4, jnp.where(o > gs, 1.0, 0.0), jnp.where(o >= gs, 1.0, 0.0))
    chosen = top2 & (lost < 0.5) & (lane < N_EXPERTS)
    a_sel = jnp.where(chosen, aff, 0.0)
    den = a_sel + in_group(a_sel, 1) + in_group(a_sel, 2) + in_group(a_sel, 3)
    gates_ref[...] = jnp.where(chosen, aff / jnp.where(chosen, den, 1.0), 0.0)


def _out_proj(x, ys, w_out, mod_l, g, router_w, router_b):
    B, S, D = x.shape
    tm = 512
    W = GROUP_WIDTH
    rw = jnp.pad(router_w, ((0, 0), (0, LANES - N_EXPERTS)))
    rw_hi = rw.astype(BF16)
    rw_lo = (rw - rw_hi.astype(F32)).astype(BF16)
    rb = jnp.pad(router_b, (0, LANES - N_EXPERTS)).reshape(1, LANES)
    tile = lambda w: pl.BlockSpec((None, tm, w), lambda b, i: (b, i, 0))
    full = lambda shape: pl.BlockSpec(shape, lambda b, i: (0,) * len(shape))
    return pl.pallas_call(
        _out_kernel,
        grid=(B, S // tm),
        in_specs=[tile(D), tile(W), tile(W), tile(W), tile(W), full((D, D)),
                  pl.BlockSpec((None, 6, D), lambda b, i: (b, 0, 0)), full((1, D)), full((D, LANES)), full((D, LANES)),
                  full((1, LANES))],
        out_specs=[tile(D), tile(D), tile(LANES)],
        out_shape=[jax.ShapeDtypeStruct((B, S, D), F32), jax.ShapeDtypeStruct((B, S, D), BF16),
                   jax.ShapeDtypeStruct((B, S, LANES), F32)],
        compiler_params=_params(("parallel", "parallel")),
        name="out_proj_router",
    )(x, *ys, w_out.astype(BF16), mod_l, g.reshape(1, D), rw_hi, rw_lo, rb)


def _moe_kernel(x_ref, h_ref, gates_ref, mod_ref, wgu_ref, wd_ref, o_ref):
    e = pl.program_id(2)

    @pl.when(e == 0)
    def _init():
        o_ref[...] = x_ref[...]

    gu = jnp.dot(h_ref[...], wgu_ref[...], preferred_element_type=F32)
    hg, hu = gu[:, 0:D_EXPERT], gu[:, D_EXPERT:2 * D_EXPERT]
    gates = gates_ref[...]
    ge = jnp.sum(jnp.where(_iota(gates.shape, 1) == e, gates, 0.0), axis=-1, keepdims=True)
    act = hg * _sigmoid(hg) * hu * ge
    o_ref[...] += mod_ref[5:6, :] * jnp.dot(act.astype(BF16), wd_ref[...], preferred_element_type=F32)


def _moe(x, h, gates, mod_l, w_gate, w_up, w_down):
    B, S, D = x.shape
    E = w_gate.shape[0]
    tm = 1024
    wgu = jnp.concatenate([w_gate, w_up], axis=2).astype(BF16)
    wd = w_down.astype(BF16)
    tile = lambda w: pl.BlockSpec((None, tm, w), lambda b, i, e: (b, i, 0))
    return pl.pallas_call(
        _moe_kernel,
        grid=(B, S // tm, E),
        in_specs=[tile(D), tile(D), tile(LANES),
                  pl.BlockSpec((None, 6, D), lambda b, i, e: (b, 0, 0)),
                  pl.BlockSpec((None, D, 2 * D_EXPERT), lambda b, i, e: (e, 0, 0)),
                  pl.BlockSpec((None, D_EXPERT, D), lambda b, i, e: (e, 0, 0))],
        out_specs=tile(D),
        out_shape=jax.ShapeDtypeStruct((B, S, D), F32),
        compiler_params=_params(("parallel", "parallel", "arbitrary")),
        name="moe",
    )(x, h, gates, mod_l, wgu, wd)


def kernel(x, c, positions, ada_w, ada_b, norm_mix_g, norm_ffn_g, w_in, w_out, nsa_q_norm, nsa_k_norm, nsa_cmp_pe, nsa_cmp_w1, nsa_cmp_w2, pool_w, pool_scale, rwkv_mu, rwkv_w0, rwkv_w2, rwkv_a0, rwkv_a2, rwkv_g2, rwkv_k_k, rwkv_k_a, rwkv_r_k, rwkv_ln_w, rwkv_ln_b, rwkv_v0, rwkv_v1, rwkv_v2, dil_q_norm, dil_k_norm, router_w, router_b, moe_w_gate, moe_w_up, moe_w_down):
    depth = ada_w.shape[0]
    mod = _modulation(c, ada_w, ada_b)
    cos, sin, ccos, csin = _rope_tables(positions)
    v_first = None
    for l in range(depth):
        q_a, kv_a, gl, u_pool, z_rwkv, q_d, kv_d = _in_proj(x, mod[l], norm_mix_g[l], _pack_w_in(w_in[l]))
        y_nsa = _nsa(q_a, kv_a, gl, cos, sin, ccos, csin, nsa_q_norm[l], nsa_k_norm[l],
                     nsa_cmp_pe[l], nsa_cmp_w1[l], nsa_cmp_w2[l])
        y_pool = _pool(u_pool, pool_w[l], pool_scale[l])
        v_res = None if l == 0 else (rwkv_v0[l - 1], rwkv_v1[l - 1], rwkv_v2[l - 1])
        y_rwkv, v_first = _rwkv(z_rwkv, v_first, v_res, rwkv_mu[l], rwkv_w0[l], rwkv_w2[l], rwkv_a0[l], rwkv_a2[l],
                                rwkv_g2[l], rwkv_k_k[l], rwkv_k_a[l], rwkv_r_k[l], rwkv_ln_w[l], rwkv_ln_b[l])
        y_dil = _dilated(q_d, kv_d, cos, sin, dil_q_norm[l], dil_k_norm[l])
        x_mid, h2, gates = _out_proj(x, (y_nsa, y_pool, y_rwkv, y_dil), w_out[l], mod[l], norm_ffn_g[l],
                                     router_w, router_b)
        x = _moe(x_mid, h2, gates, mod[l], moe_w_gate[l], moe_w_up[l], moe_w_down[l])
    return x
```

```python
import functools

import numpy as np
import jax
import jax.numpy as jnp
from jax import lax
from jax.experimental import pallas as pl
from jax.experimental.pallas import tpu as pltpu

F32 = jnp.float32
BF16 = jnp.bfloat16
HI = lax.Precision.HIGHEST

D_MODEL = 1024
HEAD_DIM = 64
GROUP_WIDTH = 256
N_HEADS = 4
RMS_EPS = 1e-6
ROPE_THETA = 500000.0
ROPE_DIM = 16
CMP_BLOCK = 32
CMP_STRIDE = 16
SEL_BLOCK = 64
SEL_TOP_N = 16
WINDOW = 512
FORCE_SCORE = 1e4
POOL_WINDOWS = (2, 4, 8, 16)
RWKV_GN_EPS = 64e-5
DIL_PATTERNS = ((128, 1), (512, 4), (2048, 16))
N_EXPERTS = 16
D_EXPERT = 256
NEG_INF = -1e30
LOG2E = 1.4426950408889634
TINY = 1e-30

LANES = 128
TQ = 128
TK = 512
RWKV_CHUNK = 64
RWKV_GROUP = 4
VMEM_LIMIT = 56 * 1024 * 1024

_IN_COLS = dict(q=(0, 256), kv=(256, 640), gl=(640, 768), pool=(768, 1024),
                rwkv=(1024, 2304), dq=(2304, 2560), dkv=(2560, 3072))
IN_PACKED = 3072
RWKV_PACKED = 1280


def _dot(a, b):
    return jnp.dot(a.astype(BF16), b.astype(BF16), preferred_element_type=F32)


def _dot_hi(a, b):
    return jnp.dot(a, b, precision=HI, preferred_element_type=F32)


def _dot_nt(a, b):
    return lax.dot_general(a, b, (((1,), (1,)), ((), ())), preferred_element_type=F32)


def _split(x):
    hi = x.astype(BF16)
    return hi, (x - hi.astype(F32)).astype(BF16)


def _dot_split(a, b):
    hi, lo = _split(a)
    b = b.astype(BF16)
    return jnp.dot(hi, b, preferred_element_type=F32) + jnp.dot(lo, b, preferred_element_type=F32)


def _dot3(a, b, nt=False):
    f = _dot_nt if nt else functools.partial(jnp.dot, preferred_element_type=F32)
    a_hi, a_lo = _split(a)
    b_hi, b_lo = _split(b)
    return f(a_hi, b_hi) + f(a_lo, b_hi) + f(a_hi, b_lo)


def _sigmoid(x):
    return 1.0 / (1.0 + jnp.exp(-x))


def _iota(shape, dim):
    return lax.broadcasted_iota(jnp.int32, shape, dim)


def _rope_lanes(y, cos, sin):
    n = y.shape[-1]
    lane = _iota(y.shape, y.ndim - 1)
    partner = jnp.where((lane & 15) < 8, pltpu.roll(y, n - 8, y.ndim - 1), pltpu.roll(y, 8, y.ndim - 1))
    return y * cos + partner * sin


def _params(sem, vmem=VMEM_LIMIT):
    return pltpu.CompilerParams(dimension_semantics=sem, vmem_limit_bytes=vmem)


def _mod_kernel(c_ref, w_ref, b_ref, o_ref):
    c = c_ref[...]
    o_ref[...] = _dot_hi(c * _sigmoid(c), w_ref[...]) + b_ref[...]


def _modulation(c, ada_w, ada_b):
    L, D, E = ada_w.shape
    B = c.shape[0]
    tn = 1024
    out = pl.pallas_call(
        _mod_kernel,
        grid=(L, E // tn),
        in_specs=[pl.BlockSpec((B, D), lambda l, j: (0, 0)),
                  pl.BlockSpec((None, D, tn), lambda l, j: (l, 0, j)),
                  pl.BlockSpec((None, 1, tn), lambda l, j: (l, 0, j))],
        out_specs=pl.BlockSpec((None, B, tn), lambda l, j: (l, 0, j)),
        out_shape=jax.ShapeDtypeStruct((L, B, E), F32),
        compiler_params=_params(("parallel", "parallel")),
        name="modulation",
    )(c, ada_w, ada_b.reshape(L, 1, E))
    return out.reshape(L, B, 6, D)


def _rope_kernel(pos_ref, cpos_ref, freq_ref, sgn_ref, cos_ref, sin_ref, ccos_ref, csin_ref):
    freq = freq_ref[...]
    sgn = sgn_ref[...]
    ang = pos_ref[...].astype(F32) * freq
    cos_ref[...] = jnp.cos(ang)
    sin_ref[...] = jnp.sin(ang) * sgn
    cang = cpos_ref[...].astype(F32) * freq
    ccos_ref[...] = jnp.cos(cang)
    csin_ref[...] = jnp.sin(cang) * sgn


def _rope_tables(positions):
    B, S = positions.shape
    half = ROPE_DIM // 2
    inv_freq = ROPE_THETA ** (-2.0 * jnp.arange(half, dtype=F32) / ROPE_DIM)
    lane = np.arange(LANES)
    rot = (lane % HEAD_DIM) < ROPE_DIM
    freq = jnp.where(rot, inv_freq[lane % half], 0.0).reshape(1, LANES)
    sgn = jnp.asarray(np.where(rot, np.where(lane % HEAD_DIM < half, -1.0, 1.0), 0.0), F32).reshape(1, LANES)
    n_cmp = S // CMP_STRIDE
    cpos = positions[:, CMP_BLOCK - 1::CMP_STRIDE]
    cpos = jnp.pad(cpos, ((0, 0), (0, n_cmp - cpos.shape[1])))
    tab = jax.ShapeDtypeStruct((B, S, LANES), F32)
    ctab = jax.ShapeDtypeStruct((B, n_cmp, LANES), F32)
    return pl.pallas_call(
        _rope_kernel,
        grid=(B,),
        in_specs=[pl.BlockSpec((None, S, 1), lambda b: (b, 0, 0)),
                  pl.BlockSpec((None, n_cmp, 1), lambda b: (b, 0, 0)),
                  pl.BlockSpec((1, LANES), lambda b: (0, 0)),
                  pl.BlockSpec((1, LANES), lambda b: (0, 0))],
        out_specs=[pl.BlockSpec((None, S, LANES), lambda b: (b, 0, 0)),
                   pl.BlockSpec((None, S, LANES), lambda b: (b, 0, 0)),
                   pl.BlockSpec((None, n_cmp, LANES), lambda b: (b, 0, 0)),
                   pl.BlockSpec((None, n_cmp, LANES), lambda b: (b, 0, 0))],
        out_shape=[tab, tab, ctab, ctab],
        compiler_params=_params(("parallel",)),
        name="rope_tables",
    )(positions.reshape(B, S, 1), cpos.reshape(B, n_cmp, 1), freq, sgn)


def _in_kernel(x_ref, mod_ref, g_ref, w_ref, *out_refs):
    x = x_ref[...]
    ms = jnp.mean(x * x, axis=-1, keepdims=True)
    y = x * lax.rsqrt(ms + RMS_EPS) * g_ref[...]
    h = y * (1.0 + mod_ref[1:2, :]) + mod_ref[0:1, :]
    p = _dot(h, w_ref[...])
    for ref, (lo, hi) in zip(out_refs, _IN_COLS.values()):
        ref[...] = p[:, lo:hi]


def _in_proj(x, mod_l, g, w_packed):
    B, S, D = x.shape
    tm = 512
    widths = [hi - lo for lo, hi in _IN_COLS.values()]
    return pl.pallas_call(
        _in_kernel,
        grid=(B, S // tm),
        in_specs=[pl.BlockSpec((None, tm, D), lambda b, i: (b, i, 0)),
                  pl.BlockSpec((None, 6, D), lambda b, i: (b, 0, 0)),
                  pl.BlockSpec((1, D), lambda b, i: (0, 0)),
                  pl.BlockSpec((D, IN_PACKED), lambda b, i: (0, 0))],
        out_specs=[pl.BlockSpec((None, tm, w), lambda b, i: (b, i, 0)) for w in widths],
        out_shape=[jax.ShapeDtypeStruct((B, S, w), F32) for w in widths],
        compiler_params=_params(("parallel", "parallel")),
        name="in_proj",
    )(x, mod_l, g.reshape(1, D), w_packed)


def _pack_w_in(w):
    D = w.shape[0]
    z = lambda n: jnp.zeros((D, n), w.dtype)
    parts = [w[:, 0:640], w[:, 640:652], z(116), w[:, 652:908],
             w[:, 908:1676], w[:, 1676:1740], z(64), w[:, 1740:1804], z(64), w[:, 1804:1964], z(96),
             w[:, 1964:2732]]
    return jnp.concatenate(parts, axis=1).astype(BF16)


def _pack_rwkv_row(v):
    z = lambda n: jnp.zeros((n,), v.dtype)
    return jnp.concatenate([v[0:768], v[768:832], z(64), v[832:896], z(64), v[896:1056], z(96)]).reshape(1, RWKV_PACKED)


def _pool_kernel(u_ref, w_ref, scale_ref, o_ref):
    u = u_ref[...]
    row = _iota(u.shape, 0)
    lane = _iota(u.shape, 1)

    def lag(x, s):
        return jnp.where(row >= s, pltpu.roll(x, s, 0), 0.0)

    s2 = u + lag(u, 1)
    s4 = s2 + lag(s2, 2)
    s8 = s4 + lag(s4, 4)
    s16 = s8 + lag(s8, 8)
    g = lane // (GROUP_WIDTH // len(POOL_WINDOWS))
    tot = jnp.where(g == 0, s2, jnp.where(g == 1, s4, jnp.where(g == 2, s8, s16)))
    win = jnp.where(g == 0, 2.0, jnp.where(g == 1, 4.0, jnp.where(g == 2, 8.0, 16.0)))
    cnt = jnp.minimum((row + 1).astype(F32), win)
    o_ref[...] = _dot(tot / cnt - u, w_ref[...]) * scale_ref[...]


def _pool(u, pool_w, pool_scale):
    B, S, W = u.shape
    wbd = jax.scipy.linalg.block_diag(*[pool_w[i] for i in range(pool_w.shape[0])]).astype(BF16)
    return pl.pallas_call(
        _pool_kernel,
        grid=(B,),
        in_specs=[pl.BlockSpec((None, S, W), lambda b: (b, 0, 0)),
                  pl.BlockSpec((W, W), lambda b: (0, 0)),
                  pl.BlockSpec((1, W), lambda b: (0, 0))],
        out_specs=pl.BlockSpec((None, S, W), lambda b: (b, 0, 0)),
        out_shape=jax.ShapeDtypeStruct((B, S, W), F32),
        compiler_params=_params(("parallel",)),
        name="pool_mixer",
    )(u, wbd, pool_scale.reshape(1, W))


def _head_mean_matrix():
    blk = np.kron(np.eye(N_HEADS), np.ones((HEAD_DIM, HEAD_DIM))) / HEAD_DIM
    return jnp.asarray(blk, F32)


def _pad_heads(x256, as_bf16=True):
    n = x256.shape[0]
    lane = _iota((n, LANES), 1)
    out = []
    for h in range(N_HEADS):
        blk = x256[:, (h // 2) * LANES:(h // 2 + 1) * LANES]
        if h % 2:
            blk = pltpu.roll(blk, HEAD_DIM, 1)
        blk = jnp.where(lane < HEAD_DIM, blk, 0.0)
        out.append(blk.astype(BF16) if as_bf16 else blk)
    return out


def _merge_heads(slabs):
    n = slabs[0].shape[0]
    lane = _iota((n, LANES), 1)
    cols = []
    for p in range(N_HEADS // 2):
        even = pltpu.roll(slabs[2 * p], HEAD_DIM, 1)
        cols.append(jnp.where(lane < HEAD_DIM, even, slabs[2 * p + 1]))
    return jnp.concatenate(cols, axis=1)


def _flash_step(carries, qs, kas, vbs, biases):
    idx = range(len(qs))
    s = [_dot_nt(qs[i], kas[i]) for i in idx]
    if biases is not None:
        s = [s[i] + biases[i] for i in idx]
    m_new = [jnp.maximum(carries[i][0], jnp.max(s[i], axis=-1, keepdims=True)) for i in idx]
    alpha = [jnp.exp2(carries[i][0] - m_new[i]) for i in idx]
    p = [jnp.exp2(s[i] - m_new[i]).astype(BF16) for i in idx]
    pv = [jnp.dot(p[i], vbs[i], preferred_element_type=F32) for i in idx]
    return tuple((m_new[i], alpha[i] * carries[i][1] + pv[i]) for i in idx)


def _flash_init(n):
    return (jnp.full((n, 1), NEG_INF, F32), jnp.zeros((n, LANES), F32))


def _flash_finish(carry):
    acc = carry[1]
    return acc / pltpu.roll(acc, HEAD_DIM, 1)


def _nsa_kernel(q_ref, gl_ref, kv_ref, zc_ref, cos_ref, sin_ref, ccos_ref, csin_ref,
                qn_ref, kn_ref, pet_ref, peb_ref, wt_ref, wb_ref, w2_ref,
                hb_ref, cover_ref, eg_ref, ctab_ref, wtab_ref,
                o_ref, ksvs_ref, kwvw_ref, vs1_ref, vw1_ref, kvc_ref):
    S = kv_ref.shape[0]
    qi = pl.program_id(1)
    n_cmp = zc_ref.shape[0]

    @pl.when(qi == 0)
    def _prep_keys():
        rows = 256

        def body(i, _):
            r0 = pl.multiple_of(i * rows, rows)
            cs = cos_ref[pl.ds(r0, rows), :]
            sn = sin_ref[pl.ds(r0, rows), :]
            lane = _iota((rows, LANES), 1)
            first = lane < HEAD_DIM
            blk_hot = jnp.where(lane - HEAD_DIM == (r0 + _iota((rows, LANES), 0)) // SEL_BLOCK, 1.0, 0.0)
            for col, j, dst, dst1 in ((128, 1, ksvs_ref, vs1_ref), (256, 2, kwvw_ref, vw1_ref)):
                slab = kv_ref[pl.ds(r0, rows), col:col + LANES]
                ms = jnp.sum(jnp.where(first, slab * slab, 0.0), axis=-1, keepdims=True) * (1.0 / HEAD_DIM)
                y = slab * lax.rsqrt(ms + RMS_EPS) * kn_ref[j:j + 1, :]
                dst[pl.ds(r0, rows), :] = jnp.where(first, _rope_lanes(y, cs, sn), blk_hot if j == 1 else slab).astype(BF16)
                dst1[pl.ds(r0, rows), :] = jnp.where(first, 1.0, slab).astype(BF16)
            return 0

        lax.fori_loop(0, S // rows, body, 0)

        zc = zc_ref[...]
        top = _dot3(zc + pet_ref[...], wt_ref[...])
        bot = _dot3(zc + peb_ref[...], wb_ref[...])
        pre = top + pltpu.roll(bot, n_cmp - 1, 0)
        act = 0.5 * pre * (1.0 + jnp.tanh(float(np.sqrt(2.0 / np.pi)) * (pre + 0.044715 * (pre * pre * pre))))
        out = _dot3(act, w2_ref[...])
        first = _iota(out.shape, 1) < HEAD_DIM
        ms = jnp.sum(jnp.where(first, out * out, 0.0), axis=-1, keepdims=True) * (1.0 / HEAD_DIM)
        y = out * lax.rsqrt(ms + RMS_EPS) * kn_ref[0:1, :]
        kvc_ref[...] = jnp.where(first, _rope_lanes(y, ccos_ref[...], csin_ref[...]), out)

    r0 = pl.multiple_of(qi * TQ, TQ)
    q = q_ref[...]
    ms = _dot_split(q * q, hb_ref[...])
    qn = q * lax.rsqrt(ms + RMS_EPS) * qn_ref[...]
    cs = cos_ref[pl.ds(r0, TQ), :]
    sn = sin_ref[pl.ds(r0, TQ), :]
    qr = _rope_lanes(qn, jnp.concatenate([cs, cs], axis=1), jnp.concatenate([sn, sn], axis=1)) * (HEAD_DIM ** -0.5)
    qst = jnp.concatenate(_pad_heads(qr, as_bf16=False), axis=0)
    n_rows = N_HEADS * TQ

    kvc = kvc_ref[...]
    s = _dot3(qst, kvc, nt=True)
    t_st = r0 + (_iota((n_rows, n_cmp), 0) & (TQ - 1))
    c_end = _iota((n_rows, n_cmp), 1) * CMP_STRIDE + (CMP_BLOCK - 1)
    mask = c_end <= t_st
    sm = jnp.where(mask, s, NEG_INF)
    m = jnp.max(sm, axis=-1, keepdims=True)
    e = jnp.where(mask, jnp.exp(sm - m), 0.0)
    p = e / jnp.maximum(jnp.sum(e, axis=-1, keepdims=True), TINY)
    o_cmp = _dot(p, kvc)

    psum = p[0:TQ] + p[TQ:2 * TQ] + p[2 * TQ:3 * TQ] + p[3 * TQ:4 * TQ]
    imp = _dot_split(psum, cover_ref[...])
    n_blk = S // SEL_BLOCK
    lane = _iota((TQ, LANES), 1)
    blk = lane & (n_blk - 1)
    tq = r0 + _iota((TQ, LANES), 0)
    cur = tq // SEL_BLOCK
    forced = (blk == 0) | (blk == cur) | (blk == cur - 1)
    imp = jnp.where(blk > cur, -1.0, jnp.where(forced, FORCE_SCORE, imp))
    rank = jnp.zeros((TQ, LANES), F32)
    for d in range(1, n_blk):
        other = pltpu.roll(imp, LANES - d, 1)
        rank = rank + jnp.where(blk + d >= n_blk,
                                jnp.where(other >= imp, 1.0, 0.0),
                                jnp.where(other > imp, 1.0, 0.0))
    sel_bias = jnp.where((lane >= HEAD_DIM) & (lane < HEAD_DIM + n_blk), jnp.where(rank < SEL_TOP_N, 0.0, NEG_INF), 0.0)
    sel_bias = jnp.concatenate([sel_bias, sel_bias], axis=0).astype(BF16)

    qb = (qst * LOG2E).astype(BF16)
    halves = [qb[0:n_rows // 2], qb[n_rows // 2:n_rows]]
    lane2 = _iota((n_rows // 2, LANES), 1)
    halves_sel = [jnp.where(lane2 < HEAD_DIM, h, sel_bias) for h in halves]

    def sel_tile(j, carry, biases):
        k0 = pl.multiple_of(j * TK, TK)
        ka, vb = ksvs_ref[pl.ds(k0, TK), :], vs1_ref[pl.ds(k0, TK), :]
        return _flash_step(carry, halves_sel, [ka, ka], [vb, vb], biases)

    ratio = TK // TQ
    last = qi // ratio
    res = lax.fori_loop(0, last, lambda j, c: sel_tile(j, c, None),
                        (_flash_init(n_rows // 2), _flash_init(n_rows // 2)))
    cb = ctab_ref[qi % ratio]
    cb = jnp.concatenate([cb, cb], axis=0)
    res = sel_tile(last, res, [cb, cb])
    o_sel = jnp.concatenate([_flash_finish(c) for c in res], axis=0)

    n_prev = WINDOW // TQ
    span = (n_prev + 1) * TQ
    k0 = pl.multiple_of(jnp.maximum(qi - n_prev, 0) * TQ, TQ)
    wb = wtab_ref[jnp.minimum(qi, n_prev)]
    wb = jnp.concatenate([wb, wb], axis=0)
    ka, vb = kwvw_ref[pl.ds(k0, span), :], vw1_ref[pl.ds(k0, span), :]
    res = _flash_step((_flash_init(n_rows // 2), _flash_init(n_rows // 2)), halves, [ka, ka], [vb, vb], [wb, wb])
    o_win = jnp.concatenate([_flash_finish(c) for c in res], axis=0)

    gate = _dot_split(_sigmoid(gl_ref[...]), eg_ref[...])
    split = lambda o: _merge_heads([o[h * TQ:(h + 1) * TQ] for h in range(N_HEADS)])
    W = GROUP_WIDTH
    o_ref[...] = (gate[:, 0:W] * split(o_cmp) + gate[:, W:2 * W] * split(o_sel) + gate[:, 2 * W:3 * W] * split(o_win))


def _nsa(q, kv, gl, cos, sin, ccos, csin, q_norm, k_norm, cmp_pe, cmp_w1, cmp_w2):
    B, S, _ = q.shape
    n_cmp = S // CMP_STRIDE
    n_blk = S // SEL_BLOCK
    hd = HEAD_DIM
    zc = kv[:, :, 0:LANES].reshape(B, n_cmp, CMP_STRIDE * LANES)

    qn = jnp.tile(q_norm, N_HEADS).reshape(1, GROUP_WIDTH)
    kn = jnp.concatenate([k_norm, jnp.ones_like(k_norm)], axis=1)

    def interleave(a, b):
        return jnp.concatenate([a, b], axis=1).reshape(1, -1)

    half = CMP_BLOCK // 2
    pet = interleave(cmp_pe[0, :half], cmp_pe[1, :half])
    peb = interleave(cmp_pe[0, half:], cmp_pe[1, half:])

    def w1_half(lo):
        wk = cmp_w1[0].reshape(CMP_BLOCK, hd, hd)[lo:lo + half]
        wv = cmp_w1[1].reshape(CMP_BLOCK, hd, hd)[lo:lo + half]
        z = jnp.zeros_like(wk)
        top = jnp.concatenate([wk, z], axis=2)
        bot = jnp.concatenate([z, wv], axis=2)
        return jnp.concatenate([top, bot], axis=1).reshape(half * 2 * hd, 2 * hd)

    wt, wb = w1_half(0), w1_half(half)
    w2 = jax.scipy.linalg.block_diag(cmp_w2[0], cmp_w2[1])
    hb = _head_mean_matrix()

    c_end = np.arange(n_cmp) * CMP_STRIDE + CMP_BLOCK - 1
    b_start = np.arange(n_blk) * SEL_BLOCK
    cover = np.maximum(np.minimum(c_end[:, None] + 1, b_start[None, :] + SEL_BLOCK)
                       - np.maximum(c_end[:, None] + 1 - CMP_BLOCK, b_start[None, :]), 0).astype(np.float32) / CMP_BLOCK
    cover[n_cmp - 1] = 0.0
    cover = jnp.asarray(np.tile(cover, (1, LANES // n_blk)), F32)
    eg = np.zeros((LANES, 3 * GROUP_WIDTH), np.float32)
    for h in range(N_HEADS):
        for br in range(3):
            eg[h * 3 + br, br * GROUP_WIDTH + h * hd: br * GROUP_WIDTH + (h + 1) * hd] = 1.0
    eg = jnp.asarray(eg)
    ratio = TK // TQ
    ctab = np.where(np.arange(TK)[None, None, :] <= np.arange(ratio)[:, None, None] * TQ + np.arange(TQ)[None, :, None],
                    0.0, NEG_INF).astype(np.float32)
    n_prev = WINDOW // TQ
    span = (n_prev + 1) * TQ
    dist = (np.arange(n_prev + 1)[:, None, None] * TQ + np.arange(TQ)[None, :, None] - np.arange(span)[None, None, :])
    wtab = np.where((dist >= 0) & (dist < WINDOW), 0.0, NEG_INF).astype(np.float32)
    ctab, wtab = jnp.asarray(ctab), jnp.asarray(wtab)

    full = lambda shape: pl.BlockSpec(shape, lambda b, i: (0,) * len(shape))
    return pl.pallas_call(
        _nsa_kernel,
        grid=(B, S // TQ),
        in_specs=[pl.BlockSpec((None, TQ, GROUP_WIDTH), lambda b, i: (b, i, 0)),
                  pl.BlockSpec((None, TQ, LANES), lambda b, i: (b, i, 0)),
                  pl.BlockSpec((None, S, 3 * LANES), lambda b, i: (b, 0, 0)),
                  pl.BlockSpec((None, n_cmp, CMP_STRIDE * LANES), lambda b, i: (b, 0, 0)),
                  pl.BlockSpec((None, S, LANES), lambda b, i: (b, 0, 0)),
                  pl.BlockSpec((None, S, LANES), lambda b, i: (b, 0, 0)),
                  pl.BlockSpec((None, n_cmp, LANES), lambda b, i: (b, 0, 0)),
                  pl.BlockSpec((None, n_cmp, LANES), lambda b, i: (b, 0, 0)),
                  full((1, GROUP_WIDTH)), full((3, LANES)),
                  full((1, CMP_STRIDE * LANES)), full((1, CMP_STRIDE * LANES)),
                  full((CMP_STRIDE * LANES, LANES)), full((CMP_STRIDE * LANES, LANES)), full((LANES, LANES)),
                  full((GROUP_WIDTH, GROUP_WIDTH)), full((n_cmp, LANES)),
                  full((LANES, 3 * GROUP_WIDTH)), full(ctab.shape), full(wtab.shape)],
        out_specs=pl.BlockSpec((None, TQ, GROUP_WIDTH), lambda b, i: (b, i, 0)),
        out_shape=jax.ShapeDtypeStruct((B, S, GROUP_WIDTH), F32),
        scratch_shapes=[pltpu.VMEM((S, LANES), BF16), pltpu.VMEM((S, LANES), BF16),
                        pltpu.VMEM((S, LANES), BF16), pltpu.VMEM((S, LANES), BF16),
                        pltpu.VMEM((n_cmp, LANES), F32)],
        compiler_params=_params(("parallel", "arbitrary")),
        name="nsa",
    )(q, gl, kv, zc, cos, sin, ccos, csin, qn, kn, pet, peb, wt, wb, w2, hb, cover, eg, ctab, wtab)


def _dil_kernel(q_ref, kv_ref, cos_ref, sin_ref, qn_ref, kn_ref, hb_ref, tb_ref, o_ref, kvs_ref, v1s_ref):
    S = kv_ref.shape[0]
    qi = pl.program_id(1)
    W = GROUP_WIDTH

    @pl.when(qi == 0)
    def _prep_keys():
        rows = 256

        def body(i, _):
            r0 = pl.multiple_of(i * rows, rows)
            cs = cos_ref[pl.ds(r0, rows), :]
            sn = sin_ref[pl.ds(r0, rows), :]
            k = kv_ref[pl.ds(r0, rows), 0:W]
            v = kv_ref[pl.ds(r0, rows), W:2 * W]
            ms = _dot_split(k * k, hb_ref[...])
            kn = k * lax.rsqrt(ms + RMS_EPS) * kn_ref[...]
            kr = _rope_lanes(kn, jnp.concatenate([cs, cs], axis=1), jnp.concatenate([sn, sn], axis=1))
            lane = _iota((rows, LANES), 1)
            for h in range(N_HEADS):
                kb = kr[:, (h // 2) * LANES:(h // 2 + 1) * LANES]
                vb = v[:, (h // 2) * LANES:(h // 2 + 1) * LANES]
                if h % 2:
                    kb = pltpu.roll(kb, HEAD_DIM, 1)
                else:
                    vb = pltpu.roll(vb, HEAD_DIM, 1)
                kvs_ref[h, pl.ds(r0, rows), :] = jnp.where(lane < HEAD_DIM, kb, vb).astype(BF16)
                v1s_ref[h, pl.ds(r0, rows), :] = jnp.where(lane < HEAD_DIM, 1.0, vb).astype(BF16)
            return 0

        lax.fori_loop(0, S // rows, body, 0)

    r0 = pl.multiple_of(qi * TQ, TQ)
    q = q_ref[...]
    ms = _dot_split(q * q, hb_ref[...])
    qn = q * lax.rsqrt(ms + RMS_EPS) * qn_ref[...]
    cs = cos_ref[pl.ds(r0, TQ), :]
    sn = sin_ref[pl.ds(r0, TQ), :]
    qr = _rope_lanes(qn, jnp.concatenate([cs, cs], axis=1), jnp.concatenate([sn, sn], axis=1)) * (HEAD_DIM ** -0.5 * LOG2E)
    qh = _pad_heads(qr)
    ratio = TK // TQ

    def body(j, carry):
        k0 = pl.multiple_of(j * TK, TK)
        bias = tb_ref[qi - ratio * j]
        return _flash_step(carry, qh, [kvs_ref[h, pl.ds(k0, TK), :] for h in range(N_HEADS)],
                           [v1s_ref[h, pl.ds(k0, TK), :] for h in range(N_HEADS)], [bias] * N_HEADS)

    res = lax.fori_loop(0, qi // ratio + 1, body, tuple(_flash_init(TQ) for _ in range(N_HEADS)))
    o_ref[...] = _merge_heads([_flash_finish(c) for c in res])


def _dil_bias_table(S):
    n = S // TQ
    d = (np.arange(n)[:, None, None] * TQ + np.arange(TQ)[None, :, None] - np.arange(TK)[None, None, :])
    cnt = np.zeros(d.shape, np.float32)
    for window, dil in DIL_PATTERNS:
        cnt += ((d >= 0) & (d <= window) & (d % dil == 0)).astype(np.float32)
    return jnp.asarray(np.where(cnt > 0, np.log2(np.maximum(cnt, 1.0)), NEG_INF).astype(np.float32))


def _dilated(q, kv, cos, sin, q_norm, k_norm):
    B, S, W = q.shape
    tb = _dil_bias_table(S)
    qn = jnp.tile(q_norm, N_HEADS).reshape(1, W)
    kn = jnp.tile(k_norm, N_HEADS).reshape(1, W)
    hb = _head_mean_matrix()
    full = lambda shape: pl.BlockSpec(shape, lambda b, i: (0,) * len(shape))
    return pl.pallas_call(
        _dil_kernel,
        grid=(B, S // TQ),
        in_specs=[pl.BlockSpec((None, TQ, W), lambda b, i: (b, i, 0)),
                  pl.BlockSpec((None, S, 2 * W), lambda b, i: (b, 0, 0)),
                  pl.BlockSpec((None, S, LANES), lambda b, i: (b, 0, 0)),
                  pl.BlockSpec((None, S, LANES), lambda b, i: (b, 0, 0)),
                  full((1, W)), full((1, W)), full((W, W)), full(tb.shape)],
        out_specs=pl.BlockSpec((None, TQ, W), lambda b, i: (b, i, 0)),
        out_shape=jax.ShapeDtypeStruct((B, S, W), F32),
        scratch_shapes=[pltpu.VMEM((N_HEADS, S, LANES), BF16), pltpu.VMEM((N_HEADS, S, LANES), BF16)],
        compiler_params=_params(("parallel", "arbitrary")),
        name="dilated",
    )(q, kv, cos, sin, qn, kn, hb, tb)


def _softplus(x):
    return jnp.maximum(x, 0.0) + jnp.log(1.0 + jnp.exp(-jnp.abs(x)))


def _rwkv_kernel(first, *refs):
    if first:
        (z_ref, mu_ref, w0_ref, w2_ref, a0_ref, a2_ref, g2_ref, kk_ref, ka_ref, rk_ref, lnw_ref, lnb_ref,
         hb_ref, y_ref, vf_out_ref, st_ref, prev_ref, ro_ref, ub_ref, colb_ref, g_ref, t2_ref) = refs
    else:
        (z_ref, vf_ref, mu_ref, w0_ref, w2_ref, a0_ref, a2_ref, g2_ref, kk_ref, ka_ref, rk_ref, lnw_ref, lnb_ref,
         v0_ref, v1_ref, v2_ref, hb_ref, y_ref, st_ref, prev_ref, ro_ref, ub_ref, colb_ref, g_ref, t2_ref) = refs
    SB = z_ref.shape[0]
    C = RWKV_CHUNK
    W = GROUP_WIDTH
    n_pairs = N_HEADS // 2
    R = RWKV_GROUP * C

    @pl.when(pl.program_id(1) == 0)
    def _reset():
        st_ref[...] = jnp.zeros_like(st_ref)
        prev_ref[...] = jnp.zeros_like(prev_ref)

    ri = _iota((C, C), 0)
    ci = _iota((C, C), 1)
    tri_incl = jnp.where(ci <= ri, 1.0, 0.0).astype(BF16)
    rr = _iota((2 * LANES, 2 * LANES), 0)
    cc = _iota((2 * LANES, 2 * LANES), 1)
    g_mask = ((((rr >> 6) & 1) == ((cc >> 6) & 1))
              & ((cc & (C - 1)) <= jnp.where(rr < LANES, (rr & (C - 1)) - 1, rr & (C - 1))))
    r2 = _iota((LANES, LANES), 0)
    c2 = _iota((LANES, LANES), 1)
    eye_f = jnp.where(r2 == c2, 1.0, 0.0)
    eye_b = eye_f.astype(BF16)
    lane_lo = _iota((C, LANES), 1) < HEAD_DIM
    hb = hb_ref[...].astype(BF16)

    def stack(x):
        return jnp.concatenate([jnp.where(lane_lo, x, 0.0), jnp.where(lane_lo, 0.0, x)], axis=0)

    def phase1(i, _):
        t0 = pl.multiple_of(i * R, R)
        zc = z_ref[pl.ds(t0, R), :]
        inside = jnp.where(i > 0, 1.0, 0.0)
        prev = inside * z_ref[pl.ds(jnp.maximum(t0 - 1, 0), 1), :] + (1.0 - inside) * prev_ref[...]
        zp = jnp.where(_iota(zc.shape, 0) == 0, prev, pltpu.roll(zc, 1, 0))
        zs = zc + (zp - zc) * mu_ref[...]
        r, k, v = zs[:, 0:W], zs[:, W:2 * W], zs[:, 2 * W:3 * W]
        xw, xa, xg = zs[:, 768:896], zs[:, 896:1024], zs[:, 1024:1280]
        w_log = -_softplus(-(w0_ref[...] + _dot(jnp.tanh(xw), w2_ref[...]))) - 0.5
        lw = -jnp.exp(w_log)
        a = _sigmoid(a0_ref[...] + _dot(xa, a2_ref[...]))
        g = _dot(_sigmoid(xg), g2_ref[...])
        if first:
            vf_out_ref[pl.ds(t0, R), :] = v
        else:
            vf = vf_ref[pl.ds(t0, R), :]
            v = v + (vf - v) * _sigmoid(v0_ref[...] + _dot(_dot(v, v1_ref[...]), v2_ref[...]))
        kkr = k * kk_ref[...]
        nrm = jnp.sqrt(_dot_split(kkr * kkr, hb) * HEAD_DIM)
        kk = kkr / jnp.maximum(nrm, 1e-12)
        k2 = k * (1.0 + (a - 1.0) * ka_ref[...])
        bonus = _dot_split(r * k2 * rk_ref[...], hb) * HEAD_DIM * v
        g_ref[pl.ds(t0, R), :] = g
        t2_ref[pl.ds(t0, R), :] = (lnb_ref[...] + bonus) * g

        items = [(c, p) for c in range(RWKV_GROUP) for p in range(n_pairs)]
        idx = range(len(items))
        sub = lambda x, c, p: x[c * C:(c + 1) * C, p * LANES:(p + 1) * LANES]
        cum = [_cumsum_rows(tri_incl, sub(lw, c, p)) for c, p in items]
        a_st, r_st, v_st, gm, tails = [], [], [], [], []
        for j, (c, p) in enumerate(items):
            lwp = sub(lw, c, p)
            cum_end = cum[j][C - 1:C, :]
            e_pos, e_neg = jnp.exp(cum[j]), jnp.exp(-cum[j])
            e_exc, e_tail = jnp.exp(cum[j] - lwp), jnp.exp(cum_end - cum[j])
            kkp, k2p = sub(kk, c, p), sub(k2, c, p)
            ka_p = kkp * sub(a, c, p)
            a_st.append(stack(-kkp * e_exc).astype(BF16))
            r_st.append(stack(sub(r, c, p) * e_pos))
            v_st.append(stack(sub(v, c, p)).astype(BF16))
            bhat = (ka_p * e_neg).astype(BF16)
            khat = (k2p * e_neg).astype(BF16)
            tails.append(jnp.concatenate([stack(ka_p * e_tail), stack(k2p * e_tail)], axis=0).astype(BF16))
            gm.append(jnp.where(g_mask,
                                _dot_nt(jnp.concatenate([a_st[j], r_st[j].astype(BF16)], axis=0),
                                        jnp.concatenate([bhat, bhat, khat, khat], axis=0)), 0.0))
            decay_end = jnp.broadcast_to(jnp.exp(cum_end), (LANES, LANES))
            hi = decay_end.astype(BF16)
            lo = (decay_end - hi.astype(F32)).astype(BF16)
            colb_ref[i * RWKV_GROUP + c, p] = _dot_nt(eye_b, hi) + _dot_nt(eye_b, lo)
        tt = [_dot_nt(eye_b, tails[j]).astype(BF16) for j in idx]
        lakv = [_dot(gm[j][0:LANES, LANES:2 * LANES], v_st[j]) for j in idx]
        mr = [gm[j][LANES:2 * LANES, :].astype(BF16) for j in idx]
        lp = [gm[j][0:LANES, 0:LANES] for j in idx]
        tinv = [eye_f + lp[j] for j in idx]
        for _ in range(5):
            lpb = [lp[j].astype(BF16) for j in idx]
            lp = [jnp.dot(lpb[j], lpb[j], preferred_element_type=F32) for j in idx]
            tinv = [tinv[j] + _dot(lp[j], tinv[j]) for j in idx]
        wu = [_dot(tinv[j], jnp.concatenate([a_st[j].astype(F32), lakv[j]], axis=1)) for j in idx]
        wm = [wu[j][:, 0:LANES].astype(BF16) for j in idx]
        u = [wu[j][:, LANES:2 * LANES].astype(BF16) for j in idx]
        mwbw = [jnp.dot(jnp.concatenate([mr[j][:, 0:LANES], tt[j][:, 0:LANES]], axis=0), wm[j],
                        preferred_element_type=F32) for j in idx]
        ub = [jnp.dot(jnp.concatenate([mr[j], tt[j]], axis=0), jnp.concatenate([u[j], v_st[j]], axis=0),
                      preferred_element_type=F32) for j in idx]
        for j, (c, p) in enumerate(items):
            n = i * RWKV_GROUP + c
            ro_ref[n, p] = jnp.concatenate([r_st[j] + mwbw[j][0:LANES], mwbw[j][LANES:2 * LANES]], axis=0).astype(BF16)
            ub_ref[n, p] = ub[j]
        return 0

    def phase2(n, _):
        t0 = pl.multiple_of(n * C, C)
        pairs = range(n_pairs)
        st = [st_ref[p] for p in pairs]
        res = [jnp.dot(ro_ref[n, p], st[p].astype(BF16), preferred_element_type=F32) for p in pairs]
        ub = [ub_ref[n, p] for p in pairs]
        for p in pairs:
            st_ref[p] = colb_ref[n, p] * st[p] + res[p][LANES:2 * LANES] + ub[p][LANES:2 * LANES]
        o_st = [res[p][0:LANES] + ub[p][0:LANES] for p in pairs]
        y_ref[pl.ds(t0, C), :] = jnp.concatenate([o[0:C] + o[C:2 * C] for o in o_st], axis=1)
        return 0

    lax.fori_loop(0, SB // R, phase1, 0)
    lax.fori_loop(0, SB // C, phase2, 0)
    prev_ref[...] = z_ref[SB - 1:SB, :]

    for t in range(SB // LANES):
        rows = slice(t * LANES, (t + 1) * LANES)
        o = y_ref[rows, :]
        dev = o - _dot_split(o, hb)
        var = _dot_split(dev * dev, hb)
        y_ref[rows, :] = dev * lax.rsqrt(var + RWKV_GN_EPS) * (lnw_ref[...] * g_ref[rows, :]) + t2_ref[rows, :]


def _cumsum_rows(tri_incl, x):
    hi = x.astype(BF16)
    lo = (x - hi.astype(F32)).astype(BF16)
    return jnp.dot(tri_incl, hi, preferred_element_type=F32) + jnp.dot(tri_incl, lo, preferred_element_type=F32)


RWKV_BLOCK = 512


def _rwkv(z, v_first, v_res, mu, w0, w2, a0, a2, g2, k_k, k_a, r_k, ln_w, ln_b):
    B, S, P = z.shape
    W = GROUP_WIDTH
    SB = RWKV_BLOCK
    ncb = SB // RWKV_CHUNK
    n_pairs = N_HEADS // 2
    first = v_res is None
    row = lambda v: v.reshape(1, W)
    padr = lambda m, n: jnp.pad(m, ((0, n - m.shape[0]), (0, 0)))
    seq = lambda w: pl.BlockSpec((None, SB, w), lambda b, j: (b, j, 0))
    full = lambda shape: pl.BlockSpec(shape, lambda b, j: (0,) * len(shape))
    args = [z]
    specs = [seq(P)]
    if not first:
        args.append(v_first)
        specs.append(seq(W))
    args += [_pack_rwkv_row(mu), row(w0), padr(w2, LANES), row(a0), padr(a2, LANES), padr(g2, 2 * LANES),
             row(k_k), row(k_a), r_k.reshape(1, W), row(ln_w), row(ln_b)]
    specs += [full((1, P)), full((1, W)), full((LANES, W)), full((1, W)), full((LANES, W)), full((2 * LANES, W)),
              full((1, W)), full((1, W)), full((1, W)), full((1, W)), full((1, W))]
    if not first:
        v0, v1, v2 = v_res
        args += [row(v0), jnp.pad(v1, ((0, 0), (0, LANES - v1.shape[1]))), padr(v2, LANES)]
        specs += [full((1, W)), full((W, LANES)), full((LANES, W))]
    args.append(_head_mean_matrix())
    specs.append(full((W, W)))
    out_shape = [jax.ShapeDtypeStruct((B, S, W), F32)]
    out_specs = [seq(W)]
    if first:
        out_shape.append(jax.ShapeDtypeStruct((B, S, W), F32))
        out_specs.append(seq(W))
    res = pl.pallas_call(
        functools.partial(_rwkv_kernel, first),
        grid=(B, S // SB),
        in_specs=specs,
        out_specs=out_specs,
        out_shape=out_shape,
        scratch_shapes=[pltpu.VMEM((n_pairs, LANES, LANES), F32), pltpu.VMEM((1, P), F32),
                        pltpu.VMEM((ncb, n_pairs, 2 * LANES, LANES), BF16),
                        pltpu.VMEM((ncb, n_pairs, 2 * LANES, LANES), F32),
                        pltpu.VMEM((ncb, n_pairs, LANES, LANES), F32),
                        pltpu.VMEM((SB, W), F32), pltpu.VMEM((SB, W), F32)],
        compiler_params=_params(("parallel", "arbitrary")),
        name="rwkv7_first" if first else "rwkv7",
    )(*args)
    return (res[0], res[1]) if first else (res[0], v_first)


def _out_kernel(x_ref, ya_ref, yb_ref, yc_ref, yd_ref, wo_ref, mod_ref, g_ref, rwh_ref, rwl_ref, rb_ref,
                xo_ref, h_ref, gates_ref):
    W = GROUP_WIDTH
    y = jnp.concatenate([ya_ref[...].astype(BF16), yb_ref[...].astype(BF16),
                         yc_ref[...].astype(BF16), yd_ref[...].astype(BF16)], axis=1)
    mix = jnp.dot(y, wo_ref[...], preferred_element_type=F32)
    x = x_ref[...] + mod_ref[2:3, :] * mix
    xo_ref[...] = x
    ms = jnp.mean(x * x, axis=-1, keepdims=True)
    h = x * lax.rsqrt(ms + RMS_EPS) * g_ref[...] * (1.0 + mod_ref[4:5, :]) + mod_ref[3:4, :]
    h_hi = h.astype(BF16)
    h_ref[...] = h_hi

    h_lo = (h - h_hi.astype(F32)).astype(BF16)
    logits = (jnp.dot(h_hi, rwh_ref[...], preferred_element_type=F32)
              + jnp.dot(h_lo, rwh_ref[...], preferred_element_type=F32)
              + jnp.dot(h_hi, rwl_ref[...], preferred_element_type=F32))
    aff = _sigmoid(logits)
    s = aff + rb_ref[...]
    lane = _iota(s.shape, 1)
    l4 = lane & 3
    grp = (lane & 15) >> 2

    def in_group(x, d):
        return jnp.where(l4 + d < 4, pltpu.roll(x, LANES - d, 1), pltpu.roll(x, 4 - d, 1))

    def across(x, d):
        return jnp.where(grp + d < 4, pltpu.roll(x, LANES - 4 * d, 1), pltpu.roll(x, 16 - 4 * d, 1))

    rank = jnp.zeros(s.shape, F32)
    for d in range(1, 4):
        o = in_group(s, d)
        rank = rank + jnp.where(l4 + d < 4, jnp.where(o > s, 1.0, 0.0), jnp.where(o >= s, 1.0, 0.0))
    top2 = rank < 2.0
    ts = jnp.where(top2, s, 0.0)
    gs = ts + in_group(ts, 1) + in_group(ts, 2) + in_group(ts, 3)
    lost = jnp.zeros(s.shape, F32)
    for d in range(1, 4):
        o = across(gs, d)
        lost = lost + jnp.where(grp + d < 4, jnp.where(o > gs, 1.0, 0.0), jnp.where(o >= gs, 1.0, 0.0))
    chosen = top2 & (lost < 0.5) & (lane < N_EXPERTS)
    a_sel = jnp.where(chosen, aff, 0.0)
    den = a_sel + in_group(a_sel, 1) + in_group(a_sel, 2) + in_group(a_sel, 3)
    gates_ref[...] = jnp.where(chosen, aff / jnp.where(chosen, den, 1.0), 0.0)


def _out_proj(x, ys, w_out, mod_l, g, router_w, router_b):
    B, S, D = x.shape
    tm = 512
    W = GROUP_WIDTH
    rw = jnp.pad(router_w, ((0, 0), (0, LANES - N_EXPERTS)))
    rw_hi = rw.astype(BF16)
    rw_lo = (rw - rw_hi.astype(F32)).astype(BF16)
    rb = jnp.pad(router_b, (0, LANES - N_EXPERTS)).reshape(1, LANES)
    tile = lambda w: pl.BlockSpec((None, tm, w), lambda b, i: (b, i, 0))
    full = lambda shape: pl.BlockSpec(shape, lambda b, i: (0,) * len(shape))
    return pl.pallas_call(
        _out_kernel,
        grid=(B, S // tm),
        in_specs=[tile(D), tile(W), tile(W), tile(W), tile(W), full((D, D)),
                  pl.BlockSpec((None, 6, D), lambda b, i: (b, 0, 0)), full((1, D)), full((D, LANES)), full((D, LANES)),
                  full((1, LANES))],
        out_specs=[tile(D), tile(D), tile(LANES)],
        out_shape=[jax.ShapeDtypeStruct((B, S, D), F32), jax.ShapeDtypeStruct((B, S, D), BF16),
                   jax.ShapeDtypeStruct((B, S, LANES), F32)],
        compiler_params=_params(("parallel", "parallel")),
        name="out_proj_router",
    )(x, *ys, w_out.astype(BF16), mod_l, g.reshape(1, D), rw_hi, rw_lo, rb)


MOE_EXPERTS_PER_STEP = 4


def _moe_kernel(x_ref, h_ref, gates_ref, mod_ref, wgu_ref, wd_ref, o_ref):
    step = pl.program_id(2)
    n = MOE_EXPERTS_PER_STEP

    @pl.when(step == 0)
    def _init():
        o_ref[...] = x_ref[...]

    h = h_ref[...]
    gates = gates_ref[...]
    lane = _iota(gates.shape, 1)
    gu = [jnp.dot(h, wgu_ref[k], preferred_element_type=F32) for k in range(n)]
    acts = []
    for k in range(n):
        ge = jnp.sum(jnp.where(lane == step * n + k, gates, 0.0), axis=-1, keepdims=True)
        hg, hu = gu[k][:, 0:D_EXPERT], gu[k][:, D_EXPERT:2 * D_EXPERT]
        acts.append((hg * _sigmoid(hg) * hu * ge).astype(BF16))
    act = jnp.concatenate(acts, axis=1)
    wd = wd_ref[...].reshape(n * D_EXPERT, wd_ref.shape[2])
    o_ref[...] += mod_ref[5:6, :] * jnp.dot(act, wd, preferred_element_type=F32)


def _moe(x, h, gates, mod_l, w_gate, w_up, w_down):
    B, S, D = x.shape
    E = w_gate.shape[0]
    n = MOE_EXPERTS_PER_STEP
    tm = 1024
    wgu = jnp.concatenate([w_gate, w_up], axis=2).astype(BF16)
    wd = w_down.astype(BF16)
    tile = lambda w: pl.BlockSpec((None, tm, w), lambda b, i, e: (b, i, 0))
    return pl.pallas_call(
        _moe_kernel,
        grid=(B, S // tm, E // n),
        in_specs=[tile(D), tile(D), tile(LANES),
                  pl.BlockSpec((None, 6, D), lambda b, i, e: (b, 0, 0)),
                  pl.BlockSpec((n, D, 2 * D_EXPERT), lambda b, i, e: (e, 0, 0)),
                  pl.BlockSpec((n, D_EXPERT, D), lambda b, i, e: (e, 0, 0))],
        out_specs=tile(D),
        out_shape=jax.ShapeDtypeStruct((B, S, D), F32),
        compiler_params=_params(("parallel", "parallel", "arbitrary")),
        name="moe",
    )(x, h, gates, mod_l, wgu, wd)


def kernel(x, c, positions, ada_w, ada_b, norm_mix_g, norm_ffn_g, w_in, w_out, nsa_q_norm, nsa_k_norm, nsa_cmp_pe, nsa_cmp_w1, nsa_cmp_w2, pool_w, pool_scale, rwkv_mu, rwkv_w0, rwkv_w2, rwkv_a0, rwkv_a2, rwkv_g2, rwkv_k_k, rwkv_k_a, rwkv_r_k, rwkv_ln_w, rwkv_ln_b, rwkv_v0, rwkv_v1, rwkv_v2, dil_q_norm, dil_k_norm, router_w, router_b, moe_w_gate, moe_w_up, moe_w_down):
    depth = ada_w.shape[0]
    mod = _modulation(c, ada_w, ada_b)
    cos, sin, ccos, csin = _rope_tables(positions)
    v_first = None
    for l in range(depth):
        q_a, kv_a, gl, u_pool, z_rwkv, q_d, kv_d = _in_proj(x, mod[l], norm_mix_g[l], _pack_w_in(w_in[l]))
        y_nsa = _nsa(q_a, kv_a, gl, cos, sin, ccos, csin, nsa_q_norm[l], nsa_k_norm[l],
                     nsa_cmp_pe[l], nsa_cmp_w1[l], nsa_cmp_w2[l])
        y_pool = _pool(u_pool, pool_w[l], pool_scale[l])
        v_res = None if l == 0 else (rwkv_v0[l - 1], rwkv_v1[l - 1], rwkv_v2[l - 1])
        y_rwkv, v_first = _rwkv(z_rwkv, v_first, v_res, rwkv_mu[l], rwkv_w0[l], rwkv_w2[l], rwkv_a0[l], rwkv_a2[l],
                                rwkv_g2[l], rwkv_k_k[l], rwkv_k_a[l], rwkv_r_k[l], rwkv_ln_w[l], rwkv_ln_b[l])
        y_dil = _dilated(q_d, kv_d, cos, sin, dil_q_norm[l], dil_k_norm[l])
        x_mid, h2, gates = _out_proj(x, (y_nsa, y_pool, y_rwkv, y_dil), w_out[l], mod[l], norm_ffn_g[l],
                                     router_w, router_b)
        x = _moe(x_mid, h2, gates, mod[l], moe_w_gate[l], moe_w_up[l], moe_w_down[l])
    return x
```

```python
import functools

import numpy as np
import jax
import jax.numpy as jnp
from jax import lax
from jax.experimental import pallas as pl
from jax.experimental.pallas import tpu as pltpu

F32 = jnp.float32
BF16 = jnp.bfloat16
HI = lax.Precision.HIGHEST

D_MODEL = 1024
HEAD_DIM = 64
GROUP_WIDTH = 256
N_HEADS = 4
RMS_EPS = 1e-6
ROPE_THETA = 500000.0
ROPE_DIM = 16
CMP_BLOCK = 32
CMP_STRIDE = 16
SEL_BLOCK = 64
SEL_TOP_N = 16
WINDOW = 512
FORCE_SCORE = 1e4
POOL_WINDOWS = (2, 4, 8, 16)
RWKV_GN_EPS = 64e-5
DIL_PATTERNS = ((128, 1), (512, 4), (2048, 16))
N_EXPERTS = 16
D_EXPERT = 256
NEG_INF = -1e30
LOG2E = 1.4426950408889634
TINY = 1e-30

LANES = 128
TQ = 256
TK = 512
RWKV_CHUNK = 64
RWKV_GROUP = 8
VMEM_LIMIT = 56 * 1024 * 1024

_IN_COLS = dict(q=(0, 256), kvc=(256, 384), kv=(384, 640), gl=(640, 768), pool=(768, 1024),
                rwkv=(1024, 2304), dq=(2304, 2560), dkv=(2560, 3072))
IN_PACKED = 3072
RWKV_PACKED = 1280


def _dot(a, b):
    return jnp.dot(a.astype(BF16), b.astype(BF16), preferred_element_type=F32)


def _dot_hi(a, b):
    return jnp.dot(a, b, precision=HI, preferred_element_type=F32)


def _dot_nt(a, b):
    return lax.dot_general(a, b, (((1,), (1,)), ((), ())), preferred_element_type=F32)


def _split(x):
    hi = x.astype(BF16)
    return hi, (x - hi.astype(F32)).astype(BF16)


def _dot_split(a, b):
    hi, lo = _split(a)
    b = b.astype(BF16)
    return jnp.dot(hi, b, preferred_element_type=F32) + jnp.dot(lo, b, preferred_element_type=F32)


def _dot3(a, b, nt=False):
    f = _dot_nt if nt else functools.partial(jnp.dot, preferred_element_type=F32)
    a_hi, a_lo = _split(a)
    b_hi, b_lo = _split(b)
    return f(a_hi, b_hi) + f(a_lo, b_hi) + f(a_hi, b_lo)


def _sigmoid(x):
    return 1.0 / (1.0 + jnp.exp(-x))


def _iota(shape, dim):
    return lax.broadcasted_iota(jnp.int32, shape, dim)


def _rope_lanes(y, cos, sin):
    n = y.shape[-1]
    lane = _iota(y.shape, y.ndim - 1)
    partner = jnp.where((lane & 15) < 8, pltpu.roll(y, n - 8, y.ndim - 1), pltpu.roll(y, 8, y.ndim - 1))
    return y * cos + partner * sin


def _params(sem, vmem=VMEM_LIMIT):
    return pltpu.CompilerParams(dimension_semantics=sem, vmem_limit_bytes=vmem)


def _mod_kernel(c_ref, w_ref, b_ref, o_ref):
    c = c_ref[...]
    o_ref[...] = _dot_hi(c * _sigmoid(c), w_ref[...]) + b_ref[...]


def _modulation(c, ada_w, ada_b):
    L, D, E = ada_w.shape
    B = c.shape[0]
    tn = 1024
    out = pl.pallas_call(
        _mod_kernel,
        grid=(L, E // tn),
        in_specs=[pl.BlockSpec((B, D), lambda l, j: (0, 0)),
                  pl.BlockSpec((None, D, tn), lambda l, j: (l, 0, j)),
                  pl.BlockSpec((None, 1, tn), lambda l, j: (l, 0, j))],
        out_specs=pl.BlockSpec((None, B, tn), lambda l, j: (l, 0, j)),
        out_shape=jax.ShapeDtypeStruct((L, B, E), F32),
        compiler_params=_params(("parallel", "parallel")),
        name="modulation",
    )(c, ada_w, ada_b.reshape(L, 1, E))
    return out.reshape(L, B, 6, D)


def _rope_kernel(pos_ref, cpos_ref, freq_ref, sgn_ref, cos_ref, sin_ref, ccos_ref, csin_ref):
    freq = freq_ref[...]
    sgn = sgn_ref[...]
    ang = pos_ref[...].astype(F32) * freq
    cos_ref[...] = jnp.cos(ang)
    sin_ref[...] = jnp.sin(ang) * sgn
    cang = cpos_ref[...].astype(F32) * freq
    ccos_ref[...] = jnp.cos(cang)
    csin_ref[...] = jnp.sin(cang) * sgn


def _rope_tables(positions):
    B, S = positions.shape
    half = ROPE_DIM // 2
    inv_freq = ROPE_THETA ** (-2.0 * jnp.arange(half, dtype=F32) / ROPE_DIM)
    lane = np.arange(LANES)
    rot = (lane % HEAD_DIM) < ROPE_DIM
    freq = jnp.where(rot, inv_freq[lane % half], 0.0).reshape(1, LANES)
    sgn = jnp.asarray(np.where(rot, np.where(lane % HEAD_DIM < half, -1.0, 1.0), 0.0), F32).reshape(1, LANES)
    n_cmp = S // CMP_STRIDE
    cpos = positions[:, CMP_BLOCK - 1::CMP_STRIDE]
    cpos = jnp.pad(cpos, ((0, 0), (0, n_cmp - cpos.shape[1])))
    tab = jax.ShapeDtypeStruct((B, S, LANES), F32)
    ctab = jax.ShapeDtypeStruct((B, n_cmp, LANES), F32)
    return pl.pallas_call(
        _rope_kernel,
        grid=(B,),
        in_specs=[pl.BlockSpec((None, S, 1), lambda b: (b, 0, 0)),
                  pl.BlockSpec((None, n_cmp, 1), lambda b: (b, 0, 0)),
                  pl.BlockSpec((1, LANES), lambda b: (0, 0)),
                  pl.BlockSpec((1, LANES), lambda b: (0, 0))],
        out_specs=[pl.BlockSpec((None, S, LANES), lambda b: (b, 0, 0)),
                   pl.BlockSpec((None, S, LANES), lambda b: (b, 0, 0)),
                   pl.BlockSpec((None, n_cmp, LANES), lambda b: (b, 0, 0)),
                   pl.BlockSpec((None, n_cmp, LANES), lambda b: (b, 0, 0))],
        out_shape=[tab, tab, ctab, ctab],
        compiler_params=_params(("parallel",)),
        name="rope_tables",
    )(positions.reshape(B, S, 1), cpos.reshape(B, n_cmp, 1), freq, sgn)


_W_IN_MOVES = ((0, 652, 0), (652, 908, 768), (908, 1676, 1024), (1676, 1740, 1792), (1740, 1804, 1920),
               (1804, 1964, 2048), (1964, 2732, 2304))


def _in_kernel(x_ref, mod_ref, g_ref, w_ref, *refs):
    out_refs, wp_ref = refs[:-1], refs[-1]

    @pl.when((pl.program_id(0) == 0) & (pl.program_id(1) == 0))
    def _pack_weights():
        rows = 128

        def body(r, _):
            r0 = pl.multiple_of(r * rows, rows)
            w = w_ref[pl.ds(r0, rows), :]
            wp_ref[pl.ds(r0, rows), :] = jnp.zeros((rows, IN_PACKED), BF16)
            for lo, hi, dst in _W_IN_MOVES:
                wp_ref[pl.ds(r0, rows), dst:dst + hi - lo] = w[:, lo:hi].astype(BF16)
            return 0

        lax.fori_loop(0, w_ref.shape[0] // rows, body, 0)

    x = x_ref[...]
    ms = jnp.mean(x * x, axis=-1, keepdims=True)
    y = x * lax.rsqrt(ms + RMS_EPS) * g_ref[...]
    h = y * (1.0 + mod_ref[1:2, :]) + mod_ref[0:1, :]
    p = jnp.dot(h.astype(BF16), wp_ref[...], preferred_element_type=F32)
    for ref, (lo, hi) in zip(out_refs, _IN_COLS.values()):
        ref[...] = p[:, lo:hi]


def _in_proj(x, mod_l, g, w_in):
    B, S, D = x.shape
    tm = 512
    widths = [hi - lo for lo, hi in _IN_COLS.values()]
    return pl.pallas_call(
        _in_kernel,
        grid=(B, S // tm),
        in_specs=[pl.BlockSpec((None, tm, D), lambda b, i: (b, i, 0)),
                  pl.BlockSpec((None, 6, D), lambda b, i: (b, 0, 0)),
                  pl.BlockSpec((1, D), lambda b, i: (0, 0)),
                  pl.BlockSpec(w_in.shape, lambda b, i: (0, 0))],
        out_specs=[pl.BlockSpec((None, tm, w), lambda b, i: (b, i, 0)) for w in widths],
        out_shape=[jax.ShapeDtypeStruct((B, S, w), F32) for w in widths],
        scratch_shapes=[pltpu.VMEM((D, IN_PACKED), BF16)],
        compiler_params=_params(("arbitrary", "arbitrary")),
        name="in_proj",
    )(x, mod_l, g.reshape(1, D), w_in)


def _pack_rwkv_row(v):
    z = lambda n: jnp.zeros((n,), v.dtype)
    return jnp.concatenate([v[0:768], v[768:832], z(64), v[832:896], z(64), v[896:1056], z(96)]).reshape(1, RWKV_PACKED)


def _pool_kernel(u_ref, w_ref, scale_ref, o_ref):
    u = u_ref[...]
    row = _iota(u.shape, 0)
    lane = _iota(u.shape, 1)

    def lag(x, s):
        return jnp.where(row >= s, pltpu.roll(x, s, 0), 0.0)

    s2 = u + lag(u, 1)
    s4 = s2 + lag(s2, 2)
    s8 = s4 + lag(s4, 4)
    s16 = s8 + lag(s8, 8)
    g = lane // (GROUP_WIDTH // len(POOL_WINDOWS))
    tot = jnp.where(g == 0, s2, jnp.where(g == 1, s4, jnp.where(g == 2, s8, s16)))
    win = jnp.where(g == 0, 2.0, jnp.where(g == 1, 4.0, jnp.where(g == 2, 8.0, 16.0)))
    cnt = jnp.minimum((row + 1).astype(F32), win)
    o_ref[...] = _dot(tot / cnt - u, w_ref[...]) * scale_ref[...]


def _pool(u, pool_w, pool_scale):
    B, S, W = u.shape
    wbd = jax.scipy.linalg.block_diag(*[pool_w[i] for i in range(pool_w.shape[0])]).astype(BF16)
    return pl.pallas_call(
        _pool_kernel,
        grid=(B,),
        in_specs=[pl.BlockSpec((None, S, W), lambda b: (b, 0, 0)),
                  pl.BlockSpec((W, W), lambda b: (0, 0)),
                  pl.BlockSpec((1, W), lambda b: (0, 0))],
        out_specs=pl.BlockSpec((None, S, W), lambda b: (b, 0, 0)),
        out_shape=jax.ShapeDtypeStruct((B, S, W), F32),
        compiler_params=_params(("parallel",)),
        name="pool_mixer",
    )(u, wbd, pool_scale.reshape(1, W))


def _head_mean_matrix():
    blk = np.kron(np.eye(N_HEADS), np.ones((HEAD_DIM, HEAD_DIM))) / HEAD_DIM
    return jnp.asarray(blk, F32)


def _pad_heads(x256, as_bf16=True):
    n = x256.shape[0]
    lane = _iota((n, LANES), 1)
    out = []
    for h in range(N_HEADS):
        blk = x256[:, (h // 2) * LANES:(h // 2 + 1) * LANES]
        if h % 2:
            blk = pltpu.roll(blk, HEAD_DIM, 1)
        blk = jnp.where(lane < HEAD_DIM, blk, 0.0)
        out.append(blk.astype(BF16) if as_bf16 else blk)
    return out


def _merge_heads(slabs):
    n = slabs[0].shape[0]
    lane = _iota((n, LANES), 1)
    cols = []
    for p in range(N_HEADS // 2):
        even = pltpu.roll(slabs[2 * p], HEAD_DIM, 1)
        cols.append(jnp.where(lane < HEAD_DIM, even, slabs[2 * p + 1]))
    return jnp.concatenate(cols, axis=1)


def _flash_step(carries, qs, kas, vbs, biases):
    idx = range(len(qs))
    s = [_dot_nt(qs[i], kas[i]) for i in idx]
    if biases is not None:
        s = [s[i] + biases[i] for i in idx]
    m_new = [jnp.maximum(carries[i][0], jnp.max(s[i], axis=-1, keepdims=True)) for i in idx]
    alpha = [jnp.exp2(carries[i][0] - m_new[i]) for i in idx]
    p = [jnp.exp2(s[i] - m_new[i]).astype(BF16) for i in idx]
    pv = [jnp.dot(p[i], vbs[i], preferred_element_type=F32) for i in idx]
    return tuple((m_new[i], alpha[i] * carries[i][1] + pv[i]) for i in idx)


def _flash_init(n):
    return (jnp.full((n, 1), NEG_INF, F32), jnp.zeros((n, LANES), F32))


def _flash_finish(carry):
    acc = carry[1]
    return acc / pltpu.roll(acc, HEAD_DIM, 1)


def _nsa_kernel(q_ref, gl_ref, kc_ref, kv_ref, cos_ref, sin_ref, ccos_ref, csin_ref,
                qn_ref, kn_ref, pet_ref, peb_ref, wt_ref, wb_ref, w2_ref,
                hb_ref, cover_ref, eg_ref, ctab_ref, wtab_ref,
                o_ref, ksvs_ref, kwvw_ref, vs1_ref, vw1_ref, kvc_ref):
    S = kv_ref.shape[0]
    qi = pl.program_id(1)
    n_cmp = S // CMP_STRIDE

    @pl.when(qi == 0)
    def _prep_keys():
        rows = 256

        def body(i, _):
            r0 = pl.multiple_of(i * rows, rows)
            cs = cos_ref[pl.ds(r0, rows), :]
            sn = sin_ref[pl.ds(r0, rows), :]
            lane = _iota((rows, LANES), 1)
            first = lane < HEAD_DIM
            blk_hot = jnp.where(lane - HEAD_DIM == (r0 + _iota((rows, LANES), 0)) // SEL_BLOCK, 1.0, 0.0)
            for col, j, dst, dst1 in ((0, 1, ksvs_ref, vs1_ref), (LANES, 2, kwvw_ref, vw1_ref)):
                slab = kv_ref[pl.ds(r0, rows), col:col + LANES]
                ms = jnp.sum(jnp.where(first, slab * slab, 0.0), axis=-1, keepdims=True) * (1.0 / HEAD_DIM)
                y = slab * lax.rsqrt(ms + RMS_EPS) * kn_ref[j:j + 1, :]
                dst[pl.ds(r0, rows), :] = jnp.where(first, _rope_lanes(y, cs, sn), blk_hot if j == 1 else slab).astype(BF16)
                dst1[pl.ds(r0, rows), :] = jnp.where(first, 1.0, slab).astype(BF16)
            return 0

        lax.fori_loop(0, S // rows, body, 0)

        top = jnp.zeros((n_cmp, LANES), F32)
        bot = jnp.zeros((n_cmp, LANES), F32)
        for i in range(CMP_STRIDE):
            tok = kc_ref[pl.ds(i, n_cmp, stride=CMP_STRIDE), :]
            cols = slice(i * LANES, (i + 1) * LANES)
            top = top + _dot3(tok + pet_ref[:, cols], wt_ref[cols, :])
            bot = bot + _dot3(tok + peb_ref[:, cols], wb_ref[cols, :])
        pre = top + pltpu.roll(bot, n_cmp - 1, 0)
        act = 0.5 * pre * (1.0 + jnp.tanh(float(np.sqrt(2.0 / np.pi)) * (pre + 0.044715 * (pre * pre * pre))))
        out = _dot3(act, w2_ref[...])
        first = _iota(out.shape, 1) < HEAD_DIM
        ms = jnp.sum(jnp.where(first, out * out, 0.0), axis=-1, keepdims=True) * (1.0 / HEAD_DIM)
        y = out * lax.rsqrt(ms + RMS_EPS) * kn_ref[0:1, :]
        kvc_ref[...] = jnp.where(first, _rope_lanes(y, ccos_ref[...], csin_ref[...]), out)

    r0 = pl.multiple_of(qi * TQ, TQ)
    q = q_ref[...]
    ms = _dot_split(q * q, hb_ref[...])
    qn = q * lax.rsqrt(ms + RMS_EPS) * qn_ref[...]
    cs = cos_ref[pl.ds(r0, TQ), :]
    sn = sin_ref[pl.ds(r0, TQ), :]
    qr = _rope_lanes(qn, jnp.concatenate([cs, cs], axis=1), jnp.concatenate([sn, sn], axis=1)) * (HEAD_DIM ** -0.5)
    qst = jnp.concatenate(_pad_heads(qr, as_bf16=False), axis=0)
    n_rows = N_HEADS * TQ

    kvc = kvc_ref[...]
    s = _dot3(qst, kvc, nt=True)
    t_st = r0 + (_iota((n_rows, n_cmp), 0) & (TQ - 1))
    c_end = _iota((n_rows, n_cmp), 1) * CMP_STRIDE + (CMP_BLOCK - 1)
    mask = c_end <= t_st
    sm = jnp.where(mask, s, NEG_INF)
    m = jnp.max(sm, axis=-1, keepdims=True)
    e = jnp.where(mask, jnp.exp(sm - m), 0.0)
    p = e / jnp.maximum(jnp.sum(e, axis=-1, keepdims=True), TINY)
    o_cmp = _dot(p, kvc)

    psum = p[0:TQ] + p[TQ:2 * TQ] + p[2 * TQ:3 * TQ] + p[3 * TQ:4 * TQ]
    imp = _dot_split(psum, cover_ref[...])
    n_blk = S // SEL_BLOCK
    lane = _iota((TQ, LANES), 1)
    blk = lane & (n_blk - 1)
    tq = r0 + _iota((TQ, LANES), 0)
    cur = tq // SEL_BLOCK
    forced = (blk == 0) | (blk == cur) | (blk == cur - 1)
    imp = jnp.where(blk > cur, -1.0, jnp.where(forced, FORCE_SCORE, imp))
    rank = jnp.zeros((TQ, LANES), F32)
    for d in range(1, n_blk):
        other = pltpu.roll(imp, LANES - d, 1)
        rank = rank + jnp.where(blk + d >= n_blk,
                                jnp.where(other >= imp, 1.0, 0.0),
                                jnp.where(other > imp, 1.0, 0.0))
    sel_bias = jnp.where((lane >= HEAD_DIM) & (lane < HEAD_DIM + n_blk), jnp.where(rank < SEL_TOP_N, 0.0, NEG_INF), 0.0)
    sel_bias = sel_bias.astype(BF16)

    qb = (qst * LOG2E).astype(BF16)
    heads = range(N_HEADS)
    q_heads = [qb[h * TQ:(h + 1) * TQ] for h in heads]
    q_heads_sel = [jnp.where(lane < HEAD_DIM, q, sel_bias) for q in q_heads]
    init = tuple(_flash_init(TQ) for _ in heads)

    def sel_tile(j, carry, biases):
        k0 = pl.multiple_of(j * TK, TK)
        ka, vb = ksvs_ref[pl.ds(k0, TK), :], vs1_ref[pl.ds(k0, TK), :]
        return _flash_step(carry, q_heads_sel, [ka] * N_HEADS, [vb] * N_HEADS, biases)

    ratio = TK // TQ
    last = qi // ratio
    res = lax.fori_loop(0, last, lambda j, c: sel_tile(j, c, None), init)
    res = sel_tile(last, res, [ctab_ref[qi % ratio]] * N_HEADS)
    o_sel = jnp.concatenate([_flash_finish(c) for c in res], axis=0)

    n_prev = WINDOW // TQ
    span = (n_prev + 1) * TQ
    k0 = pl.multiple_of(jnp.maximum(qi - n_prev, 0) * TQ, TQ)
    ka, vb = kwvw_ref[pl.ds(k0, span), :], vw1_ref[pl.ds(k0, span), :]
    res = _flash_step(init, q_heads, [ka] * N_HEADS, [vb] * N_HEADS, [wtab_ref[jnp.minimum(qi, n_prev)]] * N_HEADS)
    o_win = jnp.concatenate([_flash_finish(c) for c in res], axis=0)

    gate = _dot_split(_sigmoid(gl_ref[...]), eg_ref[...])
    split = lambda o: _merge_heads([o[h * TQ:(h + 1) * TQ] for h in range(N_HEADS)])
    W = GROUP_WIDTH
    o_ref[...] = (gate[:, 0:W] * split(o_cmp) + gate[:, W:2 * W] * split(o_sel) + gate[:, 2 * W:3 * W] * split(o_win))


def _nsa(q, kvc, kv, gl, cos, sin, ccos, csin, q_norm, k_norm, cmp_pe, cmp_w1, cmp_w2):
    B, S, _ = q.shape
    n_cmp = S // CMP_STRIDE
    n_blk = S // SEL_BLOCK
    hd = HEAD_DIM

    qn = jnp.tile(q_norm, N_HEADS).reshape(1, GROUP_WIDTH)
    kn = jnp.concatenate([k_norm, jnp.ones_like(k_norm)], axis=1)

    def interleave(a, b):
        return jnp.concatenate([a, b], axis=1).reshape(1, -1)

    half = CMP_BLOCK // 2
    pet = interleave(cmp_pe[0, :half], cmp_pe[1, :half])
    peb = interleave(cmp_pe[0, half:], cmp_pe[1, half:])

    def w1_half(lo):
        wk = cmp_w1[0].reshape(CMP_BLOCK, hd, hd)[lo:lo + half]
        wv = cmp_w1[1].reshape(CMP_BLOCK, hd, hd)[lo:lo + half]
        z = jnp.zeros_like(wk)
        top = jnp.concatenate([wk, z], axis=2)
        bot = jnp.concatenate([z, wv], axis=2)
        return jnp.concatenate([top, bot], axis=1).reshape(half * 2 * hd, 2 * hd)

    wt, wb = w1_half(0), w1_half(half)
    w2 = jax.scipy.linalg.block_diag(cmp_w2[0], cmp_w2[1])
    hb = _head_mean_matrix()

    c_end = np.arange(n_cmp) * CMP_STRIDE + CMP_BLOCK - 1
    b_start = np.arange(n_blk) * SEL_BLOCK
    cover = np.maximum(np.minimum(c_end[:, None] + 1, b_start[None, :] + SEL_BLOCK)
                       - np.maximum(c_end[:, None] + 1 - CMP_BLOCK, b_start[None, :]), 0).astype(np.float32) / CMP_BLOCK
    cover[n_cmp - 1] = 0.0
    cover = jnp.asarray(np.tile(cover, (1, LANES // n_blk)), F32)
    eg = np.zeros((LANES, 3 * GROUP_WIDTH), np.float32)
    for h in range(N_HEADS):
        for br in range(3):
            eg[h * 3 + br, br * GROUP_WIDTH + h * hd: br * GROUP_WIDTH + (h + 1) * hd] = 1.0
    eg = jnp.asarray(eg)
    ratio = TK // TQ
    ctab = np.where(np.arange(TK)[None, None, :] <= np.arange(ratio)[:, None, None] * TQ + np.arange(TQ)[None, :, None],
                    0.0, NEG_INF).astype(np.float32)
    n_prev = WINDOW // TQ
    span = (n_prev + 1) * TQ
    dist = (np.arange(n_prev + 1)[:, None, None] * TQ + np.arange(TQ)[None, :, None] - np.arange(span)[None, None, :])
    wtab = np.where((dist >= 0) & (dist < WINDOW), 0.0, NEG_INF).astype(np.float32)
    ctab, wtab = jnp.asarray(ctab), jnp.asarray(wtab)

    full = lambda shape: pl.BlockSpec(shape, lambda b, i: (0,) * len(shape))
    return pl.pallas_call(
        _nsa_kernel,
        grid=(B, S // TQ),
        in_specs=[pl.BlockSpec((None, TQ, GROUP_WIDTH), lambda b, i: (b, i, 0)),
                  pl.BlockSpec((None, TQ, LANES), lambda b, i: (b, i, 0)),
                  pl.BlockSpec((None, S, LANES), lambda b, i: (b, 0, 0)),
                  pl.BlockSpec((None, S, 2 * LANES), lambda b, i: (b, 0, 0)),
                  pl.BlockSpec((None, S, LANES), lambda b, i: (b, 0, 0)),
                  pl.BlockSpec((None, S, LANES), lambda b, i: (b, 0, 0)),
                  pl.BlockSpec((None, n_cmp, LANES), lambda b, i: (b, 0, 0)),
                  pl.BlockSpec((None, n_cmp, LANES), lambda b, i: (b, 0, 0)),
                  full((1, GROUP_WIDTH)), full((3, LANES)),
                  full((1, CMP_STRIDE * LANES)), full((1, CMP_STRIDE * LANES)),
                  full((CMP_STRIDE * LANES, LANES)), full((CMP_STRIDE * LANES, LANES)), full((LANES, LANES)),
                  full((GROUP_WIDTH, GROUP_WIDTH)), full((n_cmp, LANES)),
                  full((LANES, 3 * GROUP_WIDTH)), full(ctab.shape), full(wtab.shape)],
        out_specs=pl.BlockSpec((None, TQ, GROUP_WIDTH), lambda b, i: (b, i, 0)),
        out_shape=jax.ShapeDtypeStruct((B, S, GROUP_WIDTH), F32),
        scratch_shapes=[pltpu.VMEM((S, LANES), BF16), pltpu.VMEM((S, LANES), BF16),
                        pltpu.VMEM((S, LANES), BF16), pltpu.VMEM((S, LANES), BF16),
                        pltpu.VMEM((n_cmp, LANES), F32)],
        compiler_params=_params(("parallel", "arbitrary")),
        name="nsa",
    )(q, gl, kvc, kv, cos, sin, ccos, csin, qn, kn, pet, peb, wt, wb, w2, hb, cover, eg, ctab, wtab)


def _dil_kernel(q_ref, kv_ref, cos_ref, sin_ref, qn_ref, kn_ref, hb_ref, tb_ref, o_ref, kvs_ref, v1s_ref):
    S = kv_ref.shape[0]
    qi = pl.program_id(1)
    W = GROUP_WIDTH

    @pl.when(qi == 0)
    def _prep_keys():
        rows = 256

        def body(i, _):
            r0 = pl.multiple_of(i * rows, rows)
            cs = cos_ref[pl.ds(r0, rows), :]
            sn = sin_ref[pl.ds(r0, rows), :]
            k = kv_ref[pl.ds(r0, rows), 0:W]
            v = kv_ref[pl.ds(r0, rows), W:2 * W]
            ms = _dot_split(k * k, hb_ref[...])
            kn = k * lax.rsqrt(ms + RMS_EPS) * kn_ref[...]
            kr = _rope_lanes(kn, jnp.concatenate([cs, cs], axis=1), jnp.concatenate([sn, sn], axis=1))
            lane = _iota((rows, LANES), 1)
            for h in range(N_HEADS):
                kb = kr[:, (h // 2) * LANES:(h // 2 + 1) * LANES]
                vb = v[:, (h // 2) * LANES:(h // 2 + 1) * LANES]
                if h % 2:
                    kb = pltpu.roll(kb, HEAD_DIM, 1)
                else:
                    vb = pltpu.roll(vb, HEAD_DIM, 1)
                kvs_ref[h, pl.ds(r0, rows), :] = jnp.where(lane < HEAD_DIM, kb, vb).astype(BF16)
                v1s_ref[h, pl.ds(r0, rows), :] = jnp.where(lane < HEAD_DIM, 1.0, vb).astype(BF16)
            return 0

        lax.fori_loop(0, S // rows, body, 0)

    r0 = pl.multiple_of(qi * TQ, TQ)
    q = q_ref[...]
    ms = _dot_split(q * q, hb_ref[...])
    qn = q * lax.rsqrt(ms + RMS_EPS) * qn_ref[...]
    cs = cos_ref[pl.ds(r0, TQ), :]
    sn = sin_ref[pl.ds(r0, TQ), :]
    qr = _rope_lanes(qn, jnp.concatenate([cs, cs], axis=1), jnp.concatenate([sn, sn], axis=1)) * (HEAD_DIM ** -0.5 * LOG2E)
    qh = _pad_heads(qr)
    ratio = TK // TQ

    def body(j, carry):
        k0 = pl.multiple_of(j * TK, TK)
        bias = tb_ref[qi - ratio * j]
        return _flash_step(carry, qh, [kvs_ref[h, pl.ds(k0, TK), :] for h in range(N_HEADS)],
                           [v1s_ref[h, pl.ds(k0, TK), :] for h in range(N_HEADS)], [bias] * N_HEADS)

    res = lax.fori_loop(0, qi // ratio + 1, body, tuple(_flash_init(TQ) for _ in range(N_HEADS)))
    o_ref[...] = _merge_heads([_flash_finish(c) for c in res])


def _dil_bias_table(S):
    n = S // TQ
    d = (np.arange(n)[:, None, None] * TQ + np.arange(TQ)[None, :, None] - np.arange(TK)[None, None, :])
    cnt = np.zeros(d.shape, np.float32)
    for window, dil in DIL_PATTERNS:
        cnt += ((d >= 0) & (d <= window) & (d % dil == 0)).astype(np.float32)
    return jnp.asarray(np.where(cnt > 0, np.log2(np.maximum(cnt, 1.0)), NEG_INF).astype(np.float32))


def _dilated(q, kv, cos, sin, q_norm, k_norm):
    B, S, W = q.shape
    tb = _dil_bias_table(S)
    qn = jnp.tile(q_norm, N_HEADS).reshape(1, W)
    kn = jnp.tile(k_norm, N_HEADS).reshape(1, W)
    hb = _head_mean_matrix()
    full = lambda shape: pl.BlockSpec(shape, lambda b, i: (0,) * len(shape))
    return pl.pallas_call(
        _dil_kernel,
        grid=(B, S // TQ),
        in_specs=[pl.BlockSpec((None, TQ, W), lambda b, i: (b, i, 0)),
                  pl.BlockSpec((None, S, 2 * W), lambda b, i: (b, 0, 0)),
                  pl.BlockSpec((None, S, LANES), lambda b, i: (b, 0, 0)),
                  pl.BlockSpec((None, S, LANES), lambda b, i: (b, 0, 0)),
                  full((1, W)), full((1, W)), full((W, W)), full(tb.shape)],
        out_specs=pl.BlockSpec((None, TQ, W), lambda b, i: (b, i, 0)),
        out_shape=jax.ShapeDtypeStruct((B, S, W), F32),
        scratch_shapes=[pltpu.VMEM((N_HEADS, S, LANES), BF16), pltpu.VMEM((N_HEADS, S, LANES), BF16)],
        compiler_params=_params(("parallel", "arbitrary")),
        name="dilated",
    )(q, kv, cos, sin, qn, kn, hb, tb)


def _softplus(x):
    return jnp.maximum(x, 0.0) + jnp.log(1.0 + jnp.exp(-jnp.abs(x)))


def _rwkv_kernel(first, *refs):
    if first:
        (z_ref, mu_ref, w0_ref, w2_ref, a0_ref, a2_ref, g2_ref, kk_ref, ka_ref, rk_ref, lnw_ref, lnb_ref,
         hb_ref, y_ref, vf_out_ref, st_ref, prev_ref, ro_ref, ub_ref, colb_ref, g_ref, t2_ref) = refs
    else:
        (z_ref, vf_ref, mu_ref, w0_ref, w2_ref, a0_ref, a2_ref, g2_ref, kk_ref, ka_ref, rk_ref, lnw_ref, lnb_ref,
         v0_ref, v1_ref, v2_ref, hb_ref, y_ref, st_ref, prev_ref, ro_ref, ub_ref, colb_ref, g_ref, t2_ref) = refs
    SB = z_ref.shape[0]
    C = RWKV_CHUNK
    W = GROUP_WIDTH
    n_pairs = N_HEADS // 2
    R = RWKV_GROUP * C

    @pl.when(pl.program_id(1) == 0)
    def _reset():
        st_ref[...] = jnp.zeros_like(st_ref)
        prev_ref[...] = jnp.zeros_like(prev_ref)

    ri = _iota((C, C), 0)
    ci = _iota((C, C), 1)
    tri_incl = jnp.where(ci <= ri, 1.0, 0.0).astype(BF16)
    rr = _iota((2 * LANES, 2 * LANES), 0)
    cc = _iota((2 * LANES, 2 * LANES), 1)
    g_mask = ((((rr >> 6) & 1) == ((cc >> 6) & 1))
              & ((cc & (C - 1)) <= jnp.where(rr < LANES, (rr & (C - 1)) - 1, rr & (C - 1))))
    r2 = _iota((LANES, LANES), 0)
    c2 = _iota((LANES, LANES), 1)
    eye_f = jnp.where(r2 == c2, 1.0, 0.0)
    lane_lo = _iota((C, LANES), 1) < HEAD_DIM
    hb = hb_ref[...].astype(BF16)

    def stack(x):
        return jnp.concatenate([jnp.where(lane_lo, x, 0.0), jnp.where(lane_lo, 0.0, x)], axis=0)

    def phase1(i, _):
        t0 = pl.multiple_of(i * R, R)
        zc = z_ref[pl.ds(t0, R), :]
        inside = jnp.where(i > 0, 1.0, 0.0)
        prev = inside * z_ref[pl.ds(jnp.maximum(t0 - 1, 0), 1), :] + (1.0 - inside) * prev_ref[...]
        zp = jnp.where(_iota(zc.shape, 0) == 0, prev, pltpu.roll(zc, 1, 0))
        zs = zc + (zp - zc) * mu_ref[...]
        r, k, v = zs[:, 0:W], zs[:, W:2 * W], zs[:, 2 * W:3 * W]
        xw, xa, xg = zs[:, 768:896], zs[:, 896:1024], zs[:, 1024:1280]
        w_log = -_softplus(-(w0_ref[...] + _dot(jnp.tanh(xw), w2_ref[...]))) - 0.5
        lw = -jnp.exp(w_log)
        a = _sigmoid(a0_ref[...] + _dot(xa, a2_ref[...]))
        g = _dot(_sigmoid(xg), g2_ref[...])
        if first:
            vf_out_ref[pl.ds(t0, R), :] = v
        else:
            vf = vf_ref[pl.ds(t0, R), :]
            v = v + (vf - v) * _sigmoid(v0_ref[...] + _dot(_dot(v, v1_ref[...]), v2_ref[...]))
        kkr = k * kk_ref[...]
        nrm = jnp.sqrt(_dot_split(kkr * kkr, hb) * HEAD_DIM)
        kk = kkr / jnp.maximum(nrm, 1e-12)
        k2 = k * (1.0 + (a - 1.0) * ka_ref[...])
        bonus = _dot_split(r * k2 * rk_ref[...], hb) * HEAD_DIM * v
        g_ref[pl.ds(t0, R), :] = g
        t2_ref[pl.ds(t0, R), :] = (lnb_ref[...] + bonus) * g

        items = [(c, p) for c in range(RWKV_GROUP) for p in range(n_pairs)]
        idx = range(len(items))
        sub = lambda x, c, p: x[c * C:(c + 1) * C, p * LANES:(p + 1) * LANES]
        cum = [_cumsum_rows(tri_incl, sub(lw, c, p)) for c, p in items]
        a_st, r_st, v_st, gm, tails = [], [], [], [], []
        for j, (c, p) in enumerate(items):
            lwp = sub(lw, c, p)
            cum_end = cum[j][C - 1:C, :]
            e_pos, e_neg = jnp.exp(cum[j]), jnp.exp(-cum[j])
            e_exc, e_tail = jnp.exp(cum[j] - lwp), jnp.exp(cum_end - cum[j])
            kkp, k2p = sub(kk, c, p), sub(k2, c, p)
            ka_p = kkp * sub(a, c, p)
            a_st.append(stack(-kkp * e_exc).astype(BF16))
            r_st.append(stack(sub(r, c, p) * e_pos))
            v_st.append(stack(sub(v, c, p)).astype(BF16))
            bhat = (ka_p * e_neg).astype(BF16)
            khat = (k2p * e_neg).astype(BF16)
            tails.append(jnp.concatenate([stack(ka_p * e_tail), stack(k2p * e_tail)], axis=0))
            gm.append(jnp.where(g_mask,
                                _dot_nt(jnp.concatenate([a_st[j], r_st[j].astype(BF16)], axis=0),
                                        jnp.concatenate([bhat, bhat, khat, khat], axis=0)), 0.0))
            colb_ref[i * RWKV_GROUP + c, p] = jnp.broadcast_to(jnp.exp(cum_end), (LANES, LANES)).T
        tt = [tails[j].T.astype(BF16) for j in idx]
        lakv = [_dot(gm[j][0:LANES, LANES:2 * LANES], v_st[j]) for j in idx]
        mr = [gm[j][LANES:2 * LANES, :].astype(BF16) for j in idx]
        lp = [gm[j][0:LANES, 0:LANES] for j in idx]
        tinv = [eye_f + lp[j] for j in idx]
        for _ in range(5):
            lpb = [lp[j].astype(BF16) for j in idx]
            lp = [jnp.dot(lpb[j], lpb[j], preferred_element_type=F32) for j in idx]
            tinv = [tinv[j] + _dot(lp[j], tinv[j]) for j in idx]
        wu = [_dot(tinv[j], jnp.concatenate([a_st[j].astype(F32), lakv[j]], axis=1)) for j in idx]
        wm = [wu[j][:, 0:LANES].astype(BF16) for j in idx]
        u = [wu[j][:, LANES:2 * LANES].astype(BF16) for j in idx]
        mwbw = [jnp.dot(jnp.concatenate([mr[j][:, 0:LANES], tt[j][:, 0:LANES]], axis=0), wm[j],
                        preferred_element_type=F32) for j in idx]
        ub = [jnp.dot(jnp.concatenate([mr[j], tt[j]], axis=0), jnp.concatenate([u[j], v_st[j]], axis=0),
                      preferred_element_type=F32) for j in idx]
        for j, (c, p) in enumerate(items):
            n = i * RWKV_GROUP + c
            ro_ref[n, p] = jnp.concatenate([r_st[j] + mwbw[j][0:LANES], mwbw[j][LANES:2 * LANES]], axis=0).astype(BF16)
            ub_ref[n, p] = ub[j]
        return 0

    def phase2(n, _):
        t0 = pl.multiple_of(n * C, C)
        pairs = range(n_pairs)
        st = [st_ref[p] for p in pairs]
        res = [jnp.dot(ro_ref[n, p], st[p].astype(BF16), preferred_element_type=F32) for p in pairs]
        ub = [ub_ref[n, p] for p in pairs]
        for p in pairs:
            st_ref[p] = colb_ref[n, p] * st[p] + res[p][LANES:2 * LANES] + ub[p][LANES:2 * LANES]
        o_st = [res[p][0:LANES] + ub[p][0:LANES] for p in pairs]
        y_ref[pl.ds(t0, C), :] = jnp.concatenate([o[0:C] + o[C:2 * C] for o in o_st], axis=1)
        return 0

    lax.fori_loop(0, SB // R, phase1, 0)
    lax.fori_loop(0, SB // C, phase2, 0)
    prev_ref[...] = z_ref[SB - 1:SB, :]

    for t in range(SB // LANES):
        rows = slice(t * LANES, (t + 1) * LANES)
        o = y_ref[rows, :]
        dev = o - _dot_split(o, hb)
        var = _dot_split(dev * dev, hb)
        y_ref[rows, :] = dev * lax.rsqrt(var + RWKV_GN_EPS) * (lnw_ref[...] * g_ref[rows, :]) + t2_ref[rows, :]


def _cumsum_rows(tri_incl, x):
    hi = x.astype(BF16)
    lo = (x - hi.astype(F32)).astype(BF16)
    return jnp.dot(tri_incl, hi, preferred_element_type=F32) + jnp.dot(tri_incl, lo, preferred_element_type=F32)


RWKV_BLOCK = 512


def _rwkv(z, v_first, v_res, mu, w0, w2, a0, a2, g2, k_k, k_a, r_k, ln_w, ln_b):
    B, S, P = z.shape
    W = GROUP_WIDTH
    SB = RWKV_BLOCK
    ncb = SB // RWKV_CHUNK
    n_pairs = N_HEADS // 2
    first = v_res is None
    row = lambda v: v.reshape(1, W)
    padr = lambda m, n: jnp.pad(m, ((0, n - m.shape[0]), (0, 0)))
    seq = lambda w: pl.BlockSpec((None, SB, w), lambda b, j: (b, j, 0))
    full = lambda shape: pl.BlockSpec(shape, lambda b, j: (0,) * len(shape))
    args = [z]
    specs = [seq(P)]
    if not first:
        args.append(v_first)
        specs.append(seq(W))
    args += [_pack_rwkv_row(mu), row(w0), padr(w2, LANES), row(a0), padr(a2, LANES), padr(g2, 2 * LANES),
             row(k_k), row(k_a), r_k.reshape(1, W), row(ln_w), row(ln_b)]
    specs += [full((1, P)), full((1, W)), full((LANES, W)), full((1, W)), full((LANES, W)), full((2 * LANES, W)),
              full((1, W)), full((1, W)), full((1, W)), full((1, W)), full((1, W))]
    if not first:
        v0, v1, v2 = v_res
        args += [row(v0), jnp.pad(v1, ((0, 0), (0, LANES - v1.shape[1]))), padr(v2, LANES)]
        specs += [full((1, W)), full((W, LANES)), full((LANES, W))]
    args.append(_head_mean_matrix())
    specs.append(full((W, W)))
    out_shape = [jax.ShapeDtypeStruct((B, S, W), F32)]
    out_specs = [seq(W)]
    if first:
        out_shape.append(jax.ShapeDtypeStruct((B, S, W), F32))
        out_specs.append(seq(W))
    res = pl.pallas_call(
        functools.partial(_rwkv_kernel, first),
        grid=(B, S // SB),
        in_specs=specs,
        out_specs=out_specs,
        out_shape=out_shape,
        scratch_shapes=[pltpu.VMEM((n_pairs, LANES, LANES), F32), pltpu.VMEM((1, P), F32),
                        pltpu.VMEM((ncb, n_pairs, 2 * LANES, LANES), BF16),
                        pltpu.VMEM((ncb, n_pairs, 2 * LANES, LANES), F32),
                        pltpu.VMEM((ncb, n_pairs, LANES, LANES), F32),
                        pltpu.VMEM((SB, W), F32), pltpu.VMEM((SB, W), F32)],
        compiler_params=_params(("parallel", "arbitrary")),
        name="rwkv7_first" if first else "rwkv7",
    )(*args)
    return (res[0], res[1]) if first else (res[0], v_first)


def _out_kernel(x_ref, ya_ref, yb_ref, yc_ref, yd_ref, wo_ref, mod_ref, g_ref, rwh_ref, rwl_ref, rb_ref,
                xo_ref, h_ref, gates_ref):
    W = GROUP_WIDTH
    y = jnp.concatenate([ya_ref[...].astype(BF16), yb_ref[...].astype(BF16),
                         yc_ref[...].astype(BF16), yd_ref[...].astype(BF16)], axis=1)
    mix = jnp.dot(y, wo_ref[...], preferred_element_type=F32)
    x = x_ref[...] + mod_ref[2:3, :] * mix
    xo_ref[...] = x
    ms = jnp.mean(x * x, axis=-1, keepdims=True)
    h = x * lax.rsqrt(ms + RMS_EPS) * g_ref[...] * (1.0 + mod_ref[4:5, :]) + mod_ref[3:4, :]
    h_hi = h.astype(BF16)
    h_ref[...] = h_hi

    h_lo = (h - h_hi.astype(F32)).astype(BF16)
    logits = (jnp.dot(h_hi, rwh_ref[...], preferred_element_type=F32)
              + jnp.dot(h_lo, rwh_ref[...], preferred_element_type=F32)
              + jnp.dot(h_hi, rwl_ref[...], preferred_element_type=F32))
    aff = _sigmoid(logits)
    s = aff + rb_ref[...]
    lane = _iota(s.shape, 1)
    l4 = lane & 3
    grp = (lane & 15) >> 2

    def in_group(x, d):
        return jnp.where(l4 + d < 4, pltpu.roll(x, LANES - d, 1), pltpu.roll(x, 4 - d, 1))

    def across(x, d):
        return jnp.where(grp + d < 4, pltpu.roll(x, LANES - 4 * d, 1), pltpu.roll(x, 16 - 4 * d, 1))

    rank = jnp.zeros(s.shape, F32)
    for d in range(1, 4):
        o = in_group(s, d)
        rank = rank + jnp.where(l4 + d < 4, jnp.where(o > s, 1.0, 0.0), jnp.where(o >= s, 1.0, 0.0))
    top2 = rank < 2.0
    ts = jnp.where(top2, s, 0.0)
    gs = ts + in_group(ts, 1) + in_group(ts, 2) + in_group(ts, 3)
    lost = jnp.zeros(s.shape, F32)
    for d in range(1, 4):
        o = across(gs, d)
        lost = lost + jnp.where(grp + d < 4, jnp.where(o > gs, 1.0, 0.0), jnp.where(o >= gs, 1.0, 0.0))
    chosen = top2 & (lost < 0.5) & (lane < N_EXPERTS)
    a_sel = jnp.where(chosen, aff, 0.0)
    den = a_sel + in_group(a_sel, 1) + in_group(a_sel, 2) + in_group(a_sel, 3)
    gates_ref[...] = jnp.where(chosen, aff / jnp.where(chosen, den, 1.0), 0.0)


def _out_proj(x, ys, w_out, mod_l, g, router_w, router_b):
    B, S, D = x.shape
    tm = 512
    W = GROUP_WIDTH
    rw = jnp.pad(router_w, ((0, 0), (0, LANES - N_EXPERTS)))
    rw_hi = rw.astype(BF16)
    rw_lo = (rw - rw_hi.astype(F32)).astype(BF16)
    rb = jnp.pad(router_b, (0, LANES - N_EXPERTS)).reshape(1, LANES)
    tile = lambda w: pl.BlockSpec((None, tm, w), lambda b, i: (b, i, 0))
    full = lambda shape: pl.BlockSpec(shape, lambda b, i: (0,) * len(shape))
    return pl.pallas_call(
        _out_kernel,
        grid=(B, S // tm),
        in_specs=[tile(D), tile(W), tile(W), tile(W), tile(W), full((D, D)),
                  pl.BlockSpec((None, 6, D), lambda b, i: (b, 0, 0)), full((1, D)), full((D, LANES)), full((D, LANES)),
                  full((1, LANES))],
        out_specs=[tile(D), tile(D), tile(LANES)],
        out_shape=[jax.ShapeDtypeStruct((B, S, D), F32), jax.ShapeDtypeStruct((B, S, D), BF16),
                   jax.ShapeDtypeStruct((B, S, LANES), F32)],
        compiler_params=_params(("parallel", "parallel")),
        name="out_proj_router",
    )(x, *ys, w_out.astype(BF16), mod_l, g.reshape(1, D), rw_hi, rw_lo, rb)


MOE_EXPERTS_PER_STEP = 4


def _moe_kernel(x_ref, h_ref, gates_ref, mod_ref, wg_ref, wu_ref, wd_ref, o_ref):
    step = pl.program_id(2)
    n = MOE_EXPERTS_PER_STEP

    @pl.when(step == 0)
    def _init():
        o_ref[...] = x_ref[...]

    h = h_ref[...]
    gates = gates_ref[...]
    lane = _iota(gates.shape, 1)
    hgs = [jnp.dot(h, wg_ref[k].astype(BF16), preferred_element_type=F32) for k in range(n)]
    hus = [jnp.dot(h, wu_ref[k].astype(BF16), preferred_element_type=F32) for k in range(n)]
    acts = []
    for k in range(n):
        ge = jnp.sum(jnp.where(lane == step * n + k, gates, 0.0), axis=-1, keepdims=True)
        acts.append((hgs[k] * _sigmoid(hgs[k]) * hus[k] * ge).astype(BF16))
    act = jnp.concatenate(acts, axis=1)
    wd = wd_ref[...].astype(BF16).reshape(n * D_EXPERT, wd_ref.shape[2])
    o_ref[...] += mod_ref[5:6, :] * jnp.dot(act, wd, preferred_element_type=F32)


def _moe(x, h, gates, mod_l, w_gate, w_up, w_down):
    B, S, D = x.shape
    E = w_gate.shape[0]
    n = MOE_EXPERTS_PER_STEP
    tm = 1024
    tile = lambda w: pl.BlockSpec((None, tm, w), lambda b, i, e: (b, i, 0))
    return pl.pallas_call(
        _moe_kernel,
        grid=(B, S // tm, E // n),
        in_specs=[tile(D), tile(D), tile(LANES),
                  pl.BlockSpec((None, 6, D), lambda b, i, e: (b, 0, 0)),
                  pl.BlockSpec((n, D, D_EXPERT), lambda b, i, e: (e, 0, 0)),
                  pl.BlockSpec((n, D, D_EXPERT), lambda b, i, e: (e, 0, 0)),
                  pl.BlockSpec((n, D_EXPERT, D), lambda b, i, e: (e, 0, 0))],
        out_specs=tile(D),
        out_shape=jax.ShapeDtypeStruct((B, S, D), F32),
        compiler_params=_params(("parallel", "parallel", "arbitrary")),
        name="moe",
    )(x, h, gates, mod_l, w_gate, w_up, w_down)


def kernel(x, c, positions, ada_w, ada_b, norm_mix_g, norm_ffn_g, w_in, w_out, nsa_q_norm, nsa_k_norm, nsa_cmp_pe, nsa_cmp_w1, nsa_cmp_w2, pool_w, pool_scale, rwkv_mu, rwkv_w0, rwkv_w2, rwkv_a0, rwkv_a2, rwkv_g2, rwkv_k_k, rwkv_k_a, rwkv_r_k, rwkv_ln_w, rwkv_ln_b, rwkv_v0, rwkv_v1, rwkv_v2, dil_q_norm, dil_k_norm, router_w, router_b, moe_w_gate, moe_w_up, moe_w_down):
    depth = ada_w.shape[0]
    mod = _modulation(c, ada_w, ada_b)
    cos, sin, ccos, csin = _rope_tables(positions)
    v_first = None
    for l in range(depth):
        q_a, kvc_a, kv_a, gl, u_pool, z_rwkv, q_d, kv_d = _in_proj(x, mod[l], norm_mix_g[l], w_in[l])
        y_nsa = _nsa(q_a, kvc_a, kv_a, gl, cos, sin, ccos, csin, nsa_q_norm[l], nsa_k_norm[l],
                     nsa_cmp_pe[l], nsa_cmp_w1[l], nsa_cmp_w2[l])
        y_pool = _pool(u_pool, pool_w[l], pool_scale[l])
        v_res = None if l == 0 else (rwkv_v0[l - 1], rwkv_v1[l - 1], rwkv_v2[l - 1])
        y_rwkv, v_first = _rwkv(z_rwkv, v_first, v_res, rwkv_mu[l], rwkv_w0[l], rwkv_w2[l], rwkv_a0[l], rwkv_a2[l],
                                rwkv_g2[l], rwkv_k_k[l], rwkv_k_a[l], rwkv_r_k[l], rwkv_ln_w[l], rwkv_ln_b[l])
        y_dil = _dilated(q_d, kv_d, cos, sin, dil_q_norm[l], dil_k_norm[l])
        x_mid, h2, gates = _out_proj(x, (y_nsa, y_pool, y_rwkv, y_dil), w_out[l], mod[l], norm_ffn_g[l],
                                     router_w, router_b)
        x = _moe(x_mid, h2, gates, mod[l], moe_w_gate[l], moe_w_up[l], moe_w_down[l])
    return x
```

```python
import functools

import numpy as np
import jax
import jax.numpy as jnp
from jax import lax
from jax.experimental import pallas as pl
from jax.experimental.pallas import tpu as pltpu

F32 = jnp.float32
BF16 = jnp.bfloat16
HI = lax.Precision.HIGHEST

D_MODEL = 1024
HEAD_DIM = 64
GROUP_WIDTH = 256
N_HEADS = 4
RMS_EPS = 1e-6
ROPE_THETA = 500000.0
ROPE_DIM = 16
CMP_BLOCK = 32
CMP_STRIDE = 16
SEL_BLOCK = 64
SEL_TOP_N = 16
WINDOW = 512
FORCE_SCORE = 1e4
POOL_WINDOWS = (2, 4, 8, 16)
RWKV_GN_EPS = 64e-5
DIL_PATTERNS = ((128, 1), (512, 4), (2048, 16))
N_EXPERTS = 16
D_EXPERT = 256
NEG_INF = -1e30
LOG2E = 1.4426950408889634
TINY = 1e-30

LANES = 128
TQ = 256
DIL_TQ = 512
TK = 512
RWKV_CHUNK = 64
RWKV_GROUP = 8
VMEM_LIMIT = 56 * 1024 * 1024

_IN_COLS = dict(q=(0, 256), kvc=(256, 384), kv=(384, 640), gl=(640, 768), pool=(768, 1024),
                rwkv=(1024, 2304), dq=(2304, 2560), dkv=(2560, 3072))
IN_PACKED = 3072
RWKV_PACKED = 1280


def _dot(a, b):
    return jnp.dot(a.astype(BF16), b.astype(BF16), preferred_element_type=F32)


def _dot_hi(a, b):
    return jnp.dot(a, b, precision=HI, preferred_element_type=F32)


def _dot_nt(a, b):
    return lax.dot_general(a, b, (((1,), (1,)), ((), ())), preferred_element_type=F32)


def _split(x):
    hi = x.astype(BF16)
    return hi, (x - hi.astype(F32)).astype(BF16)


def _dot_split(a, b):
    hi, lo = _split(a)
    b = b.astype(BF16)
    return jnp.dot(hi, b, preferred_element_type=F32) + jnp.dot(lo, b, preferred_element_type=F32)


def _dot3(a, b, nt=False):
    f = _dot_nt if nt else functools.partial(jnp.dot, preferred_element_type=F32)
    a_hi, a_lo = _split(a)
    b_hi, b_lo = _split(b)
    return f(a_hi, b_hi) + f(a_lo, b_hi) + f(a_hi, b_lo)


def _sigmoid(x):
    return 1.0 / (1.0 + jnp.exp(-x))


def _iota(shape, dim):
    return lax.broadcasted_iota(jnp.int32, shape, dim)


def _rope_lanes(y, cos, sin):
    n = y.shape[-1]
    lane = _iota(y.shape, y.ndim - 1)
    partner = jnp.where((lane & 15) < 8, pltpu.roll(y, n - 8, y.ndim - 1), pltpu.roll(y, 8, y.ndim - 1))
    return y * cos + partner * sin


def _params(sem, vmem=VMEM_LIMIT):
    return pltpu.CompilerParams(dimension_semantics=sem, vmem_limit_bytes=vmem)


def _mod_kernel(c_ref, w_ref, b_ref, o_ref):
    c = c_ref[...]
    o_ref[...] = _dot_hi(c * _sigmoid(c), w_ref[...]) + b_ref[...]


def _modulation(c, ada_w, ada_b):
    L, D, E = ada_w.shape
    B = c.shape[0]
    tn = 1024
    out = pl.pallas_call(
        _mod_kernel,
        grid=(L, E // tn),
        in_specs=[pl.BlockSpec((B, D), lambda l, j: (0, 0)),
                  pl.BlockSpec((None, D, tn), lambda l, j: (l, 0, j)),
                  pl.BlockSpec((None, 1, tn), lambda l, j: (l, 0, j))],
        out_specs=pl.BlockSpec((None, B, tn), lambda l, j: (l, 0, j)),
        out_shape=jax.ShapeDtypeStruct((L, B, E), F32),
        compiler_params=_params(("parallel", "parallel")),
        name="modulation",
    )(c, ada_w, ada_b.reshape(L, 1, E))
    return out.reshape(L, B, 6, D)


def _rope_kernel(pos_ref, cpos_ref, freq_ref, sgn_ref, cos_ref, sin_ref, ccos_ref, csin_ref):
    freq = freq_ref[...]
    sgn = sgn_ref[...]
    ang = pos_ref[...].astype(F32) * freq
    cos_ref[...] = jnp.cos(ang)
    sin_ref[...] = jnp.sin(ang) * sgn
    cang = cpos_ref[...].astype(F32) * freq
    ccos_ref[...] = jnp.cos(cang)
    csin_ref[...] = jnp.sin(cang) * sgn


def _rope_tables(positions):
    B, S = positions.shape
    half = ROPE_DIM // 2
    inv_freq = ROPE_THETA ** (-2.0 * jnp.arange(half, dtype=F32) / ROPE_DIM)
    lane = np.arange(LANES)
    rot = (lane % HEAD_DIM) < ROPE_DIM
    freq = jnp.where(rot, inv_freq[lane % half], 0.0).reshape(1, LANES)
    sgn = jnp.asarray(np.where(rot, np.where(lane % HEAD_DIM < half, -1.0, 1.0), 0.0), F32).reshape(1, LANES)
    n_cmp = S // CMP_STRIDE
    cpos = positions[:, CMP_BLOCK - 1::CMP_STRIDE]
    cpos = jnp.pad(cpos, ((0, 0), (0, n_cmp - cpos.shape[1])))
    tab = jax.ShapeDtypeStruct((B, S, LANES), F32)
    ctab = jax.ShapeDtypeStruct((B, n_cmp, LANES), F32)
    return pl.pallas_call(
        _rope_kernel,
        grid=(B,),
        in_specs=[pl.BlockSpec((None, S, 1), lambda b: (b, 0, 0)),
                  pl.BlockSpec((None, n_cmp, 1), lambda b: (b, 0, 0)),
                  pl.BlockSpec((1, LANES), lambda b: (0, 0)),
                  pl.BlockSpec((1, LANES), lambda b: (0, 0))],
        out_specs=[pl.BlockSpec((None, S, LANES), lambda b: (b, 0, 0)),
                   pl.BlockSpec((None, S, LANES), lambda b: (b, 0, 0)),
                   pl.BlockSpec((None, n_cmp, LANES), lambda b: (b, 0, 0)),
                   pl.BlockSpec((None, n_cmp, LANES), lambda b: (b, 0, 0))],
        out_shape=[tab, tab, ctab, ctab],
        compiler_params=_params(("parallel",)),
        name="rope_tables",
    )(positions.reshape(B, S, 1), cpos.reshape(B, n_cmp, 1), freq, sgn)


_W_IN_MOVES = ((0, 652, 0), (652, 908, 768), (908, 1676, 1024), (1676, 1740, 1792), (1740, 1804, 1920),
               (1804, 1964, 2048), (1964, 2732, 2304))


def _in_kernel(x_ref, mod_ref, g_ref, w_ref, *refs):
    out_refs, wp_ref = refs[:-1], refs[-1]

    @pl.when((pl.program_id(0) == 0) & (pl.program_id(1) == 0))
    def _pack_weights():
        rows = 128

        def body(r, _):
            r0 = pl.multiple_of(r * rows, rows)
            w = w_ref[pl.ds(r0, rows), :]
            wp_ref[pl.ds(r0, rows), :] = jnp.zeros((rows, IN_PACKED), BF16)
            for lo, hi, dst in _W_IN_MOVES:
                wp_ref[pl.ds(r0, rows), dst:dst + hi - lo] = w[:, lo:hi].astype(BF16)
            return 0

        lax.fori_loop(0, w_ref.shape[0] // rows, body, 0)

    x = x_ref[...]
    ms = jnp.mean(x * x, axis=-1, keepdims=True)
    y = x * lax.rsqrt(ms + RMS_EPS) * g_ref[...]
    h = y * (1.0 + mod_ref[1:2, :]) + mod_ref[0:1, :]
    p = jnp.dot(h.astype(BF16), wp_ref[...], preferred_element_type=F32)
    for ref, (lo, hi) in zip(out_refs, _IN_COLS.values()):
        ref[...] = p[:, lo:hi]


def _in_proj(x, mod_l, g, w_in, layer):
    B, S, D = x.shape
    tm = 512
    widths = [hi - lo for lo, hi in _IN_COLS.values()]
    return pl.pallas_call(
        _in_kernel,
        grid=(B, S // tm),
        in_specs=[pl.BlockSpec((None, tm, D), lambda b, i: (b, i, 0)),
                  pl.BlockSpec((None, 6, D), lambda b, i: (b, 0, 0)),
                  pl.BlockSpec((1, D), lambda b, i: (0, 0)),
                  pl.BlockSpec((None,) + w_in.shape[1:], lambda b, i: (layer, 0, 0))],
        out_specs=[pl.BlockSpec((None, tm, w), lambda b, i: (b, i, 0)) for w in widths],
        out_shape=[jax.ShapeDtypeStruct((B, S, w), F32) for w in widths],
        scratch_shapes=[pltpu.VMEM((D, IN_PACKED), BF16)],
        compiler_params=_params(("arbitrary", "arbitrary")),
        name="in_proj",
    )(x, mod_l, g.reshape(1, D), w_in)


def _pack_rwkv_row(v):
    z = lambda n: jnp.zeros((n,), v.dtype)
    return jnp.concatenate([v[0:768], v[768:832], z(64), v[832:896], z(64), v[896:1056], z(96)]).reshape(1, RWKV_PACKED)


def _pool_kernel(u_ref, w_ref, scale_ref, o_ref):
    u = u_ref[...]
    row = _iota(u.shape, 0)
    lane = _iota(u.shape, 1)

    def lag(x, s):
        return jnp.where(row >= s, pltpu.roll(x, s, 0), 0.0)

    s2 = u + lag(u, 1)
    s4 = s2 + lag(s2, 2)
    s8 = s4 + lag(s4, 4)
    s16 = s8 + lag(s8, 8)
    g = lane // (GROUP_WIDTH // len(POOL_WINDOWS))
    tot = jnp.where(g == 0, s2, jnp.where(g == 1, s4, jnp.where(g == 2, s8, s16)))
    win = jnp.where(g == 0, 2.0, jnp.where(g == 1, 4.0, jnp.where(g == 2, 8.0, 16.0)))
    cnt = jnp.minimum((row + 1).astype(F32), win)
    o_ref[...] = _dot(tot / cnt - u, w_ref[...]) * scale_ref[...]


def _pool(u, pool_w, pool_scale):
    B, S, W = u.shape
    wbd = jax.scipy.linalg.block_diag(*[pool_w[i] for i in range(pool_w.shape[0])]).astype(BF16)
    return pl.pallas_call(
        _pool_kernel,
        grid=(B,),
        in_specs=[pl.BlockSpec((None, S, W), lambda b: (b, 0, 0)),
                  pl.BlockSpec((W, W), lambda b: (0, 0)),
                  pl.BlockSpec((1, W), lambda b: (0, 0))],
        out_specs=pl.BlockSpec((None, S, W), lambda b: (b, 0, 0)),
        out_shape=jax.ShapeDtypeStruct((B, S, W), F32),
        compiler_params=_params(("parallel",)),
        name="pool_mixer",
    )(u, wbd, pool_scale.reshape(1, W))


def _head_mean_matrix():
    blk = np.kron(np.eye(N_HEADS), np.ones((HEAD_DIM, HEAD_DIM))) / HEAD_DIM
    return jnp.asarray(blk, F32)


def _pad_heads(x256, as_bf16=True):
    n = x256.shape[0]
    lane = _iota((n, LANES), 1)
    out = []
    for h in range(N_HEADS):
        blk = x256[:, (h // 2) * LANES:(h // 2 + 1) * LANES]
        if h % 2:
            blk = pltpu.roll(blk, HEAD_DIM, 1)
        blk = jnp.where(lane < HEAD_DIM, blk, 0.0)
        out.append(blk.astype(BF16) if as_bf16 else blk)
    return out


def _merge_heads(slabs):
    n = slabs[0].shape[0]
    lane = _iota((n, LANES), 1)
    cols = []
    for p in range(N_HEADS // 2):
        even = pltpu.roll(slabs[2 * p], HEAD_DIM, 1)
        cols.append(jnp.where(lane < HEAD_DIM, even, slabs[2 * p + 1]))
    return jnp.concatenate(cols, axis=1)


def _flash_step(carries, qs, kas, vbs, biases):
    idx = range(len(qs))
    s = [_dot_nt(qs[i], kas[i]) for i in idx]
    if biases is not None:
        s = [s[i] + biases[i] for i in idx]
    m_new = [jnp.maximum(carries[i][0], jnp.max(s[i], axis=-1, keepdims=True)) for i in idx]
    alpha = [jnp.exp2(carries[i][0] - m_new[i]) for i in idx]
    p = [jnp.exp2(s[i] - m_new[i]).astype(BF16) for i in idx]
    pv = [jnp.dot(p[i], vbs[i], preferred_element_type=F32) for i in idx]
    return tuple((m_new[i], alpha[i] * carries[i][1] + pv[i]) for i in idx)


def _flash_init(n):
    return (jnp.full((n, 1), NEG_INF, F32), jnp.zeros((n, LANES), F32))


def _flash_finish(carry):
    acc = carry[1]
    return acc / pltpu.roll(acc, HEAD_DIM, 1)


def _nsa_kernel(q_ref, gl_ref, kc_ref, kv_ref, cos_ref, sin_ref, ccos_ref, csin_ref,
                qn_ref, kn_ref, pet_ref, peb_ref, wt_ref, wb_ref, w2_ref,
                hb_ref, cover_ref, eg_ref, ctab_ref, wtab_ref,
                o_ref, ksvs_ref, kwvw_ref, vs1_ref, vw1_ref, kvc_ref):
    S = kv_ref.shape[0]
    qi = pl.program_id(1)
    n_cmp = S // CMP_STRIDE

    @pl.when(qi == 0)
    def _prep_keys():
        rows = 256

        def body(i, _):
            r0 = pl.multiple_of(i * rows, rows)
            cs = cos_ref[pl.ds(r0, rows), :]
            sn = sin_ref[pl.ds(r0, rows), :]
            lane = _iota((rows, LANES), 1)
            first = lane < HEAD_DIM
            blk_hot = jnp.where(lane - HEAD_DIM == (r0 + _iota((rows, LANES), 0)) // SEL_BLOCK, 1.0, 0.0)
            for col, j, dst, dst1 in ((0, 1, ksvs_ref, vs1_ref), (LANES, 2, kwvw_ref, vw1_ref)):
                slab = kv_ref[pl.ds(r0, rows), col:col + LANES]
                ms = jnp.sum(jnp.where(first, slab * slab, 0.0), axis=-1, keepdims=True) * (1.0 / HEAD_DIM)
                y = slab * lax.rsqrt(ms + RMS_EPS) * kn_ref[j:j + 1, :]
                dst[pl.ds(r0, rows), :] = jnp.where(first, _rope_lanes(y, cs, sn), blk_hot if j == 1 else slab).astype(BF16)
                dst1[pl.ds(r0, rows), :] = jnp.where(first, 1.0, slab).astype(BF16)
            return 0

        lax.fori_loop(0, S // rows, body, 0)

        top = jnp.zeros((n_cmp, LANES), F32)
        bot = jnp.zeros((n_cmp, LANES), F32)
        for i in range(CMP_STRIDE):
            tok = kc_ref[pl.ds(i, n_cmp, stride=CMP_STRIDE), :]
            cols = slice(i * LANES, (i + 1) * LANES)
            top = top + _dot3(tok + pet_ref[:, cols], wt_ref[cols, :])
            bot = bot + _dot3(tok + peb_ref[:, cols], wb_ref[cols, :])
        pre = top + pltpu.roll(bot, n_cmp - 1, 0)
        act = 0.5 * pre * (1.0 + jnp.tanh(float(np.sqrt(2.0 / np.pi)) * (pre + 0.044715 * (pre * pre * pre))))
        out = _dot3(act, w2_ref[...])
        first = _iota(out.shape, 1) < HEAD_DIM
        ms = jnp.sum(jnp.where(first, out * out, 0.0), axis=-1, keepdims=True) * (1.0 / HEAD_DIM)
        y = out * lax.rsqrt(ms + RMS_EPS) * kn_ref[0:1, :]
        kvc_ref[...] = jnp.where(first, _rope_lanes(y, ccos_ref[...], csin_ref[...]), out)

    r0 = pl.multiple_of(qi * TQ, TQ)
    q = q_ref[...]
    ms = _dot_split(q * q, hb_ref[...])
    qn = q * lax.rsqrt(ms + RMS_EPS) * qn_ref[...]
    cs = cos_ref[pl.ds(r0, TQ), :]
    sn = sin_ref[pl.ds(r0, TQ), :]
    qr = _rope_lanes(qn, jnp.concatenate([cs, cs], axis=1), jnp.concatenate([sn, sn], axis=1)) * (HEAD_DIM ** -0.5)
    qst = jnp.concatenate(_pad_heads(qr, as_bf16=False), axis=0)
    n_rows = N_HEADS * TQ

    kvc = kvc_ref[...]
    s = _dot3(qst, kvc, nt=True)
    t_st = r0 + (_iota((n_rows, n_cmp), 0) & (TQ - 1))
    c_end = _iota((n_rows, n_cmp), 1) * CMP_STRIDE + (CMP_BLOCK - 1)
    mask = c_end <= t_st
    sm = jnp.where(mask, s, NEG_INF)
    m = jnp.max(sm, axis=-1, keepdims=True)
    e = jnp.where(mask, jnp.exp(sm - m), 0.0)
    p = e / jnp.maximum(jnp.sum(e, axis=-1, keepdims=True), TINY)
    o_cmp = _dot(p, kvc)

    psum = p[0:TQ] + p[TQ:2 * TQ] + p[2 * TQ:3 * TQ] + p[3 * TQ:4 * TQ]
    imp = _dot_split(psum, cover_ref[...])
    n_blk = S // SEL_BLOCK
    lane = _iota((TQ, LANES), 1)
    blk = lane & (n_blk - 1)
    tq = r0 + _iota((TQ, LANES), 0)
    cur = tq // SEL_BLOCK
    forced = (blk == 0) | (blk == cur) | (blk == cur - 1)
    imp = jnp.where(blk > cur, -1.0, jnp.where(forced, FORCE_SCORE, imp))
    rank = jnp.zeros((TQ, LANES), F32)
    for d in range(1, n_blk):
        other = pltpu.roll(imp, LANES - d, 1)
        rank = rank + jnp.where(blk + d >= n_blk,
                                jnp.where(other >= imp, 1.0, 0.0),
                                jnp.where(other > imp, 1.0, 0.0))
    sel_bias = jnp.where((lane >= HEAD_DIM) & (lane < HEAD_DIM + n_blk), jnp.where(rank < SEL_TOP_N, 0.0, NEG_INF), 0.0)
    sel_bias = sel_bias.astype(BF16)

    qb = (qst * LOG2E).astype(BF16)
    heads = range(N_HEADS)
    q_heads = [qb[h * TQ:(h + 1) * TQ] for h in heads]
    q_heads_sel = [jnp.where(lane < HEAD_DIM, q, sel_bias) for q in q_heads]
    init = tuple(_flash_init(TQ) for _ in heads)

    def sel_tile(j, carry, biases):
        k0 = pl.multiple_of(j * TK, TK)
        ka, vb = ksvs_ref[pl.ds(k0, TK), :], vs1_ref[pl.ds(k0, TK), :]
        return _flash_step(carry, q_heads_sel, [ka] * N_HEADS, [vb] * N_HEADS, biases)

    ratio = TK // TQ
    last = qi // ratio
    res = lax.fori_loop(0, last, lambda j, c: sel_tile(j, c, None), init)
    res = sel_tile(last, res, [ctab_ref[qi % ratio]] * N_HEADS)
    o_sel = jnp.concatenate([_flash_finish(c) for c in res], axis=0)

    n_prev = WINDOW // TQ
    span = (n_prev + 1) * TQ
    k0 = pl.multiple_of(jnp.maximum(qi - n_prev, 0) * TQ, TQ)
    ka, vb = kwvw_ref[pl.ds(k0, span), :], vw1_ref[pl.ds(k0, span), :]
    res = _flash_step(init, q_heads, [ka] * N_HEADS, [vb] * N_HEADS, [wtab_ref[jnp.minimum(qi, n_prev)]] * N_HEADS)
    o_win = jnp.concatenate([_flash_finish(c) for c in res], axis=0)

    gate = _dot_split(_sigmoid(gl_ref[...]), eg_ref[...])
    split = lambda o: _merge_heads([o[h * TQ:(h + 1) * TQ] for h in range(N_HEADS)])
    W = GROUP_WIDTH
    o_ref[...] = (gate[:, 0:W] * split(o_cmp) + gate[:, W:2 * W] * split(o_sel) + gate[:, 2 * W:3 * W] * split(o_win))


def _nsa(q, kvc, kv, gl, cos, sin, ccos, csin, q_norm, k_norm, cmp_pe, cmp_w1, cmp_w2):
    B, S, _ = q.shape
    n_cmp = S // CMP_STRIDE
    n_blk = S // SEL_BLOCK
    hd = HEAD_DIM

    qn = jnp.tile(q_norm, N_HEADS).reshape(1, GROUP_WIDTH)
    kn = jnp.concatenate([k_norm, jnp.ones_like(k_norm)], axis=1)

    def interleave(a, b):
        return jnp.concatenate([a, b], axis=1).reshape(1, -1)

    half = CMP_BLOCK // 2
    pet = interleave(cmp_pe[0, :half], cmp_pe[1, :half])
    peb = interleave(cmp_pe[0, half:], cmp_pe[1, half:])

    def w1_half(lo):
        wk = cmp_w1[0].reshape(CMP_BLOCK, hd, hd)[lo:lo + half]
        wv = cmp_w1[1].reshape(CMP_BLOCK, hd, hd)[lo:lo + half]
        z = jnp.zeros_like(wk)
        top = jnp.concatenate([wk, z], axis=2)
        bot = jnp.concatenate([z, wv], axis=2)
        return jnp.concatenate([top, bot], axis=1).reshape(half * 2 * hd, 2 * hd)

    wt, wb = w1_half(0), w1_half(half)
    w2 = jax.scipy.linalg.block_diag(cmp_w2[0], cmp_w2[1])
    hb = _head_mean_matrix()

    c_end = np.arange(n_cmp) * CMP_STRIDE + CMP_BLOCK - 1
    b_start = np.arange(n_blk) * SEL_BLOCK
    cover = np.maximum(np.minimum(c_end[:, None] + 1, b_start[None, :] + SEL_BLOCK)
                       - np.maximum(c_end[:, None] + 1 - CMP_BLOCK, b_start[None, :]), 0).astype(np.float32) / CMP_BLOCK
    cover[n_cmp - 1] = 0.0
    cover = jnp.asarray(np.tile(cover, (1, LANES // n_blk)), F32)
    eg = np.zeros((LANES, 3 * GROUP_WIDTH), np.float32)
    for h in range(N_HEADS):
        for br in range(3):
            eg[h * 3 + br, br * GROUP_WIDTH + h * hd: br * GROUP_WIDTH + (h + 1) * hd] = 1.0
    eg = jnp.asarray(eg)
    ratio = TK // TQ
    ctab = np.where(np.arange(TK)[None, None, :] <= np.arange(ratio)[:, None, None] * TQ + np.arange(TQ)[None, :, None],
                    0.0, NEG_INF).astype(np.float32)
    n_prev = WINDOW // TQ
    span = (n_prev + 1) * TQ
    dist = (np.arange(n_prev + 1)[:, None, None] * TQ + np.arange(TQ)[None, :, None] - np.arange(span)[None, None, :])
    wtab = np.where((dist >= 0) & (dist < WINDOW), 0.0, NEG_INF).astype(np.float32)
    ctab, wtab = jnp.asarray(ctab), jnp.asarray(wtab)

    full = lambda shape: pl.BlockSpec(shape, lambda b, i: (0,) * len(shape))
    return pl.pallas_call(
        _nsa_kernel,
        grid=(B, S // TQ),
        in_specs=[pl.BlockSpec((None, TQ, GROUP_WIDTH), lambda b, i: (b, i, 0)),
                  pl.BlockSpec((None, TQ, LANES), lambda b, i: (b, i, 0)),
                  pl.BlockSpec((None, S, LANES), lambda b, i: (b, 0, 0)),
                  pl.BlockSpec((None, S, 2 * LANES), lambda b, i: (b, 0, 0)),
                  pl.BlockSpec((None, S, LANES), lambda b, i: (b, 0, 0)),
                  pl.BlockSpec((None, S, LANES), lambda b, i: (b, 0, 0)),
                  pl.BlockSpec((None, n_cmp, LANES), lambda b, i: (b, 0, 0)),
                  pl.BlockSpec((None, n_cmp, LANES), lambda b, i: (b, 0, 0)),
                  full((1, GROUP_WIDTH)), full((3, LANES)),
                  full((1, CMP_STRIDE * LANES)), full((1, CMP_STRIDE * LANES)),
                  full((CMP_STRIDE * LANES, LANES)), full((CMP_STRIDE * LANES, LANES)), full((LANES, LANES)),
                  full((GROUP_WIDTH, GROUP_WIDTH)), full((n_cmp, LANES)),
                  full((LANES, 3 * GROUP_WIDTH)), full(ctab.shape), full(wtab.shape)],
        out_specs=pl.BlockSpec((None, TQ, GROUP_WIDTH), lambda b, i: (b, i, 0)),
        out_shape=jax.ShapeDtypeStruct((B, S, GROUP_WIDTH), F32),
        scratch_shapes=[pltpu.VMEM((S, LANES), BF16), pltpu.VMEM((S, LANES), BF16),
                        pltpu.VMEM((S, LANES), BF16), pltpu.VMEM((S, LANES), BF16),
                        pltpu.VMEM((n_cmp, LANES), F32)],
        compiler_params=_params(("parallel", "arbitrary")),
        name="nsa",
    )(q, gl, kvc, kv, cos, sin, ccos, csin, qn, kn, pet, peb, wt, wb, w2, hb, cover, eg, ctab, wtab)


def _dil_kernel(q_ref, kv_ref, cos_ref, sin_ref, qn_ref, kn_ref, hb_ref, tb_ref, o_ref, kvs_ref, v1s_ref):
    S = kv_ref.shape[0]
    qi = pl.program_id(1)
    W = GROUP_WIDTH

    @pl.when(qi == 0)
    def _prep_keys():
        rows = 256

        def body(i, _):
            r0 = pl.multiple_of(i * rows, rows)
            cs = cos_ref[pl.ds(r0, rows), :]
            sn = sin_ref[pl.ds(r0, rows), :]
            k = kv_ref[pl.ds(r0, rows), 0:W]
            v = kv_ref[pl.ds(r0, rows), W:2 * W]
            ms = _dot_split(k * k, hb_ref[...])
            kn = k * lax.rsqrt(ms + RMS_EPS) * kn_ref[...]
            kr = _rope_lanes(kn, jnp.concatenate([cs, cs], axis=1), jnp.concatenate([sn, sn], axis=1))
            lane = _iota((rows, LANES), 1)
            for h in range(N_HEADS):
                kb = kr[:, (h // 2) * LANES:(h // 2 + 1) * LANES]
                vb = v[:, (h // 2) * LANES:(h // 2 + 1) * LANES]
                if h % 2:
                    kb = pltpu.roll(kb, HEAD_DIM, 1)
                else:
                    vb = pltpu.roll(vb, HEAD_DIM, 1)
                kvs_ref[h, pl.ds(r0, rows), :] = jnp.where(lane < HEAD_DIM, kb, vb).astype(BF16)
                v1s_ref[h, pl.ds(r0, rows), :] = jnp.where(lane < HEAD_DIM, 1.0, vb).astype(BF16)
            return 0

        lax.fori_loop(0, S // rows, body, 0)

    r0 = pl.multiple_of(qi * DIL_TQ, DIL_TQ)
    q = q_ref[...]
    ms = _dot_split(q * q, hb_ref[...])
    qn = q * lax.rsqrt(ms + RMS_EPS) * qn_ref[...]
    cs = cos_ref[pl.ds(r0, DIL_TQ), :]
    sn = sin_ref[pl.ds(r0, DIL_TQ), :]
    qr = _rope_lanes(qn, jnp.concatenate([cs, cs], axis=1), jnp.concatenate([sn, sn], axis=1)) * (HEAD_DIM ** -0.5 * LOG2E)
    qh = _pad_heads(qr)
    ratio = TK // DIL_TQ

    def body(j, carry):
        k0 = pl.multiple_of(j * TK, TK)
        bias = tb_ref[qi - ratio * j]
        return _flash_step(carry, qh, [kvs_ref[h, pl.ds(k0, TK), :] for h in range(N_HEADS)],
                           [v1s_ref[h, pl.ds(k0, TK), :] for h in range(N_HEADS)], [bias] * N_HEADS)

    res = lax.fori_loop(0, qi // ratio + 1, body, tuple(_flash_init(DIL_TQ) for _ in range(N_HEADS)))
    o_ref[...] = _merge_heads([_flash_finish(c) for c in res])


def _dil_bias_table(S):
    n = S // DIL_TQ
    d = (np.arange(n)[:, None, None] * DIL_TQ + np.arange(DIL_TQ)[None, :, None] - np.arange(TK)[None, None, :])
    cnt = np.zeros(d.shape, np.float32)
    for window, dil in DIL_PATTERNS:
        cnt += ((d >= 0) & (d <= window) & (d % dil == 0)).astype(np.float32)
    return jnp.asarray(np.where(cnt > 0, np.log2(np.maximum(cnt, 1.0)), NEG_INF).astype(np.float32))


def _dilated(q, kv, cos, sin, q_norm, k_norm):
    B, S, W = q.shape
    tb = _dil_bias_table(S)
    qn = jnp.tile(q_norm, N_HEADS).reshape(1, W)
    kn = jnp.tile(k_norm, N_HEADS).reshape(1, W)
    hb = _head_mean_matrix()
    full = lambda shape: pl.BlockSpec(shape, lambda b, i: (0,) * len(shape))
    return pl.pallas_call(
        _dil_kernel,
        grid=(B, S // DIL_TQ),
        in_specs=[pl.BlockSpec((None, DIL_TQ, W), lambda b, i: (b, i, 0)),
                  pl.BlockSpec((None, S, 2 * W), lambda b, i: (b, 0, 0)),
                  pl.BlockSpec((None, S, LANES), lambda b, i: (b, 0, 0)),
                  pl.BlockSpec((None, S, LANES), lambda b, i: (b, 0, 0)),
                  full((1, W)), full((1, W)), full((W, W)), full(tb.shape)],
        out_specs=pl.BlockSpec((None, DIL_TQ, W), lambda b, i: (b, i, 0)),
        out_shape=jax.ShapeDtypeStruct((B, S, W), F32),
        scratch_shapes=[pltpu.VMEM((N_HEADS, S, LANES), BF16), pltpu.VMEM((N_HEADS, S, LANES), BF16)],
        compiler_params=_params(("parallel", "arbitrary")),
        name="dilated",
    )(q, kv, cos, sin, qn, kn, hb, tb)


def _softplus(x):
    return jnp.maximum(x, 0.0) + jnp.log(1.0 + jnp.exp(-jnp.abs(x)))


def _rwkv_kernel(first, *refs):
    if first:
        (z_ref, mu_ref, w0_ref, w2_ref, a0_ref, a2_ref, g2_ref, kk_ref, ka_ref, rk_ref, lnw_ref, lnb_ref,
         hb_ref, y_ref, vf_out_ref, st_ref, prev_ref, ro_ref, ub_ref, colb_ref, g_ref, t2_ref) = refs
    else:
        (z_ref, vf_ref, mu_ref, w0_ref, w2_ref, a0_ref, a2_ref, g2_ref, kk_ref, ka_ref, rk_ref, lnw_ref, lnb_ref,
         v0_ref, v1_ref, v2_ref, hb_ref, y_ref, st_ref, prev_ref, ro_ref, ub_ref, colb_ref, g_ref, t2_ref) = refs
    SB = z_ref.shape[0]
    C = RWKV_CHUNK
    W = GROUP_WIDTH
    n_pairs = N_HEADS // 2
    R = RWKV_GROUP * C

    @pl.when(pl.program_id(1) == 0)
    def _reset():
        st_ref[...] = jnp.zeros_like(st_ref)
        prev_ref[...] = jnp.zeros_like(prev_ref)

    ri = _iota((C, C), 0)
    ci = _iota((C, C), 1)
    tri_incl = jnp.where(ci <= ri, 1.0, 0.0).astype(BF16)
    rr = _iota((2 * LANES, 2 * LANES), 0)
    cc = _iota((2 * LANES, 2 * LANES), 1)
    g_mask = ((((rr >> 6) & 1) == ((cc >> 6) & 1))
              & ((cc & (C - 1)) <= jnp.where(rr < LANES, (rr & (C - 1)) - 1, rr & (C - 1))))
    r2 = _iota((LANES, LANES), 0)
    c2 = _iota((LANES, LANES), 1)
    eye_f = jnp.where(r2 == c2, 1.0, 0.0)
    lane_lo = _iota((C, LANES), 1) < HEAD_DIM
    hb = hb_ref[...].astype(BF16)

    def stack(x):
        return jnp.concatenate([jnp.where(lane_lo, x, 0.0), jnp.where(lane_lo, 0.0, x)], axis=0)

    def phase1(i, _):
        t0 = pl.multiple_of(i * R, R)
        zc = z_ref[pl.ds(t0, R), :]
        inside = jnp.where(i > 0, 1.0, 0.0)
        prev = inside * z_ref[pl.ds(jnp.maximum(t0 - 1, 0), 1), :] + (1.0 - inside) * prev_ref[...]
        zp = jnp.where(_iota(zc.shape, 0) == 0, prev, pltpu.roll(zc, 1, 0))
        zs = zc + (zp - zc) * mu_ref[...]
        r, k, v = zs[:, 0:W], zs[:, W:2 * W], zs[:, 2 * W:3 * W]
        xw, xa, xg = zs[:, 768:896], zs[:, 896:1024], zs[:, 1024:1280]
        w_log = -_softplus(-(w0_ref[...] + _dot(jnp.tanh(xw), w2_ref[...]))) - 0.5
        lw = -jnp.exp(w_log)
        a = _sigmoid(a0_ref[...] + _dot(xa, a2_ref[...]))
        g = _dot(_sigmoid(xg), g2_ref[...])
        if first:
            vf_out_ref[pl.ds(t0, R), :] = v
        else:
            vf = vf_ref[pl.ds(t0, R), :]
            v = v + (vf - v) * _sigmoid(v0_ref[...] + _dot(_dot(v, v1_ref[...]), v2_ref[...]))
        kkr = k * kk_ref[...]
        nrm = jnp.sqrt(_dot_split(kkr * kkr, hb) * HEAD_DIM)
        kk = kkr / jnp.maximum(nrm, 1e-12)
        k2 = k * (1.0 + (a - 1.0) * ka_ref[...])
        bonus = _dot_split(r * k2 * rk_ref[...], hb) * HEAD_DIM * v
        g_ref[pl.ds(t0, R), :] = g
        t2_ref[pl.ds(t0, R), :] = (lnb_ref[...] + bonus) * g

        items = [(c, p) for c in range(RWKV_GROUP) for p in range(n_pairs)]
        idx = range(len(items))
        sub = lambda x, c, p: x[c * C:(c + 1) * C, p * LANES:(p + 1) * LANES]
        cum = [_cumsum_rows(tri_incl, sub(lw, c, p)) for c, p in items]
        a_st, r_st, v_st, gm, tails = [], [], [], [], []
        for j, (c, p) in enumerate(items):
            lwp = sub(lw, c, p)
            cum_end = cum[j][C - 1:C, :]
            e_pos, e_neg = jnp.exp(cum[j]), jnp.exp(-cum[j])
            e_exc, e_tail = jnp.exp(cum[j] - lwp), jnp.exp(cum_end - cum[j])
            kkp, k2p = sub(kk, c, p), sub(k2, c, p)
            ka_p = kkp * sub(a, c, p)
            a_st.append(stack(-kkp * e_exc).astype(BF16))
            r_st.append(stack(sub(r, c, p) * e_pos))
            v_st.append(stack(sub(v, c, p)).astype(BF16))
            bhat = (ka_p * e_neg).astype(BF16)
            khat = (k2p * e_neg).astype(BF16)
            tails.append(jnp.concatenate([stack(ka_p * e_tail), stack(k2p * e_tail)], axis=0))
            gm.append(jnp.where(g_mask,
                                _dot_nt(jnp.concatenate([a_st[j], r_st[j].astype(BF16)], axis=0),
                                        jnp.concatenate([bhat, bhat, khat, khat], axis=0)), 0.0))
            colb_ref[i * RWKV_GROUP + c, p] = jnp.broadcast_to(jnp.exp(cum_end), (LANES, LANES)).T
        tt = [tails[j].T.astype(BF16) for j in idx]
        lakv = [_dot(gm[j][0:LANES, LANES:2 * LANES], v_st[j]) for j in idx]
        mr = [gm[j][LANES:2 * LANES, :].astype(BF16) for j in idx]
        lp = [gm[j][0:LANES, 0:LANES] for j in idx]
        tinv = [eye_f + lp[j] for j in idx]
        for _ in range(5):
            lpb = [lp[j].astype(BF16) for j in idx]
            lp = [jnp.dot(lpb[j], lpb[j], preferred_element_type=F32) for j in idx]
            tinv = [tinv[j] + _dot(lp[j], tinv[j]) for j in idx]
        wu = [_dot(tinv[j], jnp.concatenate([a_st[j].astype(F32), lakv[j]], axis=1)) for j in idx]
        wm = [wu[j][:, 0:LANES].astype(BF16) for j in idx]
        u = [wu[j][:, LANES:2 * LANES].astype(BF16) for j in idx]
        mwbw = [jnp.dot(jnp.concatenate([mr[j][:, 0:LANES], tt[j][:, 0:LANES]], axis=0), wm[j],
                        preferred_element_type=F32) for j in idx]
        ub = [jnp.dot(jnp.concatenate([mr[j], tt[j]], axis=0), jnp.concatenate([u[j], v_st[j]], axis=0),
                      preferred_element_type=F32) for j in idx]
        for j, (c, p) in enumerate(items):
            n = i * RWKV_GROUP + c
            ro_ref[n, p] = jnp.concatenate([r_st[j] + mwbw[j][0:LANES], mwbw[j][LANES:2 * LANES]], axis=0).astype(BF16)
            ub_ref[n, p] = ub[j]
        return 0

    def phase2(n, _):
        t0 = pl.multiple_of(n * C, C)
        pairs = range(n_pairs)
        st = [st_ref[p] for p in pairs]
        res = [jnp.dot(ro_ref[n, p], st[p].astype(BF16), preferred_element_type=F32) for p in pairs]
        ub = [ub_ref[n, p] for p in pairs]
        for p in pairs:
            st_ref[p] = colb_ref[n, p] * st[p] + res[p][LANES:2 * LANES] + ub[p][LANES:2 * LANES]
        o_st = [res[p][0:LANES] + ub[p][0:LANES] for p in pairs]
        y_ref[pl.ds(t0, C), :] = jnp.concatenate([o[0:C] + o[C:2 * C] for o in o_st], axis=1)
        return 0

    lax.fori_loop(0, SB // R, phase1, 0)
    lax.fori_loop(0, SB // C, phase2, 0)
    prev_ref[...] = z_ref[SB - 1:SB, :]

    for t in range(SB // LANES):
        rows = slice(t * LANES, (t + 1) * LANES)
        o = y_ref[rows, :]
        dev = o - _dot_split(o, hb)
        var = _dot_split(dev * dev, hb)
        y_ref[rows, :] = dev * lax.rsqrt(var + RWKV_GN_EPS) * (lnw_ref[...] * g_ref[rows, :]) + t2_ref[rows, :]


def _cumsum_rows(tri_incl, x):
    hi = x.astype(BF16)
    lo = (x - hi.astype(F32)).astype(BF16)
    return jnp.dot(tri_incl, hi, preferred_element_type=F32) + jnp.dot(tri_incl, lo, preferred_element_type=F32)


RWKV_BLOCK = 512


def _rwkv(z, v_first, v_res, mu, w0, w2, a0, a2, g2, k_k, k_a, r_k, ln_w, ln_b):
    B, S, P = z.shape
    W = GROUP_WIDTH
    SB = RWKV_BLOCK
    ncb = SB // RWKV_CHUNK
    n_pairs = N_HEADS // 2
    first = v_res is None
    row = lambda v: v.reshape(1, W)
    padr = lambda m, n: jnp.pad(m, ((0, n - m.shape[0]), (0, 0)))
    seq = lambda w: pl.BlockSpec((None, SB, w), lambda b, j: (b, j, 0))
    full = lambda shape: pl.BlockSpec(shape, lambda b, j: (0,) * len(shape))
    args = [z]
    specs = [seq(P)]
    if not first:
        args.append(v_first)
        specs.append(seq(W))
    args += [_pack_rwkv_row(mu), row(w0), padr(w2, LANES), row(a0), padr(a2, LANES), padr(g2, 2 * LANES),
             row(k_k), row(k_a), r_k.reshape(1, W), row(ln_w), row(ln_b)]
    specs += [full((1, P)), full((1, W)), full((LANES, W)), full((1, W)), full((LANES, W)), full((2 * LANES, W)),
              full((1, W)), full((1, W)), full((1, W)), full((1, W)), full((1, W))]
    if not first:
        v0, v1, v2 = v_res
        args += [row(v0), jnp.pad(v1, ((0, 0), (0, LANES - v1.shape[1]))), padr(v2, LANES)]
        specs += [full((1, W)), full((W, LANES)), full((LANES, W))]
    args.append(_head_mean_matrix())
    specs.append(full((W, W)))
    out_shape = [jax.ShapeDtypeStruct((B, S, W), F32)]
    out_specs = [seq(W)]
    if first:
        out_shape.append(jax.ShapeDtypeStruct((B, S, W), F32))
        out_specs.append(seq(W))
    res = pl.pallas_call(
        functools.partial(_rwkv_kernel, first),
        grid=(B, S // SB),
        in_specs=specs,
        out_specs=out_specs,
        out_shape=out_shape,
        scratch_shapes=[pltpu.VMEM((n_pairs, LANES, LANES), F32), pltpu.VMEM((1, P), F32),
                        pltpu.VMEM((ncb, n_pairs, 2 * LANES, LANES), BF16),
                        pltpu.VMEM((ncb, n_pairs, 2 * LANES, LANES), F32),
                        pltpu.VMEM((ncb, n_pairs, LANES, LANES), F32),
                        pltpu.VMEM((SB, W), F32), pltpu.VMEM((SB, W), F32)],
        compiler_params=_params(("parallel", "arbitrary")),
        name="rwkv7_first" if first else "rwkv7",
    )(*args)
    return (res[0], res[1]) if first else (res[0], v_first)


ROUTER_ROWS = 32


def _expert_lane(e):
    return (e % 4) * 8 + e // 4


def _out_kernel(x_ref, ya_ref, yb_ref, yc_ref, yd_ref, wo_ref, mod_ref, g_ref, rwh_ref, rwl_ref, rb_ref,
                xo_ref, h_ref, gates_ref):
    y = jnp.concatenate([ya_ref[...].astype(BF16), yb_ref[...].astype(BF16),
                         yc_ref[...].astype(BF16), yd_ref[...].astype(BF16)], axis=1)
    mix = jnp.dot(y, wo_ref[...], preferred_element_type=F32)
    x = x_ref[...] + mod_ref[2:3, :] * mix
    xo_ref[...] = x
    ms = jnp.mean(x * x, axis=-1, keepdims=True)
    h = x * lax.rsqrt(ms + RMS_EPS) * g_ref[...] * (1.0 + mod_ref[4:5, :]) + mod_ref[3:4, :]
    h_hi = h.astype(BF16)
    h_ref[...] = h_hi

    h_lo = (h - h_hi.astype(F32)).astype(BF16)
    logits = _dot_nt(rwh_ref[...], h_hi) + _dot_nt(rwh_ref[...], h_lo) + _dot_nt(rwl_ref[...], h_hi)
    aff = _sigmoid(logits)
    s = aff + rb_ref[...]
    n_grp = ROUTER_ROWS // 4
    slot = lambda x, j: x[j * n_grp:(j + 1) * n_grp]
    s_j = [slot(s, j) for j in range(4)]
    a_j = [slot(aff, j) for j in range(4)]
    top2 = []
    for j in range(4):
        rank = jnp.zeros(s_j[j].shape, F32)
        for m in range(4):
            if m != j:
                beats = (s_j[m] >= s_j[j]) if m < j else (s_j[m] > s_j[j])
                rank = rank + jnp.where(beats, 1.0, 0.0)
        top2.append(rank < 2.0)
    gs = sum(jnp.where(top2[j], s_j[j], 0.0) for j in range(4))
    row = _iota(gs.shape, 0)
    lost = jnp.zeros(gs.shape, F32)
    for d in range(1, n_grp):
        other = pltpu.roll(gs, d, 0)
        lost = lost + jnp.where(row >= d, jnp.where(other >= gs, 1.0, 0.0), jnp.where(other > gs, 1.0, 0.0))
    chosen = lost < 0.5
    sel = [top2[j] & chosen for j in range(4)]
    den = sum(jnp.where(sel[j], a_j[j], 0.0) for j in range(4))
    den = jnp.where(den > 0.0, den, 1.0)
    gates_t = jnp.concatenate([jnp.where(sel[j], a_j[j] / den, 0.0) for j in range(4)]
                              + [jnp.zeros((LANES - ROUTER_ROWS, s.shape[1]), F32)], axis=0)
    gates_ref[...] = gates_t.T


def _out_proj(x, ys, w_out, mod_l, g, router_w, router_b):
    B, S, D = x.shape
    tm = 512
    W = GROUP_WIDTH
    order = np.full((ROUTER_ROWS,), -1)
    order[[_expert_lane(e) for e in range(N_EXPERTS)]] = np.arange(N_EXPERTS)
    real = jnp.asarray(order >= 0)
    rw = jnp.where(real[:, None], router_w.T[np.maximum(order, 0)], 0.0)
    rw_hi = rw.astype(BF16)
    rw_lo = (rw - rw_hi.astype(F32)).astype(BF16)
    rb = jnp.where(real, router_b[np.maximum(order, 0)], NEG_INF).reshape(ROUTER_ROWS, 1)
    tile = lambda w: pl.BlockSpec((None, tm, w), lambda b, i: (b, i, 0))
    full = lambda shape: pl.BlockSpec(shape, lambda b, i: (0,) * len(shape))
    return pl.pallas_call(
        _out_kernel,
        grid=(B, S // tm),
        in_specs=[tile(D), tile(W), tile(W), tile(W), tile(W), full((D, D)),
                  pl.BlockSpec((None, 6, D), lambda b, i: (b, 0, 0)), full((1, D)),
                  full((ROUTER_ROWS, D)), full((ROUTER_ROWS, D)), full((ROUTER_ROWS, 1))],
        out_specs=[tile(D), tile(D), tile(LANES)],
        out_shape=[jax.ShapeDtypeStruct((B, S, D), F32), jax.ShapeDtypeStruct((B, S, D), BF16),
                   jax.ShapeDtypeStruct((B, S, LANES), F32)],
        compiler_params=_params(("parallel", "parallel")),
        name="out_proj_router",
    )(x, *ys, w_out.astype(BF16), mod_l, g.reshape(1, D), rw_hi, rw_lo, rb)


MOE_EXPERTS_PER_STEP = 4


def _moe_kernel(x_ref, h_ref, gates_ref, mod_ref, wg_ref, wu_ref, wd_ref, o_ref):
    step = pl.program_id(2)
    n = MOE_EXPERTS_PER_STEP

    @pl.when(step == 0)
    def _init():
        o_ref[...] = x_ref[...]

    h = h_ref[...]
    gates = gates_ref[...]
    lane = _iota(gates.shape, 1)
    hgs = [jnp.dot(h, wg_ref[k].astype(BF16), preferred_element_type=F32) for k in range(n)]
    hus = [jnp.dot(h, wu_ref[k].astype(BF16), preferred_element_type=F32) for k in range(n)]
    acts = []
    for k in range(n):
        ge = jnp.sum(jnp.where(lane == _expert_lane(step * n + k), gates, 0.0), axis=-1, keepdims=True)
        acts.append((hgs[k] * _sigmoid(hgs[k]) * hus[k] * ge).astype(BF16))
    act = jnp.concatenate(acts, axis=1)
    wd = wd_ref[...].astype(BF16).reshape(n * D_EXPERT, wd_ref.shape[2])
    o_ref[...] += mod_ref[5:6, :] * jnp.dot(act, wd, preferred_element_type=F32)


def _moe(x, h, gates, mod_l, w_gate, w_up, w_down, layer):
    B, S, D = x.shape
    E = w_gate.shape[1]
    n = MOE_EXPERTS_PER_STEP
    tm = 1024
    tile = lambda w: pl.BlockSpec((None, tm, w), lambda b, i, e: (b, i, 0))
    return pl.pallas_call(
        _moe_kernel,
        grid=(B, S // tm, E // n),
        in_specs=[tile(D), tile(D), tile(LANES),
                  pl.BlockSpec((None, 6, D), lambda b, i, e: (b, 0, 0)),
                  pl.BlockSpec((None, n, D, D_EXPERT), lambda b, i, e: (layer, e, 0, 0)),
                  pl.BlockSpec((None, n, D, D_EXPERT), lambda b, i, e: (layer, e, 0, 0)),
                  pl.BlockSpec((None, n, D_EXPERT, D), lambda b, i, e: (layer, e, 0, 0))],
        out_specs=tile(D),
        out_shape=jax.ShapeDtypeStruct((B, S, D), F32),
        compiler_params=_params(("parallel", "parallel", "arbitrary")),
        name="moe",
    )(x, h, gates, mod_l, w_gate, w_up, w_down)


def kernel(x, c, positions, ada_w, ada_b, norm_mix_g, norm_ffn_g, w_in, w_out, nsa_q_norm, nsa_k_norm, nsa_cmp_pe, nsa_cmp_w1, nsa_cmp_w2, pool_w, pool_scale, rwkv_mu, rwkv_w0, rwkv_w2, rwkv_a0, rwkv_a2, rwkv_g2, rwkv_k_k, rwkv_k_a, rwkv_r_k, rwkv_ln_w, rwkv_ln_b, rwkv_v0, rwkv_v1, rwkv_v2, dil_q_norm, dil_k_norm, router_w, router_b, moe_w_gate, moe_w_up, moe_w_down):
    depth = ada_w.shape[0]
    mod = _modulation(c, ada_w, ada_b)
    cos, sin, ccos, csin = _rope_tables(positions)
    v_first = None
    for l in range(depth):
        q_a, kvc_a, kv_a, gl, u_pool, z_rwkv, q_d, kv_d = _in_proj(x, mod[l], norm_mix_g[l], w_in, l)
        y_nsa = _nsa(q_a, kvc_a, kv_a, gl, cos, sin, ccos, csin, nsa_q_norm[l], nsa_k_norm[l],
                     nsa_cmp_pe[l], nsa_cmp_w1[l], nsa_cmp_w2[l])
        y_pool = _pool(u_pool, pool_w[l], pool_scale[l])
        v_res = None if l == 0 else (rwkv_v0[l - 1], rwkv_v1[l - 1], rwkv_v2[l - 1])
        y_rwkv, v_first = _rwkv(z_rwkv, v_first, v_res, rwkv_mu[l], rwkv_w0[l], rwkv_w2[l], rwkv_a0[l], rwkv_a2[l],
                                rwkv_g2[l], rwkv_k_k[l], rwkv_k_a[l], rwkv_r_k[l], rwkv_ln_w[l], rwkv_ln_b[l])
        y_dil = _dilated(q_d, kv_d, cos, sin, dil_q_norm[l], dil_k_norm[l])
        x_mid, h2, gates = _out_proj(x, (y_nsa, y_pool, y_rwkv, y_dil), w_out[l], mod[l], norm_ffn_g[l],
                                     router_w, router_b)
        x = _moe(x_mid, h2, gates, mod[l], moe_w_gate, moe_w_up, moe_w_down, l)
    return x
```

```python
import functools

import numpy as np
import jax
import jax.numpy as jnp
from jax import lax
from jax.experimental import pallas as pl
from jax.experimental.pallas import tpu as pltpu

F32 = jnp.float32
BF16 = jnp.bfloat16

D_MODEL = 1024
HEAD_DIM = 64
GROUP_WIDTH = 256
N_HEADS = 4
RMS_EPS = 1e-6
ROPE_THETA = 500000.0
ROPE_DIM = 16
CMP_BLOCK = 32
CMP_STRIDE = 16
SEL_BLOCK = 64
SEL_TOP_N = 16
WINDOW = 512
FORCE_SCORE = 1e4
POOL_WINDOWS = (2, 4, 8, 16)
RWKV_GN_EPS = 64e-5
DIL_PATTERNS = ((128, 1), (512, 4), (2048, 16))
N_EXPERTS = 16
D_EXPERT = 256
NEG_INF = -1e30
LOG2E = 1.4426950408889634
TINY = 1e-30

LANES = 128
TQ = 256
DIL_TQ = 512
TK = 512
RWKV_CHUNK = 64
RWKV_GROUP = 8
VMEM_LIMIT = 56 * 1024 * 1024

_IN_COLS = dict(q=(0, 256), kvc=(256, 384), kv=(384, 640), gl=(640, 768), pool=(768, 1024),
                rwkv=(1024, 2304), dq=(2304, 2560), dkv=(2560, 3072))
IN_PACKED = 3072
RWKV_PACKED = 1280


def _dot(a, b):
    return jnp.dot(a.astype(BF16), b.astype(BF16), preferred_element_type=F32)


def _dot_nt(a, b):
    return lax.dot_general(a, b, (((1,), (1,)), ((), ())), preferred_element_type=F32)


def _split(x):
    hi = x.astype(BF16)
    return hi, (x - hi.astype(F32)).astype(BF16)


def _dot_split(a, b):
    hi, lo = _split(a)
    b = b.astype(BF16)
    return jnp.dot(hi, b, preferred_element_type=F32) + jnp.dot(lo, b, preferred_element_type=F32)


def _dot3(a, b, nt=False):
    f = _dot_nt if nt else functools.partial(jnp.dot, preferred_element_type=F32)
    a_hi, a_lo = _split(a)
    b_hi, b_lo = _split(b)
    return f(a_hi, b_hi) + f(a_lo, b_hi) + f(a_hi, b_lo)


def _sigmoid(x):
    return 1.0 / (1.0 + jnp.exp(-x))


def _iota(shape, dim):
    return lax.broadcasted_iota(jnp.int32, shape, dim)


def _rope_lanes(y, cos, sin):
    n = y.shape[-1]
    lane = _iota(y.shape, y.ndim - 1)
    partner = jnp.where((lane & 15) < 8, pltpu.roll(y, n - 8, y.ndim - 1), pltpu.roll(y, 8, y.ndim - 1))
    return y * cos + partner * sin


def _params(sem, vmem=VMEM_LIMIT):
    return pltpu.CompilerParams(dimension_semantics=sem, vmem_limit_bytes=vmem)


def _mod_kernel(c_ref, w_ref, b_ref, o_ref):
    c = c_ref[...]
    o_ref[...] = _dot3(c * _sigmoid(c), w_ref[...]) + b_ref[...]


def _modulation(c, ada_w, ada_b):
    L, D, E = ada_w.shape
    B = c.shape[0]
    tn = 1024
    out = pl.pallas_call(
        _mod_kernel,
        grid=(L, E // tn),
        in_specs=[pl.BlockSpec((B, D), lambda l, j: (0, 0)),
                  pl.BlockSpec((None, D, tn), lambda l, j: (l, 0, j)),
                  pl.BlockSpec((None, 1, tn), lambda l, j: (l, 0, j))],
        out_specs=pl.BlockSpec((None, B, tn), lambda l, j: (l, 0, j)),
        out_shape=jax.ShapeDtypeStruct((L, B, E), F32),
        compiler_params=_params(("parallel", "parallel")),
        name="modulation",
    )(c, ada_w, ada_b.reshape(L, 1, E))
    return out.reshape(L, B, 6, D)


def _rope_kernel(pos_ref, cpos_ref, freq_ref, sgn_ref, cos_ref, sin_ref, ccos_ref, csin_ref):
    freq = freq_ref[...]
    sgn = sgn_ref[...]
    ang = pos_ref[...].astype(F32) * freq
    cos_ref[...] = jnp.cos(ang)
    sin_ref[...] = jnp.sin(ang) * sgn
    cang = cpos_ref[...].astype(F32) * freq
    ccos_ref[...] = jnp.cos(cang)
    csin_ref[...] = jnp.sin(cang) * sgn


def _rope_tables(positions):
    B, S = positions.shape
    half = ROPE_DIM // 2
    inv_freq = ROPE_THETA ** (-2.0 * jnp.arange(half, dtype=F32) / ROPE_DIM)
    lane = np.arange(LANES)
    rot = (lane % HEAD_DIM) < ROPE_DIM
    freq = jnp.where(rot, inv_freq[lane % half], 0.0).reshape(1, LANES)
    sgn = jnp.asarray(np.where(rot, np.where(lane % HEAD_DIM < half, -1.0, 1.0), 0.0), F32).reshape(1, LANES)
    n_cmp = S // CMP_STRIDE
    cpos = positions[:, CMP_BLOCK - 1::CMP_STRIDE]
    cpos = jnp.pad(cpos, ((0, 0), (0, n_cmp - cpos.shape[1])))
    tab = jax.ShapeDtypeStruct((B, S, LANES), F32)
    ctab = jax.ShapeDtypeStruct((B, n_cmp, LANES), F32)
    return pl.pallas_call(
        _rope_kernel,
        grid=(B,),
        in_specs=[pl.BlockSpec((None, S, 1), lambda b: (b, 0, 0)),
                  pl.BlockSpec((None, n_cmp, 1), lambda b: (b, 0, 0)),
                  pl.BlockSpec((1, LANES), lambda b: (0, 0)),
                  pl.BlockSpec((1, LANES), lambda b: (0, 0))],
        out_specs=[pl.BlockSpec((None, S, LANES), lambda b: (b, 0, 0)),
                   pl.BlockSpec((None, S, LANES), lambda b: (b, 0, 0)),
                   pl.BlockSpec((None, n_cmp, LANES), lambda b: (b, 0, 0)),
                   pl.BlockSpec((None, n_cmp, LANES), lambda b: (b, 0, 0))],
        out_shape=[tab, tab, ctab, ctab],
        compiler_params=_params(("parallel",)),
        name="rope_tables",
    )(positions.reshape(B, S, 1), cpos.reshape(B, n_cmp, 1), freq, sgn)


_W_IN_MOVES = ((0, 652, 0), (652, 908, 768), (908, 1676, 1024), (1676, 1740, 1792), (1740, 1804, 1920),
               (1804, 1964, 2048), (1964, 2732, 2304))


def _in_kernel(x_ref, mod_ref, g_ref, w_ref, *refs):
    out_refs, wp_ref = refs[:-1], refs[-1]

    @pl.when((pl.program_id(0) == 0) & (pl.program_id(1) == 0))
    def _pack_weights():
        rows = 128

        def body(r, _):
            r0 = pl.multiple_of(r * rows, rows)
            w = w_ref[pl.ds(r0, rows), :]
            wp_ref[pl.ds(r0, rows), :] = jnp.zeros((rows, IN_PACKED), BF16)
            for lo, hi, dst in _W_IN_MOVES:
                wp_ref[pl.ds(r0, rows), dst:dst + hi - lo] = w[:, lo:hi].astype(BF16)
            return 0

        lax.fori_loop(0, w_ref.shape[0] // rows, body, 0)

    x = x_ref[...]
    ms = jnp.mean(x * x, axis=-1, keepdims=True)
    y = x * lax.rsqrt(ms + RMS_EPS) * g_ref[...]
    h = y * (1.0 + mod_ref[1:2, :]) + mod_ref[0:1, :]
    p = jnp.dot(h.astype(BF16), wp_ref[...], preferred_element_type=F32)
    for ref, (lo, hi) in zip(out_refs, _IN_COLS.values()):
        ref[...] = p[:, lo:hi]


def _in_proj(x, mod_l, g, w_in, layer):
    B, S, D = x.shape
    tm = 512
    widths = [hi - lo for lo, hi in _IN_COLS.values()]
    return pl.pallas_call(
        _in_kernel,
        grid=(B, S // tm),
        in_specs=[pl.BlockSpec((None, tm, D), lambda b, i: (b, i, 0)),
                  pl.BlockSpec((None, 6, D), lambda b, i: (b, 0, 0)),
                  pl.BlockSpec((1, D), lambda b, i: (0, 0)),
                  pl.BlockSpec((None,) + w_in.shape[1:], lambda b, i: (layer, 0, 0))],
        out_specs=[pl.BlockSpec((None, tm, w), lambda b, i: (b, i, 0)) for w in widths],
        out_shape=[jax.ShapeDtypeStruct((B, S, w), F32) for w in widths],
        scratch_shapes=[pltpu.VMEM((D, IN_PACKED), BF16)],
        compiler_params=_params(("arbitrary", "arbitrary")),
        name="in_proj",
    )(x, mod_l, g.reshape(1, D), w_in)


def _pack_rwkv_row(v):
    z = lambda n: jnp.zeros((n,), v.dtype)
    return jnp.concatenate([v[0:768], v[768:832], z(64), v[832:896], z(64), v[896:1056], z(96)]).reshape(1, RWKV_PACKED)


def _pool_kernel(u_ref, w_ref, scale_ref, o_ref):
    u = u_ref[...]
    row = _iota(u.shape, 0)
    lane = _iota(u.shape, 1)

    def lag(x, s):
        return jnp.where(row >= s, pltpu.roll(x, s, 0), 0.0)

    s2 = u + lag(u, 1)
    s4 = s2 + lag(s2, 2)
    s8 = s4 + lag(s4, 4)
    s16 = s8 + lag(s8, 8)
    g = lane // (GROUP_WIDTH // len(POOL_WINDOWS))
    tot = jnp.where(g == 0, s2, jnp.where(g == 1, s4, jnp.where(g == 2, s8, s16)))
    win = jnp.where(g == 0, 2.0, jnp.where(g == 1, 4.0, jnp.where(g == 2, 8.0, 16.0)))
    cnt = jnp.minimum((row + 1).astype(F32), win)
    o_ref[...] = _dot(tot / cnt - u, w_ref[...]) * scale_ref[...]


def _pool(u, pool_w, pool_scale):
    B, S, W = u.shape
    wbd = jax.scipy.linalg.block_diag(*[pool_w[i] for i in range(pool_w.shape[0])]).astype(BF16)
    return pl.pallas_call(
        _pool_kernel,
        grid=(B,),
        in_specs=[pl.BlockSpec((None, S, W), lambda b: (b, 0, 0)),
                  pl.BlockSpec((W, W), lambda b: (0, 0)),
                  pl.BlockSpec((1, W), lambda b: (0, 0))],
        out_specs=pl.BlockSpec((None, S, W), lambda b: (b, 0, 0)),
        out_shape=jax.ShapeDtypeStruct((B, S, W), F32),
        compiler_params=_params(("parallel",)),
        name="pool_mixer",
    )(u, wbd, pool_scale.reshape(1, W))


def _head_mean_matrix():
    blk = np.kron(np.eye(N_HEADS), np.ones((HEAD_DIM, HEAD_DIM))) / HEAD_DIM
    return jnp.asarray(blk, F32)


def _pad_heads(x256, as_bf16=True):
    n = x256.shape[0]
    lane = _iota((n, LANES), 1)
    out = []
    for h in range(N_HEADS):
        blk = x256[:, (h // 2) * LANES:(h // 2 + 1) * LANES]
        if h % 2:
            blk = pltpu.roll(blk, HEAD_DIM, 1)
        blk = jnp.where(lane < HEAD_DIM, blk, 0.0)
        out.append(blk.astype(BF16) if as_bf16 else blk)
    return out


def _merge_heads(slabs):
    n = slabs[0].shape[0]
    lane = _iota((n, LANES), 1)
    cols = []
    for p in range(N_HEADS // 2):
        even = pltpu.roll(slabs[2 * p], HEAD_DIM, 1)
        cols.append(jnp.where(lane < HEAD_DIM, even, slabs[2 * p + 1]))
    return jnp.concatenate(cols, axis=1)


def _flash_step(carries, qs, kas, vbs, biases):
    idx = range(len(qs))
    s = [_dot_nt(qs[i], kas[i]) for i in idx]
    if biases is not None:
        s = [s[i] + biases[i] for i in idx]
    m_new = [jnp.maximum(carries[i][0], jnp.max(s[i], axis=-1, keepdims=True)) for i in idx]
    alpha = [jnp.exp2(carries[i][0] - m_new[i]) for i in idx]
    p = [jnp.exp2(s[i] - m_new[i]).astype(BF16) for i in idx]
    pv = [jnp.dot(p[i], vbs[i], preferred_element_type=F32) for i in idx]
    return tuple((m_new[i], alpha[i] * carries[i][1] + pv[i]) for i in idx)


def _flash_init(n):
    return (jnp.full((n, 1), NEG_INF, F32), jnp.zeros((n, LANES), F32))


def _flash_finish(carry):
    acc = carry[1]
    return acc / pltpu.roll(acc, HEAD_DIM, 1)


def _nsa_kernel(q_ref, gl_ref, kc_ref, kv_ref, cos_ref, sin_ref, ccos_ref, csin_ref,
                qn_ref, kn_ref, pet_ref, peb_ref, wt_ref, wb_ref, w2_ref,
                hb_ref, cover_ref, eg_ref, ctab_ref, wtab_ref,
                o_ref, ksvs_ref, kwvw_ref, vs1_ref, vw1_ref, kvc_ref):
    S = kv_ref.shape[0]
    qi = pl.program_id(1)
    n_cmp = S // CMP_STRIDE

    @pl.when(qi == 0)
    def _prep_keys():
        rows = 256

        def body(i, _):
            r0 = pl.multiple_of(i * rows, rows)
            cs = cos_ref[pl.ds(r0, rows), :]
            sn = sin_ref[pl.ds(r0, rows), :]
            lane = _iota((rows, LANES), 1)
            first = lane < HEAD_DIM
            blk_hot = jnp.where(lane - HEAD_DIM == (r0 + _iota((rows, LANES), 0)) // SEL_BLOCK, 1.0, 0.0)
            for col, j, dst, dst1 in ((0, 1, ksvs_ref, vs1_ref), (LANES, 2, kwvw_ref, vw1_ref)):
                slab = kv_ref[pl.ds(r0, rows), col:col + LANES]
                ms = jnp.sum(jnp.where(first, slab * slab, 0.0), axis=-1, keepdims=True) * (1.0 / HEAD_DIM)
                y = slab * lax.rsqrt(ms + RMS_EPS) * kn_ref[j:j + 1, :]
                dst[pl.ds(r0, rows), :] = jnp.where(first, _rope_lanes(y, cs, sn), blk_hot if j == 1 else slab).astype(BF16)
                dst1[pl.ds(r0, rows), :] = jnp.where(first, 1.0, slab).astype(BF16)
            return 0

        lax.fori_loop(0, S // rows, body, 0)

        top = jnp.zeros((n_cmp, LANES), F32)
        bot = jnp.zeros((n_cmp, LANES), F32)
        for i in range(CMP_STRIDE):
            tok = kc_ref[pl.ds(i, n_cmp, stride=CMP_STRIDE), :]
            cols = slice(i * LANES, (i + 1) * LANES)
            top = top + _dot3(tok + pet_ref[:, cols], wt_ref[cols, :])
            bot = bot + _dot3(tok + peb_ref[:, cols], wb_ref[cols, :])
        pre = top + pltpu.roll(bot, n_cmp - 1, 0)
        act = 0.5 * pre * (1.0 + jnp.tanh(float(np.sqrt(2.0 / np.pi)) * (pre + 0.044715 * (pre * pre * pre))))
        out = _dot3(act, w2_ref[...])
        first = _iota(out.shape, 1) < HEAD_DIM
        ms = jnp.sum(jnp.where(first, out * out, 0.0), axis=-1, keepdims=True) * (1.0 / HEAD_DIM)
        y = out * lax.rsqrt(ms + RMS_EPS) * kn_ref[0:1, :]
        kvc_ref[...] = jnp.where(first, _rope_lanes(y, ccos_ref[...], csin_ref[...]), out)

    r0 = pl.multiple_of(qi * TQ, TQ)
    q = q_ref[...]
    ms = _dot_split(q * q, hb_ref[...])
    qn = q * lax.rsqrt(ms + RMS_EPS) * qn_ref[...]
    cs = cos_ref[pl.ds(r0, TQ), :]
    sn = sin_ref[pl.ds(r0, TQ), :]
    qr = _rope_lanes(qn, jnp.concatenate([cs, cs], axis=1), jnp.concatenate([sn, sn], axis=1)) * (HEAD_DIM ** -0.5)
    qst = jnp.concatenate(_pad_heads(qr, as_bf16=False), axis=0)
    n_rows = N_HEADS * TQ

    kvc = kvc_ref[...]
    s = _dot3(qst, kvc, nt=True)
    t_st = r0 + (_iota((n_rows, n_cmp), 0) & (TQ - 1))
    c_end = _iota((n_rows, n_cmp), 1) * CMP_STRIDE + (CMP_BLOCK - 1)
    mask = c_end <= t_st
    sm = jnp.where(mask, s, NEG_INF)
    m = jnp.max(sm, axis=-1, keepdims=True)
    e = jnp.where(mask, jnp.exp(sm - m), 0.0)
    p = e / jnp.maximum(jnp.sum(e, axis=-1, keepdims=True), TINY)
    o_cmp = _dot(p, kvc)

    psum = p[0:TQ] + p[TQ:2 * TQ] + p[2 * TQ:3 * TQ] + p[3 * TQ:4 * TQ]
    ps_hi, ps_lo = _split(psum.T)
    cover_t = cover_ref[...].astype(BF16)
    imp = (jnp.dot(cover_t, ps_hi, preferred_element_type=F32) + jnp.dot(cover_t, ps_lo, preferred_element_type=F32))
    n_blk = S // SEL_BLOCK
    blk = _iota((n_blk, TQ), 0)
    cur = (r0 + _iota((n_blk, TQ), 1)) // SEL_BLOCK
    forced = (blk == 0) | (blk == cur) | (blk == cur - 1)
    imp = jnp.where(blk > cur, -1.0, jnp.where(forced, FORCE_SCORE, imp))
    rank = jnp.zeros((n_blk, TQ), F32)
    for d in range(1, n_blk):
        other = pltpu.roll(imp, n_blk - d, 0)
        rank = rank + jnp.where(blk + d >= n_blk,
                                jnp.where(other >= imp, 1.0, 0.0),
                                jnp.where(other > imp, 1.0, 0.0))
    sel_rows = jnp.where(rank < SEL_TOP_N, 0.0, NEG_INF)
    sel_bias = jnp.concatenate([jnp.zeros((HEAD_DIM, TQ), F32), sel_rows,
                                jnp.zeros((LANES - HEAD_DIM - n_blk, TQ), F32)], axis=0).T.astype(BF16)
    lane = _iota((TQ, LANES), 1)

    qb = (qst * LOG2E).astype(BF16)
    heads = range(N_HEADS)
    q_heads = [qb[h * TQ:(h + 1) * TQ] for h in heads]
    q_heads_sel = [jnp.where(lane < HEAD_DIM, q, sel_bias) for q in q_heads]
    init = tuple(_flash_init(TQ) for _ in heads)

    def sel_tile(j, carry, biases):
        k0 = pl.multiple_of(j * TK, TK)
        ka, vb = ksvs_ref[pl.ds(k0, TK), :], vs1_ref[pl.ds(k0, TK), :]
        return _flash_step(carry, q_heads_sel, [ka] * N_HEADS, [vb] * N_HEADS, biases)

    ratio = TK // TQ
    last = qi // ratio
    res = lax.fori_loop(0, last, lambda j, c: sel_tile(j, c, None), init)
    res = sel_tile(last, res, [ctab_ref[qi % ratio]] * N_HEADS)
    o_sel = jnp.concatenate([_flash_finish(c) for c in res], axis=0)

    n_prev = WINDOW // TQ
    span = (n_prev + 1) * TQ
    k0 = pl.multiple_of(jnp.maximum(qi - n_prev, 0) * TQ, TQ)
    ka, vb = kwvw_ref[pl.ds(k0, span), :], vw1_ref[pl.ds(k0, span), :]
    res = _flash_step(init, q_heads, [ka] * N_HEADS, [vb] * N_HEADS, [wtab_ref[jnp.minimum(qi, n_prev)]] * N_HEADS)
    o_win = jnp.concatenate([_flash_finish(c) for c in res], axis=0)

    gate = _dot_split(_sigmoid(gl_ref[...]), eg_ref[...])
    split = lambda o: _merge_heads([o[h * TQ:(h + 1) * TQ] for h in range(N_HEADS)])
    W = GROUP_WIDTH
    o_ref[...] = (gate[:, 0:W] * split(o_cmp) + gate[:, W:2 * W] * split(o_sel) + gate[:, 2 * W:3 * W] * split(o_win))


def _nsa(q, kvc, kv, gl, cos, sin, ccos, csin, q_norm, k_norm, cmp_pe, cmp_w1, cmp_w2):
    B, S, _ = q.shape
    n_cmp = S // CMP_STRIDE
    n_blk = S // SEL_BLOCK
    hd = HEAD_DIM

    qn = jnp.tile(q_norm, N_HEADS).reshape(1, GROUP_WIDTH)
    kn = jnp.concatenate([k_norm, jnp.ones_like(k_norm)], axis=1)

    def interleave(a, b):
        return jnp.concatenate([a, b], axis=1).reshape(1, -1)

    half = CMP_BLOCK // 2
    pet = interleave(cmp_pe[0, :half], cmp_pe[1, :half])
    peb = interleave(cmp_pe[0, half:], cmp_pe[1, half:])

    def w1_half(lo):
        wk = cmp_w1[0].reshape(CMP_BLOCK, hd, hd)[lo:lo + half]
        wv = cmp_w1[1].reshape(CMP_BLOCK, hd, hd)[lo:lo + half]
        z = jnp.zeros_like(wk)
        top = jnp.concatenate([wk, z], axis=2)
        bot = jnp.concatenate([z, wv], axis=2)
        return jnp.concatenate([top, bot], axis=1).reshape(half * 2 * hd, 2 * hd)

    wt, wb = w1_half(0), w1_half(half)
    w2 = jax.scipy.linalg.block_diag(cmp_w2[0], cmp_w2[1])
    hb = _head_mean_matrix()

    c_end = np.arange(n_cmp) * CMP_STRIDE + CMP_BLOCK - 1
    b_start = np.arange(n_blk) * SEL_BLOCK
    cover = np.maximum(np.minimum(c_end[:, None] + 1, b_start[None, :] + SEL_BLOCK)
                       - np.maximum(c_end[:, None] + 1 - CMP_BLOCK, b_start[None, :]), 0).astype(np.float32) / CMP_BLOCK
    cover[n_cmp - 1] = 0.0
    cover = jnp.asarray(cover.T, F32)
    eg = np.zeros((LANES, 3 * GROUP_WIDTH), np.float32)
    for h in range(N_HEADS):
        for br in range(3):
            eg[h * 3 + br, br * GROUP_WIDTH + h * hd: br * GROUP_WIDTH + (h + 1) * hd] = 1.0
    eg = jnp.asarray(eg)
    ratio = TK // TQ
    ctab = np.where(np.arange(TK)[None, None, :] <= np.arange(ratio)[:, None, None] * TQ + np.arange(TQ)[None, :, None],
                    0.0, NEG_INF).astype(np.float32)
    n_prev = WINDOW // TQ
    span = (n_prev + 1) * TQ
    dist = (np.arange(n_prev + 1)[:, None, None] * TQ + np.arange(TQ)[None, :, None] - np.arange(span)[None, None, :])
    wtab = np.where((dist >= 0) & (dist < WINDOW), 0.0, NEG_INF).astype(np.float32)
    ctab, wtab = jnp.asarray(ctab), jnp.asarray(wtab)

    full = lambda shape: pl.BlockSpec(shape, lambda b, i: (0,) * len(shape))
    return pl.pallas_call(
        _nsa_kernel,
        grid=(B, S // TQ),
        in_specs=[pl.BlockSpec((None, TQ, GROUP_WIDTH), lambda b, i: (b, i, 0)),
                  pl.BlockSpec((None, TQ, LANES), lambda b, i: (b, i, 0)),
                  pl.BlockSpec((None, S, LANES), lambda b, i: (b, 0, 0)),
                  pl.BlockSpec((None, S, 2 * LANES), lambda b, i: (b, 0, 0)),
                  pl.BlockSpec((None, S, LANES), lambda b, i: (b, 0, 0)),
                  pl.BlockSpec((None, S, LANES), lambda b, i: (b, 0, 0)),
                  pl.BlockSpec((None, n_cmp, LANES), lambda b, i: (b, 0, 0)),
                  pl.BlockSpec((None, n_cmp, LANES), lambda b, i: (b, 0, 0)),
                  full((1, GROUP_WIDTH)), full((3, LANES)),
                  full((1, CMP_STRIDE * LANES)), full((1, CMP_STRIDE * LANES)),
                  full((CMP_STRIDE * LANES, LANES)), full((CMP_STRIDE * LANES, LANES)), full((LANES, LANES)),
                  full((GROUP_WIDTH, GROUP_WIDTH)), full((n_blk, n_cmp)),
                  full((LANES, 3 * GROUP_WIDTH)), full(ctab.shape), full(wtab.shape)],
        out_specs=pl.BlockSpec((None, TQ, GROUP_WIDTH), lambda b, i: (b, i, 0)),
        out_shape=jax.ShapeDtypeStruct((B, S, GROUP_WIDTH), F32),
        scratch_shapes=[pltpu.VMEM((S, LANES), BF16), pltpu.VMEM((S, LANES), BF16),
                        pltpu.VMEM((S, LANES), BF16), pltpu.VMEM((S, LANES), BF16),
                        pltpu.VMEM((n_cmp, LANES), F32)],
        compiler_params=_params(("parallel", "arbitrary")),
        name="nsa",
    )(q, gl, kvc, kv, cos, sin, ccos, csin, qn, kn, pet, peb, wt, wb, w2, hb, cover, eg, ctab, wtab)


def _dil_kernel(q_ref, kv_ref, cos_ref, sin_ref, qn_ref, kn_ref, hb_ref, tb_ref, o_ref, kvs_ref, v1s_ref):
    S = kv_ref.shape[0]
    qi = pl.program_id(1)
    W = GROUP_WIDTH

    @pl.when(qi == 0)
    def _prep_keys():
        rows = 256

        def body(i, _):
            r0 = pl.multiple_of(i * rows, rows)
            cs = cos_ref[pl.ds(r0, rows), :]
            sn = sin_ref[pl.ds(r0, rows), :]
            k = kv_ref[pl.ds(r0, rows), 0:W]
            v = kv_ref[pl.ds(r0, rows), W:2 * W]
            ms = _dot_split(k * k, hb_ref[...])
            kn = k * lax.rsqrt(ms + RMS_EPS) * kn_ref[...]
            kr = _rope_lanes(kn, jnp.concatenate([cs, cs], axis=1), jnp.concatenate([sn, sn], axis=1))
            lane = _iota((rows, LANES), 1)
            for h in range(N_HEADS):
                kb = kr[:, (h // 2) * LANES:(h // 2 + 1) * LANES]
                vb = v[:, (h // 2) * LANES:(h // 2 + 1) * LANES]
                if h % 2:
                    kb = pltpu.roll(kb, HEAD_DIM, 1)
                else:
                    vb = pltpu.roll(vb, HEAD_DIM, 1)
                kvs_ref[h, pl.ds(r0, rows), :] = jnp.where(lane < HEAD_DIM, kb, vb).astype(BF16)
                v1s_ref[h, pl.ds(r0, rows), :] = jnp.where(lane < HEAD_DIM, 1.0, vb).astype(BF16)
            return 0

        lax.fori_loop(0, S // rows, body, 0)

    r0 = pl.multiple_of(qi * DIL_TQ, DIL_TQ)
    q = q_ref[...]
    ms = _dot_split(q * q, hb_ref[...])
    qn = q * lax.rsqrt(ms + RMS_EPS) * qn_ref[...]
    cs = cos_ref[pl.ds(r0, DIL_TQ), :]
    sn = sin_ref[pl.ds(r0, DIL_TQ), :]
    qr = _rope_lanes(qn, jnp.concatenate([cs, cs], axis=1), jnp.concatenate([sn, sn], axis=1)) * (HEAD_DIM ** -0.5 * LOG2E)
    qh = _pad_heads(qr)
    ratio = TK // DIL_TQ

    def body(j, carry):
        k0 = pl.multiple_of(j * TK, TK)
        bias = tb_ref[qi - ratio * j]
        return _flash_step(carry, qh, [kvs_ref[h, pl.ds(k0, TK), :] for h in range(N_HEADS)],
                           [v1s_ref[h, pl.ds(k0, TK), :] for h in range(N_HEADS)], [bias] * N_HEADS)

    res = lax.fori_loop(0, qi // ratio + 1, body, tuple(_flash_init(DIL_TQ) for _ in range(N_HEADS)))
    o_ref[...] = _merge_heads([_flash_finish(c) for c in res])


def _dil_bias_table(S):
    n = S // DIL_TQ
    d = (np.arange(n)[:, None, None] * DIL_TQ + np.arange(DIL_TQ)[None, :, None] - np.arange(TK)[None, None, :])
    cnt = np.zeros(d.shape, np.float32)
    for window, dil in DIL_PATTERNS:
        cnt += ((d >= 0) & (d <= window) & (d % dil == 0)).astype(np.float32)
    return jnp.asarray(np.where(cnt > 0, np.log2(np.maximum(cnt, 1.0)), NEG_INF).astype(np.float32))


def _dilated(q, kv, cos, sin, q_norm, k_norm):
    B, S, W = q.shape
    tb = _dil_bias_table(S)
    qn = jnp.tile(q_norm, N_HEADS).reshape(1, W)
    kn = jnp.tile(k_norm, N_HEADS).reshape(1, W)
    hb = _head_mean_matrix()
    full = lambda shape: pl.BlockSpec(shape, lambda b, i: (0,) * len(shape))
    return pl.pallas_call(
        _dil_kernel,
        grid=(B, S // DIL_TQ),
        in_specs=[pl.BlockSpec((None, DIL_TQ, W), lambda b, i: (b, i, 0)),
                  pl.BlockSpec((None, S, 2 * W), lambda b, i: (b, 0, 0)),
                  pl.BlockSpec((None, S, LANES), lambda b, i: (b, 0, 0)),
                  pl.BlockSpec((None, S, LANES), lambda b, i: (b, 0, 0)),
                  full((1, W)), full((1, W)), full((W, W)), full(tb.shape)],
        out_specs=pl.BlockSpec((None, DIL_TQ, W), lambda b, i: (b, i, 0)),
        out_shape=jax.ShapeDtypeStruct((B, S, W), F32),
        scratch_shapes=[pltpu.VMEM((N_HEADS, S, LANES), BF16), pltpu.VMEM((N_HEADS, S, LANES), BF16)],
        compiler_params=_params(("parallel", "arbitrary")),
        name="dilated",
    )(q, kv, cos, sin, qn, kn, hb, tb)


def _softplus(x):
    return jnp.maximum(x, 0.0) + jnp.log(1.0 + jnp.exp(-jnp.abs(x)))


def _rwkv_kernel(first, *refs):
    if first:
        (z_ref, mu_ref, w0_ref, w2_ref, a0_ref, a2_ref, g2_ref, kk_ref, ka_ref, rk_ref, lnw_ref, lnb_ref,
         hb_ref, y_ref, vf_out_ref, st_ref, prev_ref, ro_ref, ub_ref, colb_ref, g_ref, t2_ref) = refs
    else:
        (z_ref, vf_ref, mu_ref, w0_ref, w2_ref, a0_ref, a2_ref, g2_ref, kk_ref, ka_ref, rk_ref, lnw_ref, lnb_ref,
         v0_ref, v1_ref, v2_ref, hb_ref, y_ref, st_ref, prev_ref, ro_ref, ub_ref, colb_ref, g_ref, t2_ref) = refs
    SB = z_ref.shape[0]
    C = RWKV_CHUNK
    W = GROUP_WIDTH
    n_pairs = N_HEADS // 2
    R = RWKV_GROUP * C

    @pl.when(pl.program_id(1) == 0)
    def _reset():
        st_ref[...] = jnp.zeros_like(st_ref)
        prev_ref[...] = jnp.zeros_like(prev_ref)

    ri = _iota((C, C), 0)
    ci = _iota((C, C), 1)
    tri_incl = jnp.where(ci <= ri, 1.0, 0.0).astype(BF16)
    rr = _iota((2 * LANES, 2 * LANES), 0)
    cc = _iota((2 * LANES, 2 * LANES), 1)
    g_mask = ((((rr >> 6) & 1) == ((cc >> 6) & 1))
              & ((cc & (C - 1)) <= jnp.where(rr < LANES, (rr & (C - 1)) - 1, rr & (C - 1))))
    r2 = _iota((LANES, LANES), 0)
    c2 = _iota((LANES, LANES), 1)
    eye_f = jnp.where(r2 == c2, 1.0, 0.0)
    lane_lo = _iota((C, LANES), 1) < HEAD_DIM
    hb = hb_ref[...].astype(BF16)

    def stack(x):
        return jnp.concatenate([jnp.where(lane_lo, x, 0.0), jnp.where(lane_lo, 0.0, x)], axis=0)

    def phase1(i, _):
        t0 = pl.multiple_of(i * R, R)
        zc = z_ref[pl.ds(t0, R), :]
        inside = jnp.where(i > 0, 1.0, 0.0)
        prev = inside * z_ref[pl.ds(jnp.maximum(t0 - 1, 0), 1), :] + (1.0 - inside) * prev_ref[...]
        zp = jnp.where(_iota(zc.shape, 0) == 0, prev, pltpu.roll(zc, 1, 0))
        zs = zc + (zp - zc) * mu_ref[...]
        r, k, v = zs[:, 0:W], zs[:, W:2 * W], zs[:, 2 * W:3 * W]
        xw, xa, xg = zs[:, 768:896], zs[:, 896:1024], zs[:, 1024:1280]
        w_log = -_softplus(-(w0_ref[...] + _dot(jnp.tanh(xw), w2_ref[...]))) - 0.5
        lw = -jnp.exp(w_log)
        a = _sigmoid(a0_ref[...] + _dot(xa, a2_ref[...]))
        g = _dot(_sigmoid(xg), g2_ref[...])
        if first:
            vf_out_ref[pl.ds(t0, R), :] = v
        else:
            vf = vf_ref[pl.ds(t0, R), :]
            v = v + (vf - v) * _sigmoid(v0_ref[...] + _dot(_dot(v, v1_ref[...]), v2_ref[...]))
        kkr = k * kk_ref[...]
        nrm = jnp.sqrt(_dot_split(kkr * kkr, hb) * HEAD_DIM)
        kk = kkr / jnp.maximum(nrm, 1e-12)
        k2 = k * (1.0 + (a - 1.0) * ka_ref[...])
        bonus = _dot_split(r * k2 * rk_ref[...], hb) * HEAD_DIM * v
        g_ref[pl.ds(t0, R), :] = g
        t2_ref[pl.ds(t0, R), :] = (lnb_ref[...] + bonus) * g

        items = [(c, p) for c in range(RWKV_GROUP) for p in range(n_pairs)]
        idx = range(len(items))
        sub = lambda x, c, p: x[c * C:(c + 1) * C, p * LANES:(p + 1) * LANES]
        cum = [_cumsum_rows(tri_incl, sub(lw, c, p)) for c, p in items]
        a_st, r_st, v_st, gm, tails = [], [], [], [], []
        for j, (c, p) in enumerate(items):
            lwp = sub(lw, c, p)
            cum_end = cum[j][C - 1:C, :]
            e_pos, e_neg = jnp.exp(cum[j]), jnp.exp(-cum[j])
            e_exc, e_tail = jnp.exp(cum[j] - lwp), jnp.exp(cum_end - cum[j])
            kkp, k2p = sub(kk, c, p), sub(k2, c, p)
            ka_p = kkp * sub(a, c, p)
            a_st.append(stack(-kkp * e_exc).astype(BF16))
            r_st.append(stack(sub(r, c, p) * e_pos))
            v_st.append(stack(sub(v, c, p)).astype(BF16))
            bhat = (ka_p * e_neg).astype(BF16)
            khat = (k2p * e_neg).astype(BF16)
            tails.append(jnp.concatenate([stack(ka_p * e_tail), stack(k2p * e_tail)], axis=0))
            gm.append(jnp.where(g_mask,
                                _dot_nt(jnp.concatenate([a_st[j], r_st[j].astype(BF16)], axis=0),
                                        jnp.concatenate([bhat, bhat, khat, khat], axis=0)), 0.0))
            colb_ref[i * RWKV_GROUP + c, p] = jnp.broadcast_to(jnp.exp(cum_end), (LANES, LANES)).T
        tt = [tails[j].T.astype(BF16) for j in idx]
        lakv = [_dot(gm[j][0:LANES, LANES:2 * LANES], v_st[j]) for j in idx]
        mr = [gm[j][LANES:2 * LANES, :].astype(BF16) for j in idx]
        lp = [gm[j][0:LANES, 0:LANES] for j in idx]
        tinv = [eye_f + lp[j] for j in idx]
        for _ in range(5):
            lpb = [lp[j].astype(BF16) for j in idx]
            lp = [jnp.dot(lpb[j], lpb[j], preferred_element_type=F32) for j in idx]
            tinv = [tinv[j] + _dot(lp[j], tinv[j]) for j in idx]
        wu = [_dot(tinv[j], jnp.concatenate([a_st[j].astype(F32), lakv[j]], axis=1)) for j in idx]
        wm = [wu[j][:, 0:LANES].astype(BF16) for j in idx]
        u = [wu[j][:, LANES:2 * LANES].astype(BF16) for j in idx]
        mwbw = [jnp.dot(jnp.concatenate([mr[j][:, 0:LANES], tt[j][:, 0:LANES]], axis=0), wm[j],
                        preferred_element_type=F32) for j in idx]
        ub = [jnp.dot(jnp.concatenate([mr[j], tt[j]], axis=0), jnp.concatenate([u[j], v_st[j]], axis=0),
                      preferred_element_type=F32) for j in idx]
        for j, (c, p) in enumerate(items):
            n = i * RWKV_GROUP + c
            ro_ref[n, p] = jnp.concatenate([r_st[j] + mwbw[j][0:LANES], mwbw[j][LANES:2 * LANES]], axis=0).astype(BF16)
            ub_ref[n, p] = ub[j]
        return 0

    def phase2(n, _):
        t0 = pl.multiple_of(n * C, C)
        pairs = range(n_pairs)
        st = [st_ref[p] for p in pairs]
        res = [jnp.dot(ro_ref[n, p], st[p].astype(BF16), preferred_element_type=F32) for p in pairs]
        ub = [ub_ref[n, p] for p in pairs]
        for p in pairs:
            st_ref[p] = colb_ref[n, p] * st[p] + res[p][LANES:2 * LANES] + ub[p][LANES:2 * LANES]
        o_st = [res[p][0:LANES] + ub[p][0:LANES] for p in pairs]
        y_ref[pl.ds(t0, C), :] = jnp.concatenate([o[0:C] + o[C:2 * C] for o in o_st], axis=1)
        return 0

    lax.fori_loop(0, SB // R, phase1, 0)
    lax.fori_loop(0, SB // C, phase2, 0)
    prev_ref[...] = z_ref[SB - 1:SB, :]

    for t in range(SB // LANES):
        rows = slice(t * LANES, (t + 1) * LANES)
        o = y_ref[rows, :]
        dev = o - _dot_split(o, hb)
        var = _dot_split(dev * dev, hb)
        y_ref[rows, :] = dev * lax.rsqrt(var + RWKV_GN_EPS) * (lnw_ref[...] * g_ref[rows, :]) + t2_ref[rows, :]


def _cumsum_rows(tri_incl, x):
    hi = x.astype(BF16)
    lo = (x - hi.astype(F32)).astype(BF16)
    return jnp.dot(tri_incl, hi, preferred_element_type=F32) + jnp.dot(tri_incl, lo, preferred_element_type=F32)


RWKV_BLOCK = 512


def _rwkv(z, v_first, v_res, mu, w0, w2, a0, a2, g2, k_k, k_a, r_k, ln_w, ln_b):
    B, S, P = z.shape
    W = GROUP_WIDTH
    SB = RWKV_BLOCK
    ncb = SB // RWKV_CHUNK
    n_pairs = N_HEADS // 2
    first = v_res is None
    row = lambda v: v.reshape(1, W)
    padr = lambda m, n: jnp.pad(m, ((0, n - m.shape[0]), (0, 0)))
    seq = lambda w: pl.BlockSpec((None, SB, w), lambda b, j: (b, j, 0))
    full = lambda shape: pl.BlockSpec(shape, lambda b, j: (0,) * len(shape))
    args = [z]
    specs = [seq(P)]
    if not first:
        args.append(v_first)
        specs.append(seq(W))
    args += [_pack_rwkv_row(mu), row(w0), padr(w2, LANES), row(a0), padr(a2, LANES), padr(g2, 2 * LANES),
             row(k_k), row(k_a), r_k.reshape(1, W), row(ln_w), row(ln_b)]
    specs += [full((1, P)), full((1, W)), full((LANES, W)), full((1, W)), full((LANES, W)), full((2 * LANES, W)),
              full((1, W)), full((1, W)), full((1, W)), full((1, W)), full((1, W))]
    if not first:
        v0, v1, v2 = v_res
        args += [row(v0), jnp.pad(v1, ((0, 0), (0, LANES - v1.shape[1]))), padr(v2, LANES)]
        specs += [full((1, W)), full((W, LANES)), full((LANES, W))]
    args.append(_head_mean_matrix())
    specs.append(full((W, W)))
    out_shape = [jax.ShapeDtypeStruct((B, S, W), F32)]
    out_specs = [seq(W)]
    if first:
        out_shape.append(jax.ShapeDtypeStruct((B, S, W), F32))
        out_specs.append(seq(W))
    res = pl.pallas_call(
        functools.partial(_rwkv_kernel, first),
        grid=(B, S // SB),
        in_specs=specs,
        out_specs=out_specs,
        out_shape=out_shape,
        scratch_shapes=[pltpu.VMEM((n_pairs, LANES, LANES), F32), pltpu.VMEM((1, P), F32),
                        pltpu.VMEM((ncb, n_pairs, 2 * LANES, LANES), BF16),
                        pltpu.VMEM((ncb, n_pairs, 2 * LANES, LANES), F32),
                        pltpu.VMEM((ncb, n_pairs, LANES, LANES), F32),
                        pltpu.VMEM((SB, W), F32), pltpu.VMEM((SB, W), F32)],
        compiler_params=_params(("parallel", "arbitrary")),
        name="rwkv7_first" if first else "rwkv7",
    )(*args)
    return (res[0], res[1]) if first else (res[0], v_first)


ROUTER_ROWS = 32


def _expert_lane(e):
    return (e % 4) * 8 + e // 4


def _out_kernel(x_ref, ya_ref, yb_ref, yc_ref, yd_ref, wo_ref, mod_ref, g_ref, rwh_ref, rwl_ref, rb_ref,
                xo_ref, h_ref, gates_ref):
    y = jnp.concatenate([ya_ref[...].astype(BF16), yb_ref[...].astype(BF16),
                         yc_ref[...].astype(BF16), yd_ref[...].astype(BF16)], axis=1)
    mix = jnp.dot(y, wo_ref[...], preferred_element_type=F32)
    x = x_ref[...] + mod_ref[2:3, :] * mix
    xo_ref[...] = x
    ms = jnp.mean(x * x, axis=-1, keepdims=True)
    h = x * lax.rsqrt(ms + RMS_EPS) * g_ref[...] * (1.0 + mod_ref[4:5, :]) + mod_ref[3:4, :]
    h_hi = h.astype(BF16)
    h_ref[...] = h_hi

    h_lo = (h - h_hi.astype(F32)).astype(BF16)
    logits = _dot_nt(rwh_ref[...], h_hi) + _dot_nt(rwh_ref[...], h_lo) + _dot_nt(rwl_ref[...], h_hi)
    aff = _sigmoid(logits)
    s = aff + rb_ref[...]
    n_grp = ROUTER_ROWS // 4
    slot = lambda x, j: x[j * n_grp:(j + 1) * n_grp]
    s_j = [slot(s, j) for j in range(4)]
    a_j = [slot(aff, j) for j in range(4)]
    top2 = []
    for j in range(4):
        rank = jnp.zeros(s_j[j].shape, F32)
        for m in range(4):
            if m != j:
                beats = (s_j[m] >= s_j[j]) if m < j else (s_j[m] > s_j[j])
                rank = rank + jnp.where(beats, 1.0, 0.0)
        top2.append(rank < 2.0)
    gs = sum(jnp.where(top2[j], s_j[j], 0.0) for j in range(4))
    row = _iota(gs.shape, 0)
    lost = jnp.zeros(gs.shape, F32)
    for d in range(1, n_grp):
        other = pltpu.roll(gs, d, 0)
        lost = lost + jnp.where(row >= d, jnp.where(other >= gs, 1.0, 0.0), jnp.where(other > gs, 1.0, 0.0))
    chosen = lost < 0.5
    sel = [top2[j] & chosen for j in range(4)]
    den = sum(jnp.where(sel[j], a_j[j], 0.0) for j in range(4))
    den = jnp.where(den > 0.0, den, 1.0)
    gates_t = jnp.concatenate([jnp.where(sel[j], a_j[j] / den, 0.0) for j in range(4)]
                              + [jnp.zeros((LANES - ROUTER_ROWS, s.shape[1]), F32)], axis=0)
    gates_ref[...] = gates_t.T


def _out_proj(x, ys, w_out, mod_l, g, router_w, router_b):
    B, S, D = x.shape
    tm = 512
    W = GROUP_WIDTH
    order = np.full((ROUTER_ROWS,), -1)
    order[[_expert_lane(e) for e in range(N_EXPERTS)]] = np.arange(N_EXPERTS)
    real = jnp.asarray(order >= 0)
    rw = jnp.where(real[:, None], router_w.T[np.maximum(order, 0)], 0.0)
    rw_hi = rw.astype(BF16)
    rw_lo = (rw - rw_hi.astype(F32)).astype(BF16)
    rb = jnp.where(real, router_b[np.maximum(order, 0)], NEG_INF).reshape(ROUTER_ROWS, 1)
    tile = lambda w: pl.BlockSpec((None, tm, w), lambda b, i: (b, i, 0))
    full = lambda shape: pl.BlockSpec(shape, lambda b, i: (0,) * len(shape))
    return pl.pallas_call(
        _out_kernel,
        grid=(B, S // tm),
        in_specs=[tile(D), tile(W), tile(W), tile(W), tile(W), full((D, D)),
                  pl.BlockSpec((None, 6, D), lambda b, i: (b, 0, 0)), full((1, D)),
                  full((ROUTER_ROWS, D)), full((ROUTER_ROWS, D)), full((ROUTER_ROWS, 1))],
        out_specs=[tile(D), tile(D), tile(LANES)],
        out_shape=[jax.ShapeDtypeStruct((B, S, D), F32), jax.ShapeDtypeStruct((B, S, D), BF16),
                   jax.ShapeDtypeStruct((B, S, LANES), F32)],
        compiler_params=_params(("parallel", "parallel")),
        name="out_proj_router",
    )(x, *ys, w_out.astype(BF16), mod_l, g.reshape(1, D), rw_hi, rw_lo, rb)


MOE_EXPERTS_PER_STEP = 4


def _moe_kernel(x_ref, h_ref, gates_ref, mod_ref, wg_ref, wu_ref, wd_ref, o_ref):
    step = pl.program_id(2)
    n = MOE_EXPERTS_PER_STEP

    @pl.when(step == 0)
    def _init():
        o_ref[...] = x_ref[...]

    h = h_ref[...]
    gates = gates_ref[...]
    lane = _iota(gates.shape, 1)
    hgs = [jnp.dot(h, wg_ref[k].astype(BF16), preferred_element_type=F32) for k in range(n)]
    hus = [jnp.dot(h, wu_ref[k].astype(BF16), preferred_element_type=F32) for k in range(n)]
    acts = []
    for k in range(n):
        ge = jnp.sum(jnp.where(lane == _expert_lane(step * n + k), gates, 0.0), axis=-1, keepdims=True)
        acts.append((hgs[k] * _sigmoid(hgs[k]) * hus[k] * ge).astype(BF16))
    act = jnp.concatenate(acts, axis=1)
    wd = wd_ref[...].astype(BF16).reshape(n * D_EXPERT, wd_ref.shape[2])
    o_ref[...] += mod_ref[5:6, :] * jnp.dot(act, wd, preferred_element_type=F32)


def _moe(x, h, gates, mod_l, w_gate, w_up, w_down, layer):
    B, S, D = x.shape
    E = w_gate.shape[1]
    n = MOE_EXPERTS_PER_STEP
    tm = 1024
    tile = lambda w: pl.BlockSpec((None, tm, w), lambda b, i, e: (b, i, 0))
    return pl.pallas_call(
        _moe_kernel,
        grid=(B, S // tm, E // n),
        in_specs=[tile(D), tile(D), tile(LANES),
                  pl.BlockSpec((None, 6, D), lambda b, i, e: (b, 0, 0)),
                  pl.BlockSpec((None, n, D, D_EXPERT), lambda b, i, e: (layer, e, 0, 0)),
                  pl.BlockSpec((None, n, D, D_EXPERT), lambda b, i, e: (layer, e, 0, 0)),
                  pl.BlockSpec((None, n, D_EXPERT, D), lambda b, i, e: (layer, e, 0, 0))],
        out_specs=tile(D),
        out_shape=jax.ShapeDtypeStruct((B, S, D), F32),
        compiler_params=_params(("parallel", "parallel", "arbitrary")),
        name="moe",
    )(x, h, gates, mod_l, w_gate, w_up, w_down)


def kernel(x, c, positions, ada_w, ada_b, norm_mix_g, norm_ffn_g, w_in, w_out, nsa_q_norm, nsa_k_norm, nsa_cmp_pe, nsa_cmp_w1, nsa_cmp_w2, pool_w, pool_scale, rwkv_mu, rwkv_w0, rwkv_w2, rwkv_a0, rwkv_a2, rwkv_g2, rwkv_k_k, rwkv_k_a, rwkv_r_k, rwkv_ln_w, rwkv_ln_b, rwkv_v0, rwkv_v1, rwkv_v2, dil_q_norm, dil_k_norm, router_w, router_b, moe_w_gate, moe_w_up, moe_w_down):
    depth = ada_w.shape[0]
    mod = _modulation(c, ada_w, ada_b)
    cos, sin, ccos, csin = _rope_tables(positions)
    v_first = None
    for l in range(depth):
        q_a, kvc_a, kv_a, gl, u_pool, z_rwkv, q_d, kv_d = _in_proj(x, mod[l], norm_mix_g[l], w_in, l)
        y_nsa = _nsa(q_a, kvc_a, kv_a, gl, cos, sin, ccos, csin, nsa_q_norm[l], nsa_k_norm[l],
                     nsa_cmp_pe[l], nsa_cmp_w1[l], nsa_cmp_w2[l])
        y_pool = _pool(u_pool, pool_w[l], pool_scale[l])
        v_res = None if l == 0 else (rwkv_v0[l - 1], rwkv_v1[l - 1], rwkv_v2[l - 1])
        y_rwkv, v_first = _rwkv(z_rwkv, v_first, v_res, rwkv_mu[l], rwkv_w0[l], rwkv_w2[l], rwkv_a0[l], rwkv_a2[l],
                                rwkv_g2[l], rwkv_k_k[l], rwkv_k_a[l], rwkv_r_k[l], rwkv_ln_w[l], rwkv_ln_b[l])
        y_dil = _dilated(q_d, kv_d, cos, sin, dil_q_norm[l], dil_k_norm[l])
        x_mid, h2, gates = _out_proj(x, (y_nsa, y_pool, y_rwkv, y_dil), w_out[l], mod[l], norm_ffn_g[l],
                                     router_w, router_b)
        x = _moe(x_mid, h2, gates, mod[l], moe_w_gate, moe_w_up, moe_w_down, l)
    return x
```

```python
import functools

import numpy as np
import jax
import jax.numpy as jnp
from jax import lax
from jax.experimental import pallas as pl
from jax.experimental.pallas import tpu as pltpu

F32 = jnp.float32
BF16 = jnp.bfloat16

D_MODEL = 1024
HEAD_DIM = 64
GROUP_WIDTH = 256
N_HEADS = 4
RMS_EPS = 1e-6
ROPE_THETA = 500000.0
ROPE_DIM = 16
CMP_BLOCK = 32
CMP_STRIDE = 16
SEL_BLOCK = 64
SEL_TOP_N = 16
WINDOW = 512
FORCE_SCORE = 1e4
POOL_WINDOWS = (2, 4, 8, 16)
RWKV_GN_EPS = 64e-5
DIL_PATTERNS = ((128, 1), (512, 4), (2048, 16))
N_EXPERTS = 16
D_EXPERT = 256
NEG_INF = -1e30
LOG2E = 1.4426950408889634
TINY = 1e-30

LANES = 128
TQ = 256
DIL_TQ = 512
TK = 512
SEL_TK = 512
RWKV_CHUNK = 64
RWKV_GROUP = 8
VMEM_LIMIT = 56 * 1024 * 1024

_IN_COLS = dict(q=(0, 256), kvc=(256, 384), kv=(384, 640), gl=(640, 768), pool=(768, 1024),
                rwkv=(1024, 2304), dq=(2304, 2560), dkv=(2560, 3072))
IN_PACKED = 3072
RWKV_PACKED = 1280


def _dot(a, b):
    return jnp.dot(a.astype(BF16), b.astype(BF16), preferred_element_type=F32)


def _dot_nt(a, b):
    return lax.dot_general(a, b, (((1,), (1,)), ((), ())), preferred_element_type=F32)


def _split(x):
    hi = x.astype(BF16)
    return hi, (x - hi.astype(F32)).astype(BF16)


def _dot_split(a, b):
    hi, lo = _split(a)
    b = b.astype(BF16)
    return jnp.dot(hi, b, preferred_element_type=F32) + jnp.dot(lo, b, preferred_element_type=F32)


def _dot3(a, b, nt=False):
    f = _dot_nt if nt else functools.partial(jnp.dot, preferred_element_type=F32)
    a_hi, a_lo = _split(a)
    b_hi, b_lo = _split(b)
    return f(a_hi, b_hi) + f(a_lo, b_hi) + f(a_hi, b_lo)


def _sigmoid(x):
    return 1.0 / (1.0 + jnp.exp(-x))


def _iota(shape, dim):
    return lax.broadcasted_iota(jnp.int32, shape, dim)


def _rope_lanes(y, cos, sin):
    n = y.shape[-1]
    lane = _iota(y.shape, y.ndim - 1)
    partner = jnp.where((lane & 15) < 8, pltpu.roll(y, n - 8, y.ndim - 1), pltpu.roll(y, 8, y.ndim - 1))
    return y * cos + partner * sin


def _params(sem, vmem=VMEM_LIMIT):
    return pltpu.CompilerParams(dimension_semantics=sem, vmem_limit_bytes=vmem)


def _mod_kernel(c_ref, w_ref, b_ref, o_ref):
    c = c_ref[...]
    o_ref[...] = _dot3(c * _sigmoid(c), w_ref[...]) + b_ref[...]


def _modulation(c, ada_w, ada_b):
    L, D, E = ada_w.shape
    B = c.shape[0]
    tn = 1024
    out = pl.pallas_call(
        _mod_kernel,
        grid=(L, E // tn),
        in_specs=[pl.BlockSpec((B, D), lambda l, j: (0, 0)),
                  pl.BlockSpec((None, D, tn), lambda l, j: (l, 0, j)),
                  pl.BlockSpec((None, 1, tn), lambda l, j: (l, 0, j))],
        out_specs=pl.BlockSpec((None, B, tn), lambda l, j: (l, 0, j)),
        out_shape=jax.ShapeDtypeStruct((L, B, E), F32),
        compiler_params=_params(("parallel", "parallel")),
        name="modulation",
    )(c, ada_w, ada_b.reshape(L, 1, E))
    return out.reshape(L, B, 6, D)


def _rope_kernel(pos_ref, cpos_ref, freq_ref, sgn_ref, cos_ref, sin_ref, ccos_ref, csin_ref):
    freq = freq_ref[...]
    sgn = sgn_ref[...]
    ang = pos_ref[...].astype(F32) * freq
    cos_ref[...] = jnp.cos(ang)
    sin_ref[...] = jnp.sin(ang) * sgn
    cang = cpos_ref[...].astype(F32) * freq
    ccos_ref[...] = jnp.cos(cang)
    csin_ref[...] = jnp.sin(cang) * sgn


def _rope_tables(positions):
    B, S = positions.shape
    half = ROPE_DIM // 2
    inv_freq = ROPE_THETA ** (-2.0 * jnp.arange(half, dtype=F32) / ROPE_DIM)
    lane = np.arange(LANES)
    rot = (lane % HEAD_DIM) < ROPE_DIM
    freq = jnp.where(rot, inv_freq[lane % half], 0.0).reshape(1, LANES)
    sgn = jnp.asarray(np.where(rot, np.where(lane % HEAD_DIM < half, -1.0, 1.0), 0.0), F32).reshape(1, LANES)
    n_cmp = S // CMP_STRIDE
    cpos = positions[:, CMP_BLOCK - 1::CMP_STRIDE]
    cpos = jnp.pad(cpos, ((0, 0), (0, n_cmp - cpos.shape[1])))
    tab = jax.ShapeDtypeStruct((B, S, LANES), F32)
    ctab = jax.ShapeDtypeStruct((B, n_cmp, LANES), F32)
    return pl.pallas_call(
        _rope_kernel,
        grid=(B,),
        in_specs=[pl.BlockSpec((None, S, 1), lambda b: (b, 0, 0)),
                  pl.BlockSpec((None, n_cmp, 1), lambda b: (b, 0, 0)),
                  pl.BlockSpec((1, LANES), lambda b: (0, 0)),
                  pl.BlockSpec((1, LANES), lambda b: (0, 0))],
        out_specs=[pl.BlockSpec((None, S, LANES), lambda b: (b, 0, 0)),
                   pl.BlockSpec((None, S, LANES), lambda b: (b, 0, 0)),
                   pl.BlockSpec((None, n_cmp, LANES), lambda b: (b, 0, 0)),
                   pl.BlockSpec((None, n_cmp, LANES), lambda b: (b, 0, 0))],
        out_shape=[tab, tab, ctab, ctab],
        compiler_params=_params(("parallel",)),
        name="rope_tables",
    )(positions.reshape(B, S, 1), cpos.reshape(B, n_cmp, 1), freq, sgn)


_W_IN_MOVES = ((0, 652, 0), (652, 908, 768), (908, 1676, 1024), (1676, 1740, 1792), (1740, 1804, 1920),
               (1804, 1964, 2048), (1964, 2732, 2304))


def _in_kernel(x_ref, mod_ref, g_ref, w_ref, *refs):
    out_refs, wp_ref = refs[:-1], refs[-1]

    @pl.when((pl.program_id(0) == 0) & (pl.program_id(1) == 0))
    def _pack_weights():
        rows = 128

        def body(r, _):
            r0 = pl.multiple_of(r * rows, rows)
            w = w_ref[pl.ds(r0, rows), :]
            wp_ref[pl.ds(r0, rows), :] = jnp.zeros((rows, IN_PACKED), BF16)
            for lo, hi, dst in _W_IN_MOVES:
                wp_ref[pl.ds(r0, rows), dst:dst + hi - lo] = w[:, lo:hi].astype(BF16)
            return 0

        lax.fori_loop(0, w_ref.shape[0] // rows, body, 0)

    x = x_ref[...]
    ms = jnp.mean(x * x, axis=-1, keepdims=True)
    y = x * lax.rsqrt(ms + RMS_EPS) * g_ref[...]
    h = y * (1.0 + mod_ref[1:2, :]) + mod_ref[0:1, :]
    p = jnp.dot(h.astype(BF16), wp_ref[...], preferred_element_type=F32)
    for ref, (lo, hi) in zip(out_refs, _IN_COLS.values()):
        ref[...] = p[:, lo:hi]


def _in_proj(x, mod_l, g, w_in, layer):
    B, S, D = x.shape
    tm = 1024
    widths = [hi - lo for lo, hi in _IN_COLS.values()]
    return pl.pallas_call(
        _in_kernel,
        grid=(B, S // tm),
        in_specs=[pl.BlockSpec((None, tm, D), lambda b, i: (b, i, 0)),
                  pl.BlockSpec((None, 6, D), lambda b, i: (b, 0, 0)),
                  pl.BlockSpec((1, D), lambda b, i: (0, 0)),
                  pl.BlockSpec((None,) + w_in.shape[1:], lambda b, i: (layer, 0, 0), pipeline_mode=pl.Buffered(1))],
        out_specs=[pl.BlockSpec((None, tm, w), lambda b, i: (b, i, 0)) for w in widths],
        out_shape=[jax.ShapeDtypeStruct((B, S, w), F32) for w in widths],
        scratch_shapes=[pltpu.VMEM((D, IN_PACKED), BF16)],
        compiler_params=_params(("arbitrary", "arbitrary")),
        name="in_proj",
    )(x, mod_l, g.reshape(1, D), w_in)


def _pack_rwkv_row(v):
    z = lambda n: jnp.zeros((n,), v.dtype)
    return jnp.concatenate([v[0:768], v[768:832], z(64), v[832:896], z(64), v[896:1056], z(96)]).reshape(1, RWKV_PACKED)


def _pool_kernel(u_ref, w_ref, scale_ref, o_ref):
    u = u_ref[...]
    row = _iota(u.shape, 0)
    lane = _iota(u.shape, 1)

    def lag(x, s):
        return jnp.where(row >= s, pltpu.roll(x, s, 0), 0.0)

    s2 = u + lag(u, 1)
    s4 = s2 + lag(s2, 2)
    s8 = s4 + lag(s4, 4)
    s16 = s8 + lag(s8, 8)
    g = lane // (GROUP_WIDTH // len(POOL_WINDOWS))
    tot = jnp.where(g == 0, s2, jnp.where(g == 1, s4, jnp.where(g == 2, s8, s16)))
    win = jnp.where(g == 0, 2.0, jnp.where(g == 1, 4.0, jnp.where(g == 2, 8.0, 16.0)))
    cnt = jnp.minimum((row + 1).astype(F32), win)
    o_ref[...] = _dot(tot / cnt - u, w_ref[...]) * scale_ref[...]


def _pool(u, pool_w, pool_scale):
    B, S, W = u.shape
    wbd = jax.scipy.linalg.block_diag(*[pool_w[i] for i in range(pool_w.shape[0])]).astype(BF16)
    return pl.pallas_call(
        _pool_kernel,
        grid=(B,),
        in_specs=[pl.BlockSpec((None, S, W), lambda b: (b, 0, 0)),
                  pl.BlockSpec((W, W), lambda b: (0, 0)),
                  pl.BlockSpec((1, W), lambda b: (0, 0))],
        out_specs=pl.BlockSpec((None, S, W), lambda b: (b, 0, 0)),
        out_shape=jax.ShapeDtypeStruct((B, S, W), F32),
        compiler_params=_params(("parallel",)),
        name="pool_mixer",
    )(u, wbd, pool_scale.reshape(1, W))


def _head_mean_matrix():
    blk = np.kron(np.eye(N_HEADS), np.ones((HEAD_DIM, HEAD_DIM))) / HEAD_DIM
    return jnp.asarray(blk, F32)


def _pad_heads(x256, as_bf16=True):
    n = x256.shape[0]
    lane = _iota((n, LANES), 1)
    out = []
    for h in range(N_HEADS):
        blk = x256[:, (h // 2) * LANES:(h // 2 + 1) * LANES]
        if h % 2:
            blk = pltpu.roll(blk, HEAD_DIM, 1)
        blk = jnp.where(lane < HEAD_DIM, blk, 0.0)
        out.append(blk.astype(BF16) if as_bf16 else blk)
    return out


def _merge_heads(slabs):
    n = slabs[0].shape[0]
    lane = _iota((n, LANES), 1)
    cols = []
    for p in range(N_HEADS // 2):
        even = pltpu.roll(slabs[2 * p], HEAD_DIM, 1)
        cols.append(jnp.where(lane < HEAD_DIM, even, slabs[2 * p + 1]))
    return jnp.concatenate(cols, axis=1)


def _flash_step(carries, qs, kas, vbs, biases):
    idx = range(len(qs))
    s = [_dot_nt(qs[i], kas[i]) for i in idx]
    if biases is not None:
        s = [s[i] + biases[i] for i in idx]
    m_new = [jnp.maximum(carries[i][0], jnp.max(s[i], axis=-1, keepdims=True)) for i in idx]
    alpha = [jnp.exp2(carries[i][0] - m_new[i]) for i in idx]
    p = [jnp.exp2(s[i] - m_new[i]).astype(BF16) for i in idx]
    pv = [jnp.dot(p[i], vbs[i], preferred_element_type=F32) for i in idx]
    return tuple((m_new[i], alpha[i] * carries[i][1] + pv[i]) for i in idx)


def _flash_init(n):
    return (jnp.full((n, 1), NEG_INF, F32), jnp.zeros((n, LANES), F32))


def _flash_finish(carry):
    acc = carry[1]
    return acc / pltpu.roll(acc, HEAD_DIM, 1)


def _nsa_kernel(q_ref, gl_ref, kc_ref, kv_ref, cos_ref, sin_ref, ccos_ref, csin_ref,
                qn_ref, kn_ref, pet_ref, peb_ref, wt_ref, wb_ref, w2_ref,
                hb_ref, cover_ref, eg_ref, ctab_ref, wtab_ref,
                o_ref, ksvs_ref, kwvw_ref, vs1_ref, vw1_ref, kvc_ref):
    S = kv_ref.shape[0]
    qi = pl.program_id(1)
    n_cmp = S // CMP_STRIDE

    @pl.when(qi == 0)
    def _prep_keys():
        rows = 512

        def body(i, _):
            r0 = pl.multiple_of(i * rows, rows)
            cs = cos_ref[pl.ds(r0, rows), :]
            sn = sin_ref[pl.ds(r0, rows), :]
            lane = _iota((rows, LANES), 1)
            first = lane < HEAD_DIM
            blk_hot = jnp.where(lane - HEAD_DIM == (r0 + _iota((rows, LANES), 0)) // SEL_BLOCK, 1.0, 0.0)
            for col, j, dst, dst1 in ((0, 1, ksvs_ref, vs1_ref), (LANES, 2, kwvw_ref, vw1_ref)):
                slab = kv_ref[pl.ds(r0, rows), col:col + LANES]
                ms = jnp.sum(jnp.where(first, slab * slab, 0.0), axis=-1, keepdims=True) * (1.0 / HEAD_DIM)
                y = slab * lax.rsqrt(ms + RMS_EPS) * kn_ref[j:j + 1, :]
                dst[pl.ds(r0, rows), :] = jnp.where(first, _rope_lanes(y, cs, sn), blk_hot if j == 1 else slab).astype(BF16)
                dst1[pl.ds(r0, rows), :] = jnp.where(first, 1.0, slab).astype(BF16)
            return 0

        lax.fori_loop(0, S // rows, body, 0)

        top = jnp.zeros((n_cmp, LANES), F32)
        bot = jnp.zeros((n_cmp, LANES), F32)
        for i in range(CMP_STRIDE):
            tok = kc_ref[pl.ds(i, n_cmp, stride=CMP_STRIDE), :]
            cols = slice(i * LANES, (i + 1) * LANES)
            top = top + _dot3(tok + pet_ref[:, cols], wt_ref[cols, :])
            bot = bot + _dot3(tok + peb_ref[:, cols], wb_ref[cols, :])
        pre = top + pltpu.roll(bot, n_cmp - 1, 0)
        act = 0.5 * pre * (1.0 + jnp.tanh(float(np.sqrt(2.0 / np.pi)) * (pre + 0.044715 * (pre * pre * pre))))
        out = _dot3(act, w2_ref[...])
        first = _iota(out.shape, 1) < HEAD_DIM
        ms = jnp.sum(jnp.where(first, out * out, 0.0), axis=-1, keepdims=True) * (1.0 / HEAD_DIM)
        y = out * lax.rsqrt(ms + RMS_EPS) * kn_ref[0:1, :]
        kvc_ref[...] = jnp.where(first, _rope_lanes(y, ccos_ref[...], csin_ref[...]), out)

    r0 = pl.multiple_of(qi * TQ, TQ)
    q = q_ref[...]
    ms = _dot_split(q * q, hb_ref[...])
    qn = q * lax.rsqrt(ms + RMS_EPS) * qn_ref[...]
    cs = cos_ref[pl.ds(r0, TQ), :]
    sn = sin_ref[pl.ds(r0, TQ), :]
    qr = _rope_lanes(qn, jnp.concatenate([cs, cs], axis=1), jnp.concatenate([sn, sn], axis=1)) * (HEAD_DIM ** -0.5)
    qst = jnp.concatenate(_pad_heads(qr, as_bf16=False), axis=0)
    n_rows = N_HEADS * TQ

    kvc = kvc_ref[...]
    s = _dot3(qst, kvc, nt=True)
    t_st = r0 + (_iota((n_rows, n_cmp), 0) & (TQ - 1))
    c_end = _iota((n_rows, n_cmp), 1) * CMP_STRIDE + (CMP_BLOCK - 1)
    mask = c_end <= t_st
    sm = jnp.where(mask, s, NEG_INF)
    m = jnp.max(sm, axis=-1, keepdims=True)
    e = jnp.where(mask, jnp.exp(sm - m), 0.0)
    p = e / jnp.maximum(jnp.sum(e, axis=-1, keepdims=True), TINY)
    o_cmp = _dot(p, kvc)

    psum = p[0:TQ] + p[TQ:2 * TQ] + p[2 * TQ:3 * TQ] + p[3 * TQ:4 * TQ]
    ps_hi, ps_lo = _split(psum.T)
    cover_t = cover_ref[...].astype(BF16)
    imp = (jnp.dot(cover_t, ps_hi, preferred_element_type=F32) + jnp.dot(cover_t, ps_lo, preferred_element_type=F32))
    n_blk = S // SEL_BLOCK
    blk = _iota((n_blk, TQ), 0)
    cur = (r0 + _iota((n_blk, TQ), 1)) // SEL_BLOCK
    forced = (blk == 0) | (blk == cur) | (blk == cur - 1)
    imp = jnp.where(blk > cur, -1.0, jnp.where(forced, FORCE_SCORE, imp))
    rank = jnp.zeros((n_blk, TQ), F32)
    for d in range(1, n_blk):
        other = pltpu.roll(imp, n_blk - d, 0)
        rank = rank + jnp.where(blk + d >= n_blk,
                                jnp.where(other >= imp, 1.0, 0.0),
                                jnp.where(other > imp, 1.0, 0.0))
    sel_rows = jnp.where(rank < SEL_TOP_N, 0.0, NEG_INF)
    sel_bias = jnp.concatenate([jnp.zeros((HEAD_DIM, TQ), F32), sel_rows,
                                jnp.zeros((LANES - HEAD_DIM - n_blk, TQ), F32)], axis=0).T.astype(BF16)
    lane = _iota((TQ, LANES), 1)

    qb = (qst * LOG2E).astype(BF16)
    heads = range(N_HEADS)
    q_heads = [qb[h * TQ:(h + 1) * TQ] for h in heads]
    q_heads_sel = [jnp.where(lane < HEAD_DIM, q, sel_bias) for q in q_heads]
    init = tuple(_flash_init(TQ) for _ in heads)

    def sel_tile(j, carry, biases):
        k0 = pl.multiple_of(j * SEL_TK, SEL_TK)
        ka, vb = ksvs_ref[pl.ds(k0, SEL_TK), :], vs1_ref[pl.ds(k0, SEL_TK), :]
        return _flash_step(carry, q_heads_sel, [ka] * N_HEADS, [vb] * N_HEADS, biases)

    ratio = SEL_TK // TQ
    last = qi // ratio
    res = lax.fori_loop(0, last, lambda j, c: sel_tile(j, c, None), init)
    res = sel_tile(last, res, [ctab_ref[qi % ratio]] * N_HEADS)
    o_sel = jnp.concatenate([_flash_finish(c) for c in res], axis=0)

    n_prev = WINDOW // TQ
    span = (n_prev + 1) * TQ
    k0 = pl.multiple_of(jnp.maximum(qi - n_prev, 0) * TQ, TQ)
    ka, vb = kwvw_ref[pl.ds(k0, span), :], vw1_ref[pl.ds(k0, span), :]
    res = _flash_step(init, q_heads, [ka] * N_HEADS, [vb] * N_HEADS, [wtab_ref[jnp.minimum(qi, n_prev)]] * N_HEADS)
    o_win = jnp.concatenate([_flash_finish(c) for c in res], axis=0)

    gate = _dot_split(_sigmoid(gl_ref[...]), eg_ref[...])
    split = lambda o: _merge_heads([o[h * TQ:(h + 1) * TQ] for h in range(N_HEADS)])
    W = GROUP_WIDTH
    o_ref[...] = (gate[:, 0:W] * split(o_cmp) + gate[:, W:2 * W] * split(o_sel) + gate[:, 2 * W:3 * W] * split(o_win))


def _nsa(q, kvc, kv, gl, cos, sin, ccos, csin, q_norm, k_norm, cmp_pe, cmp_w1, cmp_w2):
    B, S, _ = q.shape
    n_cmp = S // CMP_STRIDE
    n_blk = S // SEL_BLOCK
    hd = HEAD_DIM

    qn = jnp.tile(q_norm, N_HEADS).reshape(1, GROUP_WIDTH)
    kn = jnp.concatenate([k_norm, jnp.ones_like(k_norm)], axis=1)

    def interleave(a, b):
        return jnp.concatenate([a, b], axis=1).reshape(1, -1)

    half = CMP_BLOCK // 2
    pet = interleave(cmp_pe[0, :half], cmp_pe[1, :half])
    peb = interleave(cmp_pe[0, half:], cmp_pe[1, half:])

    def w1_half(lo):
        wk = cmp_w1[0].reshape(CMP_BLOCK, hd, hd)[lo:lo + half]
        wv = cmp_w1[1].reshape(CMP_BLOCK, hd, hd)[lo:lo + half]
        z = jnp.zeros_like(wk)
        top = jnp.concatenate([wk, z], axis=2)
        bot = jnp.concatenate([z, wv], axis=2)
        return jnp.concatenate([top, bot], axis=1).reshape(half * 2 * hd, 2 * hd)

    wt, wb = w1_half(0), w1_half(half)
    w2 = jax.scipy.linalg.block_diag(cmp_w2[0], cmp_w2[1])
    hb = _head_mean_matrix()

    c_end = np.arange(n_cmp) * CMP_STRIDE + CMP_BLOCK - 1
    b_start = np.arange(n_blk) * SEL_BLOCK
    cover = np.maximum(np.minimum(c_end[:, None] + 1, b_start[None, :] + SEL_BLOCK)
                       - np.maximum(c_end[:, None] + 1 - CMP_BLOCK, b_start[None, :]), 0).astype(np.float32) / CMP_BLOCK
    cover[n_cmp - 1] = 0.0
    cover = jnp.asarray(cover.T, F32)
    eg = np.zeros((LANES, 3 * GROUP_WIDTH), np.float32)
    for h in range(N_HEADS):
        for br in range(3):
            eg[h * 3 + br, br * GROUP_WIDTH + h * hd: br * GROUP_WIDTH + (h + 1) * hd] = 1.0
    eg = jnp.asarray(eg)
    ratio = SEL_TK // TQ
    ctab = np.where(np.arange(SEL_TK)[None, None, :] <= np.arange(ratio)[:, None, None] * TQ + np.arange(TQ)[None, :, None],
                    0.0, NEG_INF).astype(np.float32)
    n_prev = WINDOW // TQ
    span = (n_prev + 1) * TQ
    dist = (np.arange(n_prev + 1)[:, None, None] * TQ + np.arange(TQ)[None, :, None] - np.arange(span)[None, None, :])
    wtab = np.where((dist >= 0) & (dist < WINDOW), 0.0, NEG_INF).astype(np.float32)
    ctab, wtab = jnp.asarray(ctab), jnp.asarray(wtab)

    full = lambda shape: pl.BlockSpec(shape, lambda b, i: (0,) * len(shape))
    return pl.pallas_call(
        _nsa_kernel,
        grid=(B, S // TQ),
        in_specs=[pl.BlockSpec((None, TQ, GROUP_WIDTH), lambda b, i: (b, i, 0)),
                  pl.BlockSpec((None, TQ, LANES), lambda b, i: (b, i, 0)),
                  pl.BlockSpec((None, S, LANES), lambda b, i: (b, 0, 0)),
                  pl.BlockSpec((None, S, 2 * LANES), lambda b, i: (b, 0, 0)),
                  pl.BlockSpec((None, S, LANES), lambda b, i: (b, 0, 0)),
                  pl.BlockSpec((None, S, LANES), lambda b, i: (b, 0, 0)),
                  pl.BlockSpec((None, n_cmp, LANES), lambda b, i: (b, 0, 0)),
                  pl.BlockSpec((None, n_cmp, LANES), lambda b, i: (b, 0, 0)),
                  full((1, GROUP_WIDTH)), full((3, LANES)),
                  full((1, CMP_STRIDE * LANES)), full((1, CMP_STRIDE * LANES)),
                  full((CMP_STRIDE * LANES, LANES)), full((CMP_STRIDE * LANES, LANES)), full((LANES, LANES)),
                  full((GROUP_WIDTH, GROUP_WIDTH)), full((n_blk, n_cmp)),
                  full((LANES, 3 * GROUP_WIDTH)), full(ctab.shape), full(wtab.shape)],
        out_specs=pl.BlockSpec((None, TQ, GROUP_WIDTH), lambda b, i: (b, i, 0)),
        out_shape=jax.ShapeDtypeStruct((B, S, GROUP_WIDTH), F32),
        scratch_shapes=[pltpu.VMEM((S, LANES), BF16), pltpu.VMEM((S, LANES), BF16),
                        pltpu.VMEM((S, LANES), BF16), pltpu.VMEM((S, LANES), BF16),
                        pltpu.VMEM((n_cmp, LANES), F32)],
        compiler_params=_params(("parallel", "arbitrary")),
        name="nsa",
    )(q, gl, kvc, kv, cos, sin, ccos, csin, qn, kn, pet, peb, wt, wb, w2, hb, cover, eg, ctab, wtab)


def _dil_kernel(q_ref, kv_ref, cos_ref, sin_ref, qn_ref, kn_ref, hb_ref, tb_ref, o_ref, kvs_ref, v1s_ref):
    S = kv_ref.shape[0]
    qi = pl.program_id(1)
    W = GROUP_WIDTH

    @pl.when(qi == 0)
    def _prep_keys():
        rows = 512

        def body(i, _):
            r0 = pl.multiple_of(i * rows, rows)
            cs = cos_ref[pl.ds(r0, rows), :]
            sn = sin_ref[pl.ds(r0, rows), :]
            k = kv_ref[pl.ds(r0, rows), 0:W]
            v = kv_ref[pl.ds(r0, rows), W:2 * W]
            ms = _dot_split(k * k, hb_ref[...])
            kn = k * lax.rsqrt(ms + RMS_EPS) * kn_ref[...]
            kr = _rope_lanes(kn, jnp.concatenate([cs, cs], axis=1), jnp.concatenate([sn, sn], axis=1))
            lane = _iota((rows, LANES), 1)
            for h in range(N_HEADS):
                kb = kr[:, (h // 2) * LANES:(h // 2 + 1) * LANES]
                vb = v[:, (h // 2) * LANES:(h // 2 + 1) * LANES]
                if h % 2:
                    kb = pltpu.roll(kb, HEAD_DIM, 1)
                else:
                    vb = pltpu.roll(vb, HEAD_DIM, 1)
                kvs_ref[h, pl.ds(r0, rows), :] = jnp.where(lane < HEAD_DIM, kb, vb).astype(BF16)
                v1s_ref[h, pl.ds(r0, rows), :] = jnp.where(lane < HEAD_DIM, 1.0, vb).astype(BF16)
            return 0

        lax.fori_loop(0, S // rows, body, 0)

    r0 = pl.multiple_of(qi * DIL_TQ, DIL_TQ)
    q = q_ref[...]
    ms = _dot_split(q * q, hb_ref[...])
    qn = q * lax.rsqrt(ms + RMS_EPS) * qn_ref[...]
    cs = cos_ref[pl.ds(r0, DIL_TQ), :]
    sn = sin_ref[pl.ds(r0, DIL_TQ), :]
    qr = _rope_lanes(qn, jnp.concatenate([cs, cs], axis=1), jnp.concatenate([sn, sn], axis=1)) * (HEAD_DIM ** -0.5 * LOG2E)
    qh = _pad_heads(qr)
    ratio = TK // DIL_TQ

    def body(j, carry):
        k0 = pl.multiple_of(j * TK, TK)
        bias = tb_ref[qi - ratio * j]
        return _flash_step(carry, qh, [kvs_ref[h, pl.ds(k0, TK), :] for h in range(N_HEADS)],
                           [v1s_ref[h, pl.ds(k0, TK), :] for h in range(N_HEADS)], [bias] * N_HEADS)

    res = lax.fori_loop(0, qi // ratio + 1, body, tuple(_flash_init(DIL_TQ) for _ in range(N_HEADS)))
    o_ref[...] = _merge_heads([_flash_finish(c) for c in res])


def _dil_bias_table(S):
    n = S // DIL_TQ
    d = (np.arange(n)[:, None, None] * DIL_TQ + np.arange(DIL_TQ)[None, :, None] - np.arange(TK)[None, None, :])
    cnt = np.zeros(d.shape, np.float32)
    for window, dil in DIL_PATTERNS:
        cnt += ((d >= 0) & (d <= window) & (d % dil == 0)).astype(np.float32)
    return jnp.asarray(np.where(cnt > 0, np.log2(np.maximum(cnt, 1.0)), NEG_INF).astype(np.float32))


def _dilated(q, kv, cos, sin, q_norm, k_norm):
    B, S, W = q.shape
    tb = _dil_bias_table(S)
    qn = jnp.tile(q_norm, N_HEADS).reshape(1, W)
    kn = jnp.tile(k_norm, N_HEADS).reshape(1, W)
    hb = _head_mean_matrix()
    full = lambda shape: pl.BlockSpec(shape, lambda b, i: (0,) * len(shape))
    return pl.pallas_call(
        _dil_kernel,
        grid=(B, S // DIL_TQ),
        in_specs=[pl.BlockSpec((None, DIL_TQ, W), lambda b, i: (b, i, 0)),
                  pl.BlockSpec((None, S, 2 * W), lambda b, i: (b, 0, 0)),
                  pl.BlockSpec((None, S, LANES), lambda b, i: (b, 0, 0)),
                  pl.BlockSpec((None, S, LANES), lambda b, i: (b, 0, 0)),
                  full((1, W)), full((1, W)), full((W, W)), full(tb.shape)],
        out_specs=pl.BlockSpec((None, DIL_TQ, W), lambda b, i: (b, i, 0)),
        out_shape=jax.ShapeDtypeStruct((B, S, W), F32),
        scratch_shapes=[pltpu.VMEM((N_HEADS, S, LANES), BF16), pltpu.VMEM((N_HEADS, S, LANES), BF16)],
        compiler_params=_params(("parallel", "arbitrary")),
        name="dilated",
    )(q, kv, cos, sin, qn, kn, hb, tb)


def _softplus(x):
    return jnp.maximum(x, 0.0) + jnp.log(1.0 + jnp.exp(-jnp.abs(x)))


def _rwkv_kernel(first, *refs):
    if first:
        (z_ref, mu_ref, w0_ref, w2_ref, a0_ref, a2_ref, g2_ref, kk_ref, ka_ref, rk_ref, lnw_ref, lnb_ref,
         hb_ref, y_ref, vf_out_ref, st_ref, prev_ref, ro_ref, ub_ref, colb_ref, g_ref, t2_ref) = refs
    else:
        (z_ref, vf_ref, mu_ref, w0_ref, w2_ref, a0_ref, a2_ref, g2_ref, kk_ref, ka_ref, rk_ref, lnw_ref, lnb_ref,
         v0_ref, v1_ref, v2_ref, hb_ref, y_ref, st_ref, prev_ref, ro_ref, ub_ref, colb_ref, g_ref, t2_ref) = refs
    SB = z_ref.shape[0]
    C = RWKV_CHUNK
    W = GROUP_WIDTH
    n_pairs = N_HEADS // 2
    R = RWKV_GROUP * C

    @pl.when(pl.program_id(1) == 0)
    def _reset():
        st_ref[...] = jnp.zeros_like(st_ref)
        prev_ref[...] = jnp.zeros_like(prev_ref)

    ri = _iota((C, C), 0)
    ci = _iota((C, C), 1)
    tri_incl = jnp.where(ci <= ri, 1.0, 0.0).astype(BF16)
    rr = _iota((2 * LANES, 2 * LANES), 0)
    cc = _iota((2 * LANES, 2 * LANES), 1)
    g_mask = ((((rr >> 6) & 1) == ((cc >> 6) & 1))
              & ((cc & (C - 1)) <= jnp.where(rr < LANES, (rr & (C - 1)) - 1, rr & (C - 1))))
    r2 = _iota((LANES, LANES), 0)
    c2 = _iota((LANES, LANES), 1)
    eye_f = jnp.where(r2 == c2, 1.0, 0.0)
    lane_lo = _iota((C, LANES), 1) < HEAD_DIM
    hb = hb_ref[...].astype(BF16)

    def stack(x):
        return jnp.concatenate([jnp.where(lane_lo, x, 0.0), jnp.where(lane_lo, 0.0, x)], axis=0)

    def phase1(i, _):
        t0 = pl.multiple_of(i * R, R)
        zc = z_ref[pl.ds(t0, R), :]
        inside = jnp.where(i > 0, 1.0, 0.0)
        prev = inside * z_ref[pl.ds(jnp.maximum(t0 - 1, 0), 1), :] + (1.0 - inside) * prev_ref[...]
        zp = jnp.where(_iota(zc.shape, 0) == 0, prev, pltpu.roll(zc, 1, 0))
        zs = zc + (zp - zc) * mu_ref[...]
        r, k, v = zs[:, 0:W], zs[:, W:2 * W], zs[:, 2 * W:3 * W]
        xw, xa, xg = zs[:, 768:896], zs[:, 896:1024], zs[:, 1024:1280]
        w_log = -_softplus(-(w0_ref[...] + _dot(jnp.tanh(xw), w2_ref[...]))) - 0.5
        lw = -jnp.exp(w_log)
        a = _sigmoid(a0_ref[...] + _dot(xa, a2_ref[...]))
        g = _dot(_sigmoid(xg), g2_ref[...])
        if first:
            vf_out_ref[pl.ds(t0, R), :] = v
        else:
            vf = vf_ref[pl.ds(t0, R), :]
            v = v + (vf - v) * _sigmoid(v0_ref[...] + _dot(_dot(v, v1_ref[...]), v2_ref[...]))
        kkr = k * kk_ref[...]
        nrm = jnp.sqrt(_dot_split(kkr * kkr, hb) * HEAD_DIM)
        kk = kkr / jnp.maximum(nrm, 1e-12)
        k2 = k * (1.0 + (a - 1.0) * ka_ref[...])
        bonus = _dot_split(r * k2 * rk_ref[...], hb) * HEAD_DIM * v
        g_ref[pl.ds(t0, R), :] = g
        t2_ref[pl.ds(t0, R), :] = (lnb_ref[...] + bonus) * g

        items = [(c, p) for c in range(RWKV_GROUP) for p in range(n_pairs)]
        idx = range(len(items))
        sub = lambda x, c, p: x[c * C:(c + 1) * C, p * LANES:(p + 1) * LANES]
        cum = [_cumsum_rows(tri_incl, sub(lw, c, p)) for c, p in items]
        a_st, r_st, v_st, gm, tails = [], [], [], [], []
        for j, (c, p) in enumerate(items):
            lwp = sub(lw, c, p)
            cum_end = cum[j][C - 1:C, :]
            e_pos, e_neg = jnp.exp(cum[j]), jnp.exp(-cum[j])
            e_exc, e_tail = jnp.exp(cum[j] - lwp), jnp.exp(cum_end - cum[j])
            kkp, k2p = sub(kk, c, p), sub(k2, c, p)
            ka_p = kkp * sub(a, c, p)
            a_st.append(stack(-kkp * e_exc).astype(BF16))
            r_st.append(stack(sub(r, c, p) * e_pos))
            v_st.append(stack(sub(v, c, p)).astype(BF16))
            bhat = (ka_p * e_neg).astype(BF16)
            khat = (k2p * e_neg).astype(BF16)
            tails.append(jnp.concatenate([stack(ka_p * e_tail), stack(k2p * e_tail)], axis=0))
            gm.append(jnp.where(g_mask,
                                _dot_nt(jnp.concatenate([a_st[j], r_st[j].astype(BF16)], axis=0),
                                        jnp.concatenate([bhat, bhat, khat, khat], axis=0)), 0.0))
            colb_ref[i * RWKV_GROUP + c, p] = jnp.broadcast_to(jnp.exp(cum_end), (LANES, LANES)).T
        tt = [tails[j].T.astype(BF16) for j in idx]
        lakv = [_dot(gm[j][0:LANES, LANES:2 * LANES], v_st[j]) for j in idx]
        mr = [gm[j][LANES:2 * LANES, :].astype(BF16) for j in idx]
        lp = [gm[j][0:LANES, 0:LANES] for j in idx]
        tinv = [eye_f + lp[j] for j in idx]
        for _ in range(5):
            lpb = [lp[j].astype(BF16) for j in idx]
            lp = [jnp.dot(lpb[j], lpb[j], preferred_element_type=F32) for j in idx]
            tinv = [tinv[j] + _dot(lp[j], tinv[j]) for j in idx]
        wu = [_dot(tinv[j], jnp.concatenate([a_st[j].astype(F32), lakv[j]], axis=1)) for j in idx]
        wm = [wu[j][:, 0:LANES].astype(BF16) for j in idx]
        u = [wu[j][:, LANES:2 * LANES].astype(BF16) for j in idx]
        mwbw = [jnp.dot(jnp.concatenate([mr[j][:, 0:LANES], tt[j][:, 0:LANES]], axis=0), wm[j],
                        preferred_element_type=F32) for j in idx]
        ub = [jnp.dot(jnp.concatenate([mr[j], tt[j]], axis=0), jnp.concatenate([u[j], v_st[j]], axis=0),
                      preferred_element_type=F32) for j in idx]
        for j, (c, p) in enumerate(items):
            n = i * RWKV_GROUP + c
            ro_ref[n, p] = jnp.concatenate([r_st[j] + mwbw[j][0:LANES], mwbw[j][LANES:2 * LANES]], axis=0).astype(BF16)
            ub_ref[n, p] = ub[j]
        return 0

    def phase2(n, _):
        t0 = pl.multiple_of(n * C, C)
        pairs = range(n_pairs)
        st = [st_ref[p] for p in pairs]
        res = [jnp.dot(ro_ref[n, p], st[p].astype(BF16), preferred_element_type=F32) for p in pairs]
        ub = [ub_ref[n, p] for p in pairs]
        for p in pairs:
            st_ref[p] = colb_ref[n, p] * st[p] + res[p][LANES:2 * LANES] + ub[p][LANES:2 * LANES]
        o_st = [res[p][0:LANES] + ub[p][0:LANES] for p in pairs]
        y_ref[pl.ds(t0, C), :] = jnp.concatenate([o[0:C] + o[C:2 * C] for o in o_st], axis=1)
        return 0

    lax.fori_loop(0, SB // R, phase1, 0)
    lax.fori_loop(0, SB // C, phase2, 0)
    prev_ref[...] = z_ref[SB - 1:SB, :]

    tiles = [slice(t * LANES, (t + 1) * LANES) for t in range(SB // LANES)]
    o = [y_ref[rows, :] for rows in tiles]
    dev = [x - _dot_split(x, hb) for x in o]
    var = [_dot_split(d * d, hb) for d in dev]
    for rows, d, v in zip(tiles, dev, var):
        y_ref[rows, :] = d * lax.rsqrt(v + RWKV_GN_EPS) * (lnw_ref[...] * g_ref[rows, :]) + t2_ref[rows, :]


def _cumsum_rows(tri_incl, x):
    hi = x.astype(BF16)
    lo = (x - hi.astype(F32)).astype(BF16)
    return jnp.dot(tri_incl, hi, preferred_element_type=F32) + jnp.dot(tri_incl, lo, preferred_element_type=F32)


RWKV_BLOCK = 512


def _rwkv(z, v_first, v_res, mu, w0, w2, a0, a2, g2, k_k, k_a, r_k, ln_w, ln_b):
    B, S, P = z.shape
    W = GROUP_WIDTH
    SB = RWKV_BLOCK
    ncb = SB // RWKV_CHUNK
    n_pairs = N_HEADS // 2
    first = v_res is None
    row = lambda v: v.reshape(1, W)
    padr = lambda m, n: jnp.pad(m, ((0, n - m.shape[0]), (0, 0)))
    seq = lambda w: pl.BlockSpec((None, SB, w), lambda b, j: (b, j, 0))
    full = lambda shape: pl.BlockSpec(shape, lambda b, j: (0,) * len(shape))
    args = [z]
    specs = [seq(P)]
    if not first:
        args.append(v_first)
        specs.append(seq(W))
    args += [_pack_rwkv_row(mu), row(w0), padr(w2, LANES), row(a0), padr(a2, LANES), padr(g2, 2 * LANES),
             row(k_k), row(k_a), r_k.reshape(1, W), row(ln_w), row(ln_b)]
    specs += [full((1, P)), full((1, W)), full((LANES, W)), full((1, W)), full((LANES, W)), full((2 * LANES, W)),
              full((1, W)), full((1, W)), full((1, W)), full((1, W)), full((1, W))]
    if not first:
        v0, v1, v2 = v_res
        args += [row(v0), jnp.pad(v1, ((0, 0), (0, LANES - v1.shape[1]))), padr(v2, LANES)]
        specs += [full((1, W)), full((W, LANES)), full((LANES, W))]
    args.append(_head_mean_matrix())
    specs.append(full((W, W)))
    out_shape = [jax.ShapeDtypeStruct((B, S, W), F32)]
    out_specs = [seq(W)]
    if first:
        out_shape.append(jax.ShapeDtypeStruct((B, S, W), F32))
        out_specs.append(seq(W))
    res = pl.pallas_call(
        functools.partial(_rwkv_kernel, first),
        grid=(B, S // SB),
        in_specs=specs,
        out_specs=out_specs,
        out_shape=out_shape,
        scratch_shapes=[pltpu.VMEM((n_pairs, LANES, LANES), F32), pltpu.VMEM((1, P), F32),
                        pltpu.VMEM((ncb, n_pairs, 2 * LANES, LANES), BF16),
                        pltpu.VMEM((ncb, n_pairs, 2 * LANES, LANES), F32),
                        pltpu.VMEM((ncb, n_pairs, LANES, LANES), F32),
                        pltpu.VMEM((SB, W), F32), pltpu.VMEM((SB, W), F32)],
        compiler_params=_params(("parallel", "arbitrary")),
        name="rwkv7_first" if first else "rwkv7",
    )(*args)
    return (res[0], res[1]) if first else (res[0], v_first)


ROUTER_ROWS = 32


def _expert_lane(e):
    return (e % 4) * 8 + e // 4


def _out_kernel(x_ref, ya_ref, yb_ref, yc_ref, yd_ref, wo_ref, mod_ref, g_ref, rwh_ref, rwl_ref, rb_ref,
                xo_ref, h_ref, gates_ref):
    y = jnp.concatenate([ya_ref[...].astype(BF16), yb_ref[...].astype(BF16),
                         yc_ref[...].astype(BF16), yd_ref[...].astype(BF16)], axis=1)
    mix = jnp.dot(y, wo_ref[...], preferred_element_type=F32)
    x = x_ref[...] + mod_ref[2:3, :] * mix
    xo_ref[...] = x
    ms = jnp.mean(x * x, axis=-1, keepdims=True)
    h = x * lax.rsqrt(ms + RMS_EPS) * g_ref[...] * (1.0 + mod_ref[4:5, :]) + mod_ref[3:4, :]
    h_hi = h.astype(BF16)
    h_ref[...] = h_hi

    h_lo = (h - h_hi.astype(F32)).astype(BF16)
    logits = _dot_nt(rwh_ref[...], h_hi) + _dot_nt(rwh_ref[...], h_lo) + _dot_nt(rwl_ref[...], h_hi)
    aff = _sigmoid(logits)
    s = aff + rb_ref[...]
    n_grp = ROUTER_ROWS // 4
    slot = lambda x, j: x[j * n_grp:(j + 1) * n_grp]
    s_j = [slot(s, j) for j in range(4)]
    a_j = [slot(aff, j) for j in range(4)]
    top2 = []
    for j in range(4):
        rank = jnp.zeros(s_j[j].shape, F32)
        for m in range(4):
            if m != j:
                beats = (s_j[m] >= s_j[j]) if m < j else (s_j[m] > s_j[j])
                rank = rank + jnp.where(beats, 1.0, 0.0)
        top2.append(rank < 2.0)
    gs = sum(jnp.where(top2[j], s_j[j], 0.0) for j in range(4))
    row = _iota(gs.shape, 0)
    lost = jnp.zeros(gs.shape, F32)
    for d in range(1, n_grp):
        other = pltpu.roll(gs, d, 0)
        lost = lost + jnp.where(row >= d, jnp.where(other >= gs, 1.0, 0.0), jnp.where(other > gs, 1.0, 0.0))
    chosen = lost < 0.5
    sel = [top2[j] & chosen for j in range(4)]
    den = sum(jnp.where(sel[j], a_j[j], 0.0) for j in range(4))
    den = jnp.where(den > 0.0, den, 1.0)
    gates_t = jnp.concatenate([jnp.where(sel[j], a_j[j] / den, 0.0) for j in range(4)]
                              + [jnp.zeros((LANES - ROUTER_ROWS, s.shape[1]), F32)], axis=0)
    gates_ref[...] = gates_t.T


def _out_proj(x, ys, w_out, mod_l, g, router_w, router_b):
    B, S, D = x.shape
    tm = 1024
    W = GROUP_WIDTH
    order = np.full((ROUTER_ROWS,), -1)
    order[[_expert_lane(e) for e in range(N_EXPERTS)]] = np.arange(N_EXPERTS)
    real = jnp.asarray(order >= 0)
    rw = jnp.where(real[:, None], router_w.T[np.maximum(order, 0)], 0.0)
    rw_hi = rw.astype(BF16)
    rw_lo = (rw - rw_hi.astype(F32)).astype(BF16)
    rb = jnp.where(real, router_b[np.maximum(order, 0)], NEG_INF).reshape(ROUTER_ROWS, 1)
    tile = lambda w: pl.BlockSpec((None, tm, w), lambda b, i: (b, i, 0))
    full = lambda shape: pl.BlockSpec(shape, lambda b, i: (0,) * len(shape))
    return pl.pallas_call(
        _out_kernel,
        grid=(B, S // tm),
        in_specs=[tile(D), tile(W), tile(W), tile(W), tile(W), full((D, D)),
                  pl.BlockSpec((None, 6, D), lambda b, i: (b, 0, 0)), full((1, D)),
                  full((ROUTER_ROWS, D)), full((ROUTER_ROWS, D)), full((ROUTER_ROWS, 1))],
        out_specs=[tile(D), tile(D), tile(LANES)],
        out_shape=[jax.ShapeDtypeStruct((B, S, D), F32), jax.ShapeDtypeStruct((B, S, D), BF16),
                   jax.ShapeDtypeStruct((B, S, LANES), F32)],
        compiler_params=_params(("parallel", "parallel")),
        name="out_proj_router",
    )(x, *ys, w_out.astype(BF16), mod_l, g.reshape(1, D), rw_hi, rw_lo, rb)


MOE_EXPERTS_PER_STEP = 4


def _moe_kernel(x_ref, h_ref, gates_ref, mod_ref, wg_ref, wu_ref, wd_ref, o_ref):
    step = pl.program_id(2)
    n = MOE_EXPERTS_PER_STEP

    @pl.when(step == 0)
    def _init():
        o_ref[...] = x_ref[...]

    h = h_ref[...]
    gates = gates_ref[...]
    lane = _iota(gates.shape, 1)
    hgs = [jnp.dot(h, wg_ref[k].astype(BF16), preferred_element_type=F32) for k in range(n)]
    hus = [jnp.dot(h, wu_ref[k].astype(BF16), preferred_element_type=F32) for k in range(n)]
    acts = []
    for k in range(n):
        ge = jnp.sum(jnp.where(lane == _expert_lane(step * n + k), gates, 0.0), axis=-1, keepdims=True)
        acts.append((hgs[k] * _sigmoid(hgs[k]) * hus[k] * ge).astype(BF16))
    act = jnp.concatenate(acts, axis=1)
    wd = wd_ref[...].astype(BF16).reshape(n * D_EXPERT, wd_ref.shape[2])
    o_ref[...] += mod_ref[5:6, :] * jnp.dot(act, wd, preferred_element_type=F32)


def _moe(x, h, gates, mod_l, w_gate, w_up, w_down, layer):
    B, S, D = x.shape
    E = w_gate.shape[1]
    n = MOE_EXPERTS_PER_STEP
    tm = 1024
    tile = lambda w: pl.BlockSpec((None, tm, w), lambda b, i, e: (b, i, 0))
    return pl.pallas_call(
        _moe_kernel,
        grid=(B, S // tm, E // n),
        in_specs=[tile(D), tile(D), tile(LANES),
                  pl.BlockSpec((None, 6, D), lambda b, i, e: (b, 0, 0)),
                  pl.BlockSpec((None, n, D, D_EXPERT), lambda b, i, e: (layer, e, 0, 0)),
                  pl.BlockSpec((None, n, D, D_EXPERT), lambda b, i, e: (layer, e, 0, 0)),
                  pl.BlockSpec((None, n, D_EXPERT, D), lambda b, i, e: (layer, e, 0, 0))],
        out_specs=tile(D),
        out_shape=jax.ShapeDtypeStruct((B, S, D), F32),
        compiler_params=_params(("parallel", "parallel", "arbitrary")),
        name="moe",
    )(x, h, gates, mod_l, w_gate, w_up, w_down)


def kernel(x, c, positions, ada_w, ada_b, norm_mix_g, norm_ffn_g, w_in, w_out, nsa_q_norm, nsa_k_norm, nsa_cmp_pe, nsa_cmp_w1, nsa_cmp_w2, pool_w, pool_scale, rwkv_mu, rwkv_w0, rwkv_w2, rwkv_a0, rwkv_a2, rwkv_g2, rwkv_k_k, rwkv_k_a, rwkv_r_k, rwkv_ln_w, rwkv_ln_b, rwkv_v0, rwkv_v1, rwkv_v2, dil_q_norm, dil_k_norm, router_w, router_b, moe_w_gate, moe_w_up, moe_w_down):
    depth = ada_w.shape[0]
    mod = _modulation(c, ada_w, ada_b)
    cos, sin, ccos, csin = _rope_tables(positions)
    v_first = None
    for l in range(depth):
        q_a, kvc_a, kv_a, gl, u_pool, z_rwkv, q_d, kv_d = _in_proj(x, mod[l], norm_mix_g[l], w_in, l)
        y_nsa = _nsa(q_a, kvc_a, kv_a, gl, cos, sin, ccos, csin, nsa_q_norm[l], nsa_k_norm[l],
                     nsa_cmp_pe[l], nsa_cmp_w1[l], nsa_cmp_w2[l])
        y_pool = _pool(u_pool, pool_w[l], pool_scale[l])
        v_res = None if l == 0 else (rwkv_v0[l - 1], rwkv_v1[l - 1], rwkv_v2[l - 1])
        y_rwkv, v_first = _rwkv(z_rwkv, v_first, v_res, rwkv_mu[l], rwkv_w0[l], rwkv_w2[l], rwkv_a0[l], rwkv_a2[l],
                                rwkv_g2[l], rwkv_k_k[l], rwkv_k_a[l], rwkv_r_k[l], rwkv_ln_w[l], rwkv_ln_b[l])
        y_dil = _dilated(q_d, kv_d, cos, sin, dil_q_norm[l], dil_k_norm[l])
        x_mid, h2, gates = _out_proj(x, (y_nsa, y_pool, y_rwkv, y_dil), w_out[l], mod[l], norm_ffn_g[l],
                                     router_w, router_b)
        x = _moe(x_mid, h2, gates, mod[l], moe_w_gate, moe_w_up, moe_w_down, l)
    return x
```

```python
import functools

import numpy as np
import jax
import jax.numpy as jnp
from jax import lax
from jax.experimental import pallas as pl
from jax.experimental.pallas import tpu as pltpu

F32 = jnp.float32
BF16 = jnp.bfloat16

D_MODEL = 1024
HEAD_DIM = 64
GROUP_WIDTH = 256
N_HEADS = 4
RMS_EPS = 1e-6
ROPE_THETA = 500000.0
ROPE_DIM = 16
CMP_BLOCK = 32
CMP_STRIDE = 16
SEL_BLOCK = 64
SEL_TOP_N = 16
WINDOW = 512
FORCE_SCORE = 1e4
POOL_WINDOWS = (2, 4, 8, 16)
RWKV_GN_EPS = 64e-5
DIL_PATTERNS = ((128, 1), (512, 4), (2048, 16))
N_EXPERTS = 16
D_EXPERT = 256
NEG_INF = -1e30
LOG2E = 1.4426950408889634
TINY = 1e-30

LANES = 128
TQ = 256
DIL_TQ = 512
TK = 512
SEL_TK = 512
RWKV_CHUNK = 64
RWKV_GROUP = 8
VMEM_LIMIT = 56 * 1024 * 1024

_IN_COLS = dict(q=(0, 256), kvc=(256, 384), kv=(384, 640), pool=(640, 896),
                rwkv=(896, 2048), dq=(2048, 2304), dkv=(2304, 2816))
IN_PACKED = 2816
RWKV_PACKED = 1152
GL_LANE = 32


def _dot(a, b):
    return jnp.dot(a.astype(BF16), b.astype(BF16), preferred_element_type=F32)


def _dot_nt(a, b):
    return lax.dot_general(a, b, (((1,), (1,)), ((), ())), preferred_element_type=F32)


def _split(x):
    hi = x.astype(BF16)
    return hi, (x - hi.astype(F32)).astype(BF16)


def _dot_split(a, b):
    hi, lo = _split(a)
    b = b.astype(BF16)
    return jnp.dot(hi, b, preferred_element_type=F32) + jnp.dot(lo, b, preferred_element_type=F32)


def _dot3(a, b, nt=False):
    f = _dot_nt if nt else functools.partial(jnp.dot, preferred_element_type=F32)
    a_hi, a_lo = _split(a)
    b_hi, b_lo = _split(b)
    return f(a_hi, b_hi) + f(a_lo, b_hi) + f(a_hi, b_lo)


def _sigmoid(x):
    return 1.0 / (1.0 + jnp.exp(-x))


def _iota(shape, dim):
    return lax.broadcasted_iota(jnp.int32, shape, dim)


def _rope_lanes(y, cos, sin):
    n = y.shape[-1]
    lane = _iota(y.shape, y.ndim - 1)
    partner = jnp.where((lane & 15) < 8, pltpu.roll(y, n - 8, y.ndim - 1), pltpu.roll(y, 8, y.ndim - 1))
    return y * cos + partner * sin


def _params(sem, vmem=VMEM_LIMIT):
    return pltpu.CompilerParams(dimension_semantics=sem, vmem_limit_bytes=vmem)


def _mod_kernel(c_ref, w_ref, b_ref, o_ref):
    c = c_ref[...]
    o_ref[...] = _dot3(c * _sigmoid(c), w_ref[...]) + b_ref[...]


def _modulation(c, ada_w, ada_b):
    L, D, E = ada_w.shape
    B = c.shape[0]
    tn = 1024
    out = pl.pallas_call(
        _mod_kernel,
        grid=(L, E // tn),
        in_specs=[pl.BlockSpec((B, D), lambda l, j: (0, 0)),
                  pl.BlockSpec((None, D, tn), lambda l, j: (l, 0, j)),
                  pl.BlockSpec((None, 1, tn), lambda l, j: (l, 0, j))],
        out_specs=pl.BlockSpec((None, B, tn), lambda l, j: (l, 0, j)),
        out_shape=jax.ShapeDtypeStruct((L, B, E), F32),
        compiler_params=_params(("parallel", "parallel")),
        name="modulation",
    )(c, ada_w, ada_b.reshape(L, 1, E))
    return out.reshape(L, B, 6, D)


ROPE_ROWS = 64


def _rope_kernel(pos_ref, cpos_ref, freq_ref, ecos_ref, esin_ref, one_ref, cos_ref, sin_ref, ccos_ref, csin_ref):
    def tables(pos_row):
        ang = freq_ref[...] * pos_row.astype(F32)
        parts = []
        for v in (jnp.cos(ang), jnp.sin(ang)):
            hi = v.astype(BF16)
            r1 = v - hi.astype(F32)
            mid = r1.astype(BF16)
            parts += [hi, mid, (r1 - mid.astype(F32)).astype(BF16)]
        n = ang.shape[1]
        stacked = jnp.concatenate(parts + [jnp.zeros((ROPE_ROWS - 6 * ang.shape[0], n), BF16)], axis=0)
        spread = lambda e: lax.dot_general(stacked, e, (((0,), (0,)), ((), ())), preferred_element_type=F32)
        return spread(ecos_ref[...]) + one_ref[...], spread(esin_ref[...])

    cos_ref[...], sin_ref[...] = tables(pos_ref[...])
    ccos_ref[...], csin_ref[...] = tables(cpos_ref[...])


def _rope_tables(positions):
    B, S = positions.shape
    half = ROPE_DIM // 2
    inv_freq = ROPE_THETA ** (-2.0 * jnp.arange(half, dtype=F32) / ROPE_DIM)
    lane = np.arange(LANES)
    rot = (lane % HEAD_DIM) < ROPE_DIM
    pick = (lane[None, :] % half == np.arange(half)[:, None]) & rot[None, :]
    sgn = np.where(lane % HEAD_DIM < half, -1.0, 1.0)
    zeros = np.zeros((3 * half, LANES))
    ecos = np.concatenate([np.tile(pick, (3, 1)), zeros, np.zeros((ROPE_ROWS - 6 * half, LANES))], axis=0)
    esin = np.concatenate([zeros, np.tile(pick * sgn[None, :], (3, 1)), np.zeros((ROPE_ROWS - 6 * half, LANES))], axis=0)
    one = jnp.asarray(np.where(rot, 0.0, 1.0), F32).reshape(1, LANES)
    n_cmp = S // CMP_STRIDE
    cpos = positions[:, CMP_BLOCK - 1::CMP_STRIDE]
    cpos = jnp.pad(cpos, ((0, 0), (0, n_cmp - cpos.shape[1])))
    tab = jax.ShapeDtypeStruct((B, S, LANES), F32)
    ctab = jax.ShapeDtypeStruct((B, n_cmp, LANES), F32)
    full = lambda shape: pl.BlockSpec(shape, lambda b: (0,) * len(shape))
    return pl.pallas_call(
        _rope_kernel,
        grid=(B,),
        in_specs=[pl.BlockSpec((None, 1, S), lambda b: (b, 0, 0)),
                  pl.BlockSpec((None, 1, n_cmp), lambda b: (b, 0, 0)),
                  full((half, 1)), full((ROPE_ROWS, LANES)), full((ROPE_ROWS, LANES)), full((1, LANES))],
        out_specs=[pl.BlockSpec((None, S, LANES), lambda b: (b, 0, 0)),
                   pl.BlockSpec((None, S, LANES), lambda b: (b, 0, 0)),
                   pl.BlockSpec((None, n_cmp, LANES), lambda b: (b, 0, 0)),
                   pl.BlockSpec((None, n_cmp, LANES), lambda b: (b, 0, 0))],
        out_shape=[tab, tab, ctab, ctab],
        compiler_params=_params(("parallel",)),
        name="rope_tables",
    )(positions.reshape(B, 1, S), cpos.reshape(B, 1, n_cmp), inv_freq.reshape(half, 1),
      jnp.asarray(ecos, BF16), jnp.asarray(esin, BF16), one)


_W_IN_MOVES = ((0, 640, 0), (640, 652, 1792 + 128 + GL_LANE), (652, 908, 640), (908, 1676, 896), (1676, 1740, 1664),
               (1740, 1804, 1728), (1804, 1964, 1792), (1964, 2732, 2048))


def _in_kernel(x_ref, mod_ref, g_ref, w_ref, *refs):
    out_refs, wp_ref = refs[:-1], refs[-1]

    @pl.when((pl.program_id(0) == 0) & (pl.program_id(1) == 0))
    def _pack_weights():
        rows = 128

        def body(r, _):
            r0 = pl.multiple_of(r * rows, rows)
            w = w_ref[pl.ds(r0, rows), :]
            wp_ref[pl.ds(r0, rows), :] = jnp.zeros((rows, IN_PACKED), BF16)
            for lo, hi, dst in _W_IN_MOVES:
                wp_ref[pl.ds(r0, rows), dst:dst + hi - lo] = w[:, lo:hi].astype(BF16)
            return 0

        lax.fori_loop(0, w_ref.shape[0] // rows, body, 0)

    x = x_ref[...]
    ms = jnp.mean(x * x, axis=-1, keepdims=True)
    y = x * lax.rsqrt(ms + RMS_EPS) * g_ref[...]
    h = y * (1.0 + mod_ref[1:2, :]) + mod_ref[0:1, :]
    p = jnp.dot(h.astype(BF16), wp_ref[...], preferred_element_type=F32)
    for ref, (lo, hi) in zip(out_refs, _IN_COLS.values()):
        ref[...] = p[:, lo:hi]


def _in_proj(x, mod_l, g, w_in, layer):
    B, S, D = x.shape
    tm = 1024
    widths = [hi - lo for lo, hi in _IN_COLS.values()]
    return pl.pallas_call(
        _in_kernel,
        grid=(B, S // tm),
        in_specs=[pl.BlockSpec((None, tm, D), lambda b, i: (b, i, 0)),
                  pl.BlockSpec((None, 6, D), lambda b, i: (b, 0, 0)),
                  pl.BlockSpec((1, D), lambda b, i: (0, 0)),
                  pl.BlockSpec((None,) + w_in.shape[1:], lambda b, i: (layer, 0, 0), pipeline_mode=pl.Buffered(1))],
        out_specs=[pl.BlockSpec((None, tm, w), lambda b, i: (b, i, 0)) for w in widths],
        out_shape=[jax.ShapeDtypeStruct((B, S, w), F32) for w in widths],
        scratch_shapes=[pltpu.VMEM((D, IN_PACKED), BF16)],
        compiler_params=_params(("arbitrary", "arbitrary")),
        name="in_proj",
    )(x, mod_l, g.reshape(1, D), w_in)


def _pack_rwkv_row(v):
    z = lambda n: jnp.zeros((n,), v.dtype)
    return jnp.concatenate([v[0:1056], z(RWKV_PACKED - 1056)]).reshape(1, RWKV_PACKED)


def _pool_kernel(u_ref, w_ref, scale_ref, o_ref):
    u = u_ref[...]
    row = _iota(u.shape, 0)
    lane = _iota(u.shape, 1)

    def lag(x, s):
        return jnp.where(row >= s, pltpu.roll(x, s, 0), 0.0)

    s2 = u + lag(u, 1)
    s4 = s2 + lag(s2, 2)
    s8 = s4 + lag(s4, 4)
    s16 = s8 + lag(s8, 8)
    g = lane // (GROUP_WIDTH // len(POOL_WINDOWS))
    tot = jnp.where(g == 0, s2, jnp.where(g == 1, s4, jnp.where(g == 2, s8, s16)))
    win = jnp.where(g == 0, 2.0, jnp.where(g == 1, 4.0, jnp.where(g == 2, 8.0, 16.0)))
    cnt = jnp.minimum((row + 1).astype(F32), win)
    o_ref[...] = _dot(tot / cnt - u, w_ref[...]) * scale_ref[...]


def _pool(u, pool_w, pool_scale):
    B, S, W = u.shape
    wbd = jax.scipy.linalg.block_diag(*[pool_w[i] for i in range(pool_w.shape[0])]).astype(BF16)
    return pl.pallas_call(
        _pool_kernel,
        grid=(B,),
        in_specs=[pl.BlockSpec((None, S, W), lambda b: (b, 0, 0)),
                  pl.BlockSpec((W, W), lambda b: (0, 0)),
                  pl.BlockSpec((1, W), lambda b: (0, 0))],
        out_specs=pl.BlockSpec((None, S, W), lambda b: (b, 0, 0)),
        out_shape=jax.ShapeDtypeStruct((B, S, W), F32),
        compiler_params=_params(("parallel",)),
        name="pool_mixer",
    )(u, wbd, pool_scale.reshape(1, W))


def _head_mean_matrix():
    blk = np.kron(np.eye(N_HEADS), np.ones((HEAD_DIM, HEAD_DIM))) / HEAD_DIM
    return jnp.asarray(blk, F32)


def _pad_heads(x256, as_bf16=True):
    n = x256.shape[0]
    lane = _iota((n, LANES), 1)
    out = []
    for h in range(N_HEADS):
        blk = x256[:, (h // 2) * LANES:(h // 2 + 1) * LANES]
        if h % 2:
            blk = pltpu.roll(blk, HEAD_DIM, 1)
        blk = jnp.where(lane < HEAD_DIM, blk, 0.0)
        out.append(blk.astype(BF16) if as_bf16 else blk)
    return out


def _merge_heads(slabs):
    n = slabs[0].shape[0]
    lane = _iota((n, LANES), 1)
    cols = []
    for p in range(N_HEADS // 2):
        even = pltpu.roll(slabs[2 * p], HEAD_DIM, 1)
        cols.append(jnp.where(lane < HEAD_DIM, even, slabs[2 * p + 1]))
    return jnp.concatenate(cols, axis=1)


def _flash_step(carries, qs, kas, vbs, biases):
    idx = range(len(qs))
    s = [_dot_nt(qs[i], kas[i]) for i in idx]
    if biases is not None:
        s = [s[i] + biases[i] for i in idx]
    m_new = [jnp.maximum(carries[i][0], jnp.max(s[i], axis=-1, keepdims=True)) for i in idx]
    alpha = [jnp.exp2(carries[i][0] - m_new[i]) for i in idx]
    p = [jnp.exp2(s[i] - m_new[i]).astype(BF16) for i in idx]
    pv = [jnp.dot(p[i], vbs[i], preferred_element_type=F32) for i in idx]
    return tuple((m_new[i], alpha[i] * carries[i][1] + pv[i]) for i in idx)


def _flash_init(n):
    return (jnp.full((n, 1), NEG_INF, F32), jnp.zeros((n, LANES), F32))


def _flash_finish(carry):
    acc = carry[1]
    return acc / pltpu.roll(acc, HEAD_DIM, 1)


def _nsa_kernel(q_ref, gl_ref, kc_ref, kv_ref, cos_ref, sin_ref, ccos_ref, csin_ref,
                qn_ref, kn_ref, pet_ref, peb_ref, wt_ref, wb_ref, w2_ref,
                hb_ref, cover_ref, eg_ref, ctab_ref, wtab_ref,
                o_ref, ksvs_ref, kwvw_ref, vs1_ref, vw1_ref, kvc_ref):
    S = kv_ref.shape[0]
    qi = pl.program_id(1)
    n_cmp = S // CMP_STRIDE

    @pl.when(qi == 0)
    def _prep_keys():
        rows = 512

        def body(i, _):
            r0 = pl.multiple_of(i * rows, rows)
            cs = cos_ref[pl.ds(r0, rows), :]
            sn = sin_ref[pl.ds(r0, rows), :]
            lane = _iota((rows, LANES), 1)
            first = lane < HEAD_DIM
            blk_hot = jnp.where(lane - HEAD_DIM == (r0 + _iota((rows, LANES), 0)) // SEL_BLOCK, 1.0, 0.0)
            for col, j, dst, dst1 in ((0, 1, ksvs_ref, vs1_ref), (LANES, 2, kwvw_ref, vw1_ref)):
                slab = kv_ref[pl.ds(r0, rows), col:col + LANES]
                ms = jnp.sum(jnp.where(first, slab * slab, 0.0), axis=-1, keepdims=True) * (1.0 / HEAD_DIM)
                y = slab * lax.rsqrt(ms + RMS_EPS) * kn_ref[j:j + 1, :]
                dst[pl.ds(r0, rows), :] = jnp.where(first, _rope_lanes(y, cs, sn), blk_hot if j == 1 else slab).astype(BF16)
                dst1[pl.ds(r0, rows), :] = jnp.where(first, 1.0, slab).astype(BF16)
            return 0

        lax.fori_loop(0, S // rows, body, 0)

        top = jnp.zeros((n_cmp, LANES), F32)
        bot = jnp.zeros((n_cmp, LANES), F32)
        for i in range(CMP_STRIDE):
            tok = kc_ref[pl.ds(i, n_cmp, stride=CMP_STRIDE), :]
            cols = slice(i * LANES, (i + 1) * LANES)
            top = top + _dot3(tok + pet_ref[:, cols], wt_ref[cols, :])
            bot = bot + _dot3(tok + peb_ref[:, cols], wb_ref[cols, :])
        pre = top + pltpu.roll(bot, n_cmp - 1, 0)
        act = 0.5 * pre * (1.0 + jnp.tanh(float(np.sqrt(2.0 / np.pi)) * (pre + 0.044715 * (pre * pre * pre))))
        out = _dot3(act, w2_ref[...])
        first = _iota(out.shape, 1) < HEAD_DIM
        ms = jnp.sum(jnp.where(first, out * out, 0.0), axis=-1, keepdims=True) * (1.0 / HEAD_DIM)
        y = out * lax.rsqrt(ms + RMS_EPS) * kn_ref[0:1, :]
        kvc_ref[...] = jnp.where(first, _rope_lanes(y, ccos_ref[...], csin_ref[...]), out)

    r0 = pl.multiple_of(qi * TQ, TQ)
    q = q_ref[...]
    ms = _dot_split(q * q, hb_ref[...])
    qn = q * lax.rsqrt(ms + RMS_EPS) * qn_ref[...]
    cs = cos_ref[pl.ds(r0, TQ), :]
    sn = sin_ref[pl.ds(r0, TQ), :]
    qr = _rope_lanes(qn, jnp.concatenate([cs, cs], axis=1), jnp.concatenate([sn, sn], axis=1)) * (HEAD_DIM ** -0.5)
    qst = jnp.concatenate(_pad_heads(qr, as_bf16=False), axis=0)
    n_rows = N_HEADS * TQ

    kvc = kvc_ref[...]
    s = _dot3(qst, kvc, nt=True)
    t_st = r0 + (_iota((n_rows, n_cmp), 0) & (TQ - 1))
    c_end = _iota((n_rows, n_cmp), 1) * CMP_STRIDE + (CMP_BLOCK - 1)
    mask = c_end <= t_st
    sm = jnp.where(mask, s, NEG_INF)
    m = jnp.max(sm, axis=-1, keepdims=True)
    e = jnp.where(mask, jnp.exp(sm - m), 0.0)
    p = e / jnp.maximum(jnp.sum(e, axis=-1, keepdims=True), TINY)
    o_cmp = _dot(p, kvc)

    psum = p[0:TQ] + p[TQ:2 * TQ] + p[2 * TQ:3 * TQ] + p[3 * TQ:4 * TQ]
    ps_hi, ps_lo = _split(psum.T)
    cover_t = cover_ref[...].astype(BF16)
    imp = (jnp.dot(cover_t, ps_hi, preferred_element_type=F32) + jnp.dot(cover_t, ps_lo, preferred_element_type=F32))
    n_blk = S // SEL_BLOCK
    blk = _iota((n_blk, TQ), 0)
    cur = (r0 + _iota((n_blk, TQ), 1)) // SEL_BLOCK
    forced = (blk == 0) | (blk == cur) | (blk == cur - 1)
    imp = jnp.where(blk > cur, -1.0, jnp.where(forced, FORCE_SCORE, imp))
    rank = jnp.zeros((n_blk, TQ), F32)
    for d in range(1, n_blk):
        other = pltpu.roll(imp, n_blk - d, 0)
        rank = rank + jnp.where(blk + d >= n_blk,
                                jnp.where(other >= imp, 1.0, 0.0),
                                jnp.where(other > imp, 1.0, 0.0))
    sel_rows = jnp.where(rank < SEL_TOP_N, 0.0, NEG_INF)
    sel_bias = jnp.concatenate([jnp.zeros((HEAD_DIM, TQ), F32), sel_rows,
                                jnp.zeros((LANES - HEAD_DIM - n_blk, TQ), F32)], axis=0).T.astype(BF16)
    lane = _iota((TQ, LANES), 1)

    qb = (qst * LOG2E).astype(BF16)
    heads = range(N_HEADS)
    q_heads = [qb[h * TQ:(h + 1) * TQ] for h in heads]
    q_heads_sel = [jnp.where(lane < HEAD_DIM, q, sel_bias) for q in q_heads]
    init = tuple(_flash_init(TQ) for _ in heads)

    def sel_tile(j, carry, biases):
        k0 = pl.multiple_of(j * SEL_TK, SEL_TK)
        ka, vb = ksvs_ref[pl.ds(k0, SEL_TK), :], vs1_ref[pl.ds(k0, SEL_TK), :]
        return _flash_step(carry, q_heads_sel, [ka] * N_HEADS, [vb] * N_HEADS, biases)

    ratio = SEL_TK // TQ
    last = qi // ratio
    res = lax.fori_loop(0, last, lambda j, c: sel_tile(j, c, None), init)
    res = sel_tile(last, res, [ctab_ref[qi % ratio]] * N_HEADS)
    o_sel = jnp.concatenate([_flash_finish(c) for c in res], axis=0)

    n_prev = WINDOW // TQ
    span = (n_prev + 1) * TQ
    k0 = pl.multiple_of(jnp.maximum(qi - n_prev, 0) * TQ, TQ)
    ka, vb = kwvw_ref[pl.ds(k0, span), :], vw1_ref[pl.ds(k0, span), :]
    res = _flash_step(init, q_heads, [ka] * N_HEADS, [vb] * N_HEADS, [wtab_ref[jnp.minimum(qi, n_prev)]] * N_HEADS)
    o_win = jnp.concatenate([_flash_finish(c) for c in res], axis=0)

    gate = _dot_split(_sigmoid(gl_ref[...]), eg_ref[...])
    split = lambda o: _merge_heads([o[h * TQ:(h + 1) * TQ] for h in range(N_HEADS)])
    W = GROUP_WIDTH
    o_ref[...] = (gate[:, 0:W] * split(o_cmp) + gate[:, W:2 * W] * split(o_sel) + gate[:, 2 * W:3 * W] * split(o_win))


def _nsa(q, kvc, kv, z_rwkv, cos, sin, ccos, csin, q_norm, k_norm, cmp_pe, cmp_w1, cmp_w2):
    B, S, _ = q.shape
    n_cmp = S // CMP_STRIDE
    n_blk = S // SEL_BLOCK
    hd = HEAD_DIM

    qn = jnp.tile(q_norm, N_HEADS).reshape(1, GROUP_WIDTH)
    kn = jnp.concatenate([k_norm, jnp.ones_like(k_norm)], axis=1)

    def interleave(a, b):
        return jnp.concatenate([a, b], axis=1).reshape(1, -1)

    half = CMP_BLOCK // 2
    pet = interleave(cmp_pe[0, :half], cmp_pe[1, :half])
    peb = interleave(cmp_pe[0, half:], cmp_pe[1, half:])

    def w1_half(lo):
        wk = cmp_w1[0].reshape(CMP_BLOCK, hd, hd)[lo:lo + half]
        wv = cmp_w1[1].reshape(CMP_BLOCK, hd, hd)[lo:lo + half]
        z = jnp.zeros_like(wk)
        top = jnp.concatenate([wk, z], axis=2)
        bot = jnp.concatenate([z, wv], axis=2)
        return jnp.concatenate([top, bot], axis=1).reshape(half * 2 * hd, 2 * hd)

    wt, wb = w1_half(0), w1_half(half)
    w2 = jax.scipy.linalg.block_diag(cmp_w2[0], cmp_w2[1])
    hb = _head_mean_matrix()

    c_end = np.arange(n_cmp) * CMP_STRIDE + CMP_BLOCK - 1
    b_start = np.arange(n_blk) * SEL_BLOCK
    cover = np.maximum(np.minimum(c_end[:, None] + 1, b_start[None, :] + SEL_BLOCK)
                       - np.maximum(c_end[:, None] + 1 - CMP_BLOCK, b_start[None, :]), 0).astype(np.float32) / CMP_BLOCK
    cover[n_cmp - 1] = 0.0
    cover = jnp.asarray(cover.T, F32)
    eg = np.zeros((LANES, 3 * GROUP_WIDTH), np.float32)
    for h in range(N_HEADS):
        for br in range(3):
            eg[GL_LANE + h * 3 + br, br * GROUP_WIDTH + h * hd: br * GROUP_WIDTH + (h + 1) * hd] = 1.0
    eg = jnp.asarray(eg)
    ratio = SEL_TK // TQ
    ctab = np.where(np.arange(SEL_TK)[None, None, :] <= np.arange(ratio)[:, None, None] * TQ + np.arange(TQ)[None, :, None],
                    0.0, NEG_INF).astype(np.float32)
    n_prev = WINDOW // TQ
    span = (n_prev + 1) * TQ
    dist = (np.arange(n_prev + 1)[:, None, None] * TQ + np.arange(TQ)[None, :, None] - np.arange(span)[None, None, :])
    wtab = np.where((dist >= 0) & (dist < WINDOW), 0.0, NEG_INF).astype(np.float32)
    ctab, wtab = jnp.asarray(ctab), jnp.asarray(wtab)

    full = lambda shape: pl.BlockSpec(shape, lambda b, i: (0,) * len(shape))
    return pl.pallas_call(
        _nsa_kernel,
        grid=(B, S // TQ),
        in_specs=[pl.BlockSpec((None, TQ, GROUP_WIDTH), lambda b, i: (b, i, 0)),
                  pl.BlockSpec((None, TQ, LANES), lambda b, i: (b, i, RWKV_PACKED // LANES - 1)),
                  pl.BlockSpec((None, S, LANES), lambda b, i: (b, 0, 0)),
                  pl.BlockSpec((None, S, 2 * LANES), lambda b, i: (b, 0, 0)),
                  pl.BlockSpec((None, S, LANES), lambda b, i: (b, 0, 0)),
                  pl.BlockSpec((None, S, LANES), lambda b, i: (b, 0, 0)),
                  pl.BlockSpec((None, n_cmp, LANES), lambda b, i: (b, 0, 0)),
                  pl.BlockSpec((None, n_cmp, LANES), lambda b, i: (b, 0, 0)),
                  full((1, GROUP_WIDTH)), full((3, LANES)),
                  full((1, CMP_STRIDE * LANES)), full((1, CMP_STRIDE * LANES)),
                  full((CMP_STRIDE * LANES, LANES)), full((CMP_STRIDE * LANES, LANES)), full((LANES, LANES)),
                  full((GROUP_WIDTH, GROUP_WIDTH)), full((n_blk, n_cmp)),
                  full((LANES, 3 * GROUP_WIDTH)), full(ctab.shape), full(wtab.shape)],
        out_specs=pl.BlockSpec((None, TQ, GROUP_WIDTH), lambda b, i: (b, i, 0)),
        out_shape=jax.ShapeDtypeStruct((B, S, GROUP_WIDTH), F32),
        scratch_shapes=[pltpu.VMEM((S, LANES), BF16), pltpu.VMEM((S, LANES), BF16),
                        pltpu.VMEM((S, LANES), BF16), pltpu.VMEM((S, LANES), BF16),
                        pltpu.VMEM((n_cmp, LANES), F32)],
        compiler_params=_params(("parallel", "arbitrary")),
        name="nsa",
    )(q, z_rwkv, kvc, kv, cos, sin, ccos, csin, qn, kn, pet, peb, wt, wb, w2, hb, cover, eg, ctab, wtab)


def _dil_kernel(q_ref, kv_ref, cos_ref, sin_ref, qn_ref, kn_ref, hb_ref, tb_ref, o_ref, kvs_ref, v1s_ref):
    S = kv_ref.shape[0]
    qi = pl.program_id(1)
    W = GROUP_WIDTH

    @pl.when(qi == 0)
    def _prep_keys():
        rows = 512

        def body(i, _):
            r0 = pl.multiple_of(i * rows, rows)
            cs = cos_ref[pl.ds(r0, rows), :]
            sn = sin_ref[pl.ds(r0, rows), :]
            k = kv_ref[pl.ds(r0, rows), 0:W]
            v = kv_ref[pl.ds(r0, rows), W:2 * W]
            ms = _dot_split(k * k, hb_ref[...])
            kn = k * lax.rsqrt(ms + RMS_EPS) * kn_ref[...]
            kr = _rope_lanes(kn, jnp.concatenate([cs, cs], axis=1), jnp.concatenate([sn, sn], axis=1))
            lane = _iota((rows, LANES), 1)
            for h in range(N_HEADS):
                kb = kr[:, (h // 2) * LANES:(h // 2 + 1) * LANES]
                vb = v[:, (h // 2) * LANES:(h // 2 + 1) * LANES]
                if h % 2:
                    kb = pltpu.roll(kb, HEAD_DIM, 1)
                else:
                    vb = pltpu.roll(vb, HEAD_DIM, 1)
                kvs_ref[h, pl.ds(r0, rows), :] = jnp.where(lane < HEAD_DIM, kb, vb).astype(BF16)
                v1s_ref[h, pl.ds(r0, rows), :] = jnp.where(lane < HEAD_DIM, 1.0, vb).astype(BF16)
            return 0

        lax.fori_loop(0, S // rows, body, 0)

    r0 = pl.multiple_of(qi * DIL_TQ, DIL_TQ)
    q = q_ref[...]
    ms = _dot_split(q * q, hb_ref[...])
    qn = q * lax.rsqrt(ms + RMS_EPS) * qn_ref[...]
    cs = cos_ref[pl.ds(r0, DIL_TQ), :]
    sn = sin_ref[pl.ds(r0, DIL_TQ), :]
    qr = _rope_lanes(qn, jnp.concatenate([cs, cs], axis=1), jnp.concatenate([sn, sn], axis=1)) * (HEAD_DIM ** -0.5 * LOG2E)
    qh = _pad_heads(qr)
    ratio = TK // DIL_TQ

    def body(j, carry):
        k0 = pl.multiple_of(j * TK, TK)
        bias = tb_ref[qi - ratio * j]
        return _flash_step(carry, qh, [kvs_ref[h, pl.ds(k0, TK), :] for h in range(N_HEADS)],
                           [v1s_ref[h, pl.ds(k0, TK), :] for h in range(N_HEADS)], [bias] * N_HEADS)

    res = lax.fori_loop(0, qi // ratio + 1, body, tuple(_flash_init(DIL_TQ) for _ in range(N_HEADS)))
    o_ref[...] = _merge_heads([_flash_finish(c) for c in res])


def _dil_bias_table(S):
    n = S // DIL_TQ
    d = (np.arange(n)[:, None, None] * DIL_TQ + np.arange(DIL_TQ)[None, :, None] - np.arange(TK)[None, None, :])
    cnt = np.zeros(d.shape, np.float32)
    for window, dil in DIL_PATTERNS:
        cnt += ((d >= 0) & (d <= window) & (d % dil == 0)).astype(np.float32)
    return jnp.asarray(np.where(cnt > 0, np.log2(np.maximum(cnt, 1.0)), NEG_INF).astype(np.float32))


def _dilated(q, kv, cos, sin, q_norm, k_norm):
    B, S, W = q.shape
    tb = _dil_bias_table(S)
    qn = jnp.tile(q_norm, N_HEADS).reshape(1, W)
    kn = jnp.tile(k_norm, N_HEADS).reshape(1, W)
    hb = _head_mean_matrix()
    full = lambda shape: pl.BlockSpec(shape, lambda b, i: (0,) * len(shape))
    return pl.pallas_call(
        _dil_kernel,
        grid=(B, S // DIL_TQ),
        in_specs=[pl.BlockSpec((None, DIL_TQ, W), lambda b, i: (b, i, 0)),
                  pl.BlockSpec((None, S, 2 * W), lambda b, i: (b, 0, 0)),
                  pl.BlockSpec((None, S, LANES), lambda b, i: (b, 0, 0)),
                  pl.BlockSpec((None, S, LANES), lambda b, i: (b, 0, 0)),
                  full((1, W)), full((1, W)), full((W, W)), full(tb.shape)],
        out_specs=pl.BlockSpec((None, DIL_TQ, W), lambda b, i: (b, i, 0)),
        out_shape=jax.ShapeDtypeStruct((B, S, W), F32),
        scratch_shapes=[pltpu.VMEM((N_HEADS, S, LANES), BF16), pltpu.VMEM((N_HEADS, S, LANES), BF16)],
        compiler_params=_params(("parallel", "arbitrary")),
        name="dilated",
    )(q, kv, cos, sin, qn, kn, hb, tb)


def _softplus(x):
    return jnp.maximum(x, 0.0) + jnp.log(1.0 + jnp.exp(-jnp.abs(x)))


def _rwkv_kernel(first, *refs):
    if first:
        (z_ref, mu_ref, w0_ref, w2_ref, a0_ref, a2_ref, g2_ref, kk_ref, ka_ref, rk_ref, lnw_ref, lnb_ref,
         hb_ref, y_ref, vf_out_ref, st_ref, prev_ref, ro_ref, ub_ref, colb_ref, g_ref, t2_ref) = refs
    else:
        (z_ref, vf_ref, mu_ref, w0_ref, w2_ref, a0_ref, a2_ref, g2_ref, kk_ref, ka_ref, rk_ref, lnw_ref, lnb_ref,
         v0_ref, v1_ref, v2_ref, hb_ref, y_ref, st_ref, prev_ref, ro_ref, ub_ref, colb_ref, g_ref, t2_ref) = refs
    SB = z_ref.shape[0]
    C = RWKV_CHUNK
    W = GROUP_WIDTH
    n_pairs = N_HEADS // 2
    R = RWKV_GROUP * C

    @pl.when(pl.program_id(1) == 0)
    def _reset():
        st_ref[...] = jnp.zeros_like(st_ref)
        prev_ref[...] = jnp.zeros_like(prev_ref)

    ri = _iota((C, C), 0)
    ci = _iota((C, C), 1)
    tri_incl = jnp.where(ci <= ri, 1.0, 0.0).astype(BF16)
    rr = _iota((2 * LANES, 2 * LANES), 0)
    cc = _iota((2 * LANES, 2 * LANES), 1)
    g_mask = ((((rr >> 6) & 1) == ((cc >> 6) & 1))
              & ((cc & (C - 1)) <= jnp.where(rr < LANES, (rr & (C - 1)) - 1, rr & (C - 1))))
    r2 = _iota((LANES, LANES), 0)
    c2 = _iota((LANES, LANES), 1)
    eye_f = jnp.where(r2 == c2, 1.0, 0.0)
    lane_lo = _iota((C, LANES), 1) < HEAD_DIM
    hb = hb_ref[...].astype(BF16)

    def stack(x):
        return jnp.concatenate([jnp.where(lane_lo, x, 0.0), jnp.where(lane_lo, 0.0, x)], axis=0)

    def phase1(i, _):
        t0 = pl.multiple_of(i * R, R)
        zc = z_ref[pl.ds(t0, R), :]
        inside = jnp.where(i > 0, 1.0, 0.0)
        prev = inside * z_ref[pl.ds(jnp.maximum(t0 - 1, 0), 1), :] + (1.0 - inside) * prev_ref[...]
        zp = jnp.where(_iota(zc.shape, 0) == 0, prev, pltpu.roll(zc, 1, 0))
        zs = zc + (zp - zc) * mu_ref[...]
        r, k, v = zs[:, 0:W], zs[:, W:2 * W], zs[:, 2 * W:3 * W]
        xwa, xg = zs[:, 768:896], zs[:, 896:1152]
        w_log = -_softplus(-(w0_ref[...] + _dot(jnp.tanh(xwa), w2_ref[...]))) - 0.5
        lw = -jnp.exp(w_log)
        a = _sigmoid(a0_ref[...] + _dot(xwa, a2_ref[...]))
        g = _dot(_sigmoid(xg), g2_ref[...])
        if first:
            vf_out_ref[pl.ds(t0, R), :] = v
        else:
            vf = vf_ref[pl.ds(t0, R), :]
            v = v + (vf - v) * _sigmoid(v0_ref[...] + _dot(_dot(v, v1_ref[...]), v2_ref[...]))
        kkr = k * kk_ref[...]
        nrm = jnp.sqrt(_dot_split(kkr * kkr, hb) * HEAD_DIM)
        kk = kkr / jnp.maximum(nrm, 1e-12)
        k2 = k * (1.0 + (a - 1.0) * ka_ref[...])
        bonus = _dot_split(r * k2 * rk_ref[...], hb) * HEAD_DIM * v
        g_ref[pl.ds(t0, R), :] = g
        t2_ref[pl.ds(t0, R), :] = (lnb_ref[...] + bonus) * g

        items = [(c, p) for c in range(RWKV_GROUP) for p in range(n_pairs)]
        idx = range(len(items))
        sub = lambda x, c, p: x[c * C:(c + 1) * C, p * LANES:(p + 1) * LANES]
        cum = [_cumsum_rows(tri_incl, sub(lw, c, p)) for c, p in items]
        a_st, r_st, v_st, gm, tails = [], [], [], [], []
        for j, (c, p) in enumerate(items):
            lwp = sub(lw, c, p)
            cum_end = cum[j][C - 1:C, :]
            e_pos, e_neg = jnp.exp(cum[j]), jnp.exp(-cum[j])
            e_exc, e_tail = jnp.exp(cum[j] - lwp), jnp.exp(cum_end - cum[j])
            kkp, k2p = sub(kk, c, p), sub(k2, c, p)
            ka_p = kkp * sub(a, c, p)
            a_st.append(stack(-kkp * e_exc).astype(BF16))
            r_st.append(stack(sub(r, c, p) * e_pos))
            v_st.append(stack(sub(v, c, p)).astype(BF16))
            bhat = (ka_p * e_neg).astype(BF16)
            khat = (k2p * e_neg).astype(BF16)
            tails.append(jnp.concatenate([stack(ka_p * e_tail), stack(k2p * e_tail)], axis=0))
            gm.append(jnp.where(g_mask,
                                _dot_nt(jnp.concatenate([a_st[j], r_st[j].astype(BF16)], axis=0),
                                        jnp.concatenate([bhat, bhat, khat, khat], axis=0)), 0.0))
            colb_ref[i * RWKV_GROUP + c, p] = jnp.broadcast_to(jnp.exp(cum_end), (LANES, LANES)).T
        tt = [tails[j].T.astype(BF16) for j in idx]
        lakv = [_dot(gm[j][0:LANES, LANES:2 * LANES], v_st[j]) for j in idx]
        mr = [gm[j][LANES:2 * LANES, :].astype(BF16) for j in idx]
        lp = [gm[j][0:LANES, 0:LANES] for j in idx]
        tinv = [eye_f + lp[j] for j in idx]
        for _ in range(5):
            lpb = [lp[j].astype(BF16) for j in idx]
            lp = [jnp.dot(lpb[j], lpb[j], preferred_element_type=F32) for j in idx]
            tinv = [tinv[j] + _dot(lp[j], tinv[j]) for j in idx]
        wu = [_dot(tinv[j], jnp.concatenate([a_st[j].astype(F32), lakv[j]], axis=1)) for j in idx]
        wm = [wu[j][:, 0:LANES].astype(BF16) for j in idx]
        u = [wu[j][:, LANES:2 * LANES].astype(BF16) for j in idx]
        mwbw = [jnp.dot(jnp.concatenate([mr[j][:, 0:LANES], tt[j][:, 0:LANES]], axis=0), wm[j],
                        preferred_element_type=F32) for j in idx]
        ub = [jnp.dot(jnp.concatenate([mr[j], tt[j]], axis=0), jnp.concatenate([u[j], v_st[j]], axis=0),
                      preferred_element_type=F32) for j in idx]
        for j, (c, p) in enumerate(items):
            n = i * RWKV_GROUP + c
            ro_ref[n, p] = jnp.concatenate([r_st[j] + mwbw[j][0:LANES], mwbw[j][LANES:2 * LANES]], axis=0).astype(BF16)
            ub_ref[n, p] = ub[j]
        return 0

    def phase2(n, _):
        t0 = pl.multiple_of(n * C, C)
        pairs = range(n_pairs)
        st = [st_ref[p] for p in pairs]
        res = [jnp.dot(ro_ref[n, p], st[p].astype(BF16), preferred_element_type=F32) for p in pairs]
        ub = [ub_ref[n, p] for p in pairs]
        for p in pairs:
            st_ref[p] = colb_ref[n, p] * st[p] + res[p][LANES:2 * LANES] + ub[p][LANES:2 * LANES]
        o_st = [res[p][0:LANES] + ub[p][0:LANES] for p in pairs]
        y_ref[pl.ds(t0, C), :] = jnp.concatenate([o[0:C] + o[C:2 * C] for o in o_st], axis=1)
        return 0

    lax.fori_loop(0, SB // R, phase1, 0)
    lax.fori_loop(0, SB // C, phase2, 0)
    prev_ref[...] = z_ref[SB - 1:SB, :]

    tiles = [slice(t * LANES, (t + 1) * LANES) for t in range(SB // LANES)]
    o = [y_ref[rows, :] for rows in tiles]
    dev = [x - _dot_split(x, hb) for x in o]
    var = [_dot_split(d * d, hb) for d in dev]
    for rows, d, v in zip(tiles, dev, var):
        y_ref[rows, :] = d * lax.rsqrt(v + RWKV_GN_EPS) * (lnw_ref[...] * g_ref[rows, :]) + t2_ref[rows, :]


def _cumsum_rows(tri_incl, x):
    hi = x.astype(BF16)
    lo = (x - hi.astype(F32)).astype(BF16)
    return jnp.dot(tri_incl, hi, preferred_element_type=F32) + jnp.dot(tri_incl, lo, preferred_element_type=F32)


RWKV_BLOCK = 512


def _rwkv(z, v_first, v_res, mu, w0, w2, a0, a2, g2, k_k, k_a, r_k, ln_w, ln_b):
    B, S, P = z.shape
    W = GROUP_WIDTH
    SB = RWKV_BLOCK
    ncb = SB // RWKV_CHUNK
    n_pairs = N_HEADS // 2
    first = v_res is None
    row = lambda v: v.reshape(1, W)
    padr = lambda m, n: jnp.pad(m, ((0, n - m.shape[0]), (0, 0)))
    seq = lambda w: pl.BlockSpec((None, SB, w), lambda b, j: (b, j, 0))
    full = lambda shape: pl.BlockSpec(shape, lambda b, j: (0,) * len(shape))
    args = [z]
    specs = [seq(P)]
    if not first:
        args.append(v_first)
        specs.append(seq(W))
    a2_pad = jnp.concatenate([jnp.zeros_like(a2), a2], axis=0)
    args += [_pack_rwkv_row(mu), row(w0), padr(w2, LANES), row(a0), a2_pad, padr(g2, 2 * LANES),
             row(k_k), row(k_a), r_k.reshape(1, W), row(ln_w), row(ln_b)]
    specs += [full((1, P)), full((1, W)), full((LANES, W)), full((1, W)), full((LANES, W)), full((2 * LANES, W)),
              full((1, W)), full((1, W)), full((1, W)), full((1, W)), full((1, W))]
    if not first:
        v0, v1, v2 = v_res
        args += [row(v0), jnp.pad(v1, ((0, 0), (0, LANES - v1.shape[1]))), padr(v2, LANES)]
        specs += [full((1, W)), full((W, LANES)), full((LANES, W))]
    args.append(_head_mean_matrix())
    specs.append(full((W, W)))
    out_shape = [jax.ShapeDtypeStruct((B, S, W), F32)]
    out_specs = [seq(W)]
    if first:
        out_shape.append(jax.ShapeDtypeStruct((B, S, W), F32))
        out_specs.append(seq(W))
    res = pl.pallas_call(
        functools.partial(_rwkv_kernel, first),
        grid=(B, S // SB),
        in_specs=specs,
        out_specs=out_specs,
        out_shape=out_shape,
        scratch_shapes=[pltpu.VMEM((n_pairs, LANES, LANES), F32), pltpu.VMEM((1, P), F32),
                        pltpu.VMEM((ncb, n_pairs, 2 * LANES, LANES), BF16),
                        pltpu.VMEM((ncb, n_pairs, 2 * LANES, LANES), F32),
                        pltpu.VMEM((ncb, n_pairs, LANES, LANES), F32),
                        pltpu.VMEM((SB, W), F32), pltpu.VMEM((SB, W), F32)],
        compiler_params=_params(("parallel", "arbitrary")),
        name="rwkv7_first" if first else "rwkv7",
    )(*args)
    return (res[0], res[1]) if first else (res[0], v_first)


ROUTER_ROWS = 32


def _expert_lane(e):
    return (e % 4) * 8 + e // 4


def _out_kernel(x_ref, ya_ref, yb_ref, yc_ref, yd_ref, wo_ref, mod_ref, g_ref, rwh_ref, rwl_ref, rb_ref,
                xo_ref, h_ref, gates_ref):
    y = jnp.concatenate([ya_ref[...].astype(BF16), yb_ref[...].astype(BF16),
                         yc_ref[...].astype(BF16), yd_ref[...].astype(BF16)], axis=1)
    mix = jnp.dot(y, wo_ref[...], preferred_element_type=F32)
    x = x_ref[...] + mod_ref[2:3, :] * mix
    xo_ref[...] = x
    ms = jnp.mean(x * x, axis=-1, keepdims=True)
    h = x * lax.rsqrt(ms + RMS_EPS) * g_ref[...] * (1.0 + mod_ref[4:5, :]) + mod_ref[3:4, :]
    h_hi = h.astype(BF16)
    h_ref[...] = h_hi

    h_lo = (h - h_hi.astype(F32)).astype(BF16)
    logits = _dot_nt(rwh_ref[...], h_hi) + _dot_nt(rwh_ref[...], h_lo) + _dot_nt(rwl_ref[...], h_hi)
    aff = _sigmoid(logits)
    s = aff + rb_ref[...]
    n_grp = ROUTER_ROWS // 4
    slot = lambda x, j: x[j * n_grp:(j + 1) * n_grp]
    s_j = [slot(s, j) for j in range(4)]
    a_j = [slot(aff, j) for j in range(4)]
    top2 = []
    for j in range(4):
        rank = jnp.zeros(s_j[j].shape, F32)
        for m in range(4):
            if m != j:
                beats = (s_j[m] >= s_j[j]) if m < j else (s_j[m] > s_j[j])
                rank = rank + jnp.where(beats, 1.0, 0.0)
        top2.append(rank < 2.0)
    gs = sum(jnp.where(top2[j], s_j[j], 0.0) for j in range(4))
    row = _iota(gs.shape, 0)
    lost = jnp.zeros(gs.shape, F32)
    for d in range(1, n_grp):
        other = pltpu.roll(gs, d, 0)
        lost = lost + jnp.where(row >= d, jnp.where(other >= gs, 1.0, 0.0), jnp.where(other > gs, 1.0, 0.0))
    chosen = lost < 0.5
    sel = [top2[j] & chosen for j in range(4)]
    den = sum(jnp.where(sel[j], a_j[j], 0.0) for j in range(4))
    den = jnp.where(den > 0.0, den, 1.0)
    gates_t = jnp.concatenate([jnp.where(sel[j], a_j[j] / den, 0.0) for j in range(4)]
                              + [jnp.zeros((LANES - ROUTER_ROWS, s.shape[1]), F32)], axis=0)
    gates_ref[...] = gates_t.T


def _out_proj(x, ys, w_out, mod_l, g, router_w, router_b):
    B, S, D = x.shape
    tm = 1024
    W = GROUP_WIDTH
    order = np.full((ROUTER_ROWS,), -1)
    order[[_expert_lane(e) for e in range(N_EXPERTS)]] = np.arange(N_EXPERTS)
    real = jnp.asarray(order >= 0)
    rw = jnp.where(real[:, None], router_w.T[np.maximum(order, 0)], 0.0)
    rw_hi = rw.astype(BF16)
    rw_lo = (rw - rw_hi.astype(F32)).astype(BF16)
    rb = jnp.where(real, router_b[np.maximum(order, 0)], NEG_INF).reshape(ROUTER_ROWS, 1)
    tile = lambda w: pl.BlockSpec((None, tm, w), lambda b, i: (b, i, 0))
    full = lambda shape: pl.BlockSpec(shape, lambda b, i: (0,) * len(shape))
    return pl.pallas_call(
        _out_kernel,
        grid=(B, S // tm),
        in_specs=[tile(D), tile(W), tile(W), tile(W), tile(W), full((D, D)),
                  pl.BlockSpec((None, 6, D), lambda b, i: (b, 0, 0)), full((1, D)),
                  full((ROUTER_ROWS, D)), full((ROUTER_ROWS, D)), full((ROUTER_ROWS, 1))],
        out_specs=[tile(D), tile(D), tile(LANES)],
        out_shape=[jax.ShapeDtypeStruct((B, S, D), F32), jax.ShapeDtypeStruct((B, S, D), BF16),
                   jax.ShapeDtypeStruct((B, S, LANES), F32)],
        compiler_params=_params(("parallel", "parallel")),
        name="out_proj_router",
    )(x, *ys, w_out.astype(BF16), mod_l, g.reshape(1, D), rw_hi, rw_lo, rb)


MOE_EXPERTS_PER_STEP = 4


def _moe_kernel(x_ref, h_ref, gates_ref, mod_ref, wg_ref, wu_ref, wd_ref, o_ref):
    step = pl.program_id(2)
    n = MOE_EXPERTS_PER_STEP

    @pl.when(step == 0)
    def _init():
        o_ref[...] = x_ref[...]

    h = h_ref[...]
    gates = gates_ref[...]
    lane = _iota(gates.shape, 1)
    hgs = [jnp.dot(h, wg_ref[k].astype(BF16), preferred_element_type=F32) for k in range(n)]
    hus = [jnp.dot(h, wu_ref[k].astype(BF16), preferred_element_type=F32) for k in range(n)]
    acts = []
    for k in range(n):
        ge = jnp.sum(jnp.where(lane == _expert_lane(step * n + k), gates, 0.0), axis=-1, keepdims=True)
        acts.append((hgs[k] * _sigmoid(hgs[k]) * hus[k] * ge).astype(BF16))
    act = jnp.concatenate(acts, axis=1)
    wd = wd_ref[...].astype(BF16).reshape(n * D_EXPERT, wd_ref.shape[2])
    o_ref[...] += mod_ref[5:6, :] * jnp.dot(act, wd, preferred_element_type=F32)


def _moe(x, h, gates, mod_l, w_gate, w_up, w_down, layer):
    B, S, D = x.shape
    E = w_gate.shape[1]
    n = MOE_EXPERTS_PER_STEP
    tm = 1024
    tile = lambda w: pl.BlockSpec((None, tm, w), lambda b, i, e: (b, i, 0))
    return pl.pallas_call(
        _moe_kernel,
        grid=(B, S // tm, E // n),
        in_specs=[tile(D), tile(D), tile(LANES),
                  pl.BlockSpec((None, 6, D), lambda b, i, e: (b, 0, 0)),
                  pl.BlockSpec((None, n, D, D_EXPERT), lambda b, i, e: (layer, e, 0, 0)),
                  pl.BlockSpec((None, n, D, D_EXPERT), lambda b, i, e: (layer, e, 0, 0)),
                  pl.BlockSpec((None, n, D_EXPERT, D), lambda b, i, e: (layer, e, 0, 0))],
        out_specs=tile(D),
        out_shape=jax.ShapeDtypeStruct((B, S, D), F32),
        compiler_params=_params(("parallel", "parallel", "arbitrary")),
        name="moe",
    )(x, h, gates, mod_l, w_gate, w_up, w_down)


def kernel(x, c, positions, ada_w, ada_b, norm_mix_g, norm_ffn_g, w_in, w_out, nsa_q_norm, nsa_k_norm, nsa_cmp_pe, nsa_cmp_w1, nsa_cmp_w2, pool_w, pool_scale, rwkv_mu, rwkv_w0, rwkv_w2, rwkv_a0, rwkv_a2, rwkv_g2, rwkv_k_k, rwkv_k_a, rwkv_r_k, rwkv_ln_w, rwkv_ln_b, rwkv_v0, rwkv_v1, rwkv_v2, dil_q_norm, dil_k_norm, router_w, router_b, moe_w_gate, moe_w_up, moe_w_down):
    depth = ada_w.shape[0]
    mod = _modulation(c, ada_w, ada_b)
    cos, sin, ccos, csin = _rope_tables(positions)
    v_first = None
    for l in range(depth):
        q_a, kvc_a, kv_a, u_pool, z_rwkv, q_d, kv_d = _in_proj(x, mod[l], norm_mix_g[l], w_in, l)
        y_nsa = _nsa(q_a, kvc_a, kv_a, z_rwkv, cos, sin, ccos, csin, nsa_q_norm[l], nsa_k_norm[l],
                     nsa_cmp_pe[l], nsa_cmp_w1[l], nsa_cmp_w2[l])
        y_pool = _pool(u_pool, pool_w[l], pool_scale[l])
        v_res = None if l == 0 else (rwkv_v0[l - 1], rwkv_v1[l - 1], rwkv_v2[l - 1])
        y_rwkv, v_first = _rwkv(z_rwkv, v_first, v_res, rwkv_mu[l], rwkv_w0[l], rwkv_w2[l], rwkv_a0[l], rwkv_a2[l],
                                rwkv_g2[l], rwkv_k_k[l], rwkv_k_a[l], rwkv_r_k[l], rwkv_ln_w[l], rwkv_ln_b[l])
        y_dil = _dilated(q_d, kv_d, cos, sin, dil_q_norm[l], dil_k_norm[l])
        x_mid, h2, gates = _out_proj(x, (y_nsa, y_pool, y_rwkv, y_dil), w_out[l], mod[l], norm_ffn_g[l],
                                     router_w, router_b)
        x = _moe(x_mid, h2, gates, mod[l], moe_w_gate, moe_w_up, moe_w_down, l)
    return x
```

```python
import functools

import numpy as np
import jax
import jax.numpy as jnp
from jax import lax
from jax.experimental import pallas as pl
from jax.experimental.pallas import tpu as pltpu

F32 = jnp.float32
BF16 = jnp.bfloat16

D_MODEL = 1024
HEAD_DIM = 64
GROUP_WIDTH = 256
N_HEADS = 4
RMS_EPS = 1e-6
ROPE_THETA = 500000.0
ROPE_DIM = 16
CMP_BLOCK = 32
CMP_STRIDE = 16
SEL_BLOCK = 64
SEL_TOP_N = 16
WINDOW = 512
FORCE_SCORE = 1e4
POOL_WINDOWS = (2, 4, 8, 16)
RWKV_GN_EPS = 64e-5
DIL_PATTERNS = ((128, 1), (512, 4), (2048, 16))
N_EXPERTS = 16
D_EXPERT = 256
NEG_INF = -1e30
LOG2E = 1.4426950408889634
TINY = 1e-30

LANES = 128
TQ = 256
DIL_TQ = 512
TK = 512
SEL_TK = 512
RWKV_CHUNK = 64
RWKV_GROUP = 8
VMEM_LIMIT = 56 * 1024 * 1024

_IN_COLS = dict(q=(0, 256), kvc=(256, 384), kv=(384, 640), pool=(640, 896),
                rwkv=(896, 2048), dq=(2048, 2304), dkv=(2304, 2816))
IN_PACKED = 2816
RWKV_PACKED = 1152
GL_LANE = 32


def _dot(a, b):
    return jnp.dot(a.astype(BF16), b.astype(BF16), preferred_element_type=F32)


def _dot_nt(a, b):
    return lax.dot_general(a, b, (((1,), (1,)), ((), ())), preferred_element_type=F32)


def _split(x):
    hi = x.astype(BF16)
    return hi, (x - hi.astype(F32)).astype(BF16)


def _dot_split(a, b):
    hi, lo = _split(a)
    b = b.astype(BF16)
    return jnp.dot(hi, b, preferred_element_type=F32) + jnp.dot(lo, b, preferred_element_type=F32)


def _dot3(a, b, nt=False):
    f = _dot_nt if nt else functools.partial(jnp.dot, preferred_element_type=F32)
    a_hi, a_lo = _split(a)
    b_hi, b_lo = _split(b)
    return f(a_hi, b_hi) + f(a_lo, b_hi) + f(a_hi, b_lo)


def _sigmoid(x):
    return 1.0 / (1.0 + jnp.exp(-x))


def _iota(shape, dim):
    return lax.broadcasted_iota(jnp.int32, shape, dim)


def _rope_lanes(y, cos, sin):
    n = y.shape[-1]
    lane = _iota(y.shape, y.ndim - 1)
    partner = jnp.where((lane & 15) < 8, pltpu.roll(y, n - 8, y.ndim - 1), pltpu.roll(y, 8, y.ndim - 1))
    return y * cos + partner * sin


def _params(sem, vmem=VMEM_LIMIT):
    return pltpu.CompilerParams(dimension_semantics=sem, vmem_limit_bytes=vmem)


def _mod_kernel(c_ref, w_ref, b_ref, o_ref):
    c = c_ref[...]
    o_ref[...] = _dot3(c * _sigmoid(c), w_ref[...]) + b_ref[...]


def _modulation(c, ada_w, ada_b):
    L, D, E = ada_w.shape
    B = c.shape[0]
    tn = 1024
    out = pl.pallas_call(
        _mod_kernel,
        grid=(L, E // tn),
        in_specs=[pl.BlockSpec((B, D), lambda l, j: (0, 0)),
                  pl.BlockSpec((None, D, tn), lambda l, j: (l, 0, j)),
                  pl.BlockSpec((None, 1, tn), lambda l, j: (l, 0, j))],
        out_specs=pl.BlockSpec((None, B, tn), lambda l, j: (l, 0, j)),
        out_shape=jax.ShapeDtypeStruct((L, B, E), F32),
        compiler_params=_params(("parallel", "parallel")),
        name="modulation",
    )(c, ada_w, ada_b.reshape(L, 1, E))
    return out.reshape(L, B, 6, D)


ROPE_ROWS = 64


def _rope_kernel(pos_ref, cpos_ref, freq_ref, ecos_ref, esin_ref, one_ref, cos_ref, sin_ref, ccos_ref, csin_ref):
    def tables(pos_row):
        ang = freq_ref[...] * pos_row.astype(F32)
        parts = []
        for v in (jnp.cos(ang), jnp.sin(ang)):
            hi = v.astype(BF16)
            r1 = v - hi.astype(F32)
            mid = r1.astype(BF16)
            parts += [hi, mid, (r1 - mid.astype(F32)).astype(BF16)]
        n = ang.shape[1]
        stacked = jnp.concatenate(parts + [jnp.zeros((ROPE_ROWS - 6 * ang.shape[0], n), BF16)], axis=0)
        spread = lambda e: lax.dot_general(stacked, e, (((0,), (0,)), ((), ())), preferred_element_type=F32)
        return spread(ecos_ref[...]) + one_ref[...], spread(esin_ref[...])

    cos_ref[...], sin_ref[...] = tables(pos_ref[...])
    ccos_ref[...], csin_ref[...] = tables(cpos_ref[...])


def _rope_tables(positions):
    B, S = positions.shape
    half = ROPE_DIM // 2
    inv_freq = ROPE_THETA ** (-2.0 * jnp.arange(half, dtype=F32) / ROPE_DIM)
    lane = np.arange(LANES)
    rot = (lane % HEAD_DIM) < ROPE_DIM
    pick = (lane[None, :] % half == np.arange(half)[:, None]) & rot[None, :]
    sgn = np.where(lane % HEAD_DIM < half, -1.0, 1.0)
    zeros = np.zeros((3 * half, LANES))
    ecos = np.concatenate([np.tile(pick, (3, 1)), zeros, np.zeros((ROPE_ROWS - 6 * half, LANES))], axis=0)
    esin = np.concatenate([zeros, np.tile(pick * sgn[None, :], (3, 1)), np.zeros((ROPE_ROWS - 6 * half, LANES))], axis=0)
    one = jnp.asarray(np.where(rot, 0.0, 1.0), F32).reshape(1, LANES)
    n_cmp = S // CMP_STRIDE
    cpos = positions[:, CMP_BLOCK - 1::CMP_STRIDE]
    cpos = jnp.pad(cpos, ((0, 0), (0, n_cmp - cpos.shape[1])))
    tab = jax.ShapeDtypeStruct((B, S, LANES), F32)
    ctab = jax.ShapeDtypeStruct((B, n_cmp, LANES), F32)
    full = lambda shape: pl.BlockSpec(shape, lambda b: (0,) * len(shape))
    return pl.pallas_call(
        _rope_kernel,
        grid=(B,),
        in_specs=[pl.BlockSpec((None, 1, S), lambda b: (b, 0, 0)),
                  pl.BlockSpec((None, 1, n_cmp), lambda b: (b, 0, 0)),
                  full((half, 1)), full((ROPE_ROWS, LANES)), full((ROPE_ROWS, LANES)), full((1, LANES))],
        out_specs=[pl.BlockSpec((None, S, LANES), lambda b: (b, 0, 0)),
                   pl.BlockSpec((None, S, LANES), lambda b: (b, 0, 0)),
                   pl.BlockSpec((None, n_cmp, LANES), lambda b: (b, 0, 0)),
                   pl.BlockSpec((None, n_cmp, LANES), lambda b: (b, 0, 0))],
        out_shape=[tab, tab, ctab, ctab],
        compiler_params=_params(("parallel",)),
        name="rope_tables",
    )(positions.reshape(B, 1, S), cpos.reshape(B, 1, n_cmp), inv_freq.reshape(half, 1),
      jnp.asarray(ecos, BF16), jnp.asarray(esin, BF16), one)


_W_IN_MOVES = ((0, 640, 0), (640, 652, 1792 + 128 + GL_LANE), (652, 908, 640), (908, 1676, 896), (1676, 1740, 1664),
               (1740, 1804, 1728), (1804, 1964, 1792), (1964, 2732, 2048))


def _in_kernel(x_ref, mod_ref, g_ref, w_ref, *refs):
    out_refs, wp_ref = refs[:-1], refs[-1]

    @pl.when((pl.program_id(0) == 0) & (pl.program_id(1) == 0))
    def _pack_weights():
        rows = 128

        def body(r, _):
            r0 = pl.multiple_of(r * rows, rows)
            w = w_ref[pl.ds(r0, rows), :]
            wp_ref[pl.ds(r0, rows), :] = jnp.zeros((rows, IN_PACKED), BF16)
            for lo, hi, dst in _W_IN_MOVES:
                wp_ref[pl.ds(r0, rows), dst:dst + hi - lo] = w[:, lo:hi]
            return 0

        lax.fori_loop(0, w_ref.shape[0] // rows, body, 0)

    x = x_ref[...]
    ms = jnp.mean(x * x, axis=-1, keepdims=True)
    y = x * lax.rsqrt(ms + RMS_EPS) * g_ref[...]
    h = y * (1.0 + mod_ref[1:2, :]) + mod_ref[0:1, :]
    p = jnp.dot(h.astype(BF16), wp_ref[...], preferred_element_type=F32)
    for ref, (lo, hi) in zip(out_refs, _IN_COLS.values()):
        ref[...] = p[:, lo:hi]


def _in_proj(x, mod_l, g, w_in, layer):
    B, S, D = x.shape
    tm = 1024
    widths = [hi - lo for lo, hi in _IN_COLS.values()]
    return pl.pallas_call(
        _in_kernel,
        grid=(B, S // tm),
        in_specs=[pl.BlockSpec((None, tm, D), lambda b, i: (b, i, 0)),
                  pl.BlockSpec((None, 6, D), lambda b, i: (b, 0, 0)),
                  pl.BlockSpec((1, D), lambda b, i: (0, 0)),
                  pl.BlockSpec((None,) + w_in.shape[1:], lambda b, i: (layer, 0, 0), pipeline_mode=pl.Buffered(1))],
        out_specs=[pl.BlockSpec((None, tm, w), lambda b, i: (b, i, 0)) for w in widths],
        out_shape=[jax.ShapeDtypeStruct((B, S, w), F32) for w in widths],
        scratch_shapes=[pltpu.VMEM((D, IN_PACKED), BF16)],
        compiler_params=_params(("arbitrary", "arbitrary")),
        name="in_proj",
    )(x, mod_l, g.reshape(1, D), w_in.astype(BF16))


def _pack_rwkv_row(v):
    z = lambda n: jnp.zeros((n,), v.dtype)
    return jnp.concatenate([v[0:1056], z(RWKV_PACKED - 1056)]).reshape(1, RWKV_PACKED)


def _pool_kernel(u_ref, w_ref, scale_ref, o_ref):
    u = u_ref[...]
    row = _iota(u.shape, 0)
    lane = _iota(u.shape, 1)

    def lag(x, s):
        return jnp.where(row >= s, pltpu.roll(x, s, 0), 0.0)

    s2 = u + lag(u, 1)
    s4 = s2 + lag(s2, 2)
    s8 = s4 + lag(s4, 4)
    s16 = s8 + lag(s8, 8)
    g = lane // (GROUP_WIDTH // len(POOL_WINDOWS))
    tot = jnp.where(g == 0, s2, jnp.where(g == 1, s4, jnp.where(g == 2, s8, s16)))
    win = jnp.where(g == 0, 2.0, jnp.where(g == 1, 4.0, jnp.where(g == 2, 8.0, 16.0)))
    cnt = jnp.minimum((row + 1).astype(F32), win)
    o_ref[...] = _dot(tot / cnt - u, w_ref[...]) * scale_ref[...]


def _pool(u, pool_w, pool_scale):
    B, S, W = u.shape
    wbd = jax.scipy.linalg.block_diag(*[pool_w[i] for i in range(pool_w.shape[0])]).astype(BF16)
    return pl.pallas_call(
        _pool_kernel,
        grid=(B,),
        in_specs=[pl.BlockSpec((None, S, W), lambda b: (b, 0, 0)),
                  pl.BlockSpec((W, W), lambda b: (0, 0)),
                  pl.BlockSpec((1, W), lambda b: (0, 0))],
        out_specs=pl.BlockSpec((None, S, W), lambda b: (b, 0, 0)),
        out_shape=jax.ShapeDtypeStruct((B, S, W), F32),
        compiler_params=_params(("parallel",)),
        name="pool_mixer",
    )(u, wbd, pool_scale.reshape(1, W))


def _head_mean_matrix():
    blk = np.kron(np.eye(N_HEADS), np.ones((HEAD_DIM, HEAD_DIM))) / HEAD_DIM
    return jnp.asarray(blk, F32)


def _pad_heads(x256, as_bf16=True):
    n = x256.shape[0]
    lane = _iota((n, LANES), 1)
    out = []
    for h in range(N_HEADS):
        blk = x256[:, (h // 2) * LANES:(h // 2 + 1) * LANES]
        if h % 2:
            blk = pltpu.roll(blk, HEAD_DIM, 1)
        blk = jnp.where(lane < HEAD_DIM, blk, 0.0)
        out.append(blk.astype(BF16) if as_bf16 else blk)
    return out


def _merge_heads(slabs):
    n = slabs[0].shape[0]
    lane = _iota((n, LANES), 1)
    cols = []
    for p in range(N_HEADS // 2):
        even = pltpu.roll(slabs[2 * p], HEAD_DIM, 1)
        cols.append(jnp.where(lane < HEAD_DIM, even, slabs[2 * p + 1]))
    return jnp.concatenate(cols, axis=1)


def _flash_step(carries, qs, kas, vbs, biases):
    idx = range(len(qs))
    s = [_dot_nt(qs[i], kas[i]) for i in idx]
    if biases is not None:
        s = [s[i] + biases[i] for i in idx]
    m_new = [jnp.maximum(carries[i][0], jnp.max(s[i], axis=-1, keepdims=True)) for i in idx]
    alpha = [jnp.exp2(carries[i][0] - m_new[i]) for i in idx]
    p = [jnp.exp2(s[i] - m_new[i]).astype(BF16) for i in idx]
    pv = [jnp.dot(p[i], vbs[i], preferred_element_type=F32) for i in idx]
    return tuple((m_new[i], alpha[i] * carries[i][1] + pv[i]) for i in idx)


def _flash_init(n):
    return (jnp.full((n, 1), NEG_INF, F32), jnp.zeros((n, LANES), F32))


def _flash_finish(carry):
    acc = carry[1]
    return acc / pltpu.roll(acc, HEAD_DIM, 1)


def _nsa_kernel(q_ref, gl_ref, kc_ref, kv_ref, cos_ref, sin_ref, ccos_ref, csin_ref,
                qn_ref, kn_ref, pet_ref, peb_ref, wt_ref, wb_ref, w2_ref,
                hb_ref, cover_ref, eg_ref, ctab_ref, wtab_ref,
                o_ref, ksvs_ref, kwvw_ref, vs1_ref, vw1_ref, kvc_ref):
    S = kv_ref.shape[0]
    qi = pl.program_id(1)
    n_cmp = S // CMP_STRIDE

    @pl.when(qi == 0)
    def _prep_keys():
        rows = 512

        def body(i, _):
            r0 = pl.multiple_of(i * rows, rows)
            cs = cos_ref[pl.ds(r0, rows), :]
            sn = sin_ref[pl.ds(r0, rows), :]
            lane = _iota((rows, LANES), 1)
            first = lane < HEAD_DIM
            blk_hot = jnp.where(lane - HEAD_DIM == (r0 + _iota((rows, LANES), 0)) // SEL_BLOCK, 1.0, 0.0)
            for col, j, dst, dst1 in ((0, 1, ksvs_ref, vs1_ref), (LANES, 2, kwvw_ref, vw1_ref)):
                slab = kv_ref[pl.ds(r0, rows), col:col + LANES]
                ms = jnp.sum(jnp.where(first, slab * slab, 0.0), axis=-1, keepdims=True) * (1.0 / HEAD_DIM)
                y = slab * lax.rsqrt(ms + RMS_EPS) * kn_ref[j:j + 1, :]
                dst[pl.ds(r0, rows), :] = jnp.where(first, _rope_lanes(y, cs, sn), blk_hot if j == 1 else slab).astype(BF16)
                dst1[pl.ds(r0, rows), :] = jnp.where(first, 1.0, slab).astype(BF16)
            return 0

        lax.fori_loop(0, S // rows, body, 0)

        top = jnp.zeros((n_cmp, LANES), F32)
        bot = jnp.zeros((n_cmp, LANES), F32)
        for i in range(CMP_STRIDE):
            tok = kc_ref[pl.ds(i, n_cmp, stride=CMP_STRIDE), :]
            cols = slice(i * LANES, (i + 1) * LANES)
            top = top + _dot3(tok + pet_ref[:, cols], wt_ref[cols, :])
            bot = bot + _dot3(tok + peb_ref[:, cols], wb_ref[cols, :])
        pre = top + pltpu.roll(bot, n_cmp - 1, 0)
        act = 0.5 * pre * (1.0 + jnp.tanh(float(np.sqrt(2.0 / np.pi)) * (pre + 0.044715 * (pre * pre * pre))))
        out = _dot3(act, w2_ref[...])
        first = _iota(out.shape, 1) < HEAD_DIM
        ms = jnp.sum(jnp.where(first, out * out, 0.0), axis=-1, keepdims=True) * (1.0 / HEAD_DIM)
        y = out * lax.rsqrt(ms + RMS_EPS) * kn_ref[0:1, :]
        kvc_ref[...] = jnp.where(first, _rope_lanes(y, ccos_ref[...], csin_ref[...]), out)

    r0 = pl.multiple_of(qi * TQ, TQ)
    q = q_ref[...]
    ms = _dot_split(q * q, hb_ref[...])
    qn = q * lax.rsqrt(ms + RMS_EPS) * qn_ref[...]
    cs = cos_ref[pl.ds(r0, TQ), :]
    sn = sin_ref[pl.ds(r0, TQ), :]
    qr = _rope_lanes(qn, jnp.concatenate([cs, cs], axis=1), jnp.concatenate([sn, sn], axis=1)) * (HEAD_DIM ** -0.5)
    qst = jnp.concatenate(_pad_heads(qr, as_bf16=False), axis=0)
    n_rows = N_HEADS * TQ

    kvc = kvc_ref[...]
    s = _dot3(qst, kvc, nt=True)
    t_st = r0 + (_iota((n_rows, n_cmp), 0) & (TQ - 1))
    c_end = _iota((n_rows, n_cmp), 1) * CMP_STRIDE + (CMP_BLOCK - 1)
    mask = c_end <= t_st
    sm = jnp.where(mask, s, NEG_INF)
    m = jnp.max(sm, axis=-1, keepdims=True)
    e = jnp.where(mask, jnp.exp(sm - m), 0.0)
    p = e / jnp.maximum(jnp.sum(e, axis=-1, keepdims=True), TINY)
    o_cmp = _dot(p, kvc)

    psum = p[0:TQ] + p[TQ:2 * TQ] + p[2 * TQ:3 * TQ] + p[3 * TQ:4 * TQ]
    ps_hi, ps_lo = _split(psum.T)
    cover_t = cover_ref[...].astype(BF16)
    imp = (jnp.dot(cover_t, ps_hi, preferred_element_type=F32) + jnp.dot(cover_t, ps_lo, preferred_element_type=F32))
    n_blk = S // SEL_BLOCK
    blk = _iota((n_blk, TQ), 0)
    cur = (r0 + _iota((n_blk, TQ), 1)) // SEL_BLOCK
    forced = (blk == 0) | (blk == cur) | (blk == cur - 1)
    imp = jnp.where(blk > cur, -1.0, jnp.where(forced, FORCE_SCORE, imp))
    rank = jnp.zeros((n_blk, TQ), F32)
    for d in range(1, n_blk):
        other = pltpu.roll(imp, n_blk - d, 0)
        rank = rank + jnp.where(blk + d >= n_blk,
                                jnp.where(other >= imp, 1.0, 0.0),
                                jnp.where(other > imp, 1.0, 0.0))
    sel_rows = jnp.where(rank < SEL_TOP_N, 0.0, NEG_INF)
    sel_bias = jnp.concatenate([jnp.zeros((HEAD_DIM, TQ), F32), sel_rows,
                                jnp.zeros((LANES - HEAD_DIM - n_blk, TQ), F32)], axis=0).T.astype(BF16)
    lane = _iota((TQ, LANES), 1)

    qb = (qst * LOG2E).astype(BF16)
    heads = range(N_HEADS)
    q_heads = [qb[h * TQ:(h + 1) * TQ] for h in heads]
    q_heads_sel = [jnp.where(lane < HEAD_DIM, q, sel_bias) for q in q_heads]
    init = tuple(_flash_init(TQ) for _ in heads)

    def sel_tile(j, carry, biases):
        k0 = pl.multiple_of(j * SEL_TK, SEL_TK)
        ka, vb = ksvs_ref[pl.ds(k0, SEL_TK), :], vs1_ref[pl.ds(k0, SEL_TK), :]
        return _flash_step(carry, q_heads_sel, [ka] * N_HEADS, [vb] * N_HEADS, biases)

    ratio = SEL_TK // TQ
    last = qi // ratio
    res = lax.fori_loop(0, last, lambda j, c: sel_tile(j, c, None), init)
    res = sel_tile(last, res, [ctab_ref[qi % ratio]] * N_HEADS)
    o_sel = jnp.concatenate([_flash_finish(c) for c in res], axis=0)

    n_prev = WINDOW // TQ
    span = (n_prev + 1) * TQ
    k0 = pl.multiple_of(jnp.maximum(qi - n_prev, 0) * TQ, TQ)
    ka, vb = kwvw_ref[pl.ds(k0, span), :], vw1_ref[pl.ds(k0, span), :]
    res = _flash_step(init, q_heads, [ka] * N_HEADS, [vb] * N_HEADS, [wtab_ref[jnp.minimum(qi, n_prev)]] * N_HEADS)
    o_win = jnp.concatenate([_flash_finish(c) for c in res], axis=0)

    gate = _dot_split(_sigmoid(gl_ref[...]), eg_ref[...])
    split = lambda o: _merge_heads([o[h * TQ:(h + 1) * TQ] for h in range(N_HEADS)])
    W = GROUP_WIDTH
    o_ref[...] = (gate[:, 0:W] * split(o_cmp) + gate[:, W:2 * W] * split(o_sel) + gate[:, 2 * W:3 * W] * split(o_win))


def _nsa(q, kvc, kv, z_rwkv, cos, sin, ccos, csin, q_norm, k_norm, cmp_pe, cmp_w1, cmp_w2):
    B, S, _ = q.shape
    n_cmp = S // CMP_STRIDE
    n_blk = S // SEL_BLOCK
    hd = HEAD_DIM

    qn = jnp.tile(q_norm, N_HEADS).reshape(1, GROUP_WIDTH)
    kn = jnp.concatenate([k_norm, jnp.ones_like(k_norm)], axis=1)

    def interleave(a, b):
        return jnp.concatenate([a, b], axis=1).reshape(1, -1)

    half = CMP_BLOCK // 2
    pet = interleave(cmp_pe[0, :half], cmp_pe[1, :half])
    peb = interleave(cmp_pe[0, half:], cmp_pe[1, half:])

    def w1_half(lo):
        wk = cmp_w1[0].reshape(CMP_BLOCK, hd, hd)[lo:lo + half]
        wv = cmp_w1[1].reshape(CMP_BLOCK, hd, hd)[lo:lo + half]
        z = jnp.zeros_like(wk)
        top = jnp.concatenate([wk, z], axis=2)
        bot = jnp.concatenate([z, wv], axis=2)
        return jnp.concatenate([top, bot], axis=1).reshape(half * 2 * hd, 2 * hd)

    wt, wb = w1_half(0), w1_half(half)
    w2 = jax.scipy.linalg.block_diag(cmp_w2[0], cmp_w2[1])
    hb = _head_mean_matrix()

    c_end = np.arange(n_cmp) * CMP_STRIDE + CMP_BLOCK - 1
    b_start = np.arange(n_blk) * SEL_BLOCK
    cover = np.maximum(np.minimum(c_end[:, None] + 1, b_start[None, :] + SEL_BLOCK)
                       - np.maximum(c_end[:, None] + 1 - CMP_BLOCK, b_start[None, :]), 0).astype(np.float32) / CMP_BLOCK
    cover[n_cmp - 1] = 0.0
    cover = jnp.asarray(cover.T, F32)
    eg = np.zeros((LANES, 3 * GROUP_WIDTH), np.float32)
    for h in range(N_HEADS):
        for br in range(3):
            eg[GL_LANE + h * 3 + br, br * GROUP_WIDTH + h * hd: br * GROUP_WIDTH + (h + 1) * hd] = 1.0
    eg = jnp.asarray(eg)
    ratio = SEL_TK // TQ
    ctab = np.where(np.arange(SEL_TK)[None, None, :] <= np.arange(ratio)[:, None, None] * TQ + np.arange(TQ)[None, :, None],
                    0.0, NEG_INF).astype(np.float32)
    n_prev = WINDOW // TQ
    span = (n_prev + 1) * TQ
    dist = (np.arange(n_prev + 1)[:, None, None] * TQ + np.arange(TQ)[None, :, None] - np.arange(span)[None, None, :])
    wtab = np.where((dist >= 0) & (dist < WINDOW), 0.0, NEG_INF).astype(np.float32)
    ctab, wtab = jnp.asarray(ctab), jnp.asarray(wtab)

    full = lambda shape: pl.BlockSpec(shape, lambda b, i: (0,) * len(shape))
    return pl.pallas_call(
        _nsa_kernel,
        grid=(B, S // TQ),
        in_specs=[pl.BlockSpec((None, TQ, GROUP_WIDTH), lambda b, i: (b, i, 0)),
                  pl.BlockSpec((None, TQ, LANES), lambda b, i: (b, i, RWKV_PACKED // LANES - 1)),
                  pl.BlockSpec((None, S, LANES), lambda b, i: (b, 0, 0)),
                  pl.BlockSpec((None, S, 2 * LANES), lambda b, i: (b, 0, 0)),
                  pl.BlockSpec((None, S, LANES), lambda b, i: (b, 0, 0)),
                  pl.BlockSpec((None, S, LANES), lambda b, i: (b, 0, 0)),
                  pl.BlockSpec((None, n_cmp, LANES), lambda b, i: (b, 0, 0)),
                  pl.BlockSpec((None, n_cmp, LANES), lambda b, i: (b, 0, 0)),
                  full((1, GROUP_WIDTH)), full((3, LANES)),
                  full((1, CMP_STRIDE * LANES)), full((1, CMP_STRIDE * LANES)),
                  full((CMP_STRIDE * LANES, LANES)), full((CMP_STRIDE * LANES, LANES)), full((LANES, LANES)),
                  full((GROUP_WIDTH, GROUP_WIDTH)), full((n_blk, n_cmp)),
                  full((LANES, 3 * GROUP_WIDTH)), full(ctab.shape), full(wtab.shape)],
        out_specs=pl.BlockSpec((None, TQ, GROUP_WIDTH), lambda b, i: (b, i, 0)),
        out_shape=jax.ShapeDtypeStruct((B, S, GROUP_WIDTH), F32),
        scratch_shapes=[pltpu.VMEM((S, LANES), BF16), pltpu.VMEM((S, LANES), BF16),
                        pltpu.VMEM((S, LANES), BF16), pltpu.VMEM((S, LANES), BF16),
                        pltpu.VMEM((n_cmp, LANES), F32)],
        compiler_params=_params(("parallel", "arbitrary")),
        name="nsa",
    )(q, z_rwkv, kvc, kv, cos, sin, ccos, csin, qn, kn, pet, peb, wt, wb, w2, hb, cover, eg, ctab, wtab)


def _dil_kernel(q_ref, kv_ref, cos_ref, sin_ref, qn_ref, kn_ref, hb_ref, tb_ref, o_ref, kvs_ref, v1s_ref):
    S = kv_ref.shape[0]
    qi = pl.program_id(1)
    W = GROUP_WIDTH

    @pl.when(qi == 0)
    def _prep_keys():
        rows = 512

        def body(i, _):
            r0 = pl.multiple_of(i * rows, rows)
            cs = cos_ref[pl.ds(r0, rows), :]
            sn = sin_ref[pl.ds(r0, rows), :]
            k = kv_ref[pl.ds(r0, rows), 0:W]
            v = kv_ref[pl.ds(r0, rows), W:2 * W]
            ms = _dot_split(k * k, hb_ref[...])
            kn = k * lax.rsqrt(ms + RMS_EPS) * kn_ref[...]
            kr = _rope_lanes(kn, jnp.concatenate([cs, cs], axis=1), jnp.concatenate([sn, sn], axis=1))
            lane = _iota((rows, LANES), 1)
            for h in range(N_HEADS):
                kb = kr[:, (h // 2) * LANES:(h // 2 + 1) * LANES]
                vb = v[:, (h // 2) * LANES:(h // 2 + 1) * LANES]
                if h % 2:
                    kb = pltpu.roll(kb, HEAD_DIM, 1)
                else:
                    vb = pltpu.roll(vb, HEAD_DIM, 1)
                kvs_ref[h, pl.ds(r0, rows), :] = jnp.where(lane < HEAD_DIM, kb, vb).astype(BF16)
                v1s_ref[h, pl.ds(r0, rows), :] = jnp.where(lane < HEAD_DIM, 1.0, vb).astype(BF16)
            return 0

        lax.fori_loop(0, S // rows, body, 0)

    r0 = pl.multiple_of(qi * DIL_TQ, DIL_TQ)
    q = q_ref[...]
    ms = _dot_split(q * q, hb_ref[...])
    qn = q * lax.rsqrt(ms + RMS_EPS) * qn_ref[...]
    cs = cos_ref[pl.ds(r0, DIL_TQ), :]
    sn = sin_ref[pl.ds(r0, DIL_TQ), :]
    qr = _rope_lanes(qn, jnp.concatenate([cs, cs], axis=1), jnp.concatenate([sn, sn], axis=1)) * (HEAD_DIM ** -0.5 * LOG2E)
    qh = _pad_heads(qr)
    ratio = TK // DIL_TQ

    def body(j, carry):
        k0 = pl.multiple_of(j * TK, TK)
        bias = tb_ref[qi - ratio * j]
        return _flash_step(carry, qh, [kvs_ref[h, pl.ds(k0, TK), :] for h in range(N_HEADS)],
                           [v1s_ref[h, pl.ds(k0, TK), :] for h in range(N_HEADS)], [bias] * N_HEADS)

    res = lax.fori_loop(0, qi // ratio + 1, body, tuple(_flash_init(DIL_TQ) for _ in range(N_HEADS)))
    o_ref[...] = _merge_heads([_flash_finish(c) for c in res])


def _dil_bias_table(S):
    n = S // DIL_TQ
    d = (np.arange(n)[:, None, None] * DIL_TQ + np.arange(DIL_TQ)[None, :, None] - np.arange(TK)[None, None, :])
    cnt = np.zeros(d.shape, np.float32)
    for window, dil in DIL_PATTERNS:
        cnt += ((d >= 0) & (d <= window) & (d % dil == 0)).astype(np.float32)
    return jnp.asarray(np.where(cnt > 0, np.log2(np.maximum(cnt, 1.0)), NEG_INF).astype(np.float32))


def _dilated(q, kv, cos, sin, q_norm, k_norm):
    B, S, W = q.shape
    tb = _dil_bias_table(S)
    qn = jnp.tile(q_norm, N_HEADS).reshape(1, W)
    kn = jnp.tile(k_norm, N_HEADS).reshape(1, W)
    hb = _head_mean_matrix()
    full = lambda shape: pl.BlockSpec(shape, lambda b, i: (0,) * len(shape))
    return pl.pallas_call(
        _dil_kernel,
        grid=(B, S // DIL_TQ),
        in_specs=[pl.BlockSpec((None, DIL_TQ, W), lambda b, i: (b, i, 0)),
                  pl.BlockSpec((None, S, 2 * W), lambda b, i: (b, 0, 0)),
                  pl.BlockSpec((None, S, LANES), lambda b, i: (b, 0, 0)),
                  pl.BlockSpec((None, S, LANES), lambda b, i: (b, 0, 0)),
                  full((1, W)), full((1, W)), full((W, W)), full(tb.shape)],
        out_specs=pl.BlockSpec((None, DIL_TQ, W), lambda b, i: (b, i, 0)),
        out_shape=jax.ShapeDtypeStruct((B, S, W), F32),
        scratch_shapes=[pltpu.VMEM((N_HEADS, S, LANES), BF16), pltpu.VMEM((N_HEADS, S, LANES), BF16)],
        compiler_params=_params(("parallel", "arbitrary")),
        name="dilated",
    )(q, kv, cos, sin, qn, kn, hb, tb)


def _softplus(x):
    return jnp.maximum(x, 0.0) + jnp.log(1.0 + jnp.exp(-jnp.abs(x)))


def _rwkv_kernel(first, *refs):
    if first:
        (z_ref, mu_ref, w0_ref, w2_ref, a0_ref, a2_ref, g2_ref, kk_ref, ka_ref, rk_ref, lnw_ref, lnb_ref,
         hb_ref, y_ref, vf_out_ref, st_ref, prev_ref, ro_ref, ub_ref, colb_ref, g_ref, t2_ref) = refs
    else:
        (z_ref, vf_ref, mu_ref, w0_ref, w2_ref, a0_ref, a2_ref, g2_ref, kk_ref, ka_ref, rk_ref, lnw_ref, lnb_ref,
         v0_ref, v1_ref, v2_ref, hb_ref, y_ref, st_ref, prev_ref, ro_ref, ub_ref, colb_ref, g_ref, t2_ref) = refs
    SB = z_ref.shape[0]
    C = RWKV_CHUNK
    W = GROUP_WIDTH
    n_pairs = N_HEADS // 2
    R = RWKV_GROUP * C

    @pl.when(pl.program_id(1) == 0)
    def _reset():
        st_ref[...] = jnp.zeros_like(st_ref)
        prev_ref[...] = jnp.zeros_like(prev_ref)

    ri = _iota((C, C), 0)
    ci = _iota((C, C), 1)
    tri_incl = jnp.where(ci <= ri, 1.0, 0.0).astype(BF16)
    rr = _iota((2 * LANES, 2 * LANES), 0)
    cc = _iota((2 * LANES, 2 * LANES), 1)
    g_mask = ((((rr >> 6) & 1) == ((cc >> 6) & 1))
              & ((cc & (C - 1)) <= jnp.where(rr < LANES, (rr & (C - 1)) - 1, rr & (C - 1))))
    r2 = _iota((LANES, LANES), 0)
    c2 = _iota((LANES, LANES), 1)
    eye_f = jnp.where(r2 == c2, 1.0, 0.0)
    lane_lo = _iota((C, LANES), 1) < HEAD_DIM
    hb = hb_ref[...].astype(BF16)

    def stack(x):
        return jnp.concatenate([jnp.where(lane_lo, x, 0.0), jnp.where(lane_lo, 0.0, x)], axis=0)

    def phase1(i, _):
        t0 = pl.multiple_of(i * R, R)
        zc = z_ref[pl.ds(t0, R), :]
        inside = jnp.where(i > 0, 1.0, 0.0)
        prev = inside * z_ref[pl.ds(jnp.maximum(t0 - 1, 0), 1), :] + (1.0 - inside) * prev_ref[...]
        zp = jnp.where(_iota(zc.shape, 0) == 0, prev, pltpu.roll(zc, 1, 0))
        zs = zc + (zp - zc) * mu_ref[...]
        r, k, v = zs[:, 0:W], zs[:, W:2 * W], zs[:, 2 * W:3 * W]
        xwa, xg = zs[:, 768:896], zs[:, 896:1152]
        w_log = -_softplus(-(w0_ref[...] + _dot(jnp.tanh(xwa), w2_ref[...]))) - 0.5
        lw = -jnp.exp(w_log)
        a = _sigmoid(a0_ref[...] + _dot(xwa, a2_ref[...]))
        g = _dot(_sigmoid(xg), g2_ref[...])
        if first:
            vf_out_ref[pl.ds(t0, R), :] = v
        else:
            vf = vf_ref[pl.ds(t0, R), :]
            v = v + (vf - v) * _sigmoid(v0_ref[...] + _dot(_dot(v, v1_ref[...]), v2_ref[...]))
        kkr = k * kk_ref[...]
        nrm = jnp.sqrt(_dot_split(kkr * kkr, hb) * HEAD_DIM)
        kk = kkr / jnp.maximum(nrm, 1e-12)
        k2 = k * (1.0 + (a - 1.0) * ka_ref[...])
        bonus = _dot_split(r * k2 * rk_ref[...], hb) * HEAD_DIM * v
        g_ref[pl.ds(t0, R), :] = g
        t2_ref[pl.ds(t0, R), :] = (lnb_ref[...] + bonus) * g

        items = [(c, p) for c in range(RWKV_GROUP) for p in range(n_pairs)]
        idx = range(len(items))
        sub = lambda x, c, p: x[c * C:(c + 1) * C, p * LANES:(p + 1) * LANES]
        cum = [_cumsum_rows(tri_incl, sub(lw, c, p)) for c, p in items]
        a_st, r_st, v_st, gm, tails = [], [], [], [], []
        for j, (c, p) in enumerate(items):
            lwp = sub(lw, c, p)
            cum_end = cum[j][C - 1:C, :]
            e_pos, e_neg = jnp.exp(cum[j]), jnp.exp(-cum[j])
            e_exc, e_tail = jnp.exp(cum[j] - lwp), jnp.exp(cum_end - cum[j])
            kkp, k2p = sub(kk, c, p), sub(k2, c, p)
            ka_p = kkp * sub(a, c, p)
            a_st.append(stack(-kkp * e_exc).astype(BF16))
            r_st.append(stack(sub(r, c, p) * e_pos))
            v_st.append(stack(sub(v, c, p)).astype(BF16))
            bhat = (ka_p * e_neg).astype(BF16)
            khat = (k2p * e_neg).astype(BF16)
            tails.append(jnp.concatenate([stack(ka_p * e_tail), stack(k2p * e_tail)], axis=0))
            gm.append(jnp.where(g_mask,
                                _dot_nt(jnp.concatenate([a_st[j], r_st[j].astype(BF16)], axis=0),
                                        jnp.concatenate([bhat, bhat, khat, khat], axis=0)), 0.0))
            colb_ref[i * RWKV_GROUP + c, p] = jnp.broadcast_to(jnp.exp(cum_end), (LANES, LANES)).T
        tt = [tails[j].T.astype(BF16) for j in idx]
        lakv = [_dot(gm[j][0:LANES, LANES:2 * LANES], v_st[j]) for j in idx]
        mr = [gm[j][LANES:2 * LANES, :].astype(BF16) for j in idx]
        lp = [gm[j][0:LANES, 0:LANES] for j in idx]
        tinv = [eye_f + lp[j] for j in idx]
        for _ in range(5):
            lpb = [lp[j].astype(BF16) for j in idx]
            lp = [jnp.dot(lpb[j], lpb[j], preferred_element_type=F32) for j in idx]
            tinv = [tinv[j] + _dot(lp[j], tinv[j]) for j in idx]
        wu = [_dot(tinv[j], jnp.concatenate([a_st[j].astype(F32), lakv[j]], axis=1)) for j in idx]
        wm = [wu[j][:, 0:LANES].astype(BF16) for j in idx]
        u = [wu[j][:, LANES:2 * LANES].astype(BF16) for j in idx]
        mwbw = [jnp.dot(jnp.concatenate([mr[j][:, 0:LANES], tt[j][:, 0:LANES]], axis=0), wm[j],
                        preferred_element_type=F32) for j in idx]
        ub = [jnp.dot(jnp.concatenate([mr[j], tt[j]], axis=0), jnp.concatenate([u[j], v_st[j]], axis=0),
                      preferred_element_type=F32) for j in idx]
        for j, (c, p) in enumerate(items):
            n = i * RWKV_GROUP + c
            ro_ref[n, p] = jnp.concatenate([r_st[j] + mwbw[j][0:LANES], mwbw[j][LANES:2 * LANES]], axis=0).astype(BF16)
            ub_ref[n, p] = ub[j]
        return 0

    def phase2(n, _):
        t0 = pl.multiple_of(n * C, C)
        pairs = range(n_pairs)
        st = [st_ref[p] for p in pairs]
        res = [jnp.dot(ro_ref[n, p], st[p].astype(BF16), preferred_element_type=F32) for p in pairs]
        ub = [ub_ref[n, p] for p in pairs]
        for p in pairs:
            st_ref[p] = colb_ref[n, p] * st[p] + res[p][LANES:2 * LANES] + ub[p][LANES:2 * LANES]
        o_st = [res[p][0:LANES] + ub[p][0:LANES] for p in pairs]
        y_ref[pl.ds(t0, C), :] = jnp.concatenate([o[0:C] + o[C:2 * C] for o in o_st], axis=1)
        return 0

    lax.fori_loop(0, SB // R, phase1, 0)
    lax.fori_loop(0, SB // C, phase2, 0)
    prev_ref[...] = z_ref[SB - 1:SB, :]

    tiles = [slice(t * LANES, (t + 1) * LANES) for t in range(SB // LANES)]
    o = [y_ref[rows, :] for rows in tiles]
    dev = [x - _dot_split(x, hb) for x in o]
    var = [_dot_split(d * d, hb) for d in dev]
    for rows, d, v in zip(tiles, dev, var):
        y_ref[rows, :] = d * lax.rsqrt(v + RWKV_GN_EPS) * (lnw_ref[...] * g_ref[rows, :]) + t2_ref[rows, :]


def _cumsum_rows(tri_incl, x):
    hi = x.astype(BF16)
    lo = (x - hi.astype(F32)).astype(BF16)
    return jnp.dot(tri_incl, hi, preferred_element_type=F32) + jnp.dot(tri_incl, lo, preferred_element_type=F32)


RWKV_BLOCK = 512


def _rwkv(z, v_first, v_res, mu, w0, w2, a0, a2, g2, k_k, k_a, r_k, ln_w, ln_b):
    B, S, P = z.shape
    W = GROUP_WIDTH
    SB = RWKV_BLOCK
    ncb = SB // RWKV_CHUNK
    n_pairs = N_HEADS // 2
    first = v_res is None
    row = lambda v: v.reshape(1, W)
    padr = lambda m, n: jnp.pad(m, ((0, n - m.shape[0]), (0, 0)))
    seq = lambda w: pl.BlockSpec((None, SB, w), lambda b, j: (b, j, 0))
    full = lambda shape: pl.BlockSpec(shape, lambda b, j: (0,) * len(shape))
    args = [z]
    specs = [seq(P)]
    if not first:
        args.append(v_first)
        specs.append(seq(W))
    a2_pad = jnp.concatenate([jnp.zeros_like(a2), a2], axis=0)
    args += [_pack_rwkv_row(mu), row(w0), padr(w2, LANES), row(a0), a2_pad, padr(g2, 2 * LANES),
             row(k_k), row(k_a), r_k.reshape(1, W), row(ln_w), row(ln_b)]
    specs += [full((1, P)), full((1, W)), full((LANES, W)), full((1, W)), full((LANES, W)), full((2 * LANES, W)),
              full((1, W)), full((1, W)), full((1, W)), full((1, W)), full((1, W))]
    if not first:
        v0, v1, v2 = v_res
        args += [row(v0), jnp.pad(v1, ((0, 0), (0, LANES - v1.shape[1]))), padr(v2, LANES)]
        specs += [full((1, W)), full((W, LANES)), full((LANES, W))]
    args.append(_head_mean_matrix())
    specs.append(full((W, W)))
    out_shape = [jax.ShapeDtypeStruct((B, S, W), F32)]
    out_specs = [seq(W)]
    if first:
        out_shape.append(jax.ShapeDtypeStruct((B, S, W), F32))
        out_specs.append(seq(W))
    res = pl.pallas_call(
        functools.partial(_rwkv_kernel, first),
        grid=(B, S // SB),
        in_specs=specs,
        out_specs=out_specs,
        out_shape=out_shape,
        scratch_shapes=[pltpu.VMEM((n_pairs, LANES, LANES), F32), pltpu.VMEM((1, P), F32),
                        pltpu.VMEM((ncb, n_pairs, 2 * LANES, LANES), BF16),
                        pltpu.VMEM((ncb, n_pairs, 2 * LANES, LANES), F32),
                        pltpu.VMEM((ncb, n_pairs, LANES, LANES), F32),
                        pltpu.VMEM((SB, W), F32), pltpu.VMEM((SB, W), F32)],
        compiler_params=_params(("parallel", "arbitrary")),
        name="rwkv7_first" if first else "rwkv7",
    )(*args)
    return (res[0], res[1]) if first else (res[0], v_first)


ROUTER_ROWS = 32


def _expert_lane(e):
    return (e % 4) * 8 + e // 4


def _out_kernel(x_ref, ya_ref, yb_ref, yc_ref, yd_ref, wo_ref, mod_ref, g_ref, rwh_ref, rwl_ref, rb_ref,
                xo_ref, h_ref, gates_ref):
    y = jnp.concatenate([ya_ref[...].astype(BF16), yb_ref[...].astype(BF16),
                         yc_ref[...].astype(BF16), yd_ref[...].astype(BF16)], axis=1)
    mix = jnp.dot(y, wo_ref[...], preferred_element_type=F32)
    x = x_ref[...] + mod_ref[2:3, :] * mix
    xo_ref[...] = x
    ms = jnp.mean(x * x, axis=-1, keepdims=True)
    h = x * lax.rsqrt(ms + RMS_EPS) * g_ref[...] * (1.0 + mod_ref[4:5, :]) + mod_ref[3:4, :]
    h_hi = h.astype(BF16)
    h_ref[...] = h_hi

    h_lo = (h - h_hi.astype(F32)).astype(BF16)
    logits = _dot_nt(rwh_ref[...], h_hi) + _dot_nt(rwh_ref[...], h_lo) + _dot_nt(rwl_ref[...], h_hi)
    aff = _sigmoid(logits)
    s = aff + rb_ref[...]
    n_grp = ROUTER_ROWS // 4
    slot = lambda x, j: x[j * n_grp:(j + 1) * n_grp]
    s_j = [slot(s, j) for j in range(4)]
    a_j = [slot(aff, j) for j in range(4)]
    top2 = []
    for j in range(4):
        rank = jnp.zeros(s_j[j].shape, F32)
        for m in range(4):
            if m != j:
                beats = (s_j[m] >= s_j[j]) if m < j else (s_j[m] > s_j[j])
                rank = rank + jnp.where(beats, 1.0, 0.0)
        top2.append(rank < 2.0)
    gs = sum(jnp.where(top2[j], s_j[j], 0.0) for j in range(4))
    row = _iota(gs.shape, 0)
    lost = jnp.zeros(gs.shape, F32)
    for d in range(1, n_grp):
        other = pltpu.roll(gs, d, 0)
        lost = lost + jnp.where(row >= d, jnp.where(other >= gs, 1.0, 0.0), jnp.where(other > gs, 1.0, 0.0))
    chosen = lost < 0.5
    sel = [top2[j] & chosen for j in range(4)]
    den = sum(jnp.where(sel[j], a_j[j], 0.0) for j in range(4))
    den = jnp.where(den > 0.0, den, 1.0)
    gates_t = jnp.concatenate([jnp.where(sel[j], a_j[j] / den, 0.0) for j in range(4)]
                              + [jnp.zeros((LANES - ROUTER_ROWS, s.shape[1]), F32)], axis=0)
    gates_ref[...] = gates_t.T


def _out_proj(x, ys, w_out, mod_l, g, router_w, router_b):
    B, S, D = x.shape
    tm = 1024
    W = GROUP_WIDTH
    order = np.full((ROUTER_ROWS,), -1)
    order[[_expert_lane(e) for e in range(N_EXPERTS)]] = np.arange(N_EXPERTS)
    real = jnp.asarray(order >= 0)
    rw = jnp.where(real[:, None], router_w.T[np.maximum(order, 0)], 0.0)
    rw_hi = rw.astype(BF16)
    rw_lo = (rw - rw_hi.astype(F32)).astype(BF16)
    rb = jnp.where(real, router_b[np.maximum(order, 0)], NEG_INF).reshape(ROUTER_ROWS, 1)
    tile = lambda w: pl.BlockSpec((None, tm, w), lambda b, i: (b, i, 0))
    full = lambda shape: pl.BlockSpec(shape, lambda b, i: (0,) * len(shape))
    return pl.pallas_call(
        _out_kernel,
        grid=(B, S // tm),
        in_specs=[tile(D), tile(W), tile(W), tile(W), tile(W), full((D, D)),
                  pl.BlockSpec((None, 6, D), lambda b, i: (b, 0, 0)), full((1, D)),
                  full((ROUTER_ROWS, D)), full((ROUTER_ROWS, D)), full((ROUTER_ROWS, 1))],
        out_specs=[tile(D), tile(D), tile(LANES)],
        out_shape=[jax.ShapeDtypeStruct((B, S, D), F32), jax.ShapeDtypeStruct((B, S, D), BF16),
                   jax.ShapeDtypeStruct((B, S, LANES), F32)],
        compiler_params=_params(("parallel", "parallel")),
        name="out_proj_router",
    )(x, *ys, w_out.astype(BF16), mod_l, g.reshape(1, D), rw_hi, rw_lo, rb)


MOE_EXPERTS_PER_STEP = 4


def _moe_kernel(x_ref, h_ref, gates_ref, mod_ref, wg_ref, wu_ref, wd_ref, o_ref):
    step = pl.program_id(2)
    n = MOE_EXPERTS_PER_STEP

    @pl.when(step == 0)
    def _init():
        o_ref[...] = x_ref[...]

    h = h_ref[...]
    gates = gates_ref[...]
    lane = _iota(gates.shape, 1)
    hgs = [jnp.dot(h, wg_ref[k].astype(BF16), preferred_element_type=F32) for k in range(n)]
    hus = [jnp.dot(h, wu_ref[k].astype(BF16), preferred_element_type=F32) for k in range(n)]
    acts = []
    for k in range(n):
        ge = jnp.sum(jnp.where(lane == _expert_lane(step * n + k), gates, 0.0), axis=-1, keepdims=True)
        acts.append((hgs[k] * _sigmoid(hgs[k]) * hus[k] * ge).astype(BF16))
    act = jnp.concatenate(acts, axis=1)
    wd = wd_ref[...].astype(BF16).reshape(n * D_EXPERT, wd_ref.shape[2])
    o_ref[...] += mod_ref[5:6, :] * jnp.dot(act, wd, preferred_element_type=F32)


def _moe(x, h, gates, mod_l, w_gate, w_up, w_down, layer):
    B, S, D = x.shape
    E = w_gate.shape[1]
    n = MOE_EXPERTS_PER_STEP
    tm = 1024
    tile = lambda w: pl.BlockSpec((None, tm, w), lambda b, i, e: (b, i, 0))
    return pl.pallas_call(
        _moe_kernel,
        grid=(B, S // tm, E // n),
        in_specs=[tile(D), tile(D), tile(LANES),
                  pl.BlockSpec((None, 6, D), lambda b, i, e: (b, 0, 0)),
                  pl.BlockSpec((None, n, D, D_EXPERT), lambda b, i, e: (layer, e, 0, 0)),
                  pl.BlockSpec((None, n, D, D_EXPERT), lambda b, i, e: (layer, e, 0, 0)),
                  pl.BlockSpec((None, n, D_EXPERT, D), lambda b, i, e: (layer, e, 0, 0))],
        out_specs=tile(D),
        out_shape=jax.ShapeDtypeStruct((B, S, D), F32),
        compiler_params=_params(("parallel", "parallel", "arbitrary")),
        name="moe",
    )(x, h, gates, mod_l, w_gate, w_up, w_down)


def kernel(x, c, positions, ada_w, ada_b, norm_mix_g, norm_ffn_g, w_in, w_out, nsa_q_norm, nsa_k_norm, nsa_cmp_pe, nsa_cmp_w1, nsa_cmp_w2, pool_w, pool_scale, rwkv_mu, rwkv_w0, rwkv_w2, rwkv_a0, rwkv_a2, rwkv_g2, rwkv_k_k, rwkv_k_a, rwkv_r_k, rwkv_ln_w, rwkv_ln_b, rwkv_v0, rwkv_v1, rwkv_v2, dil_q_norm, dil_k_norm, router_w, router_b, moe_w_gate, moe_w_up, moe_w_down):
    depth = ada_w.shape[0]
    mod = _modulation(c, ada_w, ada_b)
    cos, sin, ccos, csin = _rope_tables(positions)
    v_first = None
    for l in range(depth):
        q_a, kvc_a, kv_a, u_pool, z_rwkv, q_d, kv_d = _in_proj(x, mod[l], norm_mix_g[l], w_in, l)
        y_nsa = _nsa(q_a, kvc_a, kv_a, z_rwkv, cos, sin, ccos, csin, nsa_q_norm[l], nsa_k_norm[l],
                     nsa_cmp_pe[l], nsa_cmp_w1[l], nsa_cmp_w2[l])
        y_pool = _pool(u_pool, pool_w[l], pool_scale[l])
        v_res = None if l == 0 else (rwkv_v0[l - 1], rwkv_v1[l - 1], rwkv_v2[l - 1])
        y_rwkv, v_first = _rwkv(z_rwkv, v_first, v_res, rwkv_mu[l], rwkv_w0[l], rwkv_w2[l], rwkv_a0[l], rwkv_a2[l],
                                rwkv_g2[l], rwkv_k_k[l], rwkv_k_a[l], rwkv_r_k[l], rwkv_ln_w[l], rwkv_ln_b[l])
        y_dil = _dilated(q_d, kv_d, cos, sin, dil_q_norm[l], dil_k_norm[l])
        x_mid, h2, gates = _out_proj(x, (y_nsa, y_pool, y_rwkv, y_dil), w_out[l], mod[l], norm_ffn_g[l],
                                     router_w, router_b)
        x = _moe(x_mid, h2, gates, mod[l], moe_w_gate, moe_w_up, moe_w_down, l)
    return x
```

```python
import functools

import numpy as np
import jax
import jax.numpy as jnp
from jax import lax
from jax.experimental import pallas as pl
from jax.experimental.pallas import tpu as pltpu

F32 = jnp.float32
BF16 = jnp.bfloat16

D_MODEL = 1024
HEAD_DIM = 64
GROUP_WIDTH = 256
N_HEADS = 4
RMS_EPS = 1e-6
ROPE_THETA = 500000.0
ROPE_DIM = 16
CMP_BLOCK = 32
CMP_STRIDE = 16
SEL_BLOCK = 64
SEL_TOP_N = 16
WINDOW = 512
FORCE_SCORE = 1e4
POOL_WINDOWS = (2, 4, 8, 16)
RWKV_GN_EPS = 64e-5
DIL_PATTERNS = ((128, 1), (512, 4), (2048, 16))
N_EXPERTS = 16
D_EXPERT = 256
NEG_INF = -1e30
LOG2E = 1.4426950408889634
TINY = 1e-30

LANES = 128
TQ = 256
DIL_TQ = 512
TK = 512
SEL_TK = 512
RWKV_CHUNK = 64
RWKV_GROUP = 8
VMEM_LIMIT = 56 * 1024 * 1024

_IN_COLS = dict(q=(0, 256), kvc=(256, 384), kv=(384, 640), pool=(640, 896),
                rwkv=(896, 2048), dq=(2048, 2304), dkv=(2304, 2816))
IN_PACKED = 2816
RWKV_PACKED = 1152
GL_LANE = 32


def _dot(a, b):
    return jnp.dot(a.astype(BF16), b.astype(BF16), preferred_element_type=F32)


def _dot_nt(a, b):
    return lax.dot_general(a, b, (((1,), (1,)), ((), ())), preferred_element_type=F32)


def _split(x):
    hi = x.astype(BF16)
    return hi, (x - hi.astype(F32)).astype(BF16)


def _dot_split(a, b):
    hi, lo = _split(a)
    b = b.astype(BF16)
    return jnp.dot(hi, b, preferred_element_type=F32) + jnp.dot(lo, b, preferred_element_type=F32)


def _dot3(a, b, nt=False):
    f = _dot_nt if nt else functools.partial(jnp.dot, preferred_element_type=F32)
    a_hi, a_lo = _split(a)
    b_hi, b_lo = _split(b)
    return f(a_hi, b_hi) + f(a_lo, b_hi) + f(a_hi, b_lo)


def _sigmoid(x):
    return 1.0 / (1.0 + jnp.exp(-x))


def _iota(shape, dim):
    return lax.broadcasted_iota(jnp.int32, shape, dim)


def _rope_lanes(y, cos, sin):
    n = y.shape[-1]
    lane = _iota(y.shape, y.ndim - 1)
    partner = jnp.where((lane & 15) < 8, pltpu.roll(y, n - 8, y.ndim - 1), pltpu.roll(y, 8, y.ndim - 1))
    return y * cos + partner * sin


def _params(sem, vmem=VMEM_LIMIT):
    return pltpu.CompilerParams(dimension_semantics=sem, vmem_limit_bytes=vmem)


def _mod_kernel(c_ref, w_ref, b_ref, o_ref):
    c = c_ref[...]
    o_ref[...] = _dot3(c * _sigmoid(c), w_ref[...]) + b_ref[...]


def _modulation(c, ada_w, ada_b):
    L, D, E = ada_w.shape
    B = c.shape[0]
    tn = 1024
    out = pl.pallas_call(
        _mod_kernel,
        grid=(L, E // tn),
        in_specs=[pl.BlockSpec((B, D), lambda l, j: (0, 0)),
                  pl.BlockSpec((None, D, tn), lambda l, j: (l, 0, j)),
                  pl.BlockSpec((None, 1, tn), lambda l, j: (l, 0, j))],
        out_specs=pl.BlockSpec((None, B, tn), lambda l, j: (l, 0, j)),
        out_shape=jax.ShapeDtypeStruct((L, B, E), F32),
        compiler_params=_params(("parallel", "parallel")),
        name="modulation",
    )(c, ada_w, ada_b.reshape(L, 1, E))
    return out.reshape(L, B, 6, D)


ROPE_ROWS = 64


def _rope_kernel(pos_ref, cpos_ref, freq_ref, ecos_ref, esin_ref, one_ref, cos_ref, sin_ref, ccos_ref, csin_ref):
    def tables(pos_row):
        ang = freq_ref[...] * pos_row.astype(F32)
        parts = []
        for v in (jnp.cos(ang), jnp.sin(ang)):
            hi = v.astype(BF16)
            r1 = v - hi.astype(F32)
            mid = r1.astype(BF16)
            parts += [hi, mid, (r1 - mid.astype(F32)).astype(BF16)]
        n = ang.shape[1]
        stacked = jnp.concatenate(parts + [jnp.zeros((ROPE_ROWS - 6 * ang.shape[0], n), BF16)], axis=0)
        spread = lambda e: lax.dot_general(stacked, e, (((0,), (0,)), ((), ())), preferred_element_type=F32)
        return spread(ecos_ref[...]) + one_ref[...], spread(esin_ref[...])

    cos_ref[...], sin_ref[...] = tables(pos_ref[...])
    ccos_ref[...], csin_ref[...] = tables(cpos_ref[...])


def _rope_tables(positions):
    B, S = positions.shape
    half = ROPE_DIM // 2
    inv_freq = ROPE_THETA ** (-2.0 * jnp.arange(half, dtype=F32) / ROPE_DIM)
    lane = np.arange(LANES)
    rot = (lane % HEAD_DIM) < ROPE_DIM
    pick = (lane[None, :] % half == np.arange(half)[:, None]) & rot[None, :]
    sgn = np.where(lane % HEAD_DIM < half, -1.0, 1.0)
    zeros = np.zeros((3 * half, LANES))
    ecos = np.concatenate([np.tile(pick, (3, 1)), zeros, np.zeros((ROPE_ROWS - 6 * half, LANES))], axis=0)
    esin = np.concatenate([zeros, np.tile(pick * sgn[None, :], (3, 1)), np.zeros((ROPE_ROWS - 6 * half, LANES))], axis=0)
    one = jnp.asarray(np.where(rot, 0.0, 1.0), F32).reshape(1, LANES)
    n_cmp = S // CMP_STRIDE
    cpos = positions[:, CMP_BLOCK - 1::CMP_STRIDE]
    cpos = jnp.pad(cpos, ((0, 0), (0, n_cmp - cpos.shape[1])))
    tab = jax.ShapeDtypeStruct((B, S, LANES), F32)
    ctab = jax.ShapeDtypeStruct((B, n_cmp, LANES), F32)
    full = lambda shape: pl.BlockSpec(shape, lambda b: (0,) * len(shape))
    return pl.pallas_call(
        _rope_kernel,
        grid=(B,),
        in_specs=[pl.BlockSpec((None, 1, S), lambda b: (b, 0, 0)),
                  pl.BlockSpec((None, 1, n_cmp), lambda b: (b, 0, 0)),
                  full((half, 1)), full((ROPE_ROWS, LANES)), full((ROPE_ROWS, LANES)), full((1, LANES))],
        out_specs=[pl.BlockSpec((None, S, LANES), lambda b: (b, 0, 0)),
                   pl.BlockSpec((None, S, LANES), lambda b: (b, 0, 0)),
                   pl.BlockSpec((None, n_cmp, LANES), lambda b: (b, 0, 0)),
                   pl.BlockSpec((None, n_cmp, LANES), lambda b: (b, 0, 0))],
        out_shape=[tab, tab, ctab, ctab],
        compiler_params=_params(("parallel",)),
        name="rope_tables",
    )(positions.reshape(B, 1, S), cpos.reshape(B, 1, n_cmp), inv_freq.reshape(half, 1),
      jnp.asarray(ecos, BF16), jnp.asarray(esin, BF16), one)


_W_IN_MOVES = ((0, 640, 0), (640, 652, 1792 + 128 + GL_LANE), (652, 908, 640), (908, 1676, 896), (1676, 1740, 1664),
               (1740, 1804, 1728), (1804, 1964, 1792), (1964, 2732, 2048))


def _in_kernel(x_ref, mod_ref, g_ref, w_ref, *refs):
    out_refs, wp_ref = refs[:-1], refs[-1]

    @pl.when((pl.program_id(0) == 0) & (pl.program_id(1) == 0))
    def _pack_weights():
        rows = 128

        def body(r, _):
            r0 = pl.multiple_of(r * rows, rows)
            w = w_ref[pl.ds(r0, rows), :]
            wp_ref[pl.ds(r0, rows), :] = jnp.zeros((rows, IN_PACKED), BF16)
            for lo, hi, dst in _W_IN_MOVES:
                wp_ref[pl.ds(r0, rows), dst:dst + hi - lo] = w[:, lo:hi]
            return 0

        lax.fori_loop(0, w_ref.shape[0] // rows, body, 0)

    x = x_ref[...]
    ms = jnp.mean(x * x, axis=-1, keepdims=True)
    y = x * lax.rsqrt(ms + RMS_EPS) * g_ref[...]
    h = y * (1.0 + mod_ref[1:2, :]) + mod_ref[0:1, :]
    p = jnp.dot(h.astype(BF16), wp_ref[...], preferred_element_type=F32)
    for ref, (lo, hi) in zip(out_refs, _IN_COLS.values()):
        ref[...] = p[:, lo:hi]


def _in_proj(x, mod_l, g, w_in, layer):
    B, S, D = x.shape
    tm = 1024
    widths = [hi - lo for lo, hi in _IN_COLS.values()]
    return pl.pallas_call(
        _in_kernel,
        grid=(B, S // tm),
        in_specs=[pl.BlockSpec((None, tm, D), lambda b, i: (b, i, 0)),
                  pl.BlockSpec((None, 6, D), lambda b, i: (b, 0, 0)),
                  pl.BlockSpec((1, D), lambda b, i: (0, 0)),
                  pl.BlockSpec((None,) + w_in.shape[1:], lambda b, i: (layer, 0, 0), pipeline_mode=pl.Buffered(1))],
        out_specs=[pl.BlockSpec((None, tm, w), lambda b, i: (b, i, 0)) for w in widths],
        out_shape=[jax.ShapeDtypeStruct((B, S, w), F32) for w in widths],
        scratch_shapes=[pltpu.VMEM((D, IN_PACKED), BF16)],
        compiler_params=_params(("arbitrary", "arbitrary")),
        name="in_proj",
    )(x, mod_l, g.reshape(1, D), w_in.astype(BF16))


def _pack_rwkv_row(v):
    z = lambda n: jnp.zeros((n,), v.dtype)
    return jnp.concatenate([v[0:1056], z(RWKV_PACKED - 1056)]).reshape(1, RWKV_PACKED)


def _pool_kernel(u_ref, w_ref, scale_ref, o_ref):
    u = u_ref[...]
    row = _iota(u.shape, 0)
    lane = _iota(u.shape, 1)

    def lag(x, s):
        return jnp.where(row >= s, pltpu.roll(x, s, 0), 0.0)

    s2 = u + lag(u, 1)
    s4 = s2 + lag(s2, 2)
    s8 = s4 + lag(s4, 4)
    s16 = s8 + lag(s8, 8)
    g = lane // (GROUP_WIDTH // len(POOL_WINDOWS))
    tot = jnp.where(g == 0, s2, jnp.where(g == 1, s4, jnp.where(g == 2, s8, s16)))
    win = jnp.where(g == 0, 2.0, jnp.where(g == 1, 4.0, jnp.where(g == 2, 8.0, 16.0)))
    cnt = jnp.minimum((row + 1).astype(F32), win)
    o_ref[...] = _dot(tot / cnt - u, w_ref[...]) * scale_ref[...]


def _pool(u, pool_w, pool_scale):
    B, S, W = u.shape
    wbd = jax.scipy.linalg.block_diag(*[pool_w[i] for i in range(pool_w.shape[0])]).astype(BF16)
    return pl.pallas_call(
        _pool_kernel,
        grid=(B,),
        in_specs=[pl.BlockSpec((None, S, W), lambda b: (b, 0, 0)),
                  pl.BlockSpec((W, W), lambda b: (0, 0)),
                  pl.BlockSpec((1, W), lambda b: (0, 0))],
        out_specs=pl.BlockSpec((None, S, W), lambda b: (b, 0, 0)),
        out_shape=jax.ShapeDtypeStruct((B, S, W), F32),
        compiler_params=_params(("parallel",)),
        name="pool_mixer",
    )(u, wbd, pool_scale.reshape(1, W))


def _head_mean_matrix():
    blk = np.kron(np.eye(N_HEADS), np.ones((HEAD_DIM, HEAD_DIM))) / HEAD_DIM
    return jnp.asarray(blk, F32)


def _pad_heads(x256, as_bf16=True):
    n = x256.shape[0]
    lane = _iota((n, LANES), 1)
    out = []
    for h in range(N_HEADS):
        blk = x256[:, (h // 2) * LANES:(h // 2 + 1) * LANES]
        if h % 2:
            blk = pltpu.roll(blk, HEAD_DIM, 1)
        blk = jnp.where(lane < HEAD_DIM, blk, 0.0)
        out.append(blk.astype(BF16) if as_bf16 else blk)
    return out


def _merge_heads(slabs):
    n = slabs[0].shape[0]
    lane = _iota((n, LANES), 1)
    cols = []
    for p in range(N_HEADS // 2):
        even = pltpu.roll(slabs[2 * p], HEAD_DIM, 1)
        cols.append(jnp.where(lane < HEAD_DIM, even, slabs[2 * p + 1]))
    return jnp.concatenate(cols, axis=1)


def _flash_step(carries, qs, kas, vbs, biases):
    idx = range(len(qs))
    s = [_dot_nt(qs[i], kas[i]) for i in idx]
    if biases is not None:
        s = [s[i] + biases[i] for i in idx]
    m_new = [jnp.maximum(carries[i][0], jnp.max(s[i], axis=-1, keepdims=True)) for i in idx]
    alpha = [jnp.exp2(carries[i][0] - m_new[i]) for i in idx]
    p = [jnp.exp2(s[i] - m_new[i]).astype(BF16) for i in idx]
    pv = [jnp.dot(p[i], vbs[i], preferred_element_type=F32) for i in idx]
    return tuple((m_new[i], alpha[i] * carries[i][1] + pv[i]) for i in idx)


def _flash_init(n):
    return (jnp.full((n, 1), NEG_INF, F32), jnp.zeros((n, LANES), F32))


def _flash_finish(carry):
    acc = carry[1]
    return acc / pltpu.roll(acc, HEAD_DIM, 1)


def _nsa_kernel(q_ref, gl_ref, kc_ref, kv_ref, cos_ref, sin_ref, ccos_ref, csin_ref,
                qn_ref, kn_ref, pet_ref, peb_ref, wt_ref, wb_ref, w2_ref,
                hb_ref, cover_ref, eg_ref, ctab_ref, wtab_ref,
                o_ref, ksvs_ref, kwvw_ref, vs1_ref, vw1_ref, kvc_ref):
    S = kv_ref.shape[0]
    qi = pl.program_id(1)
    n_cmp = S // CMP_STRIDE

    @pl.when(qi == 0)
    def _prep_keys():
        rows = 512

        def body(i, _):
            r0 = pl.multiple_of(i * rows, rows)
            cs = cos_ref[pl.ds(r0, rows), :]
            sn = sin_ref[pl.ds(r0, rows), :]
            lane = _iota((rows, LANES), 1)
            first = lane < HEAD_DIM
            blk_hot = jnp.where(lane - HEAD_DIM == (r0 + _iota((rows, LANES), 0)) // SEL_BLOCK, 1.0, 0.0)
            for col, j, dst, dst1 in ((0, 1, ksvs_ref, vs1_ref), (LANES, 2, kwvw_ref, vw1_ref)):
                slab = kv_ref[pl.ds(r0, rows), col:col + LANES]
                ms = jnp.sum(jnp.where(first, slab * slab, 0.0), axis=-1, keepdims=True) * (1.0 / HEAD_DIM)
                y = slab * lax.rsqrt(ms + RMS_EPS) * kn_ref[j:j + 1, :]
                dst[pl.ds(r0, rows), :] = jnp.where(first, _rope_lanes(y, cs, sn), blk_hot if j == 1 else slab).astype(BF16)
                dst1[pl.ds(r0, rows), :] = jnp.where(first, 1.0, slab).astype(BF16)
            return 0

        lax.fori_loop(0, S // rows, body, 0)

        top = jnp.zeros((n_cmp, LANES), F32)
        bot = jnp.zeros((n_cmp, LANES), F32)
        for i in range(CMP_STRIDE):
            tok = kc_ref[pl.ds(i, n_cmp, stride=CMP_STRIDE), :]
            cols = slice(i * LANES, (i + 1) * LANES)
            top = top + _dot3(tok + pet_ref[:, cols], wt_ref[cols, :])
            bot = bot + _dot3(tok + peb_ref[:, cols], wb_ref[cols, :])
        pre = top + pltpu.roll(bot, n_cmp - 1, 0)
        act = 0.5 * pre * (1.0 + jnp.tanh(float(np.sqrt(2.0 / np.pi)) * (pre + 0.044715 * (pre * pre * pre))))
        out = _dot3(act, w2_ref[...])
        first = _iota(out.shape, 1) < HEAD_DIM
        ms = jnp.sum(jnp.where(first, out * out, 0.0), axis=-1, keepdims=True) * (1.0 / HEAD_DIM)
        y = out * lax.rsqrt(ms + RMS_EPS) * kn_ref[0:1, :]
        kvc_ref[...] = jnp.where(first, _rope_lanes(y, ccos_ref[...], csin_ref[...]), out)

    r0 = pl.multiple_of(qi * TQ, TQ)
    q = q_ref[...]
    ms = _dot_split(q * q, hb_ref[...])
    qn = q * lax.rsqrt(ms + RMS_EPS) * qn_ref[...]
    cs = cos_ref[pl.ds(r0, TQ), :]
    sn = sin_ref[pl.ds(r0, TQ), :]
    qr = _rope_lanes(qn, jnp.concatenate([cs, cs], axis=1), jnp.concatenate([sn, sn], axis=1)) * (HEAD_DIM ** -0.5)
    qst = jnp.concatenate(_pad_heads(qr, as_bf16=False), axis=0)
    n_rows = N_HEADS * TQ

    kvc = kvc_ref[...]
    s = _dot3(qst, kvc, nt=True)
    t_st = r0 + (_iota((n_rows, n_cmp), 0) & (TQ - 1))
    c_end = _iota((n_rows, n_cmp), 1) * CMP_STRIDE + (CMP_BLOCK - 1)
    mask = c_end <= t_st
    sm = jnp.where(mask, s, NEG_INF)
    m = jnp.max(sm, axis=-1, keepdims=True)
    e = jnp.where(mask, jnp.exp(sm - m), 0.0)
    p = e / jnp.maximum(jnp.sum(e, axis=-1, keepdims=True), TINY)
    o_cmp = _dot(p, kvc)

    psum = p[0:TQ] + p[TQ:2 * TQ] + p[2 * TQ:3 * TQ] + p[3 * TQ:4 * TQ]
    ps_hi, ps_lo = _split(psum.T)
    cover_t = cover_ref[...].astype(BF16)
    imp = (jnp.dot(cover_t, ps_hi, preferred_element_type=F32) + jnp.dot(cover_t, ps_lo, preferred_element_type=F32))
    n_blk = S // SEL_BLOCK
    blk = _iota((n_blk, TQ), 0)
    cur = (r0 + _iota((n_blk, TQ), 1)) // SEL_BLOCK
    forced = (blk == 0) | (blk == cur) | (blk == cur - 1)
    imp = jnp.where(blk > cur, -1.0, jnp.where(forced, FORCE_SCORE, imp))
    rank = jnp.zeros((n_blk, TQ), F32)
    for d in range(1, n_blk):
        other = pltpu.roll(imp, n_blk - d, 0)
        rank = rank + jnp.where(blk + d >= n_blk,
                                jnp.where(other >= imp, 1.0, 0.0),
                                jnp.where(other > imp, 1.0, 0.0))
    sel_rows = jnp.where(rank < SEL_TOP_N, 0.0, NEG_INF)
    sel_bias = jnp.concatenate([jnp.zeros((HEAD_DIM, TQ), F32), sel_rows,
                                jnp.zeros((LANES - HEAD_DIM - n_blk, TQ), F32)], axis=0).T.astype(BF16)
    lane = _iota((TQ, LANES), 1)

    qb = (qst * LOG2E).astype(BF16)
    heads = range(N_HEADS)
    q_heads = [qb[h * TQ:(h + 1) * TQ] for h in heads]
    q_heads_sel = [jnp.where(lane < HEAD_DIM, q, sel_bias) for q in q_heads]
    init = tuple(_flash_init(TQ) for _ in heads)

    def sel_tile(j, carry, biases):
        k0 = pl.multiple_of(j * SEL_TK, SEL_TK)
        ka, vb = ksvs_ref[pl.ds(k0, SEL_TK), :], vs1_ref[pl.ds(k0, SEL_TK), :]
        return _flash_step(carry, q_heads_sel, [ka] * N_HEADS, [vb] * N_HEADS, biases)

    ratio = SEL_TK // TQ
    last = qi // ratio
    res = lax.fori_loop(0, last, lambda j, c: sel_tile(j, c, None), init)
    res = sel_tile(last, res, [ctab_ref[qi % ratio]] * N_HEADS)
    o_sel = jnp.concatenate([_flash_finish(c) for c in res], axis=0)

    n_prev = WINDOW // TQ
    span = (n_prev + 1) * TQ
    k0 = pl.multiple_of(jnp.maximum(qi - n_prev, 0) * TQ, TQ)
    ka, vb = kwvw_ref[pl.ds(k0, span), :], vw1_ref[pl.ds(k0, span), :]
    res = _flash_step(init, q_heads, [ka] * N_HEADS, [vb] * N_HEADS, [wtab_ref[jnp.minimum(qi, n_prev)]] * N_HEADS)
    o_win = jnp.concatenate([_flash_finish(c) for c in res], axis=0)

    gate = _dot_split(_sigmoid(gl_ref[...]), eg_ref[...])
    split = lambda o: _merge_heads([o[h * TQ:(h + 1) * TQ] for h in range(N_HEADS)])
    W = GROUP_WIDTH
    o_ref[...] = (gate[:, 0:W] * split(o_cmp) + gate[:, W:2 * W] * split(o_sel) + gate[:, 2 * W:3 * W] * split(o_win))


def _nsa(q, kvc, kv, z_rwkv, cos, sin, ccos, csin, q_norm, k_norm, cmp_pe, cmp_w1, cmp_w2):
    B, S, _ = q.shape
    n_cmp = S // CMP_STRIDE
    n_blk = S // SEL_BLOCK
    hd = HEAD_DIM

    qn = jnp.tile(q_norm, N_HEADS).reshape(1, GROUP_WIDTH)
    kn = jnp.concatenate([k_norm, jnp.ones_like(k_norm)], axis=1)

    def interleave(a, b):
        return jnp.concatenate([a, b], axis=1).reshape(1, -1)

    half = CMP_BLOCK // 2
    pet = interleave(cmp_pe[0, :half], cmp_pe[1, :half])
    peb = interleave(cmp_pe[0, half:], cmp_pe[1, half:])

    def w1_half(lo):
        wk = cmp_w1[0].reshape(CMP_BLOCK, hd, hd)[lo:lo + half]
        wv = cmp_w1[1].reshape(CMP_BLOCK, hd, hd)[lo:lo + half]
        z = jnp.zeros_like(wk)
        top = jnp.concatenate([wk, z], axis=2)
        bot = jnp.concatenate([z, wv], axis=2)
        return jnp.concatenate([top, bot], axis=1).reshape(half * 2 * hd, 2 * hd)

    wt, wb = w1_half(0), w1_half(half)
    w2 = jax.scipy.linalg.block_diag(cmp_w2[0], cmp_w2[1])
    hb = _head_mean_matrix()

    c_end = np.arange(n_cmp) * CMP_STRIDE + CMP_BLOCK - 1
    b_start = np.arange(n_blk) * SEL_BLOCK
    cover = np.maximum(np.minimum(c_end[:, None] + 1, b_start[None, :] + SEL_BLOCK)
                       - np.maximum(c_end[:, None] + 1 - CMP_BLOCK, b_start[None, :]), 0).astype(np.float32) / CMP_BLOCK
    cover[n_cmp - 1] = 0.0
    cover = jnp.asarray(cover.T, F32)
    eg = np.zeros((LANES, 3 * GROUP_WIDTH), np.float32)
    for h in range(N_HEADS):
        for br in range(3):
            eg[GL_LANE + h * 3 + br, br * GROUP_WIDTH + h * hd: br * GROUP_WIDTH + (h + 1) * hd] = 1.0
    eg = jnp.asarray(eg)
    ratio = SEL_TK // TQ
    ctab = np.where(np.arange(SEL_TK)[None, None, :] <= np.arange(ratio)[:, None, None] * TQ + np.arange(TQ)[None, :, None],
                    0.0, NEG_INF).astype(np.float32)
    n_prev = WINDOW // TQ
    span = (n_prev + 1) * TQ
    dist = (np.arange(n_prev + 1)[:, None, None] * TQ + np.arange(TQ)[None, :, None] - np.arange(span)[None, None, :])
    wtab = np.where((dist >= 0) & (dist < WINDOW), 0.0, NEG_INF).astype(np.float32)
    ctab, wtab = jnp.asarray(ctab), jnp.asarray(wtab)

    full = lambda shape: pl.BlockSpec(shape, lambda b, i: (0,) * len(shape))
    return pl.pallas_call(
        _nsa_kernel,
        grid=(B, S // TQ),
        in_specs=[pl.BlockSpec((None, TQ, GROUP_WIDTH), lambda b, i: (b, i, 0)),
                  pl.BlockSpec((None, TQ, LANES), lambda b, i: (b, i, RWKV_PACKED // LANES - 1)),
                  pl.BlockSpec((None, S, LANES), lambda b, i: (b, 0, 0)),
                  pl.BlockSpec((None, S, 2 * LANES), lambda b, i: (b, 0, 0)),
                  pl.BlockSpec((None, S, LANES), lambda b, i: (b, 0, 0)),
                  pl.BlockSpec((None, S, LANES), lambda b, i: (b, 0, 0)),
                  pl.BlockSpec((None, n_cmp, LANES), lambda b, i: (b, 0, 0)),
                  pl.BlockSpec((None, n_cmp, LANES), lambda b, i: (b, 0, 0)),
                  full((1, GROUP_WIDTH)), full((3, LANES)),
                  full((1, CMP_STRIDE * LANES)), full((1, CMP_STRIDE * LANES)),
                  full((CMP_STRIDE * LANES, LANES)), full((CMP_STRIDE * LANES, LANES)), full((LANES, LANES)),
                  full((GROUP_WIDTH, GROUP_WIDTH)), full((n_blk, n_cmp)),
                  full((LANES, 3 * GROUP_WIDTH)), full(ctab.shape), full(wtab.shape)],
        out_specs=pl.BlockSpec((None, TQ, GROUP_WIDTH), lambda b, i: (b, i, 0)),
        out_shape=jax.ShapeDtypeStruct((B, S, GROUP_WIDTH), F32),
        scratch_shapes=[pltpu.VMEM((S, LANES), BF16), pltpu.VMEM((S, LANES), BF16),
                        pltpu.VMEM((S, LANES), BF16), pltpu.VMEM((S, LANES), BF16),
                        pltpu.VMEM((n_cmp, LANES), F32)],
        compiler_params=_params(("parallel", "arbitrary")),
        name="nsa",
    )(q, z_rwkv, kvc, kv, cos, sin, ccos, csin, qn, kn, pet, peb, wt, wb, w2, hb, cover, eg, ctab, wtab)


def _dil_kernel(q_ref, kv_ref, cos_ref, sin_ref, qn_ref, kn_ref, hb_ref, tb_ref, o_ref, kvs_ref, v1s_ref):
    S = kv_ref.shape[0]
    qi = pl.program_id(1)
    W = GROUP_WIDTH

    @pl.when(qi == 0)
    def _prep_keys():
        rows = 512

        def body(i, _):
            r0 = pl.multiple_of(i * rows, rows)
            cs = cos_ref[pl.ds(r0, rows), :]
            sn = sin_ref[pl.ds(r0, rows), :]
            k = kv_ref[pl.ds(r0, rows), 0:W]
            v = kv_ref[pl.ds(r0, rows), W:2 * W]
            ms = _dot_split(k * k, hb_ref[...])
            kn = k * lax.rsqrt(ms + RMS_EPS) * kn_ref[...]
            kr = _rope_lanes(kn, jnp.concatenate([cs, cs], axis=1), jnp.concatenate([sn, sn], axis=1))
            lane = _iota((rows, LANES), 1)
            for h in range(N_HEADS):
                kb = kr[:, (h // 2) * LANES:(h // 2 + 1) * LANES]
                vb = v[:, (h // 2) * LANES:(h // 2 + 1) * LANES]
                if h % 2:
                    kb = pltpu.roll(kb, HEAD_DIM, 1)
                else:
                    vb = pltpu.roll(vb, HEAD_DIM, 1)
                kvs_ref[h, pl.ds(r0, rows), :] = jnp.where(lane < HEAD_DIM, kb, vb).astype(BF16)
                v1s_ref[h, pl.ds(r0, rows), :] = jnp.where(lane < HEAD_DIM, 1.0, vb).astype(BF16)
            return 0

        lax.fori_loop(0, S // rows, body, 0)

    r0 = pl.multiple_of(qi * DIL_TQ, DIL_TQ)
    q = q_ref[...]
    ms = _dot_split(q * q, hb_ref[...])
    qn = q * lax.rsqrt(ms + RMS_EPS) * qn_ref[...]
    cs = cos_ref[pl.ds(r0, DIL_TQ), :]
    sn = sin_ref[pl.ds(r0, DIL_TQ), :]
    qr = _rope_lanes(qn, jnp.concatenate([cs, cs], axis=1), jnp.concatenate([sn, sn], axis=1)) * (HEAD_DIM ** -0.5 * LOG2E)
    qh = _pad_heads(qr)
    ratio = TK // DIL_TQ

    def body(j, carry):
        k0 = pl.multiple_of(j * TK, TK)
        bias = tb_ref[qi - ratio * j]
        return _flash_step(carry, qh, [kvs_ref[h, pl.ds(k0, TK), :] for h in range(N_HEADS)],
                           [v1s_ref[h, pl.ds(k0, TK), :] for h in range(N_HEADS)], [bias] * N_HEADS)

    res = lax.fori_loop(0, qi // ratio + 1, body, tuple(_flash_init(DIL_TQ) for _ in range(N_HEADS)))
    o_ref[...] = _merge_heads([_flash_finish(c) for c in res])


def _dil_bias_table(S):
    n = S // DIL_TQ
    d = (np.arange(n)[:, None, None] * DIL_TQ + np.arange(DIL_TQ)[None, :, None] - np.arange(TK)[None, None, :])
    cnt = np.zeros(d.shape, np.float32)
    for window, dil in DIL_PATTERNS:
        cnt += ((d >= 0) & (d <= window) & (d % dil == 0)).astype(np.float32)
    return jnp.asarray(np.where(cnt > 0, np.log2(np.maximum(cnt, 1.0)), NEG_INF).astype(np.float32))


def _dilated(q, kv, cos, sin, q_norm, k_norm):
    B, S, W = q.shape
    tb = _dil_bias_table(S)
    qn = jnp.tile(q_norm, N_HEADS).reshape(1, W)
    kn = jnp.tile(k_norm, N_HEADS).reshape(1, W)
    hb = _head_mean_matrix()
    full = lambda shape: pl.BlockSpec(shape, lambda b, i: (0,) * len(shape))
    return pl.pallas_call(
        _dil_kernel,
        grid=(B, S // DIL_TQ),
        in_specs=[pl.BlockSpec((None, DIL_TQ, W), lambda b, i: (b, i, 0)),
                  pl.BlockSpec((None, S, 2 * W), lambda b, i: (b, 0, 0)),
                  pl.BlockSpec((None, S, LANES), lambda b, i: (b, 0, 0)),
                  pl.BlockSpec((None, S, LANES), lambda b, i: (b, 0, 0)),
                  full((1, W)), full((1, W)), full((W, W)), full(tb.shape)],
        out_specs=pl.BlockSpec((None, DIL_TQ, W), lambda b, i: (b, i, 0)),
        out_shape=jax.ShapeDtypeStruct((B, S, W), F32),
        scratch_shapes=[pltpu.VMEM((N_HEADS, S, LANES), BF16), pltpu.VMEM((N_HEADS, S, LANES), BF16)],
        compiler_params=_params(("parallel", "arbitrary")),
        name="dilated",
    )(q, kv, cos, sin, qn, kn, hb, tb)


def _softplus(x):
    return jnp.maximum(x, 0.0) + jnp.log(1.0 + jnp.exp(-jnp.abs(x)))


def _rwkv_kernel(first, *refs):
    if first:
        (z_ref, mu_ref, w0_ref, w2_ref, a0_ref, a2_ref, g2_ref, kk_ref, ka_ref, rk_ref, lnw_ref, lnb_ref,
         hb_ref, y_ref, vf_out_ref, st_ref, prev_ref, ro_ref, ub_ref, colb_ref, g_ref, t2_ref) = refs
    else:
        (z_ref, vf_ref, mu_ref, w0_ref, w2_ref, a0_ref, a2_ref, g2_ref, kk_ref, ka_ref, rk_ref, lnw_ref, lnb_ref,
         v0_ref, v1_ref, v2_ref, hb_ref, y_ref, st_ref, prev_ref, ro_ref, ub_ref, colb_ref, g_ref, t2_ref) = refs
    NB, SB = z_ref.shape[0], z_ref.shape[1]
    C = RWKV_CHUNK
    W = GROUP_WIDTH
    n_pairs = N_HEADS // 2
    R = RWKV_GROUP * C

    @pl.when(pl.program_id(1) == 0)
    def _reset():
        st_ref[...] = jnp.zeros_like(st_ref)
        prev_ref[...] = jnp.zeros_like(prev_ref)

    ri = _iota((C, C), 0)
    ci = _iota((C, C), 1)
    tri_incl = jnp.where(ci <= ri, 1.0, 0.0).astype(BF16)
    rr = _iota((2 * LANES, 2 * LANES), 0)
    cc = _iota((2 * LANES, 2 * LANES), 1)
    g_mask = ((((rr >> 6) & 1) == ((cc >> 6) & 1))
              & ((cc & (C - 1)) <= jnp.where(rr < LANES, (rr & (C - 1)) - 1, rr & (C - 1))))
    r2 = _iota((LANES, LANES), 0)
    c2 = _iota((LANES, LANES), 1)
    eye_f = jnp.where(r2 == c2, 1.0, 0.0)
    lane_lo = _iota((C, LANES), 1) < HEAD_DIM
    hb = hb_ref[...].astype(BF16)

    def stack(x):
        return jnp.concatenate([jnp.where(lane_lo, x, 0.0), jnp.where(lane_lo, 0.0, x)], axis=0)

    def phase1(i, _, bb):
        t0 = pl.multiple_of(i * R, R)
        zc = z_ref[bb, pl.ds(t0, R), :]
        inside = jnp.where(i > 0, 1.0, 0.0)
        prev = inside * z_ref[bb, pl.ds(jnp.maximum(t0 - 1, 0), 1), :] + (1.0 - inside) * prev_ref[bb]
        zp = jnp.where(_iota(zc.shape, 0) == 0, prev, pltpu.roll(zc, 1, 0))
        zs = zc + (zp - zc) * mu_ref[...]
        r, k, v = zs[:, 0:W], zs[:, W:2 * W], zs[:, 2 * W:3 * W]
        xwa, xg = zs[:, 768:896], zs[:, 896:1152]
        w_log = -_softplus(-(w0_ref[...] + _dot(jnp.tanh(xwa), w2_ref[...]))) - 0.5
        lw = -jnp.exp(w_log)
        a = _sigmoid(a0_ref[...] + _dot(xwa, a2_ref[...]))
        g = _dot(_sigmoid(xg), g2_ref[...])
        if first:
            vf_out_ref[bb, pl.ds(t0, R), :] = v
        else:
            vf = vf_ref[bb, pl.ds(t0, R), :]
            v = v + (vf - v) * _sigmoid(v0_ref[...] + _dot(_dot(v, v1_ref[...]), v2_ref[...]))
        kkr = k * kk_ref[...]
        nrm = jnp.sqrt(_dot_split(kkr * kkr, hb) * HEAD_DIM)
        kk = kkr / jnp.maximum(nrm, 1e-12)
        k2 = k * (1.0 + (a - 1.0) * ka_ref[...])
        bonus = _dot_split(r * k2 * rk_ref[...], hb) * HEAD_DIM * v
        g_ref[bb, pl.ds(t0, R), :] = g
        t2_ref[bb, pl.ds(t0, R), :] = (lnb_ref[...] + bonus) * g

        items = [(c, p) for c in range(RWKV_GROUP) for p in range(n_pairs)]
        idx = range(len(items))
        sub = lambda x, c, p: x[c * C:(c + 1) * C, p * LANES:(p + 1) * LANES]
        cum = [_cumsum_rows(tri_incl, sub(lw, c, p)) for c, p in items]
        a_st, r_st, v_st, gm, tails = [], [], [], [], []
        for j, (c, p) in enumerate(items):
            lwp = sub(lw, c, p)
            cum_end = cum[j][C - 1:C, :]
            e_pos, e_neg = jnp.exp(cum[j]), jnp.exp(-cum[j])
            e_exc, e_tail = jnp.exp(cum[j] - lwp), jnp.exp(cum_end - cum[j])
            kkp, k2p = sub(kk, c, p), sub(k2, c, p)
            ka_p = kkp * sub(a, c, p)
            a_st.append(stack(-kkp * e_exc).astype(BF16))
            r_st.append(stack(sub(r, c, p) * e_pos))
            v_st.append(stack(sub(v, c, p)).astype(BF16))
            bhat = (ka_p * e_neg).astype(BF16)
            khat = (k2p * e_neg).astype(BF16)
            tails.append(jnp.concatenate([stack(ka_p * e_tail), stack(k2p * e_tail)], axis=0))
            gm.append(jnp.where(g_mask,
                                _dot_nt(jnp.concatenate([a_st[j], r_st[j].astype(BF16)], axis=0),
                                        jnp.concatenate([bhat, bhat, khat, khat], axis=0)), 0.0))
            colb_ref[bb, i * RWKV_GROUP + c, p] = jnp.broadcast_to(jnp.exp(cum_end), (LANES, LANES)).T
        tt = [tails[j].T.astype(BF16) for j in idx]
        lakv = [_dot(gm[j][0:LANES, LANES:2 * LANES], v_st[j]) for j in idx]
        mr = [gm[j][LANES:2 * LANES, :].astype(BF16) for j in idx]
        lp = [gm[j][0:LANES, 0:LANES] for j in idx]
        tinv = [eye_f + lp[j] for j in idx]
        for _ in range(5):
            lpb = [lp[j].astype(BF16) for j in idx]
            lp = [jnp.dot(lpb[j], lpb[j], preferred_element_type=F32) for j in idx]
            tinv = [tinv[j] + _dot(lp[j], tinv[j]) for j in idx]
        wu = [_dot(tinv[j], jnp.concatenate([a_st[j].astype(F32), lakv[j]], axis=1)) for j in idx]
        wm = [wu[j][:, 0:LANES].astype(BF16) for j in idx]
        u = [wu[j][:, LANES:2 * LANES].astype(BF16) for j in idx]
        mwbw = [jnp.dot(jnp.concatenate([mr[j][:, 0:LANES], tt[j][:, 0:LANES]], axis=0), wm[j],
                        preferred_element_type=F32) for j in idx]
        ub = [jnp.dot(jnp.concatenate([mr[j], tt[j]], axis=0), jnp.concatenate([u[j], v_st[j]], axis=0),
                      preferred_element_type=F32) for j in idx]
        for j, (c, p) in enumerate(items):
            n = i * RWKV_GROUP + c
            ro_ref[bb, n, p] = jnp.concatenate([r_st[j] + mwbw[j][0:LANES], mwbw[j][LANES:2 * LANES]], axis=0).astype(BF16)
            ub_ref[bb, n, p] = ub[j]
        return 0

    def phase2(n, _):
        t0 = pl.multiple_of(n * C, C)
        items = [(bb, p) for bb in range(NB) for p in range(n_pairs)]
        st = [st_ref[bb, p] for bb, p in items]
        res = [jnp.dot(ro_ref[bb, n, p], st[j].astype(BF16), preferred_element_type=F32)
               for j, (bb, p) in enumerate(items)]
        ub = [ub_ref[bb, n, p] for bb, p in items]
        for j, (bb, p) in enumerate(items):
            st_ref[bb, p] = colb_ref[bb, n, p] * st[j] + res[j][LANES:2 * LANES] + ub[j][LANES:2 * LANES]
        o_st = [res[j][0:LANES] + ub[j][0:LANES] for j in range(len(items))]
        for bb in range(NB):
            y_ref[bb, pl.ds(t0, C), :] = jnp.concatenate(
                [o[0:C] + o[C:2 * C] for o in o_st[bb * n_pairs:(bb + 1) * n_pairs]], axis=1)
        return 0

    for bb in range(NB):
        lax.fori_loop(0, SB // R, functools.partial(phase1, bb=bb), 0)
    lax.fori_loop(0, SB // C, phase2, 0)

    tiles = [(bb, slice(t * LANES, (t + 1) * LANES)) for bb in range(NB) for t in range(SB // LANES)]
    o = [y_ref[bb, rows, :] for bb, rows in tiles]
    dev = [x - _dot_split(x, hb) for x in o]
    var = [_dot_split(d * d, hb) for d in dev]
    for (bb, rows), d, v in zip(tiles, dev, var):
        y_ref[bb, rows, :] = (d * lax.rsqrt(v + RWKV_GN_EPS) * (lnw_ref[...] * g_ref[bb, rows, :])
                              + t2_ref[bb, rows, :])
    for bb in range(NB):
        prev_ref[bb] = z_ref[bb, SB - 1:SB, :]


def _cumsum_rows(tri_incl, x):
    hi = x.astype(BF16)
    lo = (x - hi.astype(F32)).astype(BF16)
    return jnp.dot(tri_incl, hi, preferred_element_type=F32) + jnp.dot(tri_incl, lo, preferred_element_type=F32)


RWKV_BLOCK = 512
RWKV_BATCH = 4


def _rwkv(z, v_first, v_res, mu, w0, w2, a0, a2, g2, k_k, k_a, r_k, ln_w, ln_b):
    B, S, P = z.shape
    W = GROUP_WIDTH
    SB = RWKV_BLOCK
    ncb = SB // RWKV_CHUNK
    n_pairs = N_HEADS // 2
    first = v_res is None
    row = lambda v: v.reshape(1, W)
    padr = lambda m, n: jnp.pad(m, ((0, n - m.shape[0]), (0, 0)))
    NB = RWKV_BATCH
    seq = lambda w: pl.BlockSpec((NB, SB, w), lambda b, j: (b, j, 0))
    full = lambda shape: pl.BlockSpec(shape, lambda b, j: (0,) * len(shape))
    args = [z]
    specs = [seq(P)]
    if not first:
        args.append(v_first)
        specs.append(seq(W))
    a2_pad = jnp.concatenate([jnp.zeros_like(a2), a2], axis=0)
    args += [_pack_rwkv_row(mu), row(w0), padr(w2, LANES), row(a0), a2_pad, padr(g2, 2 * LANES),
             row(k_k), row(k_a), r_k.reshape(1, W), row(ln_w), row(ln_b)]
    specs += [full((1, P)), full((1, W)), full((LANES, W)), full((1, W)), full((LANES, W)), full((2 * LANES, W)),
              full((1, W)), full((1, W)), full((1, W)), full((1, W)), full((1, W))]
    if not first:
        v0, v1, v2 = v_res
        args += [row(v0), jnp.pad(v1, ((0, 0), (0, LANES - v1.shape[1]))), padr(v2, LANES)]
        specs += [full((1, W)), full((W, LANES)), full((LANES, W))]
    args.append(_head_mean_matrix())
    specs.append(full((W, W)))
    out_shape = [jax.ShapeDtypeStruct((B, S, W), F32)]
    out_specs = [seq(W)]
    if first:
        out_shape.append(jax.ShapeDtypeStruct((B, S, W), F32))
        out_specs.append(seq(W))
    res = pl.pallas_call(
        functools.partial(_rwkv_kernel, first),
        grid=(B // NB, S // SB),
        in_specs=specs,
        out_specs=out_specs,
        out_shape=out_shape,
        scratch_shapes=[pltpu.VMEM((NB, n_pairs, LANES, LANES), F32), pltpu.VMEM((NB, 1, P), F32),
                        pltpu.VMEM((NB, ncb, n_pairs, 2 * LANES, LANES), BF16),
                        pltpu.VMEM((NB, ncb, n_pairs, 2 * LANES, LANES), F32),
                        pltpu.VMEM((NB, ncb, n_pairs, LANES, LANES), F32),
                        pltpu.VMEM((NB, SB, W), F32), pltpu.VMEM((NB, SB, W), F32)],
        compiler_params=_params(("parallel", "arbitrary")),
        name="rwkv7_first" if first else "rwkv7",
    )(*args)
    return (res[0], res[1]) if first else (res[0], v_first)


ROUTER_ROWS = 32


def _expert_lane(e):
    return (e % 4) * 8 + e // 4


def _out_kernel(x_ref, ya_ref, yb_ref, yc_ref, yd_ref, wo_ref, mod_ref, g_ref, rwh_ref, rwl_ref, rb_ref,
                xo_ref, h_ref, gates_ref):
    y = jnp.concatenate([ya_ref[...].astype(BF16), yb_ref[...].astype(BF16),
                         yc_ref[...].astype(BF16), yd_ref[...].astype(BF16)], axis=1)
    mix = jnp.dot(y, wo_ref[...], preferred_element_type=F32)
    x = x_ref[...] + mod_ref[2:3, :] * mix
    xo_ref[...] = x
    ms = jnp.mean(x * x, axis=-1, keepdims=True)
    h = x * lax.rsqrt(ms + RMS_EPS) * g_ref[...] * (1.0 + mod_ref[4:5, :]) + mod_ref[3:4, :]
    h_hi = h.astype(BF16)
    h_ref[...] = h_hi

    h_lo = (h - h_hi.astype(F32)).astype(BF16)
    logits = _dot_nt(rwh_ref[...], h_hi) + _dot_nt(rwh_ref[...], h_lo) + _dot_nt(rwl_ref[...], h_hi)
    aff = _sigmoid(logits)
    s = aff + rb_ref[...]
    n_grp = ROUTER_ROWS // 4
    slot = lambda x, j: x[j * n_grp:(j + 1) * n_grp]
    s_j = [slot(s, j) for j in range(4)]
    a_j = [slot(aff, j) for j in range(4)]
    top2 = []
    for j in range(4):
        rank = jnp.zeros(s_j[j].shape, F32)
        for m in range(4):
            if m != j:
                beats = (s_j[m] >= s_j[j]) if m < j else (s_j[m] > s_j[j])
                rank = rank + jnp.where(beats, 1.0, 0.0)
        top2.append(rank < 2.0)
    gs = sum(jnp.where(top2[j], s_j[j], 0.0) for j in range(4))
    row = _iota(gs.shape, 0)
    lost = jnp.zeros(gs.shape, F32)
    for d in range(1, n_grp):
        other = pltpu.roll(gs, d, 0)
        lost = lost + jnp.where(row >= d, jnp.where(other >= gs, 1.0, 0.0), jnp.where(other > gs, 1.0, 0.0))
    chosen = lost < 0.5
    sel = [top2[j] & chosen for j in range(4)]
    den = sum(jnp.where(sel[j], a_j[j], 0.0) for j in range(4))
    den = jnp.where(den > 0.0, den, 1.0)
    gates_t = jnp.concatenate([jnp.where(sel[j], a_j[j] / den, 0.0) for j in range(4)]
                              + [jnp.zeros((LANES - ROUTER_ROWS, s.shape[1]), F32)], axis=0)
    gates_ref[...] = gates_t.T


def _out_proj(x, ys, w_out, mod_l, g, router_w, router_b):
    B, S, D = x.shape
    tm = 1024
    W = GROUP_WIDTH
    order = np.full((ROUTER_ROWS,), -1)
    order[[_expert_lane(e) for e in range(N_EXPERTS)]] = np.arange(N_EXPERTS)
    real = jnp.asarray(order >= 0)
    rw = jnp.where(real[:, None], router_w.T[np.maximum(order, 0)], 0.0)
    rw_hi = rw.astype(BF16)
    rw_lo = (rw - rw_hi.astype(F32)).astype(BF16)
    rb = jnp.where(real, router_b[np.maximum(order, 0)], NEG_INF).reshape(ROUTER_ROWS, 1)
    tile = lambda w: pl.BlockSpec((None, tm, w), lambda b, i: (b, i, 0))
    full = lambda shape: pl.BlockSpec(shape, lambda b, i: (0,) * len(shape))
    return pl.pallas_call(
        _out_kernel,
        grid=(B, S // tm),
        in_specs=[tile(D), tile(W), tile(W), tile(W), tile(W), full((D, D)),
                  pl.BlockSpec((None, 6, D), lambda b, i: (b, 0, 0)), full((1, D)),
                  full((ROUTER_ROWS, D)), full((ROUTER_ROWS, D)), full((ROUTER_ROWS, 1))],
        out_specs=[tile(D), tile(D), tile(LANES)],
        out_shape=[jax.ShapeDtypeStruct((B, S, D), F32), jax.ShapeDtypeStruct((B, S, D), BF16),
                   jax.ShapeDtypeStruct((B, S, LANES), F32)],
        compiler_params=_params(("parallel", "parallel")),
        name="out_proj_router",
    )(x, *ys, w_out.astype(BF16), mod_l, g.reshape(1, D), rw_hi, rw_lo, rb)


MOE_EXPERTS_PER_STEP = 4


def _moe_kernel(x_ref, h_ref, gates_ref, mod_ref, wg_ref, wu_ref, wd_ref, o_ref):
    step = pl.program_id(2)
    n = MOE_EXPERTS_PER_STEP

    @pl.when(step == 0)
    def _init():
        o_ref[...] = x_ref[...]

    h = h_ref[...]
    gates = gates_ref[...]
    lane = _iota(gates.shape, 1)
    hgs = [jnp.dot(h, wg_ref[k].astype(BF16), preferred_element_type=F32) for k in range(n)]
    hus = [jnp.dot(h, wu_ref[k].astype(BF16), preferred_element_type=F32) for k in range(n)]
    acts = []
    for k in range(n):
        ge = jnp.sum(jnp.where(lane == _expert_lane(step * n + k), gates, 0.0), axis=-1, keepdims=True)
        acts.append((hgs[k] * _sigmoid(hgs[k]) * hus[k] * ge).astype(BF16))
    act = jnp.concatenate(acts, axis=1)
    wd = wd_ref[...].astype(BF16).reshape(n * D_EXPERT, wd_ref.shape[2])
    o_ref[...] += mod_ref[5:6, :] * jnp.dot(act, wd, preferred_element_type=F32)


def _moe(x, h, gates, mod_l, w_gate, w_up, w_down, layer):
    B, S, D = x.shape
    E = w_gate.shape[1]
    n = MOE_EXPERTS_PER_STEP
    tm = 1024
    tile = lambda w: pl.BlockSpec((None, tm, w), lambda b, i, e: (b, i, 0))
    return pl.pallas_call(
        _moe_kernel,
        grid=(B, S // tm, E // n),
        in_specs=[tile(D), tile(D), tile(LANES),
                  pl.BlockSpec((None, 6, D), lambda b, i, e: (b, 0, 0)),
                  pl.BlockSpec((None, n, D, D_EXPERT), lambda b, i, e: (layer, e, 0, 0)),
                  pl.BlockSpec((None, n, D, D_EXPERT), lambda b, i, e: (layer, e, 0, 0)),
                  pl.BlockSpec((None, n, D_EXPERT, D), lambda b, i, e: (layer, e, 0, 0))],
        out_specs=tile(D),
        out_shape=jax.ShapeDtypeStruct((B, S, D), F32),
        compiler_params=_params(("parallel", "parallel", "arbitrary")),
        name="moe",
    )(x, h, gates, mod_l, w_gate, w_up, w_down)


def kernel(x, c, positions, ada_w, ada_b, norm_mix_g, norm_ffn_g, w_in, w_out, nsa_q_norm, nsa_k_norm, nsa_cmp_pe, nsa_cmp_w1, nsa_cmp_w2, pool_w, pool_scale, rwkv_mu, rwkv_w0, rwkv_w2, rwkv_a0, rwkv_a2, rwkv_g2, rwkv_k_k, rwkv_k_a, rwkv_r_k, rwkv_ln_w, rwkv_ln_b, rwkv_v0, rwkv_v1, rwkv_v2, dil_q_norm, dil_k_norm, router_w, router_b, moe_w_gate, moe_w_up, moe_w_down):
    depth = ada_w.shape[0]
    mod = _modulation(c, ada_w, ada_b)
    cos, sin, ccos, csin = _rope_tables(positions)
    v_first = None
    for l in range(depth):
        q_a, kvc_a, kv_a, u_pool, z_rwkv, q_d, kv_d = _in_proj(x, mod[l], norm_mix_g[l], w_in, l)
        y_nsa = _nsa(q_a, kvc_a, kv_a, z_rwkv, cos, sin, ccos, csin, nsa_q_norm[l], nsa_k_norm[l],
                     nsa_cmp_pe[l], nsa_cmp_w1[l], nsa_cmp_w2[l])
        y_pool = _pool(u_pool, pool_w[l], pool_scale[l])
        v_res = None if l == 0 else (rwkv_v0[l - 1], rwkv_v1[l - 1], rwkv_v2[l - 1])
        y_rwkv, v_first = _rwkv(z_rwkv, v_first, v_res, rwkv_mu[l], rwkv_w0[l], rwkv_w2[l], rwkv_a0[l], rwkv_a2[l],
                                rwkv_g2[l], rwkv_k_k[l], rwkv_k_a[l], rwkv_r_k[l], rwkv_ln_w[l], rwkv_ln_b[l])
        y_dil = _dilated(q_d, kv_d, cos, sin, dil_q_norm[l], dil_k_norm[l])
        x_mid, h2, gates = _out_proj(x, (y_nsa, y_pool, y_rwkv, y_dil), w_out[l], mod[l], norm_ffn_g[l],
                                     router_w, router_b)
        x = _moe(x_mid, h2, gates, mod[l], moe_w_gate, moe_w_up, moe_w_down, l)
    return x
```

```python
import functools

import numpy as np
import jax
import jax.numpy as jnp
from jax import lax
from jax.experimental import pallas as pl
from jax.experimental.pallas import tpu as pltpu

F32 = jnp.float32
BF16 = jnp.bfloat16

D_MODEL = 1024
HEAD_DIM = 64
GROUP_WIDTH = 256
N_HEADS = 4
RMS_EPS = 1e-6
ROPE_THETA = 500000.0
ROPE_DIM = 16
CMP_BLOCK = 32
CMP_STRIDE = 16
SEL_BLOCK = 64
SEL_TOP_N = 16
WINDOW = 512
FORCE_SCORE = 1e4
POOL_WINDOWS = (2, 4, 8, 16)
RWKV_GN_EPS = 64e-5
DIL_PATTERNS = ((128, 1), (512, 4), (2048, 16))
N_EXPERTS = 16
D_EXPERT = 256
NEG_INF = -1e30
LOG2E = 1.4426950408889634
TINY = 1e-30

LANES = 128
TQ = 512
DIL_TQ = 512
TK = 512
SEL_TK = 512
RWKV_CHUNK = 64
RWKV_GROUP = 8
VMEM_LIMIT = 56 * 1024 * 1024

_IN_COLS = dict(q=(0, 256), kvc=(256, 384), kv=(384, 640), pool=(640, 896),
                rwkv=(896, 2048), dq=(2048, 2304), dkv=(2304, 2816))
IN_PACKED = 2816
RWKV_PACKED = 1152
GL_LANE = 32


def _dot(a, b):
    return jnp.dot(a.astype(BF16), b.astype(BF16), preferred_element_type=F32)


def _dot_nt(a, b):
    return lax.dot_general(a, b, (((1,), (1,)), ((), ())), preferred_element_type=F32)


def _split(x):
    hi = x.astype(BF16)
    return hi, (x - hi.astype(F32)).astype(BF16)


def _dot_split(a, b):
    hi, lo = _split(a)
    b = b.astype(BF16)
    return jnp.dot(hi, b, preferred_element_type=F32) + jnp.dot(lo, b, preferred_element_type=F32)


def _dot3(a, b, nt=False):
    f = _dot_nt if nt else functools.partial(jnp.dot, preferred_element_type=F32)
    a_hi, a_lo = _split(a)
    b_hi, b_lo = _split(b)
    return f(a_hi, b_hi) + f(a_lo, b_hi) + f(a_hi, b_lo)


def _sigmoid(x):
    return 1.0 / (1.0 + jnp.exp(-x))


def _iota(shape, dim):
    return lax.broadcasted_iota(jnp.int32, shape, dim)


def _rope_lanes(y, cos, sin):
    n = y.shape[-1]
    lane = _iota(y.shape, y.ndim - 1)
    partner = jnp.where((lane & 15) < 8, pltpu.roll(y, n - 8, y.ndim - 1), pltpu.roll(y, 8, y.ndim - 1))
    return y * cos + partner * sin


def _params(sem, vmem=VMEM_LIMIT):
    return pltpu.CompilerParams(dimension_semantics=sem, vmem_limit_bytes=vmem)


def _mod_kernel(c_ref, w_ref, b_ref, o_ref):
    c = c_ref[...]
    o_ref[...] = _dot3(c * _sigmoid(c), w_ref[...]) + b_ref[...]


def _modulation(c, ada_w, ada_b):
    L, D, E = ada_w.shape
    B = c.shape[0]
    tn = 1024
    out = pl.pallas_call(
        _mod_kernel,
        grid=(L, E // tn),
        in_specs=[pl.BlockSpec((B, D), lambda l, j: (0, 0)),
                  pl.BlockSpec((None, D, tn), lambda l, j: (l, 0, j)),
                  pl.BlockSpec((None, 1, tn), lambda l, j: (l, 0, j))],
        out_specs=pl.BlockSpec((None, B, tn), lambda l, j: (l, 0, j)),
        out_shape=jax.ShapeDtypeStruct((L, B, E), F32),
        compiler_params=_params(("parallel", "parallel")),
        name="modulation",
    )(c, ada_w, ada_b.reshape(L, 1, E))
    return out.reshape(L, B, 6, D)


ROPE_ROWS = 64


def _rope_kernel(pos_ref, cpos_ref, freq_ref, ecos_ref, esin_ref, one_ref, cos_ref, sin_ref, ccos_ref, csin_ref):
    def tables(pos_row):
        ang = freq_ref[...] * pos_row.astype(F32)
        parts = []
        for v in (jnp.cos(ang), jnp.sin(ang)):
            hi = v.astype(BF16)
            r1 = v - hi.astype(F32)
            mid = r1.astype(BF16)
            parts += [hi, mid, (r1 - mid.astype(F32)).astype(BF16)]
        n = ang.shape[1]
        stacked = jnp.concatenate(parts + [jnp.zeros((ROPE_ROWS - 6 * ang.shape[0], n), BF16)], axis=0)
        spread = lambda e: lax.dot_general(stacked, e, (((0,), (0,)), ((), ())), preferred_element_type=F32)
        return spread(ecos_ref[...]) + one_ref[...], spread(esin_ref[...])

    cos_ref[...], sin_ref[...] = tables(pos_ref[...])
    ccos_ref[...], csin_ref[...] = tables(cpos_ref[...])


def _rope_tables(positions):
    B, S = positions.shape
    half = ROPE_DIM // 2
    inv_freq = ROPE_THETA ** (-2.0 * jnp.arange(half, dtype=F32) / ROPE_DIM)
    lane = np.arange(LANES)
    rot = (lane % HEAD_DIM) < ROPE_DIM
    pick = (lane[None, :] % half == np.arange(half)[:, None]) & rot[None, :]
    sgn = np.where(lane % HEAD_DIM < half, -1.0, 1.0)
    zeros = np.zeros((3 * half, LANES))
    ecos = np.concatenate([np.tile(pick, (3, 1)), zeros, np.zeros((ROPE_ROWS - 6 * half, LANES))], axis=0)
    esin = np.concatenate([zeros, np.tile(pick * sgn[None, :], (3, 1)), np.zeros((ROPE_ROWS - 6 * half, LANES))], axis=0)
    one = jnp.asarray(np.where(rot, 0.0, 1.0), F32).reshape(1, LANES)
    n_cmp = S // CMP_STRIDE
    cpos = positions[:, CMP_BLOCK - 1::CMP_STRIDE]
    cpos = jnp.pad(cpos, ((0, 0), (0, n_cmp - cpos.shape[1])))
    tab = jax.ShapeDtypeStruct((B, S, LANES), F32)
    ctab = jax.ShapeDtypeStruct((B, n_cmp, LANES), F32)
    full = lambda shape: pl.BlockSpec(shape, lambda b: (0,) * len(shape))
    return pl.pallas_call(
        _rope_kernel,
        grid=(B,),
        in_specs=[pl.BlockSpec((None, 1, S), lambda b: (b, 0, 0)),
                  pl.BlockSpec((None, 1, n_cmp), lambda b: (b, 0, 0)),
                  full((half, 1)), full((ROPE_ROWS, LANES)), full((ROPE_ROWS, LANES)), full((1, LANES))],
        out_specs=[pl.BlockSpec((None, S, LANES), lambda b: (b, 0, 0)),
                   pl.BlockSpec((None, S, LANES), lambda b: (b, 0, 0)),
                   pl.BlockSpec((None, n_cmp, LANES), lambda b: (b, 0, 0)),
                   pl.BlockSpec((None, n_cmp, LANES), lambda b: (b, 0, 0))],
        out_shape=[tab, tab, ctab, ctab],
        compiler_params=_params(("parallel",)),
        name="rope_tables",
    )(positions.reshape(B, 1, S), cpos.reshape(B, 1, n_cmp), inv_freq.reshape(half, 1),
      jnp.asarray(ecos, BF16), jnp.asarray(esin, BF16), one)


_W_IN_MOVES = ((0, 640, 0), (640, 652, 1792 + 128 + GL_LANE), (652, 908, 640), (908, 1676, 896), (1676, 1740, 1664),
               (1740, 1804, 1728), (1804, 1964, 1792), (1964, 2732, 2048))


def _in_kernel(x_ref, mod_ref, g_ref, w_ref, *refs):
    out_refs, wp_ref = refs[:-1], refs[-1]

    @pl.when((pl.program_id(0) == 0) & (pl.program_id(1) == 0))
    def _pack_weights():
        rows = 128

        def body(r, _):
            r0 = pl.multiple_of(r * rows, rows)
            w = w_ref[pl.ds(r0, rows), :]
            wp_ref[pl.ds(r0, rows), :] = jnp.zeros((rows, IN_PACKED), BF16)
            for lo, hi, dst in _W_IN_MOVES:
                wp_ref[pl.ds(r0, rows), dst:dst + hi - lo] = w[:, lo:hi]
            return 0

        lax.fori_loop(0, w_ref.shape[0] // rows, body, 0)

    x = x_ref[...]
    ms = jnp.mean(x * x, axis=-1, keepdims=True)
    y = x * lax.rsqrt(ms + RMS_EPS) * g_ref[...]
    h = y * (1.0 + mod_ref[1:2, :]) + mod_ref[0:1, :]
    p = jnp.dot(h.astype(BF16), wp_ref[...], preferred_element_type=F32)
    for ref, (lo, hi) in zip(out_refs, _IN_COLS.values()):
        ref[...] = p[:, lo:hi]


def _in_proj(x, mod_l, g, w_in, layer):
    B, S, D = x.shape
    tm = 1024
    widths = [hi - lo for lo, hi in _IN_COLS.values()]
    return pl.pallas_call(
        _in_kernel,
        grid=(B, S // tm),
        in_specs=[pl.BlockSpec((None, tm, D), lambda b, i: (b, i, 0)),
                  pl.BlockSpec((None, 6, D), lambda b, i: (b, 0, 0)),
                  pl.BlockSpec((1, D), lambda b, i: (0, 0)),
                  pl.BlockSpec((None,) + w_in.shape[1:], lambda b, i: (layer, 0, 0), pipeline_mode=pl.Buffered(1))],
        out_specs=[pl.BlockSpec((None, tm, w), lambda b, i: (b, i, 0)) for w in widths],
        out_shape=[jax.ShapeDtypeStruct((B, S, w), F32) for w in widths],
        scratch_shapes=[pltpu.VMEM((D, IN_PACKED), BF16)],
        compiler_params=_params(("arbitrary", "arbitrary")),
        name="in_proj",
    )(x, mod_l, g.reshape(1, D), w_in.astype(BF16))


def _pack_rwkv_row(v):
    z = lambda n: jnp.zeros((n,), v.dtype)
    return jnp.concatenate([v[0:1056], z(RWKV_PACKED - 1056)]).reshape(1, RWKV_PACKED)


def _pool_kernel(u_ref, w_ref, scale_ref, o_ref):
    u = u_ref[...]
    row = _iota(u.shape, 0)
    lane = _iota(u.shape, 1)

    def lag(x, s):
        return jnp.where(row >= s, pltpu.roll(x, s, 0), 0.0)

    s2 = u + lag(u, 1)
    s4 = s2 + lag(s2, 2)
    s8 = s4 + lag(s4, 4)
    s16 = s8 + lag(s8, 8)
    g = lane // (GROUP_WIDTH // len(POOL_WINDOWS))
    tot = jnp.where(g == 0, s2, jnp.where(g == 1, s4, jnp.where(g == 2, s8, s16)))
    win = jnp.where(g == 0, 2.0, jnp.where(g == 1, 4.0, jnp.where(g == 2, 8.0, 16.0)))
    cnt = jnp.minimum((row + 1).astype(F32), win)
    o_ref[...] = _dot(tot / cnt - u, w_ref[...]) * scale_ref[...]


def _pool(u, pool_w, pool_scale):
    B, S, W = u.shape
    wbd = jax.scipy.linalg.block_diag(*[pool_w[i] for i in range(pool_w.shape[0])]).astype(BF16)
    return pl.pallas_call(
        _pool_kernel,
        grid=(B,),
        in_specs=[pl.BlockSpec((None, S, W), lambda b: (b, 0, 0)),
                  pl.BlockSpec((W, W), lambda b: (0, 0)),
                  pl.BlockSpec((1, W), lambda b: (0, 0))],
        out_specs=pl.BlockSpec((None, S, W), lambda b: (b, 0, 0)),
        out_shape=jax.ShapeDtypeStruct((B, S, W), F32),
        compiler_params=_params(("parallel",)),
        name="pool_mixer",
    )(u, wbd, pool_scale.reshape(1, W))


def _head_mean_matrix():
    blk = np.kron(np.eye(N_HEADS), np.ones((HEAD_DIM, HEAD_DIM))) / HEAD_DIM
    return jnp.asarray(blk, F32)


def _pad_heads(x256, as_bf16=True):
    n = x256.shape[0]
    lane = _iota((n, LANES), 1)
    out = []
    for h in range(N_HEADS):
        blk = x256[:, (h // 2) * LANES:(h // 2 + 1) * LANES]
        if h % 2:
            blk = pltpu.roll(blk, HEAD_DIM, 1)
        blk = jnp.where(lane < HEAD_DIM, blk, 0.0)
        out.append(blk.astype(BF16) if as_bf16 else blk)
    return out


def _merge_heads(slabs):
    n = slabs[0].shape[0]
    lane = _iota((n, LANES), 1)
    cols = []
    for p in range(N_HEADS // 2):
        even = pltpu.roll(slabs[2 * p], HEAD_DIM, 1)
        cols.append(jnp.where(lane < HEAD_DIM, even, slabs[2 * p + 1]))
    return jnp.concatenate(cols, axis=1)


def _flash_step(carries, qs, kas, vbs, biases):
    idx = range(len(qs))
    s = [_dot_nt(qs[i], kas[i]) for i in idx]
    if biases is not None:
        s = [s[i] + biases[i] for i in idx]
    m_new = [jnp.maximum(carries[i][0], jnp.max(s[i], axis=-1, keepdims=True)) for i in idx]
    alpha = [jnp.exp2(carries[i][0] - m_new[i]) for i in idx]
    p = [jnp.exp2(s[i] - m_new[i]).astype(BF16) for i in idx]
    pv = [jnp.dot(p[i], vbs[i], preferred_element_type=F32) for i in idx]
    return tuple((m_new[i], alpha[i] * carries[i][1] + pv[i]) for i in idx)


def _flash_init(n):
    return (jnp.full((n, 1), NEG_INF, F32), jnp.zeros((n, LANES), F32))


def _flash_finish(carry):
    acc = carry[1]
    return acc / pltpu.roll(acc, HEAD_DIM, 1)


def _nsa_kernel(q_ref, gl_ref, kc_ref, kv_ref, cos_ref, sin_ref, ccos_ref, csin_ref,
                qn_ref, kn_ref, pet_ref, peb_ref, wt_ref, wb_ref, w2_ref,
                hb_ref, cover_ref, eg_ref, ctab_ref, wtab_ref,
                o_ref, ksvs_ref, kwvw_ref, vs1_ref, vw1_ref, kvc_ref):
    S = kv_ref.shape[0]
    qi = pl.program_id(1)
    n_cmp = S // CMP_STRIDE

    @pl.when(qi == 0)
    def _prep_keys():
        rows = 512

        def body(i, _):
            r0 = pl.multiple_of(i * rows, rows)
            cs = cos_ref[pl.ds(r0, rows), :]
            sn = sin_ref[pl.ds(r0, rows), :]
            lane = _iota((rows, LANES), 1)
            first = lane < HEAD_DIM
            blk_hot = jnp.where(lane - HEAD_DIM == (r0 + _iota((rows, LANES), 0)) // SEL_BLOCK, 1.0, 0.0)
            for col, j, dst, dst1 in ((0, 1, ksvs_ref, vs1_ref), (LANES, 2, kwvw_ref, vw1_ref)):
                slab = kv_ref[pl.ds(r0, rows), col:col + LANES]
                ms = jnp.sum(jnp.where(first, slab * slab, 0.0), axis=-1, keepdims=True) * (1.0 / HEAD_DIM)
                y = slab * lax.rsqrt(ms + RMS_EPS) * kn_ref[j:j + 1, :]
                dst[pl.ds(r0, rows), :] = jnp.where(first, _rope_lanes(y, cs, sn), blk_hot if j == 1 else slab).astype(BF16)
                dst1[pl.ds(r0, rows), :] = jnp.where(first, 1.0, slab).astype(BF16)
            return 0

        lax.fori_loop(0, S // rows, body, 0)

        top = jnp.zeros((n_cmp, LANES), F32)
        bot = jnp.zeros((n_cmp, LANES), F32)
        for i in range(CMP_STRIDE):
            tok = kc_ref[pl.ds(i, n_cmp, stride=CMP_STRIDE), :]
            cols = slice(i * LANES, (i + 1) * LANES)
            top = top + _dot3(tok + pet_ref[:, cols], wt_ref[cols, :])
            bot = bot + _dot3(tok + peb_ref[:, cols], wb_ref[cols, :])
        pre = top + pltpu.roll(bot, n_cmp - 1, 0)
        act = 0.5 * pre * (1.0 + jnp.tanh(float(np.sqrt(2.0 / np.pi)) * (pre + 0.044715 * (pre * pre * pre))))
        out = _dot3(act, w2_ref[...])
        first = _iota(out.shape, 1) < HEAD_DIM
        ms = jnp.sum(jnp.where(first, out * out, 0.0), axis=-1, keepdims=True) * (1.0 / HEAD_DIM)
        y = out * lax.rsqrt(ms + RMS_EPS) * kn_ref[0:1, :]
        kvc_ref[...] = jnp.where(first, _rope_lanes(y, ccos_ref[...], csin_ref[...]), out)

    r0 = pl.multiple_of(qi * TQ, TQ)
    q = q_ref[...]
    ms = _dot_split(q * q, hb_ref[...])
    qn = q * lax.rsqrt(ms + RMS_EPS) * qn_ref[...]
    cs = cos_ref[pl.ds(r0, TQ), :]
    sn = sin_ref[pl.ds(r0, TQ), :]
    qr = _rope_lanes(qn, jnp.concatenate([cs, cs], axis=1), jnp.concatenate([sn, sn], axis=1)) * (HEAD_DIM ** -0.5)
    qst = jnp.concatenate(_pad_heads(qr, as_bf16=False), axis=0)
    n_rows = N_HEADS * TQ

    kvc = kvc_ref[...]
    s = _dot3(qst, kvc, nt=True)
    t_st = r0 + (_iota((n_rows, n_cmp), 0) & (TQ - 1))
    c_end = _iota((n_rows, n_cmp), 1) * CMP_STRIDE + (CMP_BLOCK - 1)
    mask = c_end <= t_st
    sm = jnp.where(mask, s, NEG_INF)
    m = jnp.max(sm, axis=-1, keepdims=True)
    e = jnp.where(mask, jnp.exp(sm - m), 0.0)
    p = e / jnp.maximum(jnp.sum(e, axis=-1, keepdims=True), TINY)
    o_cmp = _dot(p, kvc)

    psum = p[0:TQ] + p[TQ:2 * TQ] + p[2 * TQ:3 * TQ] + p[3 * TQ:4 * TQ]
    ps_hi, ps_lo = _split(psum.T)
    cover_t = cover_ref[...].astype(BF16)
    imp = (jnp.dot(cover_t, ps_hi, preferred_element_type=F32) + jnp.dot(cover_t, ps_lo, preferred_element_type=F32))
    n_blk = S // SEL_BLOCK
    blk = _iota((n_blk, TQ), 0)
    cur = (r0 + _iota((n_blk, TQ), 1)) // SEL_BLOCK
    forced = (blk == 0) | (blk == cur) | (blk == cur - 1)
    imp = jnp.where(blk > cur, -1.0, jnp.where(forced, FORCE_SCORE, imp))
    rank = jnp.zeros((n_blk, TQ), F32)
    for d in range(1, n_blk):
        other = pltpu.roll(imp, n_blk - d, 0)
        rank = rank + jnp.where(blk + d >= n_blk,
                                jnp.where(other >= imp, 1.0, 0.0),
                                jnp.where(other > imp, 1.0, 0.0))
    sel_rows = jnp.where(rank < SEL_TOP_N, 0.0, NEG_INF)
    sel_bias = jnp.concatenate([jnp.zeros((HEAD_DIM, TQ), F32), sel_rows,
                                jnp.zeros((LANES - HEAD_DIM - n_blk, TQ), F32)], axis=0).T.astype(BF16)
    lane = _iota((TQ, LANES), 1)

    qb = (qst * LOG2E).astype(BF16)
    heads = range(N_HEADS)
    q_heads = [qb[h * TQ:(h + 1) * TQ] for h in heads]
    q_heads_sel = [jnp.where(lane < HEAD_DIM, q, sel_bias) for q in q_heads]
    init = tuple(_flash_init(TQ) for _ in heads)

    def sel_tile(j, carry, biases):
        k0 = pl.multiple_of(j * SEL_TK, SEL_TK)
        ka, vb = ksvs_ref[pl.ds(k0, SEL_TK), :], vs1_ref[pl.ds(k0, SEL_TK), :]
        return _flash_step(carry, q_heads_sel, [ka] * N_HEADS, [vb] * N_HEADS, biases)

    ratio = SEL_TK // TQ
    last = qi // ratio
    res = lax.fori_loop(0, last, lambda j, c: sel_tile(j, c, None), init)
    res = sel_tile(last, res, [ctab_ref[qi % ratio]] * N_HEADS)
    o_sel = jnp.concatenate([_flash_finish(c) for c in res], axis=0)

    n_prev = WINDOW // TQ
    span = (n_prev + 1) * TQ
    k0 = pl.multiple_of(jnp.maximum(qi - n_prev, 0) * TQ, TQ)
    ka, vb = kwvw_ref[pl.ds(k0, span), :], vw1_ref[pl.ds(k0, span), :]
    res = _flash_step(init, q_heads, [ka] * N_HEADS, [vb] * N_HEADS, [wtab_ref[jnp.minimum(qi, n_prev)]] * N_HEADS)
    o_win = jnp.concatenate([_flash_finish(c) for c in res], axis=0)

    gate = _dot_split(_sigmoid(gl_ref[...]), eg_ref[...])
    split = lambda o: _merge_heads([o[h * TQ:(h + 1) * TQ] for h in range(N_HEADS)])
    W = GROUP_WIDTH
    o_ref[...] = (gate[:, 0:W] * split(o_cmp) + gate[:, W:2 * W] * split(o_sel) + gate[:, 2 * W:3 * W] * split(o_win))


def _nsa(q, kvc, kv, z_rwkv, cos, sin, ccos, csin, q_norm, k_norm, cmp_pe, cmp_w1, cmp_w2):
    B, S, _ = q.shape
    n_cmp = S // CMP_STRIDE
    n_blk = S // SEL_BLOCK
    hd = HEAD_DIM

    qn = jnp.tile(q_norm, N_HEADS).reshape(1, GROUP_WIDTH)
    kn = jnp.concatenate([k_norm, jnp.ones_like(k_norm)], axis=1)

    def interleave(a, b):
        return jnp.concatenate([a, b], axis=1).reshape(1, -1)

    half = CMP_BLOCK // 2
    pet = interleave(cmp_pe[0, :half], cmp_pe[1, :half])
    peb = interleave(cmp_pe[0, half:], cmp_pe[1, half:])

    def w1_half(lo):
        wk = cmp_w1[0].reshape(CMP_BLOCK, hd, hd)[lo:lo + half]
        wv = cmp_w1[1].reshape(CMP_BLOCK, hd, hd)[lo:lo + half]
        z = jnp.zeros_like(wk)
        top = jnp.concatenate([wk, z], axis=2)
        bot = jnp.concatenate([z, wv], axis=2)
        return jnp.concatenate([top, bot], axis=1).reshape(half * 2 * hd, 2 * hd)

    wt, wb = w1_half(0), w1_half(half)
    w2 = jax.scipy.linalg.block_diag(cmp_w2[0], cmp_w2[1])
    hb = _head_mean_matrix()

    c_end = np.arange(n_cmp) * CMP_STRIDE + CMP_BLOCK - 1
    b_start = np.arange(n_blk) * SEL_BLOCK
    cover = np.maximum(np.minimum(c_end[:, None] + 1, b_start[None, :] + SEL_BLOCK)
                       - np.maximum(c_end[:, None] + 1 - CMP_BLOCK, b_start[None, :]), 0).astype(np.float32) / CMP_BLOCK
    cover[n_cmp - 1] = 0.0
    cover = jnp.asarray(cover.T, F32)
    eg = np.zeros((LANES, 3 * GROUP_WIDTH), np.float32)
    for h in range(N_HEADS):
        for br in range(3):
            eg[GL_LANE + h * 3 + br, br * GROUP_WIDTH + h * hd: br * GROUP_WIDTH + (h + 1) * hd] = 1.0
    eg = jnp.asarray(eg)
    ratio = SEL_TK // TQ
    ctab = np.where(np.arange(SEL_TK)[None, None, :] <= np.arange(ratio)[:, None, None] * TQ + np.arange(TQ)[None, :, None],
                    0.0, NEG_INF).astype(np.float32)
    n_prev = WINDOW // TQ
    span = (n_prev + 1) * TQ
    dist = (np.arange(n_prev + 1)[:, None, None] * TQ + np.arange(TQ)[None, :, None] - np.arange(span)[None, None, :])
    wtab = np.where((dist >= 0) & (dist < WINDOW), 0.0, NEG_INF).astype(np.float32)
    ctab, wtab = jnp.asarray(ctab), jnp.asarray(wtab)

    full = lambda shape: pl.BlockSpec(shape, lambda b, i: (0,) * len(shape))
    return pl.pallas_call(
        _nsa_kernel,
        grid=(B, S // TQ),
        in_specs=[pl.BlockSpec((None, TQ, GROUP_WIDTH), lambda b, i: (b, i, 0)),
                  pl.BlockSpec((None, TQ, LANES), lambda b, i: (b, i, RWKV_PACKED // LANES - 1)),
                  pl.BlockSpec((None, S, LANES), lambda b, i: (b, 0, 0)),
                  pl.BlockSpec((None, S, 2 * LANES), lambda b, i: (b, 0, 0)),
                  pl.BlockSpec((None, S, LANES), lambda b, i: (b, 0, 0)),
                  pl.BlockSpec((None, S, LANES), lambda b, i: (b, 0, 0)),
                  pl.BlockSpec((None, n_cmp, LANES), lambda b, i: (b, 0, 0)),
                  pl.BlockSpec((None, n_cmp, LANES), lambda b, i: (b, 0, 0)),
                  full((1, GROUP_WIDTH)), full((3, LANES)),
                  full((1, CMP_STRIDE * LANES)), full((1, CMP_STRIDE * LANES)),
                  full((CMP_STRIDE * LANES, LANES)), full((CMP_STRIDE * LANES, LANES)), full((LANES, LANES)),
                  full((GROUP_WIDTH, GROUP_WIDTH)), full((n_blk, n_cmp)),
                  full((LANES, 3 * GROUP_WIDTH)), full(ctab.shape), full(wtab.shape)],
        out_specs=pl.BlockSpec((None, TQ, GROUP_WIDTH), lambda b, i: (b, i, 0)),
        out_shape=jax.ShapeDtypeStruct((B, S, GROUP_WIDTH), F32),
        scratch_shapes=[pltpu.VMEM((S, LANES), BF16), pltpu.VMEM((S, LANES), BF16),
                        pltpu.VMEM((S, LANES), BF16), pltpu.VMEM((S, LANES), BF16),
                        pltpu.VMEM((n_cmp, LANES), F32)],
        compiler_params=_params(("parallel", "arbitrary")),
        name="nsa",
    )(q, z_rwkv, kvc, kv, cos, sin, ccos, csin, qn, kn, pet, peb, wt, wb, w2, hb, cover, eg, ctab, wtab)


def _dil_kernel(q_ref, kv_ref, cos_ref, sin_ref, qn_ref, kn_ref, hb_ref, tb_ref, o_ref, kvs_ref, v1s_ref):
    S = kv_ref.shape[0]
    qi = pl.program_id(1)
    W = GROUP_WIDTH

    @pl.when(qi == 0)
    def _prep_keys():
        rows = 512

        def body(i, _):
            r0 = pl.multiple_of(i * rows, rows)
            cs = cos_ref[pl.ds(r0, rows), :]
            sn = sin_ref[pl.ds(r0, rows), :]
            k = kv_ref[pl.ds(r0, rows), 0:W]
            v = kv_ref[pl.ds(r0, rows), W:2 * W]
            ms = _dot_split(k * k, hb_ref[...])
            kn = k * lax.rsqrt(ms + RMS_EPS) * kn_ref[...]
            kr = _rope_lanes(kn, jnp.concatenate([cs, cs], axis=1), jnp.concatenate([sn, sn], axis=1))
            lane = _iota((rows, LANES), 1)
            for h in range(N_HEADS):
                kb = kr[:, (h // 2) * LANES:(h // 2 + 1) * LANES]
                vb = v[:, (h // 2) * LANES:(h // 2 + 1) * LANES]
                if h % 2:
                    kb = pltpu.roll(kb, HEAD_DIM, 1)
                else:
                    vb = pltpu.roll(vb, HEAD_DIM, 1)
                kvs_ref[h, pl.ds(r0, rows), :] = jnp.where(lane < HEAD_DIM, kb, vb).astype(BF16)
                v1s_ref[h, pl.ds(r0, rows), :] = jnp.where(lane < HEAD_DIM, 1.0, vb).astype(BF16)
            return 0

        lax.fori_loop(0, S // rows, body, 0)

    r0 = pl.multiple_of(qi * DIL_TQ, DIL_TQ)
    q = q_ref[...]
    ms = _dot_split(q * q, hb_ref[...])
    qn = q * lax.rsqrt(ms + RMS_EPS) * qn_ref[...]
    cs = cos_ref[pl.ds(r0, DIL_TQ), :]
    sn = sin_ref[pl.ds(r0, DIL_TQ), :]
    qr = _rope_lanes(qn, jnp.concatenate([cs, cs], axis=1), jnp.concatenate([sn, sn], axis=1)) * (HEAD_DIM ** -0.5 * LOG2E)
    qh = _pad_heads(qr)
    ratio = TK // DIL_TQ

    def body(j, carry):
        k0 = pl.multiple_of(j * TK, TK)
        bias = tb_ref[qi - ratio * j]
        return _flash_step(carry, qh, [kvs_ref[h, pl.ds(k0, TK), :] for h in range(N_HEADS)],
                           [v1s_ref[h, pl.ds(k0, TK), :] for h in range(N_HEADS)], [bias] * N_HEADS)

    res = lax.fori_loop(0, qi // ratio + 1, body, tuple(_flash_init(DIL_TQ) for _ in range(N_HEADS)))
    o_ref[...] = _merge_heads([_flash_finish(c) for c in res])


def _dil_bias_table(S):
    n = S // DIL_TQ
    d = (np.arange(n)[:, None, None] * DIL_TQ + np.arange(DIL_TQ)[None, :, None] - np.arange(TK)[None, None, :])
    cnt = np.zeros(d.shape, np.float32)
    for window, dil in DIL_PATTERNS:
        cnt += ((d >= 0) & (d <= window) & (d % dil == 0)).astype(np.float32)
    return jnp.asarray(np.where(cnt > 0, np.log2(np.maximum(cnt, 1.0)), NEG_INF).astype(np.float32))


def _dilated(q, kv, cos, sin, q_norm, k_norm):
    B, S, W = q.shape
    tb = _dil_bias_table(S)
    qn = jnp.tile(q_norm, N_HEADS).reshape(1, W)
    kn = jnp.tile(k_norm, N_HEADS).reshape(1, W)
    hb = _head_mean_matrix()
    full = lambda shape: pl.BlockSpec(shape, lambda b, i: (0,) * len(shape))
    return pl.pallas_call(
        _dil_kernel,
        grid=(B, S // DIL_TQ),
        in_specs=[pl.BlockSpec((None, DIL_TQ, W), lambda b, i: (b, i, 0)),
                  pl.BlockSpec((None, S, 2 * W), lambda b, i: (b, 0, 0)),
                  pl.BlockSpec((None, S, LANES), lambda b, i: (b, 0, 0)),
                  pl.BlockSpec((None, S, LANES), lambda b, i: (b, 0, 0)),
                  full((1, W)), full((1, W)), full((W, W)), full(tb.shape)],
        out_specs=pl.BlockSpec((None, DIL_TQ, W), lambda b, i: (b, i, 0)),
        out_shape=jax.ShapeDtypeStruct((B, S, W), F32),
        scratch_shapes=[pltpu.VMEM((N_HEADS, S, LANES), BF16), pltpu.VMEM((N_HEADS, S, LANES), BF16)],
        compiler_params=_params(("parallel", "arbitrary")),
        name="dilated",
    )(q, kv, cos, sin, qn, kn, hb, tb)


def _softplus(x):
    return jnp.maximum(x, 0.0) + jnp.log(1.0 + jnp.exp(-jnp.abs(x)))


def _rwkv_kernel(first, *refs):
    if first:
        (z_ref, mu_ref, w0_ref, w2_ref, a0_ref, a2_ref, g2_ref, kk_ref, ka_ref, rk_ref, lnw_ref, lnb_ref,
         hb_ref, y_ref, vf_out_ref, st_ref, prev_ref, ro_ref, ub_ref, colb_ref, g_ref, t2_ref) = refs
    else:
        (z_ref, vf_ref, mu_ref, w0_ref, w2_ref, a0_ref, a2_ref, g2_ref, kk_ref, ka_ref, rk_ref, lnw_ref, lnb_ref,
         v0_ref, v1_ref, v2_ref, hb_ref, y_ref, st_ref, prev_ref, ro_ref, ub_ref, colb_ref, g_ref, t2_ref) = refs
    NB, SB = z_ref.shape[0], z_ref.shape[1]
    C = RWKV_CHUNK
    W = GROUP_WIDTH
    n_pairs = N_HEADS // 2
    R = RWKV_GROUP * C

    @pl.when(pl.program_id(1) == 0)
    def _reset():
        st_ref[...] = jnp.zeros_like(st_ref)
        prev_ref[...] = jnp.zeros_like(prev_ref)

    ri = _iota((C, C), 0)
    ci = _iota((C, C), 1)
    tri_incl = jnp.where(ci <= ri, 1.0, 0.0).astype(BF16)
    rr = _iota((2 * LANES, 2 * LANES), 0)
    cc = _iota((2 * LANES, 2 * LANES), 1)
    g_mask = ((((rr >> 6) & 1) == ((cc >> 6) & 1))
              & ((cc & (C - 1)) <= jnp.where(rr < LANES, (rr & (C - 1)) - 1, rr & (C - 1))))
    r2 = _iota((LANES, LANES), 0)
    c2 = _iota((LANES, LANES), 1)
    eye_f = jnp.where(r2 == c2, 1.0, 0.0)
    lane_lo = _iota((C, LANES), 1) < HEAD_DIM
    hb = hb_ref[...].astype(BF16)

    def stack(x):
        return jnp.concatenate([jnp.where(lane_lo, x, 0.0), jnp.where(lane_lo, 0.0, x)], axis=0)

    def phase1(i, _, bb):
        t0 = pl.multiple_of(i * R, R)
        zc = z_ref[bb, pl.ds(t0, R), :]
        inside = jnp.where(i > 0, 1.0, 0.0)
        prev = inside * z_ref[bb, pl.ds(jnp.maximum(t0 - 1, 0), 1), :] + (1.0 - inside) * prev_ref[bb]
        zp = jnp.where(_iota(zc.shape, 0) == 0, prev, pltpu.roll(zc, 1, 0))
        zs = zc + (zp - zc) * mu_ref[...]
        r, k, v = zs[:, 0:W], zs[:, W:2 * W], zs[:, 2 * W:3 * W]
        xwa, xg = zs[:, 768:896], zs[:, 896:1152]
        w_log = -_softplus(-(w0_ref[...] + _dot(jnp.tanh(xwa), w2_ref[...]))) - 0.5
        lw = -jnp.exp(w_log)
        a = _sigmoid(a0_ref[...] + _dot(xwa, a2_ref[...]))
        g = _dot(_sigmoid(xg), g2_ref[...])
        if first:
            vf_out_ref[bb, pl.ds(t0, R), :] = v
        else:
            vf = vf_ref[bb, pl.ds(t0, R), :]
            v = v + (vf - v) * _sigmoid(v0_ref[...] + _dot(_dot(v, v1_ref[...]), v2_ref[...]))
        kkr = k * kk_ref[...]
        nrm = jnp.sqrt(_dot_split(kkr * kkr, hb) * HEAD_DIM)
        kk = kkr / jnp.maximum(nrm, 1e-12)
        k2 = k * (1.0 + (a - 1.0) * ka_ref[...])
        bonus = _dot_split(r * k2 * rk_ref[...], hb) * HEAD_DIM * v
        g_ref[bb, pl.ds(t0, R), :] = g
        t2_ref[bb, pl.ds(t0, R), :] = (lnb_ref[...] + bonus) * g

        items = [(c, p) for c in range(RWKV_GROUP) for p in range(n_pairs)]
        idx = range(len(items))
        sub = lambda x, c, p: x[c * C:(c + 1) * C, p * LANES:(p + 1) * LANES]
        cum = [_cumsum_rows(tri_incl, sub(lw, c, p)) for c, p in items]
        a_st, r_st, v_st, gm, tails = [], [], [], [], []
        for j, (c, p) in enumerate(items):
            lwp = sub(lw, c, p)
            cum_end = cum[j][C - 1:C, :]
            e_pos, e_neg = jnp.exp(cum[j]), jnp.exp(-cum[j])
            e_exc, e_tail = jnp.exp(cum[j] - lwp), jnp.exp(cum_end - cum[j])
            kkp, k2p = sub(kk, c, p), sub(k2, c, p)
            ka_p = kkp * sub(a, c, p)
            a_st.append(stack(-kkp * e_exc).astype(BF16))
            r_st.append(stack(sub(r, c, p) * e_pos))
            v_st.append(stack(sub(v, c, p)).astype(BF16))
            bhat = (ka_p * e_neg).astype(BF16)
            khat = (k2p * e_neg).astype(BF16)
            tails.append(jnp.concatenate([stack(ka_p * e_tail), stack(k2p * e_tail)], axis=0))
            gm.append(jnp.where(g_mask,
                                _dot_nt(jnp.concatenate([a_st[j], r_st[j].astype(BF16)], axis=0),
                                        jnp.concatenate([bhat, bhat, khat, khat], axis=0)), 0.0))
            colb_ref[bb, i * RWKV_GROUP + c, p] = jnp.broadcast_to(jnp.exp(cum_end), (LANES, LANES)).T
        tt = [tails[j].T.astype(BF16) for j in idx]
        lakv = [_dot(gm[j][0:LANES, LANES:2 * LANES], v_st[j]) for j in idx]
        mr = [gm[j][LANES:2 * LANES, :].astype(BF16) for j in idx]
        lp = [gm[j][0:LANES, 0:LANES] for j in idx]
        tinv = [eye_f + lp[j] for j in idx]
        for _ in range(5):
            lpb = [lp[j].astype(BF16) for j in idx]
            lp = [jnp.dot(lpb[j], lpb[j], preferred_element_type=F32) for j in idx]
            tinv = [tinv[j] + _dot(lp[j], tinv[j]) for j in idx]
        wu = [_dot(tinv[j], jnp.concatenate([a_st[j].astype(F32), lakv[j]], axis=1)) for j in idx]
        wm = [wu[j][:, 0:LANES].astype(BF16) for j in idx]
        u = [wu[j][:, LANES:2 * LANES].astype(BF16) for j in idx]
        mwbw = [jnp.dot(jnp.concatenate([mr[j][:, 0:LANES], tt[j][:, 0:LANES]], axis=0), wm[j],
                        preferred_element_type=F32) for j in idx]
        ub = [jnp.dot(jnp.concatenate([mr[j], tt[j]], axis=0), jnp.concatenate([u[j], v_st[j]], axis=0),
                      preferred_element_type=F32) for j in idx]
        for j, (c, p) in enumerate(items):
            n = i * RWKV_GROUP + c
            ro_ref[bb, n, p] = jnp.concatenate([r_st[j] + mwbw[j][0:LANES], mwbw[j][LANES:2 * LANES]], axis=0).astype(BF16)
            ub_ref[bb, n, p] = ub[j]
        return 0

    def phase2(n, _):
        t0 = pl.multiple_of(n * C, C)
        items = [(bb, p) for bb in range(NB) for p in range(n_pairs)]
        st = [st_ref[bb, p] for bb, p in items]
        res = [jnp.dot(ro_ref[bb, n, p], st[j].astype(BF16), preferred_element_type=F32)
               for j, (bb, p) in enumerate(items)]
        ub = [ub_ref[bb, n, p] for bb, p in items]
        for j, (bb, p) in enumerate(items):
            st_ref[bb, p] = colb_ref[bb, n, p] * st[j] + res[j][LANES:2 * LANES] + ub[j][LANES:2 * LANES]
        o_st = [res[j][0:LANES] + ub[j][0:LANES] for j in range(len(items))]
        for bb in range(NB):
            y_ref[bb, pl.ds(t0, C), :] = jnp.concatenate(
                [o[0:C] + o[C:2 * C] for o in o_st[bb * n_pairs:(bb + 1) * n_pairs]], axis=1)
        return 0

    for bb in range(NB):
        lax.fori_loop(0, SB // R, functools.partial(phase1, bb=bb), 0)
    lax.fori_loop(0, SB // C, phase2, 0)

    tiles = [(bb, slice(t * LANES, (t + 1) * LANES)) for bb in range(NB) for t in range(SB // LANES)]
    o = [y_ref[bb, rows, :] for bb, rows in tiles]
    dev = [x - _dot_split(x, hb) for x in o]
    var = [_dot_split(d * d, hb) for d in dev]
    for (bb, rows), d, v in zip(tiles, dev, var):
        y_ref[bb, rows, :] = (d * lax.rsqrt(v + RWKV_GN_EPS) * (lnw_ref[...] * g_ref[bb, rows, :])
                              + t2_ref[bb, rows, :])
    for bb in range(NB):
        prev_ref[bb] = z_ref[bb, SB - 1:SB, :]


def _cumsum_rows(tri_incl, x):
    hi = x.astype(BF16)
    lo = (x - hi.astype(F32)).astype(BF16)
    return jnp.dot(tri_incl, hi, preferred_element_type=F32) + jnp.dot(tri_incl, lo, preferred_element_type=F32)


RWKV_BLOCK = 512
RWKV_BATCH = 4


def _rwkv(z, v_first, v_res, mu, w0, w2, a0, a2, g2, k_k, k_a, r_k, ln_w, ln_b):
    B, S, P = z.shape
    W = GROUP_WIDTH
    SB = RWKV_BLOCK
    ncb = SB // RWKV_CHUNK
    n_pairs = N_HEADS // 2
    first = v_res is None
    row = lambda v: v.reshape(1, W)
    padr = lambda m, n: jnp.pad(m, ((0, n - m.shape[0]), (0, 0)))
    NB = RWKV_BATCH
    seq = lambda w: pl.BlockSpec((NB, SB, w), lambda b, j: (b, j, 0))
    full = lambda shape: pl.BlockSpec(shape, lambda b, j: (0,) * len(shape))
    args = [z]
    specs = [seq(P)]
    if not first:
        args.append(v_first)
        specs.append(seq(W))
    a2_pad = jnp.concatenate([jnp.zeros_like(a2), a2], axis=0)
    args += [_pack_rwkv_row(mu), row(w0), padr(w2, LANES), row(a0), a2_pad, padr(g2, 2 * LANES),
             row(k_k), row(k_a), r_k.reshape(1, W), row(ln_w), row(ln_b)]
    specs += [full((1, P)), full((1, W)), full((LANES, W)), full((1, W)), full((LANES, W)), full((2 * LANES, W)),
              full((1, W)), full((1, W)), full((1, W)), full((1, W)), full((1, W))]
    if not first:
        v0, v1, v2 = v_res
        args += [row(v0), jnp.pad(v1, ((0, 0), (0, LANES - v1.shape[1]))), padr(v2, LANES)]
        specs += [full((1, W)), full((W, LANES)), full((LANES, W))]
    args.append(_head_mean_matrix())
    specs.append(full((W, W)))
    out_shape = [jax.ShapeDtypeStruct((B, S, W), F32)]
    out_specs = [seq(W)]
    if first:
        out_shape.append(jax.ShapeDtypeStruct((B, S, W), F32))
        out_specs.append(seq(W))
    res = pl.pallas_call(
        functools.partial(_rwkv_kernel, first),
        grid=(B // NB, S // SB),
        in_specs=specs,
        out_specs=out_specs,
        out_shape=out_shape,
        scratch_shapes=[pltpu.VMEM((NB, n_pairs, LANES, LANES), F32), pltpu.VMEM((NB, 1, P), F32),
                        pltpu.VMEM((NB, ncb, n_pairs, 2 * LANES, LANES), BF16),
                        pltpu.VMEM((NB, ncb, n_pairs, 2 * LANES, LANES), F32),
                        pltpu.VMEM((NB, ncb, n_pairs, LANES, LANES), F32),
                        pltpu.VMEM((NB, SB, W), F32), pltpu.VMEM((NB, SB, W), F32)],
        compiler_params=_params(("parallel", "arbitrary")),
        name="rwkv7_first" if first else "rwkv7",
    )(*args)
    return (res[0], res[1]) if first else (res[0], v_first)


ROUTER_ROWS = 32


def _expert_lane(e):
    return (e % 4) * 8 + e // 4


def _out_kernel(x_ref, ya_ref, yb_ref, yc_ref, yd_ref, wo_ref, mod_ref, g_ref, rwh_ref, rwl_ref, rb_ref,
                xo_ref, h_ref, gates_ref):
    y = jnp.concatenate([ya_ref[...].astype(BF16), yb_ref[...].astype(BF16),
                         yc_ref[...].astype(BF16), yd_ref[...].astype(BF16)], axis=1)
    mix = jnp.dot(y, wo_ref[...], preferred_element_type=F32)
    x = x_ref[...] + mod_ref[2:3, :] * mix
    xo_ref[...] = x
    ms = jnp.mean(x * x, axis=-1, keepdims=True)
    h = x * lax.rsqrt(ms + RMS_EPS) * g_ref[...] * (1.0 + mod_ref[4:5, :]) + mod_ref[3:4, :]
    h_hi = h.astype(BF16)
    h_ref[...] = h_hi

    h_lo = (h - h_hi.astype(F32)).astype(BF16)
    logits = _dot_nt(rwh_ref[...], h_hi) + _dot_nt(rwh_ref[...], h_lo) + _dot_nt(rwl_ref[...], h_hi)
    aff = _sigmoid(logits)
    s = aff + rb_ref[...]
    n_grp = ROUTER_ROWS // 4
    slot = lambda x, j: x[j * n_grp:(j + 1) * n_grp]
    s_j = [slot(s, j) for j in range(4)]
    a_j = [slot(aff, j) for j in range(4)]
    top2 = []
    for j in range(4):
        rank = jnp.zeros(s_j[j].shape, F32)
        for m in range(4):
            if m != j:
                beats = (s_j[m] >= s_j[j]) if m < j else (s_j[m] > s_j[j])
                rank = rank + jnp.where(beats, 1.0, 0.0)
        top2.append(rank < 2.0)
    gs = sum(jnp.where(top2[j], s_j[j], 0.0) for j in range(4))
    row = _iota(gs.shape, 0)
    lost = jnp.zeros(gs.shape, F32)
    for d in range(1, n_grp):
        other = pltpu.roll(gs, d, 0)
        lost = lost + jnp.where(row >= d, jnp.where(other >= gs, 1.0, 0.0), jnp.where(other > gs, 1.0, 0.0))
    chosen = lost < 0.5
    sel = [top2[j] & chosen for j in range(4)]
    den = sum(jnp.where(sel[j], a_j[j], 0.0) for j in range(4))
    den = jnp.where(den > 0.0, den, 1.0)
    gates_t = jnp.concatenate([jnp.where(sel[j], a_j[j] / den, 0.0) for j in range(4)]
                              + [jnp.zeros((LANES - ROUTER_ROWS, s.shape[1]), F32)], axis=0)
    gates_ref[...] = gates_t.T


def _out_proj(x, ys, w_out, mod_l, g, router_w, router_b):
    B, S, D = x.shape
    tm = 1024
    W = GROUP_WIDTH
    order = np.full((ROUTER_ROWS,), -1)
    order[[_expert_lane(e) for e in range(N_EXPERTS)]] = np.arange(N_EXPERTS)
    real = jnp.asarray(order >= 0)
    rw = jnp.where(real[:, None], router_w.T[np.maximum(order, 0)], 0.0)
    rw_hi = rw.astype(BF16)
    rw_lo = (rw - rw_hi.astype(F32)).astype(BF16)
    rb = jnp.where(real, router_b[np.maximum(order, 0)], NEG_INF).reshape(ROUTER_ROWS, 1)
    tile = lambda w: pl.BlockSpec((None, tm, w), lambda b, i: (b, i, 0))
    full = lambda shape: pl.BlockSpec(shape, lambda b, i: (0,) * len(shape))
    return pl.pallas_call(
        _out_kernel,
        grid=(B, S // tm),
        in_specs=[tile(D), tile(W), tile(W), tile(W), tile(W), full((D, D)),
                  pl.BlockSpec((None, 6, D), lambda b, i: (b, 0, 0)), full((1, D)),
                  full((ROUTER_ROWS, D)), full((ROUTER_ROWS, D)), full((ROUTER_ROWS, 1))],
        out_specs=[tile(D), tile(D), tile(LANES)],
        out_shape=[jax.ShapeDtypeStruct((B, S, D), F32), jax.ShapeDtypeStruct((B, S, D), BF16),
                   jax.ShapeDtypeStruct((B, S, LANES), F32)],
        compiler_params=_params(("parallel", "parallel")),
        name="out_proj_router",
    )(x, *ys, w_out.astype(BF16), mod_l, g.reshape(1, D), rw_hi, rw_lo, rb)


MOE_EXPERTS_PER_STEP = 4


def _moe_kernel(x_ref, h_ref, gates_ref, mod_ref, wg_ref, wu_ref, wd_ref, o_ref):
    step = pl.program_id(2)
    n = MOE_EXPERTS_PER_STEP

    @pl.when(step == 0)
    def _init():
        o_ref[...] = x_ref[...]

    h = h_ref[...]
    gates = gates_ref[...]
    lane = _iota(gates.shape, 1)
    hgs = [jnp.dot(h, wg_ref[k].astype(BF16), preferred_element_type=F32) for k in range(n)]
    hus = [jnp.dot(h, wu_ref[k].astype(BF16), preferred_element_type=F32) for k in range(n)]
    acts = []
    for k in range(n):
        ge = jnp.sum(jnp.where(lane == _expert_lane(step * n + k), gates, 0.0), axis=-1, keepdims=True)
        acts.append((hgs[k] * _sigmoid(hgs[k]) * hus[k] * ge).astype(BF16))
    act = jnp.concatenate(acts, axis=1)
    wd = wd_ref[...].astype(BF16).reshape(n * D_EXPERT, wd_ref.shape[2])
    o_ref[...] += mod_ref[5:6, :] * jnp.dot(act, wd, preferred_element_type=F32)


def _moe(x, h, gates, mod_l, w_gate, w_up, w_down, layer):
    B, S, D = x.shape
    E = w_gate.shape[1]
    n = MOE_EXPERTS_PER_STEP
    tm = 1024
    tile = lambda w: pl.BlockSpec((None, tm, w), lambda b, i, e: (b, i, 0))
    return pl.pallas_call(
        _moe_kernel,
        grid=(B, S // tm, E // n),
        in_specs=[tile(D), tile(D), tile(LANES),
                  pl.BlockSpec((None, 6, D), lambda b, i, e: (b, 0, 0)),
                  pl.BlockSpec((None, n, D, D_EXPERT), lambda b, i, e: (layer, e, 0, 0)),
                  pl.BlockSpec((None, n, D, D_EXPERT), lambda b, i, e: (layer, e, 0, 0)),
                  pl.BlockSpec((None, n, D_EXPERT, D), lambda b, i, e: (layer, e, 0, 0))],
        out_specs=tile(D),
        out_shape=jax.ShapeDtypeStruct((B, S, D), F32),
        compiler_params=_params(("parallel", "parallel", "arbitrary")),
        name="moe",
    )(x, h, gates, mod_l, w_gate, w_up, w_down)


def kernel(x, c, positions, ada_w, ada_b, norm_mix_g, norm_ffn_g, w_in, w_out, nsa_q_norm, nsa_k_norm, nsa_cmp_pe, nsa_cmp_w1, nsa_cmp_w2, pool_w, pool_scale, rwkv_mu, rwkv_w0, rwkv_w2, rwkv_a0, rwkv_a2, rwkv_g2, rwkv_k_k, rwkv_k_a, rwkv_r_k, rwkv_ln_w, rwkv_ln_b, rwkv_v0, rwkv_v1, rwkv_v2, dil_q_norm, dil_k_norm, router_w, router_b, moe_w_gate, moe_w_up, moe_w_down):
    depth = ada_w.shape[0]
    mod = _modulation(c, ada_w, ada_b)
    cos, sin, ccos, csin = _rope_tables(positions)
    v_first = None
    for l in range(depth):
        q_a, kvc_a, kv_a, u_pool, z_rwkv, q_d, kv_d = _in_proj(x, mod[l], norm_mix_g[l], w_in, l)
        y_nsa = _nsa(q_a, kvc_a, kv_a, z_rwkv, cos, sin, ccos, csin, nsa_q_norm[l], nsa_k_norm[l],
                     nsa_cmp_pe[l], nsa_cmp_w1[l], nsa_cmp_w2[l])
        y_pool = _pool(u_pool, pool_w[l], pool_scale[l])
        v_res = None if l == 0 else (rwkv_v0[l - 1], rwkv_v1[l - 1], rwkv_v2[l - 1])
        y_rwkv, v_first = _rwkv(z_rwkv, v_first, v_res, rwkv_mu[l], rwkv_w0[l], rwkv_w2[l], rwkv_a0[l], rwkv_a2[l],
                                rwkv_g2[l], rwkv_k_k[l], rwkv_k_a[l], rwkv_r_k[l], rwkv_ln_w[l], rwkv_ln_b[l])
        y_dil = _dilated(q_d, kv_d, cos, sin, dil_q_norm[l], dil_k_norm[l])
        x_mid, h2, gates = _out_proj(x, (y_nsa, y_pool, y_rwkv, y_dil), w_out[l], mod[l], norm_ffn_g[l],
                                     router_w, router_b)
        x = _moe(x_mid, h2, gates, mod[l], moe_w_gate, moe_w_up, moe_w_down, l)
    return x
```
